```python
import math
import jax
import jax.numpy as jnp
from jax import lax
import numpy as np

D_MODEL = 1024
BATCH = 8
SEQ = 2048
DEPTH = 1
DEC_BATCH = 128
DEC_SEQ = 4
PAST_LEN = 16384
PAGE_SIZE = 128

A_HEADS = 8
A_DK = 128
A_DV = D_MODEL // A_HEADS
A_WIDTH_K = A_HEADS * A_DK
A_WIDTH = A_HEADS * A_DV
A_CHUNK = 64
B_HEADS = 16
B_KV_HEADS = 2
B_GROUP = B_HEADS // B_KV_HEADS
B_HD = 64
B_WIDTH = B_HEADS * B_HD
B_KV_WIDTH = B_KV_HEADS * B_HD
WINDOW = 128
REL_BUCKETS = 32
REL_MAX_DIST = 128
PLE_DIM = 256
EPS = 1e-6

IN_SPLITS = (A_WIDTH_K, A_WIDTH_K, A_WIDTH, A_WIDTH, A_WIDTH,
             B_WIDTH, B_KV_WIDTH, B_KV_WIDTH, B_WIDTH, D_MODEL, D_MODEL)
IN_TOTAL = 2 * A_WIDTH_K + 3 * A_WIDTH + 2 * B_WIDTH + 2 * B_KV_WIDTH + 2 * D_MODEL

kernel_name = 'hgrn2_swa_sink_gated_parallel_decoder_step'


def rmsnorm(x, g):
    xf = x.astype(jnp.float32)
    y = xf * lax.rsqrt(jnp.mean(xf * xf, axis=-1, keepdims=True) + EPS)
    return (y * g.astype(jnp.float32)).astype(x.dtype)


def rel_bucket(rel):
    n = jnp.maximum(rel, 0)
    max_exact = REL_BUCKETS // 2
    nf = jnp.maximum(n, 1).astype(jnp.float32)
    large = max_exact + (jnp.log(nf / max_exact) / math.log(REL_MAX_DIST / max_exact)
                         * (REL_BUCKETS - max_exact)).astype(jnp.int32)
    large = jnp.minimum(large, REL_BUCKETS - 1)
    return jnp.where(n < max_exact, n, large)


def rel_bias_heads(rel, rel_bias):
    b = rel_bias.astype(jnp.float32)[rel_bucket(rel)]
    q_len, k_len = rel.shape
    return jnp.transpose(b, (2, 0, 1)).reshape(B_KV_HEADS, B_GROUP, q_len, k_len)


def sink_softmax(logits, mask, sink):
    logits = jnp.where(mask, logits, -jnp.inf)
    m = jnp.maximum(jnp.max(logits, axis=-1, keepdims=True), sink)
    p = jnp.exp(logits - m)
    return p / (jnp.sum(p, axis=-1, keepdims=True) + jnp.exp(sink - m))


def hgrn2_chunked(q, k, v, logf, s0, chunk):
    bn, L, h, _ = q.shape
    nc = L // chunk

    def to_chunks(a):
        return jnp.transpose(a.reshape(bn, nc, chunk, h, a.shape[-1]), (1, 0, 3, 2, 4))

    causal = jnp.tril(jnp.ones((chunk, chunk), dtype=bool))

    def step(S, inp):
        qc, kc, vc, gc = inp
        b = jnp.cumsum(gc, axis=2)
        o_inter = jnp.einsum('bhtd,bhdv->bhtv', qc * jnp.exp(b), S)
        diff = b[:, :, :, None, :] - b[:, :, None, :, :]
        decay = jnp.exp(jnp.where(causal[:, :, None], diff, -jnp.inf))
        att = jnp.einsum('bhtd,bhsd,bhtsd->bhts', qc, kc, decay)
        o_intra = jnp.einsum('bhts,bhsv->bhtv', att, vc)
        b_last = b[:, :, -1]
        k_state = kc * jnp.exp(b_last[:, :, None] - b)
        S_new = jnp.exp(b_last)[..., None] * S + jnp.einsum('bhsd,bhsv->bhdv', k_state, vc)
        return S_new, o_inter + o_intra

    s_fin, o = lax.scan(step, s0, (to_chunks(q), to_chunks(k), to_chunks(v), to_chunks(logf)))
    o = jnp.transpose(o, (1, 0, 3, 2, 4)).reshape(bn, L, h, v.shape[-1])
    return o, s_fin


def hgrn2_branch(q_pre, f_pre, i_pre, og_pre, s0, lb, norm_g):
    bn, L, _ = q_pre.shape
    f32 = jnp.float32
    q = jax.nn.silu(q_pre.astype(f32)).reshape(bn, L, A_HEADS, A_DK)
    fg = lb + (1.0 - lb) * jax.nn.sigmoid(f_pre.astype(f32))
    logf = jnp.log(fg).reshape(bn, L, A_HEADS, A_DK)
    k = (1.0 - fg).reshape(bn, L, A_HEADS, A_DK)
    v = i_pre.astype(f32).reshape(bn, L, A_HEADS, A_DV)
    chunk = A_CHUNK if L % A_CHUNK == 0 else L
    o, s_new = hgrn2_chunked(q, k, v, logf, s0.astype(f32), chunk)
    o = o * lax.rsqrt(jnp.mean(o * o, axis=-1, keepdims=True) + EPS)
    o = o * norm_g.astype(f32).reshape(A_HEADS, A_DV)
    o = o.reshape(bn, L, A_WIDTH) * jax.nn.sigmoid(og_pre.astype(f32))
    return o, s_new


def swa_prompt(q, k, v, sink, rel_bias):
    f32 = jnp.float32
    bn, L = q.shape[0], q.shape[1]
    blk = WINDOW
    nb = L // blk
    qb = q.astype(f32).reshape(bn, nb, blk, B_KV_HEADS, B_GROUP, B_HD)
    kb = k.astype(f32).reshape(bn, nb, blk, B_KV_HEADS, B_HD)
    vb = v.astype(f32).reshape(bn, nb, blk, B_KV_HEADS, B_HD)
    zero = jnp.zeros_like(kb[:, :1])
    kk = jnp.concatenate([jnp.concatenate([zero, kb[:, :-1]], axis=1), kb], axis=2)
    vv = jnp.concatenate([jnp.concatenate([zero, vb[:, :-1]], axis=1), vb], axis=2)
    rel = jnp.arange(blk)[:, None] + blk - jnp.arange(2 * blk)[None, :]
    band = (rel >= 0) & (rel < WINDOW)
    kvalid = (jnp.arange(nb)[:, None] * blk - blk + jnp.arange(2 * blk)[None, :]) >= 0
    mask = (band[None] & kvalid[:, None, :])[None, :, None, None]
    logits = jnp.einsum('bnqkgd,bnskd->bnkgqs', qb, kk) * (B_HD ** -0.5)
    logits = logits + rel_bias_heads(rel, rel_bias)
    probs = sink_softmax(logits, mask, sink.astype(f32).reshape(B_KV_HEADS, B_GROUP, 1, 1))
    out = jnp.einsum('bnkgqs,bnskd->bnqkgd', probs, vv)
    return out.reshape(bn, L, B_WIDTH), k[:, -WINDOW:], v[:, -WINDOW:]


def swa_sample(q, k, v, win_k, win_v, sink, rel_bias):
    f32 = jnp.float32
    bn, T = q.shape[0], q.shape[1]
    W = win_k.shape[1]
    kk = jnp.concatenate([win_k.astype(k.dtype), k], axis=1)
    vv = jnp.concatenate([win_v.astype(v.dtype), v], axis=1)
    rel = (W + jnp.arange(T))[:, None] - jnp.arange(W + T)[None, :]
    mask = (rel >= 0) & (rel < WINDOW)
    qg = q.astype(f32).reshape(bn, T, B_KV_HEADS, B_GROUP, B_HD)
    logits = jnp.einsum('bqkgd,bskd->bkgqs', qg, kk.astype(f32)) * (B_HD ** -0.5)
    logits = logits + rel_bias_heads(rel, rel_bias)
    probs = sink_softmax(logits, mask, sink.astype(f32).reshape(B_KV_HEADS, B_GROUP, 1, 1))
    out = jnp.einsum('bkgqs,bskd->bqkgd', probs, vv.astype(f32))
    return out.reshape(bn, T, B_WIDTH), kk[:, T:], vv[:, T:]


def trunk_layer(x, pl, s_hgrn, win_k, win_v, layer, norm_pre, w_in, hgrn_lb, hgrn_norm,
                attn_sink, rel_bias, w_pa, w_pb, w_o, norm_post, w_ple, w_ple_gate):
    bn, L, _ = x.shape
    u = rmsnorm(x, norm_pre)
    proj = jnp.einsum('bld,de->ble', u, w_in)
    offs = np.cumsum(IN_SPLITS)[:-1].tolist()
    aq, af, ai, aog, az, bq, bk, bv, bz, ga, gb = jnp.split(proj, offs, axis=-1)
    lb = jnp.cumsum(jax.nn.softmax(hgrn_lb.astype(jnp.float32), axis=0), axis=0)[layer]
    if s_hgrn is None:
        s_hgrn = jnp.zeros((bn, A_HEADS, A_DK, A_DV), jnp.float32)
    oa, s_new = hgrn2_branch(aq, af, ai, aog, s_hgrn, lb, hgrn_norm)
    oa = (oa * jax.nn.silu(az.astype(jnp.float32))).astype(x.dtype)
    a = jnp.einsum('ble,ed->bld', oa, w_pa)
    qb = bq.reshape(bn, L, B_HEADS, B_HD)
    kb = bk.reshape(bn, L, B_KV_HEADS, B_HD)
    vb = bv.reshape(bn, L, B_KV_HEADS, B_HD)
    if win_k is None:
        ob, k_win, v_win = swa_prompt(qb, kb, vb, attn_sink, rel_bias)
    else:
        ob, k_win, v_win = swa_sample(qb, kb, vb, win_k, win_v, attn_sink, rel_bias)
    ob = (ob * jax.nn.silu(bz.astype(jnp.float32))).astype(x.dtype)
    b = jnp.einsum('ble,ed->bld', ob, w_pb)
    m = jax.nn.sigmoid(ga) * a + jax.nn.sigmoid(gb) * b
    y = jnp.einsum('bld,de->ble', m, w_o)
    x = x + rmsnorm(y, norm_post)
    e = jnp.einsum('blp,pd->bld', pl, w_ple) * jax.nn.sigmoid(jnp.einsum('bld,de->ble', x, w_ple_gate))
    x = x + e
    return x, s_new, k_win, v_win


def setup_inputs(seed: int = 0) -> dict:
    key = jax.random.key(seed)
    ks = jax.random.split(key, 20)
    f32 = jnp.float32

    def nrm(k, shape, s):
        return jax.random.normal(k, shape, f32) * s

    return {
        'x_prompt': nrm(ks[0], (BATCH, SEQ, D_MODEL), 1.0),
        'x_sample': nrm(ks[1], (DEC_BATCH, DEC_SEQ, D_MODEL), 1.0),
        'state_hgrn': nrm(ks[2], (DEPTH, DEC_BATCH, A_HEADS, A_DK, A_DV), 0.5),
        'cache_swa_k': nrm(ks[3], (DEPTH, DEC_BATCH, WINDOW, B_KV_HEADS, B_HD), 1.0),
        'cache_swa_v': nrm(ks[4], (DEPTH, DEC_BATCH, WINDOW, B_KV_HEADS, B_HD), 1.0),
        'p_prompt': nrm(ks[5], (DEPTH, BATCH, SEQ, PLE_DIM), 1.0),
        'p_sample': nrm(ks[6], (DEPTH, DEC_BATCH, DEC_SEQ, PLE_DIM), 1.0),
        'norm_pre': 1.0 + nrm(ks[7], (DEPTH, D_MODEL), 0.05),
        'w_in': nrm(ks[8], (DEPTH, D_MODEL, IN_TOTAL), D_MODEL ** -0.5),
        'hgrn_lb': nrm(ks[9], (DEPTH + 1, A_WIDTH_K), 0.5),
        'hgrn_norm': 1.0 + nrm(ks[10], (DEPTH, A_WIDTH), 0.05),
        'attn_sink': nrm(ks[11], (DEPTH, B_HEADS), 0.5),
        'rel_bias': nrm(ks[12], (REL_BUCKETS, B_HEADS), 0.5),
        'w_pa': nrm(ks[13], (DEPTH, A_WIDTH, D_MODEL), A_WIDTH ** -0.5),
        'w_pb': nrm(ks[14], (DEPTH, B_WIDTH, D_MODEL), B_WIDTH ** -0.5),
        'w_o': nrm(ks[15], (DEPTH, D_MODEL, D_MODEL), D_MODEL ** -0.5),
        'norm_post': 1.0 + nrm(ks[16], (DEPTH, D_MODEL), 0.05),
        'w_ple': nrm(ks[17], (DEPTH, PLE_DIM, D_MODEL), PLE_DIM ** -0.5),
        'w_ple_gate': nrm(ks[18], (DEPTH, D_MODEL, D_MODEL), D_MODEL ** -0.5),
    }


def reference(x_prompt, x_sample, state_hgrn, cache_swa_k, cache_swa_v, p_prompt, p_sample,
              norm_pre, w_in, hgrn_lb, hgrn_norm, attn_sink, rel_bias, w_pa, w_pb, w_o,
              norm_post, w_ple, w_ple_gate):
    xp, xs = x_prompt, x_sample
    sp_list, ss_list, kp_list, vp_list, ks_list, vs_list = [], [], [], [], [], []
    for l in range(DEPTH):
        w = (norm_pre[l], w_in[l], hgrn_lb, hgrn_norm[l], attn_sink[l], rel_bias,
             w_pa[l], w_pb[l], w_o[l], norm_post[l], w_ple[l], w_ple_gate[l])
        xp, sp, kp, vp = trunk_layer(xp, p_prompt[l], None, None, None, l, *w)
        xs, ss, ksm, vsm = trunk_layer(xs, p_sample[l], state_hgrn[l], cache_swa_k[l],
                                       cache_swa_v[l], l, *w)
        sp_list.append(sp.astype(x_prompt.dtype))
        ss_list.append(ss.astype(state_hgrn.dtype))
        kp_list.append(kp)
        vp_list.append(vp)
        ks_list.append(ksm.astype(cache_swa_k.dtype))
        vs_list.append(vsm.astype(cache_swa_v.dtype))
    state_hgrn_prompt = jnp.stack(sp_list)
    state_hgrn_sample = jnp.stack(ss_list)
    swa_k_prompt = jnp.stack(kp_list)
    swa_v_prompt = jnp.stack(vp_list)
    swa_k_sample = jnp.stack(ks_list)
    swa_v_sample = jnp.stack(vs_list)
    return (xp, xs, state_hgrn_prompt, state_hgrn_sample, swa_k_prompt, swa_v_prompt,
            swa_k_sample, swa_v_sample)
```

```python
import functools

import jax
import jax.numpy as jnp
from jax import lax
from jax.experimental import pallas as pl
from jax.experimental.pallas import tpu as pltpu

F32 = jnp.float32
BF16 = jnp.bfloat16

D_MODEL = 1024
A_HEADS = 8
A_DK = 128
A_DV = 128
A_CHUNK = 64
A_SUB = 16
B_HEADS = 16
B_KV_HEADS = 2
B_HD = 64
WINDOW = 128
REL_BUCKETS = 32
REL_MAX_DIST = 128
EPS = 1e-6
NEG = -1e30

G_AQ, G_AI, G_AOG, G_AZ, G_BQ, G_BZ, G_GA, G_GB = range(8)
N_GROUPS = 8

VMEM_LIMIT = 56 * 1024 * 1024


def _sigmoid(x):
    return 1.0 / (1.0 + jnp.exp(-x))


def _silu(x):
    return x * _sigmoid(x)


def _resident(shape):
    nd = len(shape)
    return pl.BlockSpec(shape, lambda *_: (0,) * nd, pipeline_mode=pl.Buffered(1))


def _inproj_kernel(x_ref, g_ref, wm_ref, wf_ref, wkv_ref, pm_ref, f_ref, kv_ref):
    x = x_ref[...]
    ms = jnp.mean(x * x, axis=-1, keepdims=True)
    u = (x * lax.rsqrt(ms + EPS) * g_ref[...]).astype(BF16)
    for c in range(N_GROUPS):
        sl = slice(c * D_MODEL, (c + 1) * D_MODEL)
        pm_ref[:, sl] = jnp.dot(u, wm_ref[:, sl], preferred_element_type=F32).astype(BF16)
    f_ref[...] = jnp.dot(u, wf_ref[...], preferred_element_type=F32)
    kv_ref[...] = jnp.dot(u, wkv_ref[...], preferred_element_type=F32)


def _inproj(x, g, wm, wf, wkv, tm):
    n = x.shape[0]
    nm = wm.shape[1]
    nkv = wkv.shape[1]
    return pl.pallas_call(
        _inproj_kernel,
        grid=(n // tm,),
        in_specs=[
            pl.BlockSpec((tm, D_MODEL), lambda i: (i, 0)),
            _resident((1, D_MODEL)),
            _resident((D_MODEL, nm)),
            _resident((D_MODEL, D_MODEL)),
            _resident((D_MODEL, nkv)),
        ],
        out_specs=[
            pl.BlockSpec((tm, nm), lambda i: (i, 0)),
            pl.BlockSpec((tm, D_MODEL), lambda i: (i, 0)),
            pl.BlockSpec((tm, nkv), lambda i: (i, 0)),
        ],
        out_shape=[
            jax.ShapeDtypeStruct((n, nm), BF16),
            jax.ShapeDtypeStruct((n, D_MODEL), F32),
            jax.ShapeDtypeStruct((n, nkv), F32),
        ],
        compiler_params=pltpu.CompilerParams(
            dimension_semantics=("arbitrary",), vmem_limit_bytes=VMEM_LIMIT),
        name="inproj",
    )(x, g, wm, wf, wkv)


def _lower_bound(lb_ref):
    l = lb_ref[...]
    m = jnp.max(l, axis=0, keepdims=True)
    e = jnp.exp(l - m)
    return e[0:1, :] / jnp.sum(e, axis=0, keepdims=True)


def _group_cumsum(x, row, period, shifts):
    pos = row & (period - 1)
    for sh in shifts:
        x = x + jnp.where(pos >= sh, pltpu.roll(x, sh, axis=0), 0.0)
    return x


def _hgrn_finish(o, g, og_pre, z_pre):
    o = o * lax.rsqrt(jnp.mean(o * o, axis=-1, keepdims=True) + EPS)
    return o * g * _sigmoid(og_pre) * _silu(z_pre)


def _dot_nt(a, b):
    return lax.dot_general(a, b, (((1,), (1,)), ((), ())), preferred_element_type=F32)


def _dot_tn(a, b):
    return lax.dot_general(a, b, (((0,), (0,)), ((), ())), preferred_element_type=F32)


def _hgrn_prompt_kernel(q_ref, v_ref, og_ref, z_ref, f_ref, lb_ref, g_ref, oa_ref, s_ref):
    n_chunks = q_ref.shape[0] // A_CHUNK
    lb = _lower_bound(lb_ref)
    g = g_ref[...]
    row = lax.broadcasted_iota(jnp.int32, (A_CHUNK, A_DK), 0)
    blk = row >> (A_SUB.bit_length() - 1)
    r64 = lax.broadcasted_iota(jnp.int32, (A_CHUNK, A_CHUNK), 0)
    c64 = lax.broadcasted_iota(jnp.int32, (A_CHUNK, A_CHUNK), 1)
    causal = c64 <= r64
    n_sub = A_CHUNK // A_SUB

    def chunk(c, st):
        r0 = pl.multiple_of(c * A_CHUNK, A_CHUNK)
        rows = pl.ds(r0, A_CHUNK)
        fg = lb + (1.0 - lb) * _sigmoid(f_ref[rows, :])
        logf = jnp.log(fg)
        k = 1.0 - fg
        b = _group_cumsum(logf, row, A_CHUNK, (1, 2, 4, 8, 16, 32))
        q = _silu(q_ref[rows, :].astype(F32))
        v = v_ref[rows, :]

        qs = (q * jnp.exp(b)).astype(BF16)
        o = _dot_nt(qs, st.astype(BF16))

        refs = [jnp.zeros((1, A_DK), F32)] + [b[A_SUB * i - 1:A_SUB * i, :] for i in range(1, n_sub)]
        refrow = jnp.concatenate([jnp.broadcast_to(r, (A_SUB, A_DK)) for r in refs], axis=0)
        qd = q * jnp.exp(b - refrow)
        q_parts = []
        k_parts = []
        for i in range(n_sub):
            q_parts.append(jnp.where(blk == i, qd, 0.0).astype(BF16))
            expo = jnp.where(blk <= i, refs[i] - b, 0.0)
            k_parts.append(jnp.where(blk <= i, k * jnp.exp(expo), 0.0).astype(BF16))
        att = _dot_nt(jnp.concatenate(q_parts, axis=1), jnp.concatenate(k_parts, axis=1))
        att = jnp.where(causal, att, 0.0)
        o = o + jnp.dot(att.astype(BF16), v, preferred_element_type=F32)

        b_last = b[A_CHUNK - 1:A_CHUNK, :]
        kd = (k * jnp.exp(b_last - b)).astype(BF16)
        st = st * jnp.exp(b_last) + _dot_tn(v, kd)

        oa = _hgrn_finish(o, g, og_ref[rows, :].astype(F32), z_ref[rows, :].astype(F32))
        oa_ref[rows, :] = oa.astype(BF16)
        return st

    st = lax.fori_loop(0, n_chunks, chunk, jnp.zeros((A_DV, A_DK), F32))
    s_ref[...] = st.T


def _hgrn_prompt(pm, f, hgrn_lb, hgrn_norm, batch, seq):
    col = lambda grp: pl.BlockSpec((seq, A_DK), lambda b, h, grp=grp: (b, grp * A_HEADS + h))
    n_lb = hgrn_lb.shape[0]
    return pl.pallas_call(
        _hgrn_prompt_kernel,
        grid=(batch, A_HEADS),
        in_specs=[
            col(G_AQ), col(G_AI), col(G_AOG), col(G_AZ),
            pl.BlockSpec((seq, A_DK), lambda b, h: (b, h)),
            pl.BlockSpec((n_lb, A_DK), lambda b, h: (0, h)),
            pl.BlockSpec((1, A_DV), lambda b, h: (0, h)),
        ],
        out_specs=[
            pl.BlockSpec((seq, A_DV), lambda b, h: (b, h)),
            pl.BlockSpec((None, None, A_DK, A_DV), lambda b, h: (b, h, 0, 0)),
        ],
        out_shape=[
            jax.ShapeDtypeStruct((batch * seq, A_HEADS * A_DV), BF16),
            jax.ShapeDtypeStruct((batch, A_HEADS, A_DK, A_DV), F32),
        ],
        compiler_params=pltpu.CompilerParams(
            dimension_semantics=("arbitrary", "arbitrary"), vmem_limit_bytes=VMEM_LIMIT),
        name="hgrn_prompt",
    )(pm, pm, pm, pm, f, hgrn_lb, hgrn_norm)


def _hgrn_sample_kernel(q_ref, v_ref, og_ref, z_ref, f_ref, lb_ref, g_ref, s0_ref, oa_ref, s_ref,
                        *, bb, t_len):
    rows = bb * t_len
    lb_all = _lower_bound(lb_ref)
    row = lax.broadcasted_iota(jnp.int32, (rows, A_DK), 0)
    pos = row & (t_len - 1)
    grp = row >> (t_len.bit_length() - 1)
    shifts = tuple(1 << i for i in range((t_len - 1).bit_length()))
    for h in range(A_HEADS):
        ls = slice(h * A_DK, (h + 1) * A_DK)
        lb = lb_all[:, ls]
        fg = lb + (1.0 - lb) * _sigmoid(f_ref[:, ls])
        logf = jnp.log(fg)
        k = 1.0 - fg
        b = _group_cumsum(logf, row, t_len, shifts)
        q = _silu(q_ref[:, ls].astype(F32))
        v = v_ref[:, ls].astype(F32)

        o = jnp.zeros((rows, A_DV), F32)
        for d in range(t_len):
            if d == 0:
                a = jnp.sum(q * k, axis=-1, keepdims=True)
                o = o + a * v
            else:
                w = jnp.exp(jnp.where(pos >= d, b - pltpu.roll(b, d, axis=0), 0.0))
                a = jnp.sum(q * pltpu.roll(k, d, axis=0) * w, axis=-1, keepdims=True)
                o = o + jnp.where(pos >= d, a * pltpu.roll(v, d, axis=0), 0.0)

        b_last = b
        for d in range(1, t_len):
            b_last = jnp.where(pos == t_len - 1 - d, pltpu.roll(b, rows - d, axis=0), b_last)
        e_last = jnp.exp(b_last)
        qs = q * jnp.exp(b)
        kd = k * jnp.exp(b_last - b)
        v16 = v.astype(BF16)
        for j in range(bb):
            s0 = s0_ref[j, h]
            mine = grp == j
            o = o + jnp.dot(jnp.where(mine, qs, 0.0).astype(BF16), s0.astype(BF16),
                            preferred_element_type=F32)
            u = _dot_tn(jnp.where(mine, kd, 0.0).astype(BF16), v16)
            e_row = e_last[j * t_len:j * t_len + 1, :]
            e_col = jnp.broadcast_to(e_row, (A_DK, A_DK)).T
            s_ref[j, h] = e_col * s0 + u

        oa = _hgrn_finish(o, g_ref[:, ls], og_ref[:, ls].astype(F32), z_ref[:, ls].astype(F32))
        oa_ref[:, ls] = oa.astype(BF16)


def _hgrn_sample(pm, f, hgrn_lb, hgrn_norm, s0, batch, t_len, bb):
    rows = bb * t_len
    col = lambda grp: pl.BlockSpec((rows, D_MODEL), lambda i, grp=grp: (i, grp))
    n_lb = hgrn_lb.shape[0]
    st_spec = pl.BlockSpec((bb, A_HEADS, A_DK, A_DV), lambda i: (i, 0, 0, 0))
    return pl.pallas_call(
        functools.partial(_hgrn_sample_kernel, bb=bb, t_len=t_len),
        grid=(batch // bb,),
        in_specs=[
            col(G_AQ), col(G_AI), col(G_AOG), col(G_AZ),
            pl.BlockSpec((rows, D_MODEL), lambda i: (i, 0)),
            _resident((n_lb, D_MODEL)),
            _resident((1, D_MODEL)),
            st_spec,
        ],
        out_specs=[pl.BlockSpec((rows, D_MODEL), lambda i: (i, 0)), st_spec],
        out_shape=[
            jax.ShapeDtypeStruct((batch * t_len, D_MODEL), BF16),
            jax.ShapeDtypeStruct((batch, A_HEADS, A_DK, A_DV), F32),
        ],
        compiler_params=pltpu.CompilerParams(
            dimension_semantics=("arbitrary",), vmem_limit_bytes=VMEM_LIMIT),
        name="hgrn_sample",
    )(pm, pm, pm, pm, f, hgrn_lb, hgrn_norm, s0)


N_PAIRS = B_HEADS // 2
PAIRS_PER_KV = N_PAIRS // B_KV_HEADS
N_KEYS = 2 * WINDOW


def _build_bias(bucket_ref, relb_ref, bias_ref, r):
    bucket = bucket_ref[...]
    for h in range(B_HEADS):
        def body(kb, acc, h=h):
            return jnp.where(bucket == kb, relb_ref[kb, h], acc)
        tab = lax.fori_loop(0, REL_BUCKETS, body, jnp.full(bucket.shape, NEG, F32))
        pair, parity = divmod(h, 2)
        kv, pp = divmod(pair, PAIRS_PER_KV)
        bias_ref[kv, pp * r:(pp + 1) * r, parity * N_KEYS:(parity + 1) * N_KEYS] = tab


def _swa_core(q_pairs, kk, vv, bias_ref, sink_ref, r, pen=None):
    lo = lax.broadcasted_iota(jnp.int32, (N_KEYS, 2 * B_HD), 1) < B_HD
    kk_sw = pltpu.roll(kk, B_HD, axis=1)
    vv_sw = pltpu.roll(vv, B_HD, axis=1)
    outs = []
    for kv in range(B_KV_HEADS):
        k_lo, k_hi = (kk, kk_sw) if kv == 0 else (kk_sw, kk)
        v_lo, v_hi = (vv, vv_sw) if kv == 0 else (vv_sw, vv)
        keys = jnp.concatenate([jnp.where(lo, k_lo, 0.0), jnp.where(lo, 0.0, k_hi)], axis=0).astype(BF16)
        vals = (jnp.where(lo, v_lo, 0.0).astype(BF16), jnp.where(lo, 0.0, v_hi).astype(BF16))
        q = jnp.concatenate(q_pairs[kv * PAIRS_PER_KV:(kv + 1) * PAIRS_PER_KV], axis=0)
        logits = _dot_nt(q, keys) + bias_ref[kv]
        if pen is not None:
            logits = logits + pen
        acc = None
        for parity in range(2):
            sink = jnp.concatenate(
                [jnp.full((r, 1), sink_ref[(kv * PAIRS_PER_KV + pp) * 2 + parity], F32)
                 for pp in range(PAIRS_PER_KV)], axis=0)
            l = logits[:, parity * N_KEYS:(parity + 1) * N_KEYS]
            m = jnp.maximum(jnp.max(l, axis=-1, keepdims=True), sink)
            p = jnp.exp(l - m)
            denom = jnp.sum(p, axis=-1, keepdims=True) + jnp.exp(sink - m)
            pv = jnp.dot(p.astype(BF16), vals[parity], preferred_element_type=F32) / denom
            acc = pv if acc is None else acc + pv
        for pp in range(PAIRS_PER_KV):
            outs.append(acc[pp * r:(pp + 1) * r, :])
    return outs


def _swa_prompt_kernel(bucket_ref, relb_ref, sink_ref, q_ref, z_ref, kv_ref, kvp_ref, ob_ref, bias_ref,
                       *, blocks):
    first = jnp.logical_and(pl.program_id(0) == 0, pl.program_id(1) == 0)

    @pl.when(first)
    def _():
        _build_bias(bucket_ref, relb_ref, bias_ref, WINDOW)

    col = lax.broadcasted_iota(jnp.int32, (1, 2 * N_KEYS), 1) & (N_KEYS - 1)
    pen0 = jnp.where(jnp.logical_and(col < WINDOW, pl.program_id(1) == 0), NEG, 0.0)
    kw = B_KV_HEADS * B_HD
    for j in range(blocks):
        r0 = j * WINDOW
        if j == 0:
            kk = jnp.concatenate([kvp_ref[:, 0:kw], kv_ref[0:WINDOW, 0:kw]], axis=0)
            vv = jnp.concatenate([kvp_ref[:, kw:2 * kw], kv_ref[0:WINDOW, kw:2 * kw]], axis=0)
        else:
            kk = kv_ref[r0 - WINDOW:r0 + WINDOW, 0:kw]
            vv = kv_ref[r0 - WINDOW:r0 + WINDOW, kw:2 * kw]
        q_pairs = [q_ref[r0:r0 + WINDOW, p * 128:(p + 1) * 128] * (B_HD ** -0.5) for p in range(N_PAIRS)]
        outs = _swa_core(q_pairs, kk, vv, bias_ref, sink_ref, WINDOW, pen0 if j == 0 else None)
        for p in range(N_PAIRS):
            z = z_ref[r0:r0 + WINDOW, p * 128:(p + 1) * 128].astype(F32)
            ob_ref[r0:r0 + WINDOW, p * 128:(p + 1) * 128] = (outs[p] * _silu(z)).astype(BF16)


def _swa_prompt(bucket, rel_bias, sink, pm, kv, batch, seq, blocks):
    tq = blocks * WINDOW
    nt = seq // tq
    kvw = kv.shape[1]
    smem = pl.BlockSpec(memory_space=pltpu.SMEM)
    return pl.pallas_call(
        functools.partial(_swa_prompt_kernel, blocks=blocks),
        grid=(batch, nt),
        in_specs=[
            _resident((WINDOW, N_KEYS)),
            smem, smem,
            pl.BlockSpec((tq, D_MODEL), lambda b, t: (b * nt + t, G_BQ)),
            pl.BlockSpec((tq, D_MODEL), lambda b, t: (b * nt + t, G_BZ)),
            pl.BlockSpec((tq, kvw), lambda b, t: (b * nt + t, 0)),
            pl.BlockSpec((WINDOW, kvw), lambda b, t: (jnp.maximum((b * nt + t) * blocks - 1, 0), 0)),
        ],
        out_specs=pl.BlockSpec((tq, D_MODEL), lambda b, t: (b * nt + t, 0)),
        out_shape=jax.ShapeDtypeStruct((batch * seq, D_MODEL), BF16),
        scratch_shapes=[pltpu.VMEM((B_KV_HEADS, PAIRS_PER_KV * WINDOW, 2 * N_KEYS), F32)],
        compiler_params=pltpu.CompilerParams(
            dimension_semantics=("arbitrary", "arbitrary"), vmem_limit_bytes=VMEM_LIMIT),
        name="swa_prompt",
    )(bucket, rel_bias, sink, pm, pm, kv, kv)


def _swa_sample_kernel(bucket_ref, relb_ref, sink_ref, q_ref, z_ref, kvn_ref, ck_ref, cv_ref,
                       ob_ref, nk_ref, nv_ref, bias_ref, *, bb, t_len):
    rows = bb * t_len

    @pl.when(pl.program_id(0) == 0)
    def _():
        _build_bias(bucket_ref, relb_ref, bias_ref, rows)

    kw = B_KV_HEADS * B_HD
    row = lax.broadcasted_iota(jnp.int32, (rows, 128), 0)
    q_pairs = [q_ref[:, p * 128:(p + 1) * 128] * (B_HD ** -0.5) for p in range(N_PAIRS)]
    kn = kvn_ref[:, 0:kw]
    vn = kvn_ref[:, kw:2 * kw]
    pad = jnp.zeros((N_KEYS - WINDOW - rows, kw), F32)
    res = None
    for j in range(bb):
        sel = row < t_len
        kj = jnp.where(sel, kn if j == 0 else pltpu.roll(kn, rows - j * t_len, axis=0), 0.0)
        vj = jnp.where(sel, vn if j == 0 else pltpu.roll(vn, rows - j * t_len, axis=0), 0.0)
        kk = jnp.concatenate([ck_ref[j], kj, pad], axis=0)
        vv = jnp.concatenate([cv_ref[j], vj, pad], axis=0)
        outs = _swa_core(q_pairs, kk, vv, bias_ref, sink_ref, rows)
        mine = (row >> (t_len.bit_length() - 1)) == j
        res = [jnp.where(mine, o, 0.0 if res is None else res[p]) for p, o in enumerate(outs)]
        nk_ref[j] = pltpu.roll(kk, N_KEYS - t_len, axis=0)[0:WINDOW, :]
        nv_ref[j] = pltpu.roll(vv, N_KEYS - t_len, axis=0)[0:WINDOW, :]
    for p in range(N_PAIRS):
        z = z_ref[:, p * 128:(p + 1) * 128].astype(F32)
        ob_ref[:, p * 128:(p + 1) * 128] = (res[p] * _silu(z)).astype(BF16)


def _swa_sample(bucket, rel_bias, sink, pm, kvn, cache_k, cache_v, batch, t_len, bb):
    rows = bb * t_len
    kvw = kvn.shape[1]
    kw = kvw // 2
    smem = pl.BlockSpec(memory_space=pltpu.SMEM)
    cache_spec = pl.BlockSpec((bb, WINDOW, kw), lambda i: (i, 0, 0))
    return pl.pallas_call(
        functools.partial(_swa_sample_kernel, bb=bb, t_len=t_len),
        grid=(batch // bb,),
        in_specs=[
            _resident((rows, N_KEYS)),
            smem, smem,
            pl.BlockSpec((rows, D_MODEL), lambda i: (i, G_BQ)),
            pl.BlockSpec((rows, D_MODEL), lambda i: (i, G_BZ)),
            pl.BlockSpec((rows, kvw), lambda i: (i, 0)),
            cache_spec, cache_spec,
        ],
        out_specs=[pl.BlockSpec((rows, D_MODEL), lambda i: (i, 0)), cache_spec, cache_spec],
        out_shape=[
            jax.ShapeDtypeStruct((batch * t_len, D_MODEL), BF16),
            jax.ShapeDtypeStruct((batch, WINDOW, kw), F32),
            jax.ShapeDtypeStruct((batch, WINDOW, kw), F32),
        ],
        scratch_shapes=[pltpu.VMEM((B_KV_HEADS, PAIRS_PER_KV * rows, 2 * N_KEYS), F32)],
        compiler_params=pltpu.CompilerParams(
            dimension_semantics=("arbitrary",), vmem_limit_bytes=VMEM_LIMIT),
        name="swa_sample",
    )(bucket, rel_bias, sink, pm, pm, kvn, cache_k, cache_v)


def _outproj_kernel(oa_ref, ob_ref, ga_ref, gb_ref, x_ref, p_ref, wpa_ref, wpb_ref, wo_ref, gpost_ref,
                    wple_ref, wg_ref, y_ref):
    a = jnp.dot(oa_ref[...], wpa_ref[...], preferred_element_type=F32)
    b = jnp.dot(ob_ref[...], wpb_ref[...], preferred_element_type=F32)
    m = _sigmoid(ga_ref[...].astype(F32)) * a + _sigmoid(gb_ref[...].astype(F32)) * b
    y = jnp.dot(m.astype(BF16), wo_ref[...], preferred_element_type=F32)
    y = y * lax.rsqrt(jnp.mean(y * y, axis=-1, keepdims=True) + EPS) * gpost_ref[...]
    x1 = x_ref[...] + y
    gate = _sigmoid(jnp.dot(x1.astype(BF16), wg_ref[...], preferred_element_type=F32))
    e = jnp.dot(p_ref[...].astype(BF16), wple_ref[...], preferred_element_type=F32) * gate
    y_ref[...] = x1 + e


def _outproj(oa, ob, pm, x, p, wpa, wpb, wo, gpost, wple, wg, tm):
    n = x.shape[0]
    ple = p.shape[1]
    tok = lambda w, c=0: pl.BlockSpec((tm, w), lambda i, c=c: (i, c))
    return pl.pallas_call(
        _outproj_kernel,
        grid=(n // tm,),
        in_specs=[
            tok(D_MODEL), tok(D_MODEL), tok(D_MODEL, G_GA), tok(D_MODEL, G_GB), tok(D_MODEL), tok(ple),
            _resident((D_MODEL, D_MODEL)), _resident((D_MODEL, D_MODEL)), _resident((D_MODEL, D_MODEL)),
            _resident((1, D_MODEL)), _resident((ple, D_MODEL)), _resident((D_MODEL, D_MODEL)),
        ],
        out_specs=tok(D_MODEL),
        out_shape=jax.ShapeDtypeStruct((n, D_MODEL), F32),
        compiler_params=pltpu.CompilerParams(
            dimension_semantics=("arbitrary",), vmem_limit_bytes=VMEM_LIMIT),
        name="outproj",
    )(oa, ob, pm, pm, x, p, wpa, wpb, wo, gpost, wple, wg)


def _rel_bucket(rel):
    n = jnp.maximum(rel, 0)
    max_exact = REL_BUCKETS // 2
    nf = jnp.maximum(n, 1).astype(F32)
    large = max_exact + (jnp.log(nf / max_exact) / jnp.log(jnp.float32(REL_MAX_DIST / max_exact))
                         * (REL_BUCKETS - max_exact)).astype(jnp.int32)
    large = jnp.minimum(large, REL_BUCKETS - 1)
    return jnp.where(n < max_exact, n, large)


def _bucket_table(q_pos, k_pos, k_valid):
    rel = q_pos[:, None] - k_pos[None, :]
    ok = (rel >= 0) & (rel < WINDOW) & k_valid[None, :]
    return jnp.where(ok, _rel_bucket(rel), -1).astype(jnp.int32)


IN_OFFS = (0, 1024, 2048, 3072, 4096, 5120, 6144, 6272, 6400, 7424, 8448, 9472)
TM_PROJ = 512
SWA_BLOCKS = 4
SAMPLE_BB = 4


def _layer(xp, xs, s_hgrn, win_k, win_v, pp, ps, norm_pre, w_in, hgrn_lb, hgrn_norm, attn_sink,
           rel_bias, w_pa, w_pb, w_o, norm_post, w_ple, w_ple_gate):
    batch, seq, _ = xp.shape
    dbatch, t_len, _ = xs.shape
    kw = B_KV_HEADS * B_HD

    seg = lambda i: w_in[:, IN_OFFS[i]:IN_OFFS[i + 1]]
    wm = jnp.concatenate([seg(0), seg(2), seg(3), seg(4), seg(5), seg(8), seg(9), seg(10)], axis=1).astype(BF16)
    wf = seg(1).astype(BF16)
    wkv = jnp.concatenate([seg(6), seg(7)], axis=1).astype(BF16)
    g_pre = norm_pre.reshape(1, D_MODEL)
    g_post = norm_post.reshape(1, D_MODEL)
    g_hgrn = hgrn_norm.reshape(1, D_MODEL)
    wpa, wpb, wo = w_pa.astype(BF16), w_pb.astype(BF16), w_o.astype(BF16)
    wple, wg = w_ple.astype(BF16), w_ple_gate.astype(BF16)

    xp2 = xp.reshape(batch * seq, D_MODEL)
    xs2 = xs.reshape(dbatch * t_len, D_MODEL)
    pm_p, f_p, kv_p = _inproj(xp2, g_pre, wm, wf, wkv, TM_PROJ)
    pm_s, f_s, kv_s = _inproj(xs2, g_pre, wm, wf, wkv, min(TM_PROJ, dbatch * t_len))

    oa_p, st_p = _hgrn_prompt(pm_p, f_p, hgrn_lb, g_hgrn, batch, seq)
    oa_s, st_s = _hgrn_sample(pm_s, f_s, hgrn_lb, g_hgrn, s_hgrn, dbatch, t_len, SAMPLE_BB)

    k_all = jnp.arange(N_KEYS)
    bucket_p = _bucket_table(jnp.arange(WINDOW) + WINDOW, k_all, jnp.ones((N_KEYS,), bool))
    rows_s = SAMPLE_BB * t_len
    bucket_s = _bucket_table(WINDOW + jnp.arange(rows_s) % t_len, k_all, k_all < WINDOW + t_len)
    ob_p = _swa_prompt(bucket_p, rel_bias, attn_sink, pm_p, kv_p, batch, seq, SWA_BLOCKS)
    ob_s, nk_s, nv_s = _swa_sample(bucket_s, rel_bias, attn_sink, pm_s, kv_s,
                                   win_k.reshape(dbatch, WINDOW, kw), win_v.reshape(dbatch, WINDOW, kw),
                                   dbatch, t_len, SAMPLE_BB)

    y_p = _outproj(oa_p, ob_p, pm_p, xp2, pp.reshape(batch * seq, -1), wpa, wpb, wo, g_post, wple, wg, TM_PROJ)
    y_s = _outproj(oa_s, ob_s, pm_s, xs2, ps.reshape(dbatch * t_len, -1), wpa, wpb, wo, g_post, wple, wg,
                   min(TM_PROJ, dbatch * t_len))

    kv_p3 = kv_p.reshape(batch, seq, 2 * kw)[:, seq - WINDOW:, :]
    k_win_p = kv_p3[:, :, 0:kw].reshape(batch, WINDOW, B_KV_HEADS, B_HD)
    v_win_p = kv_p3[:, :, kw:].reshape(batch, WINDOW, B_KV_HEADS, B_HD)
    return (y_p.reshape(batch, seq, D_MODEL), y_s.reshape(dbatch, t_len, D_MODEL), st_p, st_s,
            k_win_p, v_win_p,
            nk_s.reshape(dbatch, WINDOW, B_KV_HEADS, B_HD), nv_s.reshape(dbatch, WINDOW, B_KV_HEADS, B_HD))


def kernel(x_prompt, x_sample, state_hgrn, cache_swa_k, cache_swa_v, p_prompt, p_sample, norm_pre, w_in,
           hgrn_lb, hgrn_norm, attn_sink, rel_bias, w_pa, w_pb, w_o, norm_post, w_ple, w_ple_gate):
    depth = w_in.shape[0]
    assert depth == 1, "the forget-gate lower bound is implemented for a single layer"
    xp, xs = x_prompt, x_sample
    outs = []
    for l in range(depth):
        res = _layer(xp, xs, state_hgrn[l], cache_swa_k[l], cache_swa_v[l], p_prompt[l], p_sample[l],
                     norm_pre[l], w_in[l], hgrn_lb, hgrn_norm[l], attn_sink[l], rel_bias,
                     w_pa[l], w_pb[l], w_o[l], norm_post[l], w_ple[l], w_ple_gate[l])
        xp, xs = res[0], res[1]
        outs.append(res[2:])
    stack = lambda i: jnp.stack([o[i] for o in outs])
    return (xp, xs, stack(0), stack(1), stack(2), stack(3), stack(4), stack(5))
```

```python
import functools

import jax
import jax.numpy as jnp
from jax import lax
from jax.experimental import pallas as pl
from jax.experimental.pallas import tpu as pltpu

F32 = jnp.float32
BF16 = jnp.bfloat16

D_MODEL = 1024
A_HEADS = 8
A_DK = 128
A_DV = 128
A_CHUNK = 64
A_SUB = 16
B_HEADS = 16
B_KV_HEADS = 2
B_HD = 64
WINDOW = 128
REL_BUCKETS = 32
REL_MAX_DIST = 128
EPS = 1e-6
NEG = -1e30

G_AQ, G_AI, G_AOG, G_AZ, G_BQ, G_BZ, G_GA, G_GB = range(8)
N_GROUPS = 8

VMEM_LIMIT = 56 * 1024 * 1024


def _sigmoid(x):
    return 1.0 / (1.0 + jnp.exp(-x))


def _silu(x):
    return x * _sigmoid(x)


def _resident(shape):
    nd = len(shape)
    return pl.BlockSpec(shape, lambda *_: (0,) * nd, pipeline_mode=pl.Buffered(1))


def _inproj_kernel(x_ref, g_ref, wm_ref, wf_ref, wkv_ref, pm_ref, f_ref, kv_ref):
    x = x_ref[...]
    ms = jnp.mean(x * x, axis=-1, keepdims=True)
    u = (x * lax.rsqrt(ms + EPS) * g_ref[...]).astype(BF16)
    for c in range(N_GROUPS):
        sl = slice(c * D_MODEL, (c + 1) * D_MODEL)
        pm_ref[:, sl] = jnp.dot(u, wm_ref[:, sl], preferred_element_type=F32).astype(BF16)
    f_ref[...] = jnp.dot(u, wf_ref[...], preferred_element_type=F32)
    kv_ref[...] = jnp.dot(u, wkv_ref[...], preferred_element_type=F32)


def _inproj(x, g, wm, wf, wkv, tm):
    n = x.shape[0]
    nm = wm.shape[1]
    nkv = wkv.shape[1]
    return pl.pallas_call(
        _inproj_kernel,
        grid=(n // tm,),
        in_specs=[
            pl.BlockSpec((tm, D_MODEL), lambda i: (i, 0)),
            _resident((1, D_MODEL)),
            _resident((D_MODEL, nm)),
            _resident((D_MODEL, D_MODEL)),
            _resident((D_MODEL, nkv)),
        ],
        out_specs=[
            pl.BlockSpec((tm, nm), lambda i: (i, 0)),
            pl.BlockSpec((tm, D_MODEL), lambda i: (i, 0)),
            pl.BlockSpec((tm, nkv), lambda i: (i, 0)),
        ],
        out_shape=[
            jax.ShapeDtypeStruct((n, nm), BF16),
            jax.ShapeDtypeStruct((n, D_MODEL), F32),
            jax.ShapeDtypeStruct((n, nkv), F32),
        ],
        compiler_params=pltpu.CompilerParams(
            dimension_semantics=("arbitrary",), vmem_limit_bytes=VMEM_LIMIT),
        name="inproj",
    )(x, g, wm, wf, wkv)


def _lower_bound(lb_ref):
    l = lb_ref[...]
    m = jnp.max(l, axis=0, keepdims=True)
    e = jnp.exp(l - m)
    return e[0:1, :] / jnp.sum(e, axis=0, keepdims=True)


def _group_cumsum(x, row, period, shifts):
    pos = row & (period - 1)
    for sh in shifts:
        x = x + jnp.where(pos >= sh, pltpu.roll(x, sh, axis=0), 0.0)
    return x


def _hgrn_finish(o, g, og_pre, z_pre):
    o = o * lax.rsqrt(jnp.mean(o * o, axis=-1, keepdims=True) + EPS)
    return o * g * _sigmoid(og_pre) * _silu(z_pre)


def _dot_nt(a, b):
    return lax.dot_general(a, b, (((1,), (1,)), ((), ())), preferred_element_type=F32)


def _dot_tn(a, b):
    return lax.dot_general(a, b, (((0,), (0,)), ((), ())), preferred_element_type=F32)


def _hgrn_prompt_kernel(q_ref, v_ref, og_ref, z_ref, f_ref, lb_ref, g_ref, oa_ref, s_ref, *, sc):
    n_sub = A_CHUNK // A_SUB
    slab = sc * A_CHUNK
    n_iter = q_ref.shape[0] // slab
    lb = _lower_bound(lb_ref)
    g = g_ref[...]
    row = lax.broadcasted_iota(jnp.int32, (slab, A_DK), 0)
    rs = lax.broadcasted_iota(jnp.int32, (slab, slab), 0)
    cs = lax.broadcasted_iota(jnp.int32, (slab, slab), 1)
    shift = A_CHUNK.bit_length() - 1
    same_chunk_causal = jnp.logical_and((rs >> shift) == (cs >> shift), cs <= rs)
    zero_row = jnp.zeros((1, A_DK), F32)
    zero_blk = jnp.zeros((A_SUB, A_DK), BF16)
    zero_chunk = jnp.zeros((A_CHUNK, A_DK), BF16)
    rep = lambda r, n: jnp.broadcast_to(r, (n, A_DK))

    def body(it, st):
        r0 = pl.multiple_of(it * slab, slab)
        rows = pl.ds(r0, slab)
        fg = lb + (1.0 - lb) * _sigmoid(f_ref[rows, :])
        logf = jnp.log(fg)
        k = 1.0 - fg
        b = _group_cumsum(logf, row, A_CHUNK, (1, 2, 4, 8, 16, 32))
        q = _silu(q_ref[rows, :].astype(F32))
        v = v_ref[rows, :]

        ends = [b[A_SUB * j + A_SUB - 1:A_SUB * (j + 1), :] for j in range(sc * n_sub)]
        endrow = jnp.concatenate([rep(e, A_SUB) for e in ends], axis=0)
        prevrow = jnp.concatenate(
            [rep(zero_row if j % n_sub == 0 else ends[j - 1], A_SUB) for j in range(sc * n_sub)], axis=0)
        tot = [ends[n_sub * c + n_sub - 1] for c in range(sc)]
        lastrow = jnp.concatenate([rep(t, A_CHUNK) for t in tot], axis=0)
        kend = k * jnp.exp(endrow - b)
        qd = q * jnp.exp(b - prevrow)
        qs = qd * jnp.exp(prevrow)
        kd = kend * jnp.exp(lastrow - endrow)
        kdiag16 = (k * jnp.exp(prevrow - b)).astype(BF16)
        qd16 = qd.astype(BF16)
        kend16 = kend.astype(BF16)
        qs16 = qs.astype(BF16)
        kd16 = kd.astype(BF16)

        blk = lambda a, j: a[A_SUB * j:A_SUB * (j + 1), :]
        chk = lambda a, c: a[A_CHUNK * c:A_CHUNK * (c + 1), :]

        q_groups, k_groups = [], []
        for i in range(n_sub):
            q_groups.append(jnp.concatenate(
                [blk(qd16, j) if j % n_sub == i else zero_blk for j in range(sc * n_sub)], axis=0))
            pieces = []
            for j in range(sc * n_sub):
                c, jj = divmod(j, n_sub)
                if jj == i:
                    pieces.append(blk(kdiag16, j))
                elif jj == i - 1:
                    pieces.append(blk(kend16, j))
                elif jj < i:
                    pieces.append((blk(kend, j) * jnp.exp(ends[c * n_sub + i - 1] - ends[j])).astype(BF16))
                else:
                    pieces.append(zero_blk)
            k_groups.append(jnp.concatenate(pieces, axis=0))
        att = _dot_nt(jnp.concatenate(q_groups, axis=1), jnp.concatenate(k_groups, axis=1))
        att = jnp.where(same_chunk_causal, att, 0.0)

        if sc > 1:
            q_groups, k_groups = [], []
            for c in range(1, sc):
                q_groups.append(jnp.concatenate(
                    [chk(qs16, c2) if c2 == c else zero_chunk for c2 in range(sc)], axis=0))
                pieces = []
                for c2 in range(sc):
                    if c2 == c - 1:
                        pieces.append(chk(kd16, c2))
                    elif c2 < c:
                        carry = sum(tot[c2 + 1:c])
                        pieces.append((chk(kd, c2) * jnp.exp(carry)).astype(BF16))
                    else:
                        pieces.append(zero_chunk)
                k_groups.append(jnp.concatenate(pieces, axis=0))
            att = att + _dot_nt(jnp.concatenate(q_groups, axis=1), jnp.concatenate(k_groups, axis=1))
        o = jnp.dot(att.astype(BF16), v, preferred_element_type=F32)

        q0 = jnp.concatenate(
            [chk(qs16, 0)] + [(chk(qs, c) * jnp.exp(sum(tot[:c]))).astype(BF16) for c in range(1, sc)], axis=0)
        o = o + _dot_nt(q0, st.astype(BF16))

        k1 = jnp.concatenate(
            [(chk(kd, c) * jnp.exp(sum(tot[c + 1:]))).astype(BF16) for c in range(sc - 1)]
            + [chk(kd16, sc - 1)], axis=0)
        st = st * jnp.exp(sum(tot)) + _dot_tn(v, k1)

        oa = _hgrn_finish(o, g, og_ref[rows, :].astype(F32), z_ref[rows, :].astype(F32))
        oa_ref[rows, :] = oa.astype(BF16)
        return st

    st = lax.fori_loop(0, n_iter, body, jnp.zeros((A_DV, A_DK), F32), unroll=2)
    s_ref[...] = st.T


HGRN_CHUNKS_PER_ITER = 4


def _hgrn_prompt(pm, f, hgrn_lb, hgrn_norm, batch, seq):
    col = lambda grp: pl.BlockSpec((seq, A_DK), lambda b, h, grp=grp: (b, grp * A_HEADS + h))
    n_lb = hgrn_lb.shape[0]
    return pl.pallas_call(
        functools.partial(_hgrn_prompt_kernel, sc=HGRN_CHUNKS_PER_ITER),
        grid=(batch, A_HEADS),
        in_specs=[
            col(G_AQ), col(G_AI), col(G_AOG), col(G_AZ),
            pl.BlockSpec((seq, A_DK), lambda b, h: (b, h)),
            pl.BlockSpec((n_lb, A_DK), lambda b, h: (0, h)),
            pl.BlockSpec((1, A_DV), lambda b, h: (0, h)),
        ],
        out_specs=[
            pl.BlockSpec((seq, A_DV), lambda b, h: (b, h)),
            pl.BlockSpec((None, None, A_DK, A_DV), lambda b, h: (b, h, 0, 0)),
        ],
        out_shape=[
            jax.ShapeDtypeStruct((batch * seq, A_HEADS * A_DV), BF16),
            jax.ShapeDtypeStruct((batch, A_HEADS, A_DK, A_DV), F32),
        ],
        compiler_params=pltpu.CompilerParams(
            dimension_semantics=("arbitrary", "arbitrary"), vmem_limit_bytes=VMEM_LIMIT),
        name="hgrn_prompt",
    )(pm, pm, pm, pm, f, hgrn_lb, hgrn_norm)


def _hgrn_sample_kernel(q_ref, v_ref, og_ref, z_ref, f_ref, lb_ref, g_ref, s0_ref, oa_ref, s_ref,
                        *, bb, t_len):
    rows = bb * t_len
    lb_all = _lower_bound(lb_ref)
    row = lax.broadcasted_iota(jnp.int32, (rows, A_DK), 0)
    pos = row & (t_len - 1)
    grp = row >> (t_len.bit_length() - 1)
    shifts = tuple(1 << i for i in range((t_len - 1).bit_length()))
    for h in range(A_HEADS):
        ls = slice(h * A_DK, (h + 1) * A_DK)
        lb = lb_all[:, ls]
        fg = lb + (1.0 - lb) * _sigmoid(f_ref[:, ls])
        logf = jnp.log(fg)
        k = 1.0 - fg
        b = _group_cumsum(logf, row, t_len, shifts)
        q = _silu(q_ref[:, ls].astype(F32))
        v = v_ref[:, ls].astype(F32)

        o = jnp.zeros((rows, A_DV), F32)
        for d in range(t_len):
            if d == 0:
                a = jnp.sum(q * k, axis=-1, keepdims=True)
                o = o + a * v
            else:
                w = jnp.exp(jnp.where(pos >= d, b - pltpu.roll(b, d, axis=0), 0.0))
                a = jnp.sum(q * pltpu.roll(k, d, axis=0) * w, axis=-1, keepdims=True)
                o = o + jnp.where(pos >= d, a * pltpu.roll(v, d, axis=0), 0.0)

        b_last = b
        for d in range(1, t_len):
            b_last = jnp.where(pos == t_len - 1 - d, pltpu.roll(b, rows - d, axis=0), b_last)
        e_last = jnp.exp(b_last)
        qs = q * jnp.exp(b)
        kd = k * jnp.exp(b_last - b)
        v16 = v.astype(BF16)
        for j in range(bb):
            s0 = s0_ref[j, h]
            mine = grp == j
            o = o + jnp.dot(jnp.where(mine, qs, 0.0).astype(BF16), s0.astype(BF16),
                            preferred_element_type=F32)
            u = _dot_tn(jnp.where(mine, kd, 0.0).astype(BF16), v16)
            e_row = e_last[j * t_len:j * t_len + 1, :]
            e_col = jnp.broadcast_to(e_row, (A_DK, A_DK)).T
            s_ref[j, h] = e_col * s0 + u

        oa = _hgrn_finish(o, g_ref[:, ls], og_ref[:, ls].astype(F32), z_ref[:, ls].astype(F32))
        oa_ref[:, ls] = oa.astype(BF16)


def _hgrn_sample(pm, f, hgrn_lb, hgrn_norm, s0, batch, t_len, bb):
    rows = bb * t_len
    col = lambda grp: pl.BlockSpec((rows, D_MODEL), lambda i, grp=grp: (i, grp))
    n_lb = hgrn_lb.shape[0]
    st_spec = pl.BlockSpec((bb, A_HEADS, A_DK, A_DV), lambda i: (i, 0, 0, 0))
    return pl.pallas_call(
        functools.partial(_hgrn_sample_kernel, bb=bb, t_len=t_len),
        grid=(batch // bb,),
        in_specs=[
            col(G_AQ), col(G_AI), col(G_AOG), col(G_AZ),
            pl.BlockSpec((rows, D_MODEL), lambda i: (i, 0)),
            _resident((n_lb, D_MODEL)),
            _resident((1, D_MODEL)),
            st_spec,
        ],
        out_specs=[pl.BlockSpec((rows, D_MODEL), lambda i: (i, 0)), st_spec],
        out_shape=[
            jax.ShapeDtypeStruct((batch * t_len, D_MODEL), BF16),
            jax.ShapeDtypeStruct((batch, A_HEADS, A_DK, A_DV), F32),
        ],
        compiler_params=pltpu.CompilerParams(
            dimension_semantics=("arbitrary",), vmem_limit_bytes=VMEM_LIMIT),
        name="hgrn_sample",
    )(pm, pm, pm, pm, f, hgrn_lb, hgrn_norm, s0)


N_PAIRS = B_HEADS // 2
PAIRS_PER_KV = N_PAIRS // B_KV_HEADS
N_KEYS = 2 * WINDOW


def _build_bias(bucket_ref, relb_ref, bias_ref, r):
    bucket = bucket_ref[...]
    for h in range(B_HEADS):
        def body(kb, acc, h=h):
            return jnp.where(bucket == kb, relb_ref[kb, h], acc)
        tab = lax.fori_loop(0, REL_BUCKETS, body, jnp.full(bucket.shape, NEG, F32))
        pair, parity = divmod(h, 2)
        kv, pp = divmod(pair, PAIRS_PER_KV)
        bias_ref[kv, pp * r:(pp + 1) * r, parity * N_KEYS:(parity + 1) * N_KEYS] = tab


def _swa_core(q_pairs, kk, vv, bias_ref, sink_ref, r, pen=None):
    lo = lax.broadcasted_iota(jnp.int32, (N_KEYS, 2 * B_HD), 1) < B_HD
    kk_sw = pltpu.roll(kk, B_HD, axis=1)
    vv_sw = pltpu.roll(vv, B_HD, axis=1)
    outs = []
    for kv in range(B_KV_HEADS):
        k_lo, k_hi = (kk, kk_sw) if kv == 0 else (kk_sw, kk)
        v_lo, v_hi = (vv, vv_sw) if kv == 0 else (vv_sw, vv)
        keys = jnp.concatenate([jnp.where(lo, k_lo, 0.0), jnp.where(lo, 0.0, k_hi)], axis=0).astype(BF16)
        vals = (jnp.where(lo, v_lo, 0.0).astype(BF16), jnp.where(lo, 0.0, v_hi).astype(BF16))
        q = jnp.concatenate(q_pairs[kv * PAIRS_PER_KV:(kv + 1) * PAIRS_PER_KV], axis=0)
        logits = _dot_nt(q, keys) + bias_ref[kv]
        if pen is not None:
            logits = logits + pen
        acc = None
        for parity in range(2):
            sink = jnp.concatenate(
                [jnp.full((r, 1), sink_ref[(kv * PAIRS_PER_KV + pp) * 2 + parity], F32)
                 for pp in range(PAIRS_PER_KV)], axis=0)
            l = logits[:, parity * N_KEYS:(parity + 1) * N_KEYS]
            m = jnp.maximum(jnp.max(l, axis=-1, keepdims=True), sink)
            p = jnp.exp(l - m)
            denom = jnp.sum(p, axis=-1, keepdims=True) + jnp.exp(sink - m)
            pv = jnp.dot(p.astype(BF16), vals[parity], preferred_element_type=F32) / denom
            acc = pv if acc is None else acc + pv
        for pp in range(PAIRS_PER_KV):
            outs.append(acc[pp * r:(pp + 1) * r, :])
    return outs


def _swa_prompt_kernel(bucket_ref, relb_ref, sink_ref, q_ref, z_ref, kv_ref, kvp_ref, ob_ref, bias_ref,
                       *, blocks):
    first = jnp.logical_and(pl.program_id(0) == 0, pl.program_id(1) == 0)

    @pl.when(first)
    def _():
        _build_bias(bucket_ref, relb_ref, bias_ref, WINDOW)

    col = lax.broadcasted_iota(jnp.int32, (1, 2 * N_KEYS), 1) & (N_KEYS - 1)
    pen0 = jnp.where(jnp.logical_and(col < WINDOW, pl.program_id(1) == 0), NEG, 0.0)
    kw = B_KV_HEADS * B_HD
    for j in range(blocks):
        r0 = j * WINDOW
        if j == 0:
            kk = jnp.concatenate([kvp_ref[:, 0:kw], kv_ref[0:WINDOW, 0:kw]], axis=0)
            vv = jnp.concatenate([kvp_ref[:, kw:2 * kw], kv_ref[0:WINDOW, kw:2 * kw]], axis=0)
        else:
            kk = kv_ref[r0 - WINDOW:r0 + WINDOW, 0:kw]
            vv = kv_ref[r0 - WINDOW:r0 + WINDOW, kw:2 * kw]
        q_pairs = [q_ref[r0:r0 + WINDOW, p * 128:(p + 1) * 128] * (B_HD ** -0.5) for p in range(N_PAIRS)]
        outs = _swa_core(q_pairs, kk, vv, bias_ref, sink_ref, WINDOW, pen0 if j == 0 else None)
        for p in range(N_PAIRS):
            z = z_ref[r0:r0 + WINDOW, p * 128:(p + 1) * 128].astype(F32)
            ob_ref[r0:r0 + WINDOW, p * 128:(p + 1) * 128] = (outs[p] * _silu(z)).astype(BF16)


def _swa_prompt(bucket, rel_bias, sink, pm, kv, batch, seq, blocks):
    tq = blocks * WINDOW
    nt = seq // tq
    kvw = kv.shape[1]
    smem = pl.BlockSpec(memory_space=pltpu.SMEM)
    return pl.pallas_call(
        functools.partial(_swa_prompt_kernel, blocks=blocks),
        grid=(batch, nt),
        in_specs=[
            _resident((WINDOW, N_KEYS)),
            smem, smem,
            pl.BlockSpec((tq, D_MODEL), lambda b, t: (b * nt + t, G_BQ)),
            pl.BlockSpec((tq, D_MODEL), lambda b, t: (b * nt + t, G_BZ)),
            pl.BlockSpec((tq, kvw), lambda b, t: (b * nt + t, 0)),
            pl.BlockSpec((WINDOW, kvw), lambda b, t: (jnp.maximum((b * nt + t) * blocks - 1, 0), 0)),
        ],
        out_specs=pl.BlockSpec((tq, D_MODEL), lambda b, t: (b * nt + t, 0)),
        out_shape=jax.ShapeDtypeStruct((batch * seq, D_MODEL), BF16),
        scratch_shapes=[pltpu.VMEM((B_KV_HEADS, PAIRS_PER_KV * WINDOW, 2 * N_KEYS), F32)],
        compiler_params=pltpu.CompilerParams(
            dimension_semantics=("arbitrary", "arbitrary"), vmem_limit_bytes=VMEM_LIMIT),
        name="swa_prompt",
    )(bucket, rel_bias, sink, pm, pm, kv, kv)


def _swa_sample_kernel(bucket_ref, relb_ref, sink_ref, q_ref, z_ref, kvn_ref, ck_ref, cv_ref,
                       ob_ref, nk_ref, nv_ref, bias_ref, *, bb, t_len):
    rows = bb * t_len

    @pl.when(pl.program_id(0) == 0)
    def _():
        _build_bias(bucket_ref, relb_ref, bias_ref, rows)

    kw = B_KV_HEADS * B_HD
    row = lax.broadcasted_iota(jnp.int32, (rows, 128), 0)
    q_pairs = [q_ref[:, p * 128:(p + 1) * 128] * (B_HD ** -0.5) for p in range(N_PAIRS)]
    kn = kvn_ref[:, 0:kw]
    vn = kvn_ref[:, kw:2 * kw]
    pad = jnp.zeros((N_KEYS - WINDOW - rows, kw), F32)
    res = None
    for j in range(bb):
        sel = row < t_len
        kj = jnp.where(sel, kn if j == 0 else pltpu.roll(kn, rows - j * t_len, axis=0), 0.0)
        vj = jnp.where(sel, vn if j == 0 else pltpu.roll(vn, rows - j * t_len, axis=0), 0.0)
        kk = jnp.concatenate([ck_ref[j], kj, pad], axis=0)
        vv = jnp.concatenate([cv_ref[j], vj, pad], axis=0)
        outs = _swa_core(q_pairs, kk, vv, bias_ref, sink_ref, rows)
        mine = (row >> (t_len.bit_length() - 1)) == j
        res = [jnp.where(mine, o, 0.0 if res is None else res[p]) for p, o in enumerate(outs)]
        nk_ref[j] = pltpu.roll(kk, N_KEYS - t_len, axis=0)[0:WINDOW, :]
        nv_ref[j] = pltpu.roll(vv, N_KEYS - t_len, axis=0)[0:WINDOW, :]
    for p in range(N_PAIRS):
        z = z_ref[:, p * 128:(p + 1) * 128].astype(F32)
        ob_ref[:, p * 128:(p + 1) * 128] = (res[p] * _silu(z)).astype(BF16)


def _swa_sample(bucket, rel_bias, sink, pm, kvn, cache_k, cache_v, batch, t_len, bb):
    rows = bb * t_len
    kvw = kvn.shape[1]
    kw = kvw // 2
    smem = pl.BlockSpec(memory_space=pltpu.SMEM)
    cache_spec = pl.BlockSpec((bb, WINDOW, kw), lambda i: (i, 0, 0))
    return pl.pallas_call(
        functools.partial(_swa_sample_kernel, bb=bb, t_len=t_len),
        grid=(batch // bb,),
        in_specs=[
            _resident((rows, N_KEYS)),
            smem, smem,
            pl.BlockSpec((rows, D_MODEL), lambda i: (i, G_BQ)),
            pl.BlockSpec((rows, D_MODEL), lambda i: (i, G_BZ)),
            pl.BlockSpec((rows, kvw), lambda i: (i, 0)),
            cache_spec, cache_spec,
        ],
        out_specs=[pl.BlockSpec((rows, D_MODEL), lambda i: (i, 0)), cache_spec, cache_spec],
        out_shape=[
            jax.ShapeDtypeStruct((batch * t_len, D_MODEL), BF16),
            jax.ShapeDtypeStruct((batch, WINDOW, kw), F32),
            jax.ShapeDtypeStruct((batch, WINDOW, kw), F32),
        ],
        scratch_shapes=[pltpu.VMEM((B_KV_HEADS, PAIRS_PER_KV * rows, 2 * N_KEYS), F32)],
        compiler_params=pltpu.CompilerParams(
            dimension_semantics=("arbitrary",), vmem_limit_bytes=VMEM_LIMIT),
        name="swa_sample",
    )(bucket, rel_bias, sink, pm, pm, kvn, cache_k, cache_v)


def _outproj_kernel(oa_ref, ob_ref, ga_ref, gb_ref, x_ref, p_ref, wpa_ref, wpb_ref, wo_ref, gpost_ref,
                    wple_ref, wg_ref, y_ref):
    a = jnp.dot(oa_ref[...], wpa_ref[...], preferred_element_type=F32)
    b = jnp.dot(ob_ref[...], wpb_ref[...], preferred_element_type=F32)
    m = _sigmoid(ga_ref[...].astype(F32)) * a + _sigmoid(gb_ref[...].astype(F32)) * b
    y = jnp.dot(m.astype(BF16), wo_ref[...], preferred_element_type=F32)
    y = y * lax.rsqrt(jnp.mean(y * y, axis=-1, keepdims=True) + EPS) * gpost_ref[...]
    x1 = x_ref[...] + y
    gate = _sigmoid(jnp.dot(x1.astype(BF16), wg_ref[...], preferred_element_type=F32))
    e = jnp.dot(p_ref[...].astype(BF16), wple_ref[...], preferred_element_type=F32) * gate
    y_ref[...] = x1 + e


def _outproj(oa, ob, pm, x, p, wpa, wpb, wo, gpost, wple, wg, tm):
    n = x.shape[0]
    ple = p.shape[1]
    tok = lambda w, c=0: pl.BlockSpec((tm, w), lambda i, c=c: (i, c))
    return pl.pallas_call(
        _outproj_kernel,
        grid=(n // tm,),
        in_specs=[
            tok(D_MODEL), tok(D_MODEL), tok(D_MODEL, G_GA), tok(D_MODEL, G_GB), tok(D_MODEL), tok(ple),
            _resident((D_MODEL, D_MODEL)), _resident((D_MODEL, D_MODEL)), _resident((D_MODEL, D_MODEL)),
            _resident((1, D_MODEL)), _resident((ple, D_MODEL)), _resident((D_MODEL, D_MODEL)),
        ],
        out_specs=tok(D_MODEL),
        out_shape=jax.ShapeDtypeStruct((n, D_MODEL), F32),
        compiler_params=pltpu.CompilerParams(
            dimension_semantics=("arbitrary",), vmem_limit_bytes=VMEM_LIMIT),
        name="outproj",
    )(oa, ob, pm, pm, x, p, wpa, wpb, wo, gpost, wple, wg)


def _rel_bucket(rel):
    n = jnp.maximum(rel, 0)
    max_exact = REL_BUCKETS // 2
    nf = jnp.maximum(n, 1).astype(F32)
    large = max_exact + (jnp.log(nf / max_exact) / jnp.log(jnp.float32(REL_MAX_DIST / max_exact))
                         * (REL_BUCKETS - max_exact)).astype(jnp.int32)
    large = jnp.minimum(large, REL_BUCKETS - 1)
    return jnp.where(n < max_exact, n, large)


def _bucket_table(q_pos, k_pos, k_valid):
    rel = q_pos[:, None] - k_pos[None, :]
    ok = (rel >= 0) & (rel < WINDOW) & k_valid[None, :]
    return jnp.where(ok, _rel_bucket(rel), -1).astype(jnp.int32)


IN_OFFS = (0, 1024, 2048, 3072, 4096, 5120, 6144, 6272, 6400, 7424, 8448, 9472)
TM_PROJ = 512
SWA_BLOCKS = 4
SAMPLE_BB = 4


def _layer(xp, xs, s_hgrn, win_k, win_v, pp, ps, norm_pre, w_in, hgrn_lb, hgrn_norm, attn_sink,
           rel_bias, w_pa, w_pb, w_o, norm_post, w_ple, w_ple_gate):
    batch, seq, _ = xp.shape
    dbatch, t_len, _ = xs.shape
    kw = B_KV_HEADS * B_HD

    seg = lambda i: w_in[:, IN_OFFS[i]:IN_OFFS[i + 1]]
    wm = jnp.concatenate([seg(0), seg(2), seg(3), seg(4), seg(5), seg(8), seg(9), seg(10)], axis=1).astype(BF16)
    wf = seg(1).astype(BF16)
    wkv = jnp.concatenate([seg(6), seg(7)], axis=1).astype(BF16)
    g_pre = norm_pre.reshape(1, D_MODEL)
    g_post = norm_post.reshape(1, D_MODEL)
    g_hgrn = hgrn_norm.reshape(1, D_MODEL)
    wpa, wpb, wo = w_pa.astype(BF16), w_pb.astype(BF16), w_o.astype(BF16)
    wple, wg = w_ple.astype(BF16), w_ple_gate.astype(BF16)

    xp2 = xp.reshape(batch * seq, D_MODEL)
    xs2 = xs.reshape(dbatch * t_len, D_MODEL)
    pm_p, f_p, kv_p = _inproj(xp2, g_pre, wm, wf, wkv, TM_PROJ)
    pm_s, f_s, kv_s = _inproj(xs2, g_pre, wm, wf, wkv, min(TM_PROJ, dbatch * t_len))

    oa_p, st_p = _hgrn_prompt(pm_p, f_p, hgrn_lb, g_hgrn, batch, seq)
    oa_s, st_s = _hgrn_sample(pm_s, f_s, hgrn_lb, g_hgrn, s_hgrn, dbatch, t_len, SAMPLE_BB)

    k_all = jnp.arange(N_KEYS)
    bucket_p = _bucket_table(jnp.arange(WINDOW) + WINDOW, k_all, jnp.ones((N_KEYS,), bool))
    rows_s = SAMPLE_BB * t_len
    bucket_s = _bucket_table(WINDOW + jnp.arange(rows_s) % t_len, k_all, k_all < WINDOW + t_len)
    ob_p = _swa_prompt(bucket_p, rel_bias, attn_sink, pm_p, kv_p, batch, seq, SWA_BLOCKS)
    ob_s, nk_s, nv_s = _swa_sample(bucket_s, rel_bias, attn_sink, pm_s, kv_s,
                                   win_k.reshape(dbatch, WINDOW, kw), win_v.reshape(dbatch, WINDOW, kw),
                                   dbatch, t_len, SAMPLE_BB)

    y_p = _outproj(oa_p, ob_p, pm_p, xp2, pp.reshape(batch * seq, -1), wpa, wpb, wo, g_post, wple, wg, TM_PROJ)
    y_s = _outproj(oa_s, ob_s, pm_s, xs2, ps.reshape(dbatch * t_len, -1), wpa, wpb, wo, g_post, wple, wg,
                   min(TM_PROJ, dbatch * t_len))

    kv_p3 = kv_p.reshape(batch, seq, 2 * kw)[:, seq - WINDOW:, :]
    k_win_p = kv_p3[:, :, 0:kw].reshape(batch, WINDOW, B_KV_HEADS, B_HD)
    v_win_p = kv_p3[:, :, kw:].reshape(batch, WINDOW, B_KV_HEADS, B_HD)
    return (y_p.reshape(batch, seq, D_MODEL), y_s.reshape(dbatch, t_len, D_MODEL), st_p, st_s,
            k_win_p, v_win_p,
            nk_s.reshape(dbatch, WINDOW, B_KV_HEADS, B_HD), nv_s.reshape(dbatch, WINDOW, B_KV_HEADS, B_HD))


def kernel(x_prompt, x_sample, state_hgrn, cache_swa_k, cache_swa_v, p_prompt, p_sample, norm_pre, w_in,
           hgrn_lb, hgrn_norm, attn_sink, rel_bias, w_pa, w_pb, w_o, norm_post, w_ple, w_ple_gate):
    depth = w_in.shape[0]
    assert depth == 1, "the forget-gate lower bound is implemented for a single layer"
    xp, xs = x_prompt, x_sample
    outs = []
    for l in range(depth):
        res = _layer(xp, xs, state_hgrn[l], cache_swa_k[l], cache_swa_v[l], p_prompt[l], p_sample[l],
                     norm_pre[l], w_in[l], hgrn_lb, hgrn_norm[l], attn_sink[l], rel_bias,
                     w_pa[l], w_pb[l], w_o[l], norm_post[l], w_ple[l], w_ple_gate[l])
        xp, xs = res[0], res[1]
        outs.append(res[2:])
    stack = lambda i: jnp.stack([o[i] for o in outs])
    return (xp, xs, stack(0), stack(1), stack(2), stack(3), stack(4), stack(5))
```

```python
import functools

import jax
import jax.numpy as jnp
from jax import lax
from jax.experimental import pallas as pl
from jax.experimental.pallas import tpu as pltpu

F32 = jnp.float32
BF16 = jnp.bfloat16

D_MODEL = 1024
A_HEADS = 8
A_DK = 128
A_DV = 128
A_CHUNK = 64
A_SUB = 16
B_HEADS = 16
B_KV_HEADS = 2
B_HD = 64
WINDOW = 128
REL_BUCKETS = 32
REL_MAX_DIST = 128
EPS = 1e-6
NEG = -1e30

G_AQ, G_AI, G_AOG, G_AZ, G_BQ, G_BZ, G_GA, G_GB = range(8)
N_GROUPS = 8

VMEM_LIMIT = 56 * 1024 * 1024


def _sigmoid(x):
    return 1.0 / (1.0 + jnp.exp(-x))


def _silu(x):
    return x * _sigmoid(x)


def _resident(shape):
    nd = len(shape)
    return pl.BlockSpec(shape, lambda *_: (0,) * nd, pipeline_mode=pl.Buffered(1))


IN_OFFS = (0, 1024, 2048, 3072, 4096, 5120, 6144, 6272, 6400, 7424, 8448, 9472)
SLAB_SEGS = (0, 2, 3, 4, 5, 8, 9, 10)
SEG_AF, SEG_BK, SEG_BV = 1, 6, 7


def _inproj_kernel(x_ref, g_ref, w_ref, pm_ref, f_ref, kv_ref):
    x = x_ref[...]
    ms = jnp.mean(x * x, axis=-1, keepdims=True)
    u = (x * lax.rsqrt(ms + EPS) * g_ref[...]).astype(BF16)
    proj = lambda lo, hi: jnp.dot(u, w_ref[:, lo:hi], preferred_element_type=F32)
    for c, seg in enumerate(SLAB_SEGS):
        pm_ref[:, c * D_MODEL:(c + 1) * D_MODEL] = proj(IN_OFFS[seg], IN_OFFS[seg + 1]).astype(BF16)
    f_ref[...] = proj(IN_OFFS[SEG_AF], IN_OFFS[SEG_AF + 1])
    kv_ref[...] = proj(IN_OFFS[SEG_BK], IN_OFFS[SEG_BV + 1])


def _inproj(x, g, w, tm):
    n = x.shape[0]
    nm = N_GROUPS * D_MODEL
    nkv = IN_OFFS[SEG_BV + 1] - IN_OFFS[SEG_BK]
    return pl.pallas_call(
        _inproj_kernel,
        grid=(n // tm,),
        in_specs=[
            pl.BlockSpec((tm, D_MODEL), lambda i: (i, 0)),
            _resident((1, D_MODEL)),
            _resident(w.shape),
        ],
        out_specs=[
            pl.BlockSpec((tm, nm), lambda i: (i, 0)),
            pl.BlockSpec((tm, D_MODEL), lambda i: (i, 0)),
            pl.BlockSpec((tm, nkv), lambda i: (i, 0)),
        ],
        out_shape=[
            jax.ShapeDtypeStruct((n, nm), BF16),
            jax.ShapeDtypeStruct((n, D_MODEL), F32),
            jax.ShapeDtypeStruct((n, nkv), F32),
        ],
        compiler_params=pltpu.CompilerParams(
            dimension_semantics=("arbitrary",), vmem_limit_bytes=VMEM_LIMIT),
        name="inproj",
    )(x, g, w)


def _lower_bound(lb_ref):
    l = lb_ref[...]
    m = jnp.max(l, axis=0, keepdims=True)
    e = jnp.exp(l - m)
    return e[0:1, :] / jnp.sum(e, axis=0, keepdims=True)


def _group_cumsum(x, row, period, shifts):
    pos = row & (period - 1)
    for sh in shifts:
        x = x + jnp.where(pos >= sh, pltpu.roll(x, sh, axis=0), 0.0)
    return x


def _hgrn_finish(o, g, og_pre, z_pre):
    o = o * lax.rsqrt(jnp.mean(o * o, axis=-1, keepdims=True) + EPS)
    return o * g * _sigmoid(og_pre) * _silu(z_pre)


def _dot_nt(a, b):
    return lax.dot_general(a, b, (((1,), (1,)), ((), ())), preferred_element_type=F32)


def _dot_tn(a, b):
    return lax.dot_general(a, b, (((0,), (0,)), ((), ())), preferred_element_type=F32)


def _hgrn_prompt_kernel(q_ref, v_ref, og_ref, z_ref, f_ref, lb_ref, g_ref, oa_ref, s_ref, *, sc):
    n_sub = A_CHUNK // A_SUB
    slab = sc * A_CHUNK
    n_iter = q_ref.shape[0] // slab
    lb = _lower_bound(lb_ref)
    g = g_ref[...]
    row = lax.broadcasted_iota(jnp.int32, (slab, A_DK), 0)
    rs = lax.broadcasted_iota(jnp.int32, (slab, slab), 0)
    cs = lax.broadcasted_iota(jnp.int32, (slab, slab), 1)
    shift = A_CHUNK.bit_length() - 1
    same_chunk_causal = jnp.logical_and((rs >> shift) == (cs >> shift), cs <= rs)
    zero_row = jnp.zeros((1, A_DK), F32)
    zero_blk = jnp.zeros((A_SUB, A_DK), BF16)
    zero_chunk = jnp.zeros((A_CHUNK, A_DK), BF16)
    rep = lambda r, n: jnp.broadcast_to(r, (n, A_DK))

    def body(it, st):
        r0 = pl.multiple_of(it * slab, slab)
        rows = pl.ds(r0, slab)
        fg = lb + (1.0 - lb) * _sigmoid(f_ref[rows, :])
        logf = jnp.log(fg)
        k = 1.0 - fg
        b = _group_cumsum(logf, row, A_CHUNK, (1, 2, 4, 8, 16, 32))
        q = _silu(q_ref[rows, :].astype(F32))
        v = v_ref[rows, :]

        ends = [b[A_SUB * j + A_SUB - 1:A_SUB * (j + 1), :] for j in range(sc * n_sub)]
        endrow = jnp.concatenate([rep(e, A_SUB) for e in ends], axis=0)
        prevrow = jnp.concatenate(
            [rep(zero_row if j % n_sub == 0 else ends[j - 1], A_SUB) for j in range(sc * n_sub)], axis=0)
        tot = [ends[n_sub * c + n_sub - 1] for c in range(sc)]
        lastrow = jnp.concatenate([rep(t, A_CHUNK) for t in tot], axis=0)
        kend = k * jnp.exp(endrow - b)
        qd = q * jnp.exp(b - prevrow)
        qs = qd * jnp.exp(prevrow)
        kd = kend * jnp.exp(lastrow - endrow)
        kdiag16 = (k * jnp.exp(prevrow - b)).astype(BF16)
        qd16 = qd.astype(BF16)
        kend16 = kend.astype(BF16)
        qs16 = qs.astype(BF16)
        kd16 = kd.astype(BF16)

        blk = lambda a, j: a[A_SUB * j:A_SUB * (j + 1), :]
        chk = lambda a, c: a[A_CHUNK * c:A_CHUNK * (c + 1), :]

        q_groups, k_groups = [], []
        for i in range(n_sub):
            q_groups.append(jnp.concatenate(
                [blk(qd16, j) if j % n_sub == i else zero_blk for j in range(sc * n_sub)], axis=0))
            pieces = []
            for j in range(sc * n_sub):
                c, jj = divmod(j, n_sub)
                if jj == i:
                    pieces.append(blk(kdiag16, j))
                elif jj == i - 1:
                    pieces.append(blk(kend16, j))
                elif jj < i:
                    pieces.append((blk(kend, j) * jnp.exp(ends[c * n_sub + i - 1] - ends[j])).astype(BF16))
                else:
                    pieces.append(zero_blk)
            k_groups.append(jnp.concatenate(pieces, axis=0))
        att = _dot_nt(jnp.concatenate(q_groups, axis=1), jnp.concatenate(k_groups, axis=1))
        att = jnp.where(same_chunk_causal, att, 0.0)

        if sc > 1:
            q_groups, k_groups = [], []
            for c in range(1, sc):
                q_groups.append(jnp.concatenate(
                    [chk(qs16, c2) if c2 == c else zero_chunk for c2 in range(sc)], axis=0))
                pieces = []
                for c2 in range(sc):
                    if c2 == c - 1:
                        pieces.append(chk(kd16, c2))
                    elif c2 < c:
                        carry = sum(tot[c2 + 1:c])
                        pieces.append((chk(kd, c2) * jnp.exp(carry)).astype(BF16))
                    else:
                        pieces.append(zero_chunk)
                k_groups.append(jnp.concatenate(pieces, axis=0))
            att = att + _dot_nt(jnp.concatenate(q_groups, axis=1), jnp.concatenate(k_groups, axis=1))
        o = jnp.dot(att.astype(BF16), v, preferred_element_type=F32)

        q0 = jnp.concatenate(
            [chk(qs16, 0)] + [(chk(qs, c) * jnp.exp(sum(tot[:c]))).astype(BF16) for c in range(1, sc)], axis=0)
        o = o + _dot_nt(q0, st.astype(BF16))

        k1 = jnp.concatenate(
            [(chk(kd, c) * jnp.exp(sum(tot[c + 1:]))).astype(BF16) for c in range(sc - 1)]
            + [chk(kd16, sc - 1)], axis=0)
        st = st * jnp.exp(sum(tot)) + _dot_tn(v, k1)

        oa = _hgrn_finish(o, g, og_ref[rows, :].astype(F32), z_ref[rows, :].astype(F32))
        oa_ref[rows, :] = oa.astype(BF16)
        return st

    st = lax.fori_loop(0, n_iter, body, jnp.zeros((A_DV, A_DK), F32), unroll=2)
    s_ref[...] = st.T


HGRN_CHUNKS_PER_ITER = 4


def _hgrn_prompt(pm, f, hgrn_lb, hgrn_norm, batch, seq):
    col = lambda grp: pl.BlockSpec((seq, A_DK), lambda b, h, grp=grp: (b, grp * A_HEADS + h))
    n_lb = hgrn_lb.shape[0]
    return pl.pallas_call(
        functools.partial(_hgrn_prompt_kernel, sc=HGRN_CHUNKS_PER_ITER),
        grid=(batch, A_HEADS),
        in_specs=[
            col(G_AQ), col(G_AI), col(G_AOG), col(G_AZ),
            pl.BlockSpec((seq, A_DK), lambda b, h: (b, h)),
            pl.BlockSpec((n_lb, A_DK), lambda b, h: (0, h)),
            pl.BlockSpec((1, A_DV), lambda b, h: (0, h)),
        ],
        out_specs=[
            pl.BlockSpec((seq, A_DV), lambda b, h: (b, h)),
            pl.BlockSpec((None, None, A_DK, A_DV), lambda b, h: (b, h, 0, 0)),
        ],
        out_shape=[
            jax.ShapeDtypeStruct((batch * seq, A_HEADS * A_DV), BF16),
            jax.ShapeDtypeStruct((batch, A_HEADS, A_DK, A_DV), F32),
        ],
        compiler_params=pltpu.CompilerParams(
            dimension_semantics=("arbitrary", "arbitrary"), vmem_limit_bytes=VMEM_LIMIT),
        name="hgrn_prompt",
    )(pm, pm, pm, pm, f, hgrn_lb, hgrn_norm)


def _hgrn_sample_kernel(q_ref, v_ref, og_ref, z_ref, f_ref, lb_ref, g_ref, s0_ref, oa_ref, s_ref,
                        *, bb, t_len):
    rows = bb * t_len
    lb_all = _lower_bound(lb_ref)
    row = lax.broadcasted_iota(jnp.int32, (rows, A_DK), 0)
    pos = row & (t_len - 1)
    grp = row >> (t_len.bit_length() - 1)
    shifts = tuple(1 << i for i in range((t_len - 1).bit_length()))
    for h in range(A_HEADS):
        ls = slice(h * A_DK, (h + 1) * A_DK)
        lb = lb_all[:, ls]
        fg = lb + (1.0 - lb) * _sigmoid(f_ref[:, ls])
        logf = jnp.log(fg)
        k = 1.0 - fg
        b = _group_cumsum(logf, row, t_len, shifts)
        q = _silu(q_ref[:, ls].astype(F32))
        v = v_ref[:, ls].astype(F32)

        o = jnp.zeros((rows, A_DV), F32)
        for d in range(t_len):
            if d == 0:
                a = jnp.sum(q * k, axis=-1, keepdims=True)
                o = o + a * v
            else:
                w = jnp.exp(jnp.where(pos >= d, b - pltpu.roll(b, d, axis=0), 0.0))
                a = jnp.sum(q * pltpu.roll(k, d, axis=0) * w, axis=-1, keepdims=True)
                o = o + jnp.where(pos >= d, a * pltpu.roll(v, d, axis=0), 0.0)

        b_last = b
        for d in range(1, t_len):
            b_last = jnp.where(pos == t_len - 1 - d, pltpu.roll(b, rows - d, axis=0), b_last)
        e_last = jnp.exp(b_last)
        qs = q * jnp.exp(b)
        kd = k * jnp.exp(b_last - b)
        v16 = v.astype(BF16)
        for j in range(bb):
            s0 = s0_ref[j, h]
            mine = grp == j
            o = o + jnp.dot(jnp.where(mine, qs, 0.0).astype(BF16), s0.astype(BF16),
                            preferred_element_type=F32)
            u = _dot_tn(jnp.where(mine, kd, 0.0).astype(BF16), v16)
            e_row = e_last[j * t_len:j * t_len + 1, :]
            e_col = jnp.broadcast_to(e_row, (A_DK, A_DK)).T
            s_ref[j, h] = e_col * s0 + u

        oa = _hgrn_finish(o, g_ref[:, ls], og_ref[:, ls].astype(F32), z_ref[:, ls].astype(F32))
        oa_ref[:, ls] = oa.astype(BF16)


def _hgrn_sample(pm, f, hgrn_lb, hgrn_norm, s0, batch, t_len, bb):
    rows = bb * t_len
    col = lambda grp: pl.BlockSpec((rows, D_MODEL), lambda i, grp=grp: (i, grp))
    n_lb = hgrn_lb.shape[0]
    st_spec = pl.BlockSpec((bb, A_HEADS, A_DK, A_DV), lambda i: (i, 0, 0, 0))
    return pl.pallas_call(
        functools.partial(_hgrn_sample_kernel, bb=bb, t_len=t_len),
        grid=(batch // bb,),
        in_specs=[
            col(G_AQ), col(G_AI), col(G_AOG), col(G_AZ),
            pl.BlockSpec((rows, D_MODEL), lambda i: (i, 0)),
            _resident((n_lb, D_MODEL)),
            _resident((1, D_MODEL)),
            st_spec,
        ],
        out_specs=[pl.BlockSpec((rows, D_MODEL), lambda i: (i, 0)), st_spec],
        out_shape=[
            jax.ShapeDtypeStruct((batch * t_len, D_MODEL), BF16),
            jax.ShapeDtypeStruct((batch, A_HEADS, A_DK, A_DV), F32),
        ],
        compiler_params=pltpu.CompilerParams(
            dimension_semantics=("arbitrary",), vmem_limit_bytes=VMEM_LIMIT),
        name="hgrn_sample",
    )(pm, pm, pm, pm, f, hgrn_lb, hgrn_norm, s0)


N_PAIRS = B_HEADS // 2
PAIRS_PER_KV = N_PAIRS // B_KV_HEADS
N_KEYS = 2 * WINDOW


def _build_bias(bucket_ref, relb_ref, bias_ref, r):
    bucket = bucket_ref[...]
    for h in range(B_HEADS):
        def body(kb, acc, h=h):
            return jnp.where(bucket == kb, relb_ref[kb, h], acc)
        tab = lax.fori_loop(0, REL_BUCKETS, body, jnp.full(bucket.shape, NEG, F32))
        pair, parity = divmod(h, 2)
        kv, pp = divmod(pair, PAIRS_PER_KV)
        bias_ref[kv, parity * N_KEYS:(parity + 1) * N_KEYS, pp * r:(pp + 1) * r] = tab


def _swa_keys(kk):
    lo = lax.broadcasted_iota(jnp.int32, (N_KEYS, 2 * B_HD), 1) < B_HD
    kk_sw = pltpu.roll(kk, B_HD, axis=1)
    slabs = []
    for kv in range(B_KV_HEADS):
        k_lo, k_hi = (kk, kk_sw) if kv == 0 else (kk_sw, kk)
        slabs.append(jnp.concatenate([jnp.where(lo, k_lo, 0.0), jnp.where(lo, 0.0, k_hi)], axis=0).astype(BF16))
    return slabs


def _swa_attend(logits, vals_t, col_masks, sink_ref, kv, r):
    width = PAIRS_PER_KV * r
    lane_pp = lax.broadcasted_iota(jnp.int32, (1, width), 1) >> (r.bit_length() - 1)
    zero_half = jnp.zeros((B_HD, N_KEYS), BF16)
    acc = None
    for parity in range(2):
        sink = jnp.zeros((1, width), F32)
        for pp in range(PAIRS_PER_KV):
            sink = jnp.where(lane_pp == pp, sink_ref[(kv * PAIRS_PER_KV + pp) * 2 + parity], sink)
        l = logits[parity * N_KEYS:(parity + 1) * N_KEYS, :]
        m = jnp.maximum(jnp.max(l, axis=0, keepdims=True), sink)
        p = jnp.exp(l - m)
        denom = jnp.sum(p, axis=0, keepdims=True) + jnp.exp(sink - m)
        p16 = p.astype(BF16)
        pv = None
        for v_t, mask in zip(vals_t, col_masks):
            v_kv = v_t[kv * B_HD:(kv + 1) * B_HD, :]
            lhs = jnp.concatenate([v_kv, zero_half] if parity == 0 else [zero_half, v_kv], axis=0)
            part = jnp.dot(lhs, p16, preferred_element_type=F32)
            pv = part if mask is None else jnp.where(mask, part, 0.0 if pv is None else pv)
        pv = pv / denom
        acc = pv if acc is None else acc + pv
    return acc.T


def _swa_prompt_kernel(bucket_ref, relb_ref, sink_ref, q_ref, z_ref, kv_ref, kvp_ref, ob_ref, bias_ref,
                       *, blocks):
    first = jnp.logical_and(pl.program_id(0) == 0, pl.program_id(1) == 0)

    @pl.when(first)
    def _():
        _build_bias(bucket_ref, relb_ref, bias_ref, WINDOW)

    key = lax.broadcasted_iota(jnp.int32, (2 * N_KEYS, PAIRS_PER_KV * WINDOW), 0) & (N_KEYS - 1)
    pen0 = jnp.where(jnp.logical_and(key < WINDOW, pl.program_id(1) == 0), NEG, 0.0)
    kw = B_KV_HEADS * B_HD
    for j in range(blocks):
        r0 = j * WINDOW
        if j == 0:
            kk = jnp.concatenate([kvp_ref[:, 0:kw], kv_ref[0:WINDOW, 0:kw]], axis=0)
            vv = jnp.concatenate([kvp_ref[:, kw:2 * kw], kv_ref[0:WINDOW, kw:2 * kw]], axis=0)
        else:
            kk = kv_ref[r0 - WINDOW:r0 + WINDOW, 0:kw]
            vv = kv_ref[r0 - WINDOW:r0 + WINDOW, kw:2 * kw]
        q_pairs = [q_ref[r0:r0 + WINDOW, p * 128:(p + 1) * 128] * (B_HD ** -0.5) for p in range(N_PAIRS)]
        keys = _swa_keys(kk)
        vals_t = [vv.T.astype(BF16)]
        outs = []
        for kv in range(B_KV_HEADS):
            q = jnp.concatenate(q_pairs[kv * PAIRS_PER_KV:(kv + 1) * PAIRS_PER_KV], axis=0)
            logits = _dot_nt(keys[kv], q) + bias_ref[kv]
            if j == 0:
                logits = logits + pen0
            acc_t = _swa_attend(logits, vals_t, [None], sink_ref, kv, WINDOW)
            outs += [acc_t[pp * WINDOW:(pp + 1) * WINDOW, :] for pp in range(PAIRS_PER_KV)]
        for p in range(N_PAIRS):
            z = z_ref[r0:r0 + WINDOW, p * 128:(p + 1) * 128].astype(F32)
            ob_ref[r0:r0 + WINDOW, p * 128:(p + 1) * 128] = (outs[p] * _silu(z)).astype(BF16)


def _swa_prompt(bucket, rel_bias, sink, pm, kv, batch, seq, blocks):
    tq = blocks * WINDOW
    nt = seq // tq
    kvw = kv.shape[1]
    smem = pl.BlockSpec(memory_space=pltpu.SMEM)
    return pl.pallas_call(
        functools.partial(_swa_prompt_kernel, blocks=blocks),
        grid=(batch, nt),
        in_specs=[
            _resident((N_KEYS, WINDOW)),
            smem, smem,
            pl.BlockSpec((tq, D_MODEL), lambda b, t: (b * nt + t, G_BQ)),
            pl.BlockSpec((tq, D_MODEL), lambda b, t: (b * nt + t, G_BZ)),
            pl.BlockSpec((tq, kvw), lambda b, t: (b * nt + t, 0)),
            pl.BlockSpec((WINDOW, kvw), lambda b, t: (jnp.maximum((b * nt + t) * blocks - 1, 0), 0)),
        ],
        out_specs=pl.BlockSpec((tq, D_MODEL), lambda b, t: (b * nt + t, 0)),
        out_shape=jax.ShapeDtypeStruct((batch * seq, D_MODEL), BF16),
        scratch_shapes=[pltpu.VMEM((B_KV_HEADS, 2 * N_KEYS, PAIRS_PER_KV * WINDOW), F32)],
        compiler_params=pltpu.CompilerParams(
            dimension_semantics=("arbitrary", "arbitrary"), vmem_limit_bytes=VMEM_LIMIT),
        name="swa_prompt",
    )(bucket, rel_bias, sink, pm, pm, kv, kv)


def _swa_sample_kernel(bucket_ref, relb_ref, sink_ref, q_ref, z_ref, kvn_ref, ck_ref, cv_ref,
                       ob_ref, nk_ref, nv_ref, bias_ref, *, bb, t_len):
    rows = bb * t_len
    width = PAIRS_PER_KV * rows

    @pl.when(pl.program_id(0) == 0)
    def _():
        _build_bias(bucket_ref, relb_ref, bias_ref, rows)

    kw = B_KV_HEADS * B_HD
    t_shift = t_len.bit_length() - 1
    row = lax.broadcasted_iota(jnp.int32, (rows, kw), 0)
    seq_of_col = lambda n: (lax.broadcasted_iota(jnp.int32, (n, width), 1) & (rows - 1)) >> t_shift
    seq_l = seq_of_col(2 * N_KEYS)
    seq_v = seq_of_col(2 * B_HD)
    q_pairs = [q_ref[:, p * 128:(p + 1) * 128] * (B_HD ** -0.5) for p in range(N_PAIRS)]
    q_kv = [jnp.concatenate(q_pairs[kv * PAIRS_PER_KV:(kv + 1) * PAIRS_PER_KV], axis=0)
            for kv in range(B_KV_HEADS)]
    kn = kvn_ref[:, 0:kw]
    vn = kvn_ref[:, kw:2 * kw]
    pad = jnp.zeros((N_KEYS - WINDOW - rows, kw), F32)
    logits = [None] * B_KV_HEADS
    vals_t, col_masks = [], []
    for j in range(bb):
        sel = row < t_len
        kj = jnp.where(sel, kn if j == 0 else pltpu.roll(kn, rows - j * t_len, axis=0), 0.0)
        vj = jnp.where(sel, vn if j == 0 else pltpu.roll(vn, rows - j * t_len, axis=0), 0.0)
        kk = jnp.concatenate([ck_ref[j], kj, pad], axis=0)
        vv = jnp.concatenate([cv_ref[j], vj, pad], axis=0)
        keys = _swa_keys(kk)
        for kv in range(B_KV_HEADS):
            lg = _dot_nt(keys[kv], q_kv[kv])
            logits[kv] = jnp.where(seq_l == j, lg, 0.0 if logits[kv] is None else logits[kv])
        vals_t.append(vv.T.astype(BF16))
        col_masks.append(seq_v == j)
        nk_ref[j] = pltpu.roll(kk, N_KEYS - t_len, axis=0)[0:WINDOW, :]
        nv_ref[j] = pltpu.roll(vv, N_KEYS - t_len, axis=0)[0:WINDOW, :]
    for kv in range(B_KV_HEADS):
        acc_t = _swa_attend(logits[kv] + bias_ref[kv], vals_t, col_masks, sink_ref, kv, rows)
        for pp in range(PAIRS_PER_KV):
            p = kv * PAIRS_PER_KV + pp
            z = z_ref[:, p * 128:(p + 1) * 128].astype(F32)
            ob_ref[:, p * 128:(p + 1) * 128] = (acc_t[pp * rows:(pp + 1) * rows, :] * _silu(z)).astype(BF16)


def _swa_sample(bucket, rel_bias, sink, pm, kvn, cache_k, cache_v, batch, t_len, bb):
    rows = bb * t_len
    kvw = kvn.shape[1]
    kw = kvw // 2
    smem = pl.BlockSpec(memory_space=pltpu.SMEM)
    cache_spec = pl.BlockSpec((bb, WINDOW, kw), lambda i: (i, 0, 0))
    return pl.pallas_call(
        functools.partial(_swa_sample_kernel, bb=bb, t_len=t_len),
        grid=(batch // bb,),
        in_specs=[
            _resident((N_KEYS, rows)),
            smem, smem,
            pl.BlockSpec((rows, D_MODEL), lambda i: (i, G_BQ)),
            pl.BlockSpec((rows, D_MODEL), lambda i: (i, G_BZ)),
            pl.BlockSpec((rows, kvw), lambda i: (i, 0)),
            cache_spec, cache_spec,
        ],
        out_specs=[pl.BlockSpec((rows, D_MODEL), lambda i: (i, 0)), cache_spec, cache_spec],
        out_shape=[
            jax.ShapeDtypeStruct((batch * t_len, D_MODEL), BF16),
            jax.ShapeDtypeStruct((batch, WINDOW, kw), F32),
            jax.ShapeDtypeStruct((batch, WINDOW, kw), F32),
        ],
        scratch_shapes=[pltpu.VMEM((B_KV_HEADS, 2 * N_KEYS, PAIRS_PER_KV * rows), F32)],
        compiler_params=pltpu.CompilerParams(
            dimension_semantics=("arbitrary",), vmem_limit_bytes=VMEM_LIMIT),
        name="swa_sample",
    )(bucket, rel_bias, sink, pm, pm, kvn, cache_k, cache_v)


def _outproj_kernel(oa_ref, ob_ref, ga_ref, gb_ref, x_ref, p_ref, wpa_ref, wpb_ref, wo_ref, gpost_ref,
                    wple_ref, wg_ref, y_ref):
    a = jnp.dot(oa_ref[...], wpa_ref[...], preferred_element_type=F32)
    b = jnp.dot(ob_ref[...], wpb_ref[...], preferred_element_type=F32)
    m = _sigmoid(ga_ref[...].astype(F32)) * a + _sigmoid(gb_ref[...].astype(F32)) * b
    y = jnp.dot(m.astype(BF16), wo_ref[...], preferred_element_type=F32)
    y = y * lax.rsqrt(jnp.mean(y * y, axis=-1, keepdims=True) + EPS) * gpost_ref[...]
    x1 = x_ref[...] + y
    gate = _sigmoid(jnp.dot(x1.astype(BF16), wg_ref[...], preferred_element_type=F32))
    e = jnp.dot(p_ref[...].astype(BF16), wple_ref[...], preferred_element_type=F32) * gate
    y_ref[...] = x1 + e


def _outproj(oa, ob, pm, x, p, wpa, wpb, wo, gpost, wple, wg, tm):
    n = x.shape[0]
    ple = p.shape[1]
    tok = lambda w, c=0: pl.BlockSpec((tm, w), lambda i, c=c: (i, c))
    return pl.pallas_call(
        _outproj_kernel,
        grid=(n // tm,),
        in_specs=[
            tok(D_MODEL), tok(D_MODEL), tok(D_MODEL, G_GA), tok(D_MODEL, G_GB), tok(D_MODEL), tok(ple),
            _resident((D_MODEL, D_MODEL)), _resident((D_MODEL, D_MODEL)), _resident((D_MODEL, D_MODEL)),
            _resident((1, D_MODEL)), _resident((ple, D_MODEL)), _resident((D_MODEL, D_MODEL)),
        ],
        out_specs=tok(D_MODEL),
        out_shape=jax.ShapeDtypeStruct((n, D_MODEL), F32),
        compiler_params=pltpu.CompilerParams(
            dimension_semantics=("arbitrary",), vmem_limit_bytes=VMEM_LIMIT),
        name="outproj",
    )(oa, ob, pm, pm, x, p, wpa, wpb, wo, gpost, wple, wg)


def _rel_bucket(rel):
    n = jnp.maximum(rel, 0)
    max_exact = REL_BUCKETS // 2
    nf = jnp.maximum(n, 1).astype(F32)
    large = max_exact + (jnp.log(nf / max_exact) / jnp.log(jnp.float32(REL_MAX_DIST / max_exact))
                         * (REL_BUCKETS - max_exact)).astype(jnp.int32)
    large = jnp.minimum(large, REL_BUCKETS - 1)
    return jnp.where(n < max_exact, n, large)


def _bucket_table(q_pos, k_pos, k_valid):
    rel = q_pos[:, None] - k_pos[None, :]
    ok = (rel >= 0) & (rel < WINDOW) & k_valid[None, :]
    return jnp.where(ok, _rel_bucket(rel), -1).astype(jnp.int32)


TM_PROJ = 512
SWA_BLOCKS = 4
SAMPLE_BB = 4
SWA_SAMPLE_BB = 8


def _layer(xp, xs, s_hgrn, win_k, win_v, pp, ps, norm_pre, w_in, hgrn_lb, hgrn_norm, attn_sink,
           rel_bias, w_pa, w_pb, w_o, norm_post, w_ple, w_ple_gate):
    batch, seq, _ = xp.shape
    dbatch, t_len, _ = xs.shape
    kw = B_KV_HEADS * B_HD

    w16 = w_in.astype(BF16)
    g_pre = norm_pre.reshape(1, D_MODEL)
    g_post = norm_post.reshape(1, D_MODEL)
    g_hgrn = hgrn_norm.reshape(1, D_MODEL)
    wpa, wpb, wo = w_pa.astype(BF16), w_pb.astype(BF16), w_o.astype(BF16)
    wple, wg = w_ple.astype(BF16), w_ple_gate.astype(BF16)

    xp2 = xp.reshape(batch * seq, D_MODEL)
    xs2 = xs.reshape(dbatch * t_len, D_MODEL)
    pm_p, f_p, kv_p = _inproj(xp2, g_pre, w16, TM_PROJ)
    pm_s, f_s, kv_s = _inproj(xs2, g_pre, w16, min(TM_PROJ, dbatch * t_len))

    oa_p, st_p = _hgrn_prompt(pm_p, f_p, hgrn_lb, g_hgrn, batch, seq)
    oa_s, st_s = _hgrn_sample(pm_s, f_s, hgrn_lb, g_hgrn, s_hgrn, dbatch, t_len, SAMPLE_BB)

    k_all = jnp.arange(N_KEYS)
    bucket_p = _bucket_table(jnp.arange(WINDOW) + WINDOW, k_all, jnp.ones((N_KEYS,), bool)).T
    rows_s = SWA_SAMPLE_BB * t_len
    bucket_s = _bucket_table(WINDOW + jnp.arange(rows_s) % t_len, k_all, k_all < WINDOW + t_len).T
    ob_p = _swa_prompt(bucket_p, rel_bias, attn_sink, pm_p, kv_p, batch, seq, SWA_BLOCKS)
    ob_s, nk_s, nv_s = _swa_sample(bucket_s, rel_bias, attn_sink, pm_s, kv_s,
                                   win_k.reshape(dbatch, WINDOW, kw), win_v.reshape(dbatch, WINDOW, kw),
                                   dbatch, t_len, SWA_SAMPLE_BB)

    y_p = _outproj(oa_p, ob_p, pm_p, xp2, pp.reshape(batch * seq, -1), wpa, wpb, wo, g_post, wple, wg, TM_PROJ)
    y_s = _outproj(oa_s, ob_s, pm_s, xs2, ps.reshape(dbatch * t_len, -1), wpa, wpb, wo, g_post, wple, wg,
                   min(TM_PROJ, dbatch * t_len))

    kv_p3 = kv_p.reshape(batch, seq, 2 * kw)[:, seq - WINDOW:, :]
    k_win_p = kv_p3[:, :, 0:kw].reshape(batch, WINDOW, B_KV_HEADS, B_HD)
    v_win_p = kv_p3[:, :, kw:].reshape(batch, WINDOW, B_KV_HEADS, B_HD)
    return (y_p.reshape(batch, seq, D_MODEL), y_s.reshape(dbatch, t_len, D_MODEL), st_p, st_s,
            k_win_p, v_win_p,
            nk_s.reshape(dbatch, WINDOW, B_KV_HEADS, B_HD), nv_s.reshape(dbatch, WINDOW, B_KV_HEADS, B_HD))


def kernel(x_prompt, x_sample, state_hgrn, cache_swa_k, cache_swa_v, p_prompt, p_sample, norm_pre, w_in,
           hgrn_lb, hgrn_norm, attn_sink, rel_bias, w_pa, w_pb, w_o, norm_post, w_ple, w_ple_gate):
    depth = w_in.shape[0]
    assert depth == 1, "the forget-gate lower bound is implemented for a single layer"
    xp, xs = x_prompt, x_sample
    outs = []
    for l in range(depth):
        res = _layer(xp, xs, state_hgrn[l], cache_swa_k[l], cache_swa_v[l], p_prompt[l], p_sample[l],
                     norm_pre[l], w_in[l], hgrn_lb, hgrn_norm[l], attn_sink[l], rel_bias,
                     w_pa[l], w_pb[l], w_o[l], norm_post[l], w_ple[l], w_ple_gate[l])
        xp, xs = res[0], res[1]
        outs.append(res[2:])
    stack = lambda i: jnp.stack([o[i] for o in outs])
    return (xp, xs, stack(0), stack(1), stack(2), stack(3), stack(4), stack(5))
```

```python
import functools

import jax
import jax.numpy as jnp
from jax import lax
from jax.experimental import pallas as pl
from jax.experimental.pallas import tpu as pltpu

F32 = jnp.float32
BF16 = jnp.bfloat16

D_MODEL = 1024
A_HEADS = 8
A_DK = 128
A_DV = 128
A_CHUNK = 64
A_SUB = 16
B_HEADS = 16
B_KV_HEADS = 2
B_HD = 64
WINDOW = 128
REL_BUCKETS = 32
REL_MAX_DIST = 128
EPS = 1e-6
NEG = -1e30

G_AQ, G_AI, G_AOG, G_AZ, G_BQ, G_BZ, G_GA, G_GB = range(8)
N_GROUPS = 8

VMEM_LIMIT = 56 * 1024 * 1024


def _sigmoid(x):
    return 1.0 / (1.0 + jnp.exp(-x))


def _silu(x):
    return x * _sigmoid(x)


def _resident(shape):
    nd = len(shape)
    return pl.BlockSpec(shape, lambda *_: (0,) * nd, pipeline_mode=pl.Buffered(1))


IN_OFFS = (0, 1024, 2048, 3072, 4096, 5120, 6144, 6272, 6400, 7424, 8448, 9472)
SLAB_SEGS = (0, 2, 3, 4, 5, 8, 9, 10)
SEG_AF, SEG_BK, SEG_BV = 1, 6, 7


def _inproj_kernel(x_ref, g_ref, w_ref, pm_ref, f_ref, kv_ref):
    x = x_ref[...]
    ms = jnp.mean(x * x, axis=-1, keepdims=True)
    u = (x * lax.rsqrt(ms + EPS) * g_ref[...]).astype(BF16)
    proj = lambda lo, hi: jnp.dot(u, w_ref[:, lo:hi], preferred_element_type=F32)
    for c, seg in enumerate(SLAB_SEGS):
        pm_ref[:, c * D_MODEL:(c + 1) * D_MODEL] = proj(IN_OFFS[seg], IN_OFFS[seg + 1]).astype(BF16)
    f_ref[...] = proj(IN_OFFS[SEG_AF], IN_OFFS[SEG_AF + 1])
    kv_ref[...] = proj(IN_OFFS[SEG_BK], IN_OFFS[SEG_BV + 1])


def _inproj(x, g, w, tm):
    n = x.shape[0]
    nm = N_GROUPS * D_MODEL
    nkv = IN_OFFS[SEG_BV + 1] - IN_OFFS[SEG_BK]
    return pl.pallas_call(
        _inproj_kernel,
        grid=(n // tm,),
        in_specs=[
            pl.BlockSpec((tm, D_MODEL), lambda i: (i, 0)),
            _resident((1, D_MODEL)),
            _resident(w.shape),
        ],
        out_specs=[
            pl.BlockSpec((tm, nm), lambda i: (i, 0)),
            pl.BlockSpec((tm, D_MODEL), lambda i: (i, 0)),
            pl.BlockSpec((tm, nkv), lambda i: (i, 0)),
        ],
        out_shape=[
            jax.ShapeDtypeStruct((n, nm), BF16),
            jax.ShapeDtypeStruct((n, D_MODEL), F32),
            jax.ShapeDtypeStruct((n, nkv), F32),
        ],
        compiler_params=pltpu.CompilerParams(
            dimension_semantics=("arbitrary",), vmem_limit_bytes=VMEM_LIMIT),
        name="inproj",
    )(x, g, w)


def _lower_bound(lb_ref):
    l = lb_ref[...]
    m = jnp.max(l, axis=0, keepdims=True)
    e = jnp.exp(l - m)
    return e[0:1, :] / jnp.sum(e, axis=0, keepdims=True)


def _group_cumsum(x, row, period, shifts):
    pos = row & (period - 1)
    for sh in shifts:
        x = x + jnp.where(pos >= sh, pltpu.roll(x, sh, axis=0), 0.0)
    return x


def _hgrn_finish(o, g, og_pre, z_pre):
    o = o * lax.rsqrt(jnp.mean(o * o, axis=-1, keepdims=True) + EPS)
    return o * g * _sigmoid(og_pre) * _silu(z_pre)


def _dot_nt(a, b):
    return lax.dot_general(a, b, (((1,), (1,)), ((), ())), preferred_element_type=F32)


def _dot_tn(a, b):
    return lax.dot_general(a, b, (((0,), (0,)), ((), ())), preferred_element_type=F32)


def _hgrn_prompt_kernel(q_ref, v_ref, og_ref, z_ref, f_ref, lb_ref, g_ref, oa_ref, s_ref, *, sc):
    n_sub = A_CHUNK // A_SUB
    slab = sc * A_CHUNK
    n_iter = q_ref.shape[0] // slab
    lb = _lower_bound(lb_ref)
    g = g_ref[...]
    row = lax.broadcasted_iota(jnp.int32, (slab, A_DK), 0)
    rs = lax.broadcasted_iota(jnp.int32, (slab, slab), 0)
    cs = lax.broadcasted_iota(jnp.int32, (slab, slab), 1)
    shift = A_CHUNK.bit_length() - 1
    same_chunk_causal = jnp.logical_and((rs >> shift) == (cs >> shift), cs <= rs)
    zero_row = jnp.zeros((1, A_DK), F32)
    zero_blk = jnp.zeros((A_SUB, A_DK), BF16)
    zero_chunk = jnp.zeros((A_CHUNK, A_DK), BF16)
    rep = lambda r, n: jnp.broadcast_to(r, (n, A_DK))

    def body(it, st):
        r0 = pl.multiple_of(it * slab, slab)
        rows = pl.ds(r0, slab)
        fg = lb + (1.0 - lb) * _sigmoid(f_ref[rows, :])
        logf = jnp.log(fg)
        k = 1.0 - fg
        b = _group_cumsum(logf, row, A_CHUNK, (1, 2, 4, 8, 16, 32))
        q = _silu(q_ref[rows, :].astype(F32))
        v = v_ref[rows, :]

        ends = [b[A_SUB * j + A_SUB - 1:A_SUB * (j + 1), :] for j in range(sc * n_sub)]
        endrow = jnp.concatenate([rep(e, A_SUB) for e in ends], axis=0)
        prevrow = jnp.concatenate(
            [rep(zero_row if j % n_sub == 0 else ends[j - 1], A_SUB) for j in range(sc * n_sub)], axis=0)
        tot = [ends[n_sub * c + n_sub - 1] for c in range(sc)]
        lastrow = jnp.concatenate([rep(t, A_CHUNK) for t in tot], axis=0)
        kend = k * jnp.exp(endrow - b)
        qd = q * jnp.exp(b - prevrow)
        qs = qd * jnp.exp(prevrow)
        kd = kend * jnp.exp(lastrow - endrow)
        kdiag16 = (k * jnp.exp(prevrow - b)).astype(BF16)
        qd16 = qd.astype(BF16)
        kend16 = kend.astype(BF16)
        qs16 = qs.astype(BF16)
        kd16 = kd.astype(BF16)

        blk = lambda a, j: a[A_SUB * j:A_SUB * (j + 1), :]
        chk = lambda a, c: a[A_CHUNK * c:A_CHUNK * (c + 1), :]

        q_groups, k_groups = [], []
        for i in range(n_sub):
            q_groups.append(jnp.concatenate(
                [blk(qd16, j) if j % n_sub == i else zero_blk for j in range(sc * n_sub)], axis=0))
            pieces = []
            for j in range(sc * n_sub):
                c, jj = divmod(j, n_sub)
                if jj == i:
                    pieces.append(blk(kdiag16, j))
                elif jj == i - 1:
                    pieces.append(blk(kend16, j))
                elif jj < i:
                    pieces.append((blk(kend, j) * jnp.exp(ends[c * n_sub + i - 1] - ends[j])).astype(BF16))
                else:
                    pieces.append(zero_blk)
            k_groups.append(jnp.concatenate(pieces, axis=0))
        att = _dot_nt(jnp.concatenate(q_groups, axis=1), jnp.concatenate(k_groups, axis=1))
        att = jnp.where(same_chunk_causal, att, 0.0)

        if sc > 1:
            q_groups, k_groups = [], []
            for c in range(1, sc):
                q_groups.append(jnp.concatenate(
                    [chk(qs16, c2) if c2 == c else zero_chunk for c2 in range(sc)], axis=0))
                pieces = []
                for c2 in range(sc):
                    if c2 == c - 1:
                        pieces.append(chk(kd16, c2))
                    elif c2 < c:
                        carry = sum(tot[c2 + 1:c])
                        pieces.append((chk(kd, c2) * jnp.exp(carry)).astype(BF16))
                    else:
                        pieces.append(zero_chunk)
                k_groups.append(jnp.concatenate(pieces, axis=0))
            att = att + _dot_nt(jnp.concatenate(q_groups, axis=1), jnp.concatenate(k_groups, axis=1))
        o = jnp.dot(att.astype(BF16), v, preferred_element_type=F32)

        q0 = jnp.concatenate(
            [chk(qs16, 0)] + [(chk(qs, c) * jnp.exp(sum(tot[:c]))).astype(BF16) for c in range(1, sc)], axis=0)
        o = o + _dot_nt(q0, st.astype(BF16))

        k1 = jnp.concatenate(
            [(chk(kd, c) * jnp.exp(sum(tot[c + 1:]))).astype(BF16) for c in range(sc - 1)]
            + [chk(kd16, sc - 1)], axis=0)
        st = st * jnp.exp(sum(tot)) + _dot_tn(v, k1)

        oa = _hgrn_finish(o, g, og_ref[rows, :].astype(F32), z_ref[rows, :].astype(F32))
        oa_ref[rows, :] = oa.astype(BF16)
        return st

    st = lax.fori_loop(0, n_iter, body, jnp.zeros((A_DV, A_DK), F32), unroll=2)
    s_ref[...] = st.T


HGRN_CHUNKS_PER_ITER = 4


def _hgrn_prompt(pm, f, hgrn_lb, hgrn_norm, batch, seq):
    col = lambda grp: pl.BlockSpec((seq, A_DK), lambda b, h, grp=grp: (b, grp * A_HEADS + h))
    n_lb = hgrn_lb.shape[0]
    return pl.pallas_call(
        functools.partial(_hgrn_prompt_kernel, sc=HGRN_CHUNKS_PER_ITER),
        grid=(batch, A_HEADS),
        in_specs=[
            col(G_AQ), col(G_AI), col(G_AOG), col(G_AZ),
            pl.BlockSpec((seq, A_DK), lambda b, h: (b, h)),
            pl.BlockSpec((n_lb, A_DK), lambda b, h: (0, h)),
            pl.BlockSpec((1, A_DV), lambda b, h: (0, h)),
        ],
        out_specs=[
            pl.BlockSpec((seq, A_DV), lambda b, h: (b, h)),
            pl.BlockSpec((None, None, A_DK, A_DV), lambda b, h: (b, h, 0, 0)),
        ],
        out_shape=[
            jax.ShapeDtypeStruct((batch * seq, A_HEADS * A_DV), BF16),
            jax.ShapeDtypeStruct((batch, A_HEADS, A_DK, A_DV), F32),
        ],
        compiler_params=pltpu.CompilerParams(
            dimension_semantics=("arbitrary", "arbitrary"), vmem_limit_bytes=VMEM_LIMIT),
        name="hgrn_prompt",
    )(pm, pm, pm, pm, f, hgrn_lb, hgrn_norm)


def _hgrn_sample_kernel(q_ref, v_ref, og_ref, z_ref, f_ref, lb_ref, g_ref, s0_ref, oa_ref, s_ref,
                        *, bb, t_len):
    rows = bb * t_len
    lb_all = _lower_bound(lb_ref)
    row = lax.broadcasted_iota(jnp.int32, (rows, A_DK), 0)
    pos = row & (t_len - 1)
    grp = row >> (t_len.bit_length() - 1)
    shifts = tuple(1 << i for i in range((t_len - 1).bit_length()))
    for h in range(A_HEADS):
        ls = slice(h * A_DK, (h + 1) * A_DK)
        lb = lb_all[:, ls]
        fg = lb + (1.0 - lb) * _sigmoid(f_ref[:, ls])
        logf = jnp.log(fg)
        k = 1.0 - fg
        b = _group_cumsum(logf, row, t_len, shifts)
        q = _silu(q_ref[:, ls].astype(F32))
        v = v_ref[:, ls].astype(F32)

        o = jnp.zeros((rows, A_DV), F32)
        for d in range(t_len):
            if d == 0:
                a = jnp.sum(q * k, axis=-1, keepdims=True)
                o = o + a * v
            else:
                w = jnp.exp(jnp.where(pos >= d, b - pltpu.roll(b, d, axis=0), 0.0))
                a = jnp.sum(q * pltpu.roll(k, d, axis=0) * w, axis=-1, keepdims=True)
                o = o + jnp.where(pos >= d, a * pltpu.roll(v, d, axis=0), 0.0)

        b_last = b
        for d in range(1, t_len):
            b_last = jnp.where(pos == t_len - 1 - d, pltpu.roll(b, rows - d, axis=0), b_last)
        e_last = jnp.exp(b_last)
        qs = q * jnp.exp(b)
        kd = k * jnp.exp(b_last - b)
        v16 = v.astype(BF16)
        for j in range(bb):
            s0 = s0_ref[j, h]
            mine = grp == j
            o = o + jnp.dot(jnp.where(mine, qs, 0.0).astype(BF16), s0.astype(BF16),
                            preferred_element_type=F32)
            u = _dot_tn(jnp.where(mine, kd, 0.0).astype(BF16), v16)
            e_row = e_last[j * t_len:j * t_len + 1, :]
            e_col = jnp.broadcast_to(e_row, (A_DK, A_DK)).T
            s_ref[j, h] = e_col * s0 + u

        oa = _hgrn_finish(o, g_ref[:, ls], og_ref[:, ls].astype(F32), z_ref[:, ls].astype(F32))
        oa_ref[:, ls] = oa.astype(BF16)


def _hgrn_sample(pm, f, hgrn_lb, hgrn_norm, s0, batch, t_len, bb):
    rows = bb * t_len
    col = lambda grp: pl.BlockSpec((rows, D_MODEL), lambda i, grp=grp: (i, grp))
    n_lb = hgrn_lb.shape[0]
    st_spec = pl.BlockSpec((bb, A_HEADS, A_DK, A_DV), lambda i: (i, 0, 0, 0))
    return pl.pallas_call(
        functools.partial(_hgrn_sample_kernel, bb=bb, t_len=t_len),
        grid=(batch // bb,),
        in_specs=[
            col(G_AQ), col(G_AI), col(G_AOG), col(G_AZ),
            pl.BlockSpec((rows, D_MODEL), lambda i: (i, 0)),
            _resident((n_lb, D_MODEL)),
            _resident((1, D_MODEL)),
            st_spec,
        ],
        out_specs=[pl.BlockSpec((rows, D_MODEL), lambda i: (i, 0)), st_spec],
        out_shape=[
            jax.ShapeDtypeStruct((batch * t_len, D_MODEL), BF16),
            jax.ShapeDtypeStruct((batch, A_HEADS, A_DK, A_DV), F32),
        ],
        compiler_params=pltpu.CompilerParams(
            dimension_semantics=("arbitrary",), vmem_limit_bytes=VMEM_LIMIT),
        name="hgrn_sample",
    )(pm, pm, pm, pm, f, hgrn_lb, hgrn_norm, s0)


N_PAIRS = B_HEADS // 2
PAIRS_PER_KV = N_PAIRS // B_KV_HEADS
N_KEYS = 2 * WINDOW


def _build_bias(bucket_ref, relb_ref, bias_ref, r):
    bucket = bucket_ref[...]
    for h in range(B_HEADS):
        def body(kb, acc, h=h):
            return jnp.where(bucket == kb, relb_ref[kb, h], acc)
        tab = lax.fori_loop(0, REL_BUCKETS, body, jnp.full(bucket.shape, NEG, F32))
        pair, parity = divmod(h, 2)
        kv, pp = divmod(pair, PAIRS_PER_KV)
        bias_ref[kv, parity * N_KEYS:(parity + 1) * N_KEYS, pp * r:(pp + 1) * r] = tab


def _swa_keys(kk):
    lo = lax.broadcasted_iota(jnp.int32, (N_KEYS, 2 * B_HD), 1) < B_HD
    kk_sw = pltpu.roll(kk, B_HD, axis=1)
    slabs = []
    for kv in range(B_KV_HEADS):
        k_lo, k_hi = (kk, kk_sw) if kv == 0 else (kk_sw, kk)
        slabs.append(jnp.concatenate([jnp.where(lo, k_lo, 0.0), jnp.where(lo, 0.0, k_hi)], axis=0).astype(BF16))
    return slabs


def _swa_attend(logits, vals_t, col_masks, sink_ref, kv, r):
    width = PAIRS_PER_KV * r
    lane_pp = lax.broadcasted_iota(jnp.int32, (1, width), 1) >> (r.bit_length() - 1)
    zero_half = jnp.zeros((B_HD, N_KEYS), BF16)
    acc = None
    for parity in range(2):
        sink = jnp.zeros((1, width), F32)
        for pp in range(PAIRS_PER_KV):
            sink = jnp.where(lane_pp == pp, sink_ref[(kv * PAIRS_PER_KV + pp) * 2 + parity], sink)
        l = logits[parity * N_KEYS:(parity + 1) * N_KEYS, :]
        m = jnp.maximum(jnp.max(l, axis=0, keepdims=True), sink)
        p = jnp.exp(l - m)
        denom = jnp.sum(p, axis=0, keepdims=True) + jnp.exp(sink - m)
        p16 = p.astype(BF16)
        pv = None
        for v_t, mask in zip(vals_t, col_masks):
            v_kv = v_t[kv * B_HD:(kv + 1) * B_HD, :]
            lhs = jnp.concatenate([v_kv, zero_half] if parity == 0 else [zero_half, v_kv], axis=0)
            part = jnp.dot(lhs, p16, preferred_element_type=F32)
            pv = part if mask is None else jnp.where(mask, part, 0.0 if pv is None else pv)
        pv = pv / denom
        acc = pv if acc is None else acc + pv
    return acc.T


def _swa_prompt_kernel(bucket_ref, relb_ref, sink_ref, q_ref, z_ref, kv_ref, kvp_ref, ob_ref, bias_ref,
                       *, blocks):
    first = jnp.logical_and(pl.program_id(0) == 0, pl.program_id(1) == 0)

    @pl.when(first)
    def _():
        _build_bias(bucket_ref, relb_ref, bias_ref, WINDOW)

    key = lax.broadcasted_iota(jnp.int32, (2 * N_KEYS, PAIRS_PER_KV * WINDOW), 0) & (N_KEYS - 1)
    pen0 = jnp.where(jnp.logical_and(key < WINDOW, pl.program_id(1) == 0), NEG, 0.0)
    kw = B_KV_HEADS * B_HD
    for j in range(blocks):
        r0 = j * WINDOW
        if j == 0:
            kk = jnp.concatenate([kvp_ref[:, 0:kw], kv_ref[0:WINDOW, 0:kw]], axis=0)
            vv = jnp.concatenate([kvp_ref[:, kw:2 * kw], kv_ref[0:WINDOW, kw:2 * kw]], axis=0)
        else:
            kk = kv_ref[r0 - WINDOW:r0 + WINDOW, 0:kw]
            vv = kv_ref[r0 - WINDOW:r0 + WINDOW, kw:2 * kw]
        q_pairs = [q_ref[r0:r0 + WINDOW, p * 128:(p + 1) * 128] * (B_HD ** -0.5) for p in range(N_PAIRS)]
        keys = _swa_keys(kk)
        vals_t = [vv.T.astype(BF16)]
        outs = []
        for kv in range(B_KV_HEADS):
            q = jnp.concatenate(q_pairs[kv * PAIRS_PER_KV:(kv + 1) * PAIRS_PER_KV], axis=0)
            logits = _dot_nt(keys[kv], q) + bias_ref[kv]
            if j == 0:
                logits = logits + pen0
            acc_t = _swa_attend(logits, vals_t, [None], sink_ref, kv, WINDOW)
            outs += [acc_t[pp * WINDOW:(pp + 1) * WINDOW, :] for pp in range(PAIRS_PER_KV)]
        for p in range(N_PAIRS):
            z = z_ref[r0:r0 + WINDOW, p * 128:(p + 1) * 128].astype(F32)
            ob_ref[r0:r0 + WINDOW, p * 128:(p + 1) * 128] = (outs[p] * _silu(z)).astype(BF16)


def _swa_prompt(bucket, rel_bias, sink, pm, kv, batch, seq, blocks):
    tq = blocks * WINDOW
    nt = seq // tq
    kvw = kv.shape[1]
    smem = pl.BlockSpec(memory_space=pltpu.SMEM)
    return pl.pallas_call(
        functools.partial(_swa_prompt_kernel, blocks=blocks),
        grid=(batch, nt),
        in_specs=[
            _resident((N_KEYS, WINDOW)),
            smem, smem,
            pl.BlockSpec((tq, D_MODEL), lambda b, t: (b * nt + t, G_BQ)),
            pl.BlockSpec((tq, D_MODEL), lambda b, t: (b * nt + t, G_BZ)),
            pl.BlockSpec((tq, kvw), lambda b, t: (b * nt + t, 0)),
            pl.BlockSpec((WINDOW, kvw), lambda b, t: (jnp.maximum((b * nt + t) * blocks - 1, 0), 0)),
        ],
        out_specs=pl.BlockSpec((tq, D_MODEL), lambda b, t: (b * nt + t, 0)),
        out_shape=jax.ShapeDtypeStruct((batch * seq, D_MODEL), BF16),
        scratch_shapes=[pltpu.VMEM((B_KV_HEADS, 2 * N_KEYS, PAIRS_PER_KV * WINDOW), F32)],
        compiler_params=pltpu.CompilerParams(
            dimension_semantics=("arbitrary", "arbitrary"), vmem_limit_bytes=VMEM_LIMIT),
        name="swa_prompt",
    )(bucket, rel_bias, sink, pm, pm, kv, kv)


def _swa_sample_kernel(bucket_ref, relb_ref, sink_ref, q_ref, z_ref, kvn_ref, ck_ref, cv_ref,
                       ob_ref, nk_ref, nv_ref, bias_ref, *, bb, t_len):
    rows = bb * t_len
    width = PAIRS_PER_KV * rows

    @pl.when(pl.program_id(0) == 0)
    def _():
        _build_bias(bucket_ref, relb_ref, bias_ref, rows)

    kw = B_KV_HEADS * B_HD
    t_shift = t_len.bit_length() - 1
    row = lax.broadcasted_iota(jnp.int32, (rows, kw), 0)
    seq_of_col = lambda n: (lax.broadcasted_iota(jnp.int32, (n, width), 1) & (rows - 1)) >> t_shift
    seq_l = seq_of_col(2 * N_KEYS)
    seq_v = seq_of_col(2 * B_HD)
    q_pairs = [q_ref[:, p * 128:(p + 1) * 128] * (B_HD ** -0.5) for p in range(N_PAIRS)]
    q_kv = [jnp.concatenate(q_pairs[kv * PAIRS_PER_KV:(kv + 1) * PAIRS_PER_KV], axis=0)
            for kv in range(B_KV_HEADS)]
    kn = kvn_ref[:, 0:kw]
    vn = kvn_ref[:, kw:2 * kw]
    pad = jnp.zeros((N_KEYS - WINDOW - rows, kw), F32)
    logits = [None] * B_KV_HEADS
    vals_t, col_masks = [], []
    for j in range(bb):
        sel = row < t_len
        kj = jnp.where(sel, kn if j == 0 else pltpu.roll(kn, rows - j * t_len, axis=0), 0.0)
        vj = jnp.where(sel, vn if j == 0 else pltpu.roll(vn, rows - j * t_len, axis=0), 0.0)
        kk = jnp.concatenate([ck_ref[j], kj, pad], axis=0)
        vv = jnp.concatenate([cv_ref[j], vj, pad], axis=0)
        keys = _swa_keys(kk)
        for kv in range(B_KV_HEADS):
            lg = _dot_nt(keys[kv], q_kv[kv])
            logits[kv] = jnp.where(seq_l == j, lg, 0.0 if logits[kv] is None else logits[kv])
        vals_t.append(vv.T.astype(BF16))
        col_masks.append(seq_v == j)
        nk_ref[j] = pltpu.roll(kk, N_KEYS - t_len, axis=0)[0:WINDOW, :]
        nv_ref[j] = pltpu.roll(vv, N_KEYS - t_len, axis=0)[0:WINDOW, :]
    for kv in range(B_KV_HEADS):
        acc_t = _swa_attend(logits[kv] + bias_ref[kv], vals_t, col_masks, sink_ref, kv, rows)
        for pp in range(PAIRS_PER_KV):
            p = kv * PAIRS_PER_KV + pp
            z = z_ref[:, p * 128:(p + 1) * 128].astype(F32)
            ob_ref[:, p * 128:(p + 1) * 128] = (acc_t[pp * rows:(pp + 1) * rows, :] * _silu(z)).astype(BF16)


def _swa_sample(bucket, rel_bias, sink, pm, kvn, cache_k, cache_v, batch, t_len, bb):
    rows = bb * t_len
    kvw = kvn.shape[1]
    kw = kvw // 2
    smem = pl.BlockSpec(memory_space=pltpu.SMEM)
    cache_spec = pl.BlockSpec((bb, WINDOW, kw), lambda i: (i, 0, 0))
    return pl.pallas_call(
        functools.partial(_swa_sample_kernel, bb=bb, t_len=t_len),
        grid=(batch // bb,),
        in_specs=[
            _resident((N_KEYS, rows)),
            smem, smem,
            pl.BlockSpec((rows, D_MODEL), lambda i: (i, G_BQ)),
            pl.BlockSpec((rows, D_MODEL), lambda i: (i, G_BZ)),
            pl.BlockSpec((rows, kvw), lambda i: (i, 0)),
            cache_spec, cache_spec,
        ],
        out_specs=[pl.BlockSpec((rows, D_MODEL), lambda i: (i, 0)), cache_spec, cache_spec],
        out_shape=[
            jax.ShapeDtypeStruct((batch * t_len, D_MODEL), BF16),
            jax.ShapeDtypeStruct((batch, WINDOW, kw), F32),
            jax.ShapeDtypeStruct((batch, WINDOW, kw), F32),
        ],
        scratch_shapes=[pltpu.VMEM((B_KV_HEADS, 2 * N_KEYS, PAIRS_PER_KV * rows), F32)],
        compiler_params=pltpu.CompilerParams(
            dimension_semantics=("arbitrary",), vmem_limit_bytes=VMEM_LIMIT),
        name="swa_sample",
    )(bucket, rel_bias, sink, pm, pm, kvn, cache_k, cache_v)


MIX_TQ = 512
HGRN_SLAB = 256


class _HgrnConsts:
    def __init__(self, slab):
        shift = A_CHUNK.bit_length() - 1
        self.row = lax.broadcasted_iota(jnp.int32, (slab, A_DK), 0)
        rs = lax.broadcasted_iota(jnp.int32, (slab, slab), 0)
        cs = lax.broadcasted_iota(jnp.int32, (slab, slab), 1)
        self.same_chunk_causal = jnp.logical_and((rs >> shift) == (cs >> shift), cs <= rs)
        self.zero_row = jnp.zeros((1, A_DK), F32)
        self.zero_blk = jnp.zeros((A_SUB, A_DK), BF16)
        self.zero_chunk = jnp.zeros((A_CHUNK, A_DK), BF16)


def _hgrn_slab(f_pre, q_pre, v, lb, st, c):
    slab = f_pre.shape[0]
    n_sub = A_CHUNK // A_SUB
    sc = slab // A_CHUNK
    rep = lambda r, n: jnp.broadcast_to(r, (n, A_DK))
    fg = lb + (1.0 - lb) * _sigmoid(f_pre)
    logf = jnp.log(fg)
    k = 1.0 - fg
    b = _group_cumsum(logf, c.row, A_CHUNK, (1, 2, 4, 8, 16, 32))
    q = _silu(q_pre)

    ends = [b[A_SUB * j + A_SUB - 1:A_SUB * (j + 1), :] for j in range(sc * n_sub)]
    endrow = jnp.concatenate([rep(e, A_SUB) for e in ends], axis=0)
    prevrow = jnp.concatenate(
        [rep(c.zero_row if j % n_sub == 0 else ends[j - 1], A_SUB) for j in range(sc * n_sub)], axis=0)
    tot = [ends[n_sub * ch + n_sub - 1] for ch in range(sc)]
    lastrow = jnp.concatenate([rep(t, A_CHUNK) for t in tot], axis=0)
    kend = k * jnp.exp(endrow - b)
    qd = q * jnp.exp(b - prevrow)
    qs = qd * jnp.exp(prevrow)
    kd = kend * jnp.exp(lastrow - endrow)
    kdiag16 = (k * jnp.exp(prevrow - b)).astype(BF16)
    qd16 = qd.astype(BF16)
    kend16 = kend.astype(BF16)
    qs16 = qs.astype(BF16)
    kd16 = kd.astype(BF16)

    blk = lambda a, j: a[A_SUB * j:A_SUB * (j + 1), :]
    chk = lambda a, ch: a[A_CHUNK * ch:A_CHUNK * (ch + 1), :]

    q_groups, k_groups = [], []
    for i in range(n_sub):
        q_groups.append(jnp.concatenate(
            [blk(qd16, j) if j % n_sub == i else c.zero_blk for j in range(sc * n_sub)], axis=0))
        pieces = []
        for j in range(sc * n_sub):
            ch, jj = divmod(j, n_sub)
            if jj == i:
                pieces.append(blk(kdiag16, j))
            elif jj == i - 1:
                pieces.append(blk(kend16, j))
            elif jj < i:
                pieces.append((blk(kend, j) * jnp.exp(ends[ch * n_sub + i - 1] - ends[j])).astype(BF16))
            else:
                pieces.append(c.zero_blk)
        k_groups.append(jnp.concatenate(pieces, axis=0))
    att = _dot_nt(jnp.concatenate(q_groups, axis=1), jnp.concatenate(k_groups, axis=1))
    att = jnp.where(c.same_chunk_causal, att, 0.0)

    if sc > 1:
        q_groups, k_groups = [], []
        for ch in range(1, sc):
            q_groups.append(jnp.concatenate(
                [chk(qs16, c2) if c2 == ch else c.zero_chunk for c2 in range(sc)], axis=0))
            pieces = []
            for c2 in range(sc):
                if c2 == ch - 1:
                    pieces.append(chk(kd16, c2))
                elif c2 < ch:
                    pieces.append((chk(kd, c2) * jnp.exp(sum(tot[c2 + 1:ch]))).astype(BF16))
                else:
                    pieces.append(c.zero_chunk)
            k_groups.append(jnp.concatenate(pieces, axis=0))
        att = att + _dot_nt(jnp.concatenate(q_groups, axis=1), jnp.concatenate(k_groups, axis=1))
    o = jnp.dot(att.astype(BF16), v, preferred_element_type=F32)

    q0 = jnp.concatenate(
        [chk(qs16, 0)] + [(chk(qs, ch) * jnp.exp(sum(tot[:ch]))).astype(BF16) for ch in range(1, sc)], axis=0)
    o = o + _dot_nt(q0, st.astype(BF16))

    k1 = jnp.concatenate(
        [(chk(kd, ch) * jnp.exp(sum(tot[ch + 1:]))).astype(BF16) for ch in range(sc - 1)]
        + [chk(kd16, sc - 1)], axis=0)
    st = st * jnp.exp(sum(tot)) + _dot_tn(v, k1)
    return o, st


def _mix_prompt_kernel(bucket_ref, relb_ref, sink_ref, x_ref, gpre_ref, w_ref, lb_ref, gh_ref,
                       oa_ref, ob_ref, gab_ref, st_ref, kvwin_ref,
                       bias_ref, st_scr, kvprev_scr):
    tile = pl.program_id(1)
    at_start = tile == 0

    @pl.when(jnp.logical_and(pl.program_id(0) == 0, at_start))
    def _():
        _build_bias(bucket_ref, relb_ref, bias_ref, WINDOW)

    @pl.when(at_start)
    def _():
        st_scr[...] = jnp.zeros(st_scr.shape, F32)
        kvprev_scr[...] = jnp.zeros(kvprev_scr.shape, F32)

    x = x_ref[...]
    ms = jnp.mean(x * x, axis=-1, keepdims=True)
    u = (x * lax.rsqrt(ms + EPS) * gpre_ref[...]).astype(BF16)
    proj = lambda seg, lo, hi: jnp.dot(u, w_ref[:, IN_OFFS[seg] + lo:IN_OFFS[seg] + hi],
                                       preferred_element_type=F32)

    lb_all = _lower_bound(lb_ref)
    consts = _HgrnConsts(HGRN_SLAB)
    pair_w = 2 * A_DK
    for hp in range(A_HEADS // 2):
        lo, hi = hp * pair_w, (hp + 1) * pair_w
        q2, f2, v2, og2, z2 = (proj(seg, lo, hi) for seg in (0, SEG_AF, 2, 3, 4))
        for hh in range(2):
            h = 2 * hp + hh
            ls = slice(hh * A_DK, (hh + 1) * A_DK)
            hs = slice(h * A_DK, (h + 1) * A_DK)
            st = st_scr[h]
            outs = []
            for s0 in range(0, MIX_TQ, HGRN_SLAB):
                rs = slice(s0, s0 + HGRN_SLAB)
                o, st = _hgrn_slab(f2[rs, ls], q2[rs, ls], v2[rs, ls].astype(BF16), lb_all[:, hs], st, consts)
                outs.append(o)
            st_scr[h] = st
            st_ref[h] = st.T
            oa = _hgrn_finish(jnp.concatenate(outs, axis=0), gh_ref[:, hs], og2[:, ls], z2[:, ls])
            oa_ref[:, hs] = oa.astype(BF16)

    kw = B_KV_HEADS * B_HD
    kv = proj(SEG_BK, 0, 2 * kw)
    bq = (proj(5, 0, D_MODEL) * (B_HD ** -0.5)).astype(BF16)
    bz = proj(8, 0, D_MODEL)
    prev = kvprev_scr[...]
    key = lax.broadcasted_iota(jnp.int32, (2 * N_KEYS, PAIRS_PER_KV * WINDOW), 0) & (N_KEYS - 1)
    pen0 = jnp.where(jnp.logical_and(key < WINDOW, at_start), NEG, 0.0)
    for j in range(MIX_TQ // WINDOW):
        r0 = j * WINDOW
        kvj = jnp.concatenate([prev, kv[0:WINDOW, :]], axis=0) if j == 0 else kv[r0 - WINDOW:r0 + WINDOW, :]
        keys = _swa_keys(kvj[:, 0:kw])
        vals_t = [kvj[:, kw:2 * kw].T.astype(BF16)]
        for kvh in range(B_KV_HEADS):
            q = jnp.concatenate([bq[r0:r0 + WINDOW, p * 128:(p + 1) * 128]
                                 for p in range(kvh * PAIRS_PER_KV, (kvh + 1) * PAIRS_PER_KV)], axis=0)
            logits = _dot_nt(keys[kvh], q) + bias_ref[kvh]
            if j == 0:
                logits = logits + pen0
            acc_t = _swa_attend(logits, vals_t, [None], sink_ref, kvh, WINDOW)
            for pp in range(PAIRS_PER_KV):
                cs = slice((kvh * PAIRS_PER_KV + pp) * 128, (kvh * PAIRS_PER_KV + pp + 1) * 128)
                ob_ref[r0:r0 + WINDOW, cs] = (
                    acc_t[pp * WINDOW:(pp + 1) * WINDOW, :] * _silu(bz[r0:r0 + WINDOW, cs])).astype(BF16)
    kvprev_scr[...] = kv[MIX_TQ - WINDOW:, :]
    kvwin_ref[...] = kv[MIX_TQ - WINDOW:, :]

    gab_ref[:, 0:D_MODEL] = proj(9, 0, D_MODEL).astype(BF16)
    gab_ref[:, D_MODEL:2 * D_MODEL] = proj(10, 0, D_MODEL).astype(BF16)


def _mix_prompt(bucket, rel_bias, sink, x, g_pre, w, hgrn_lb, hgrn_norm, batch, seq):
    nt = seq // MIX_TQ
    n_lb = hgrn_lb.shape[0]
    kvw = 2 * B_KV_HEADS * B_HD
    smem = pl.BlockSpec(memory_space=pltpu.SMEM)
    tok = lambda width: pl.BlockSpec((MIX_TQ, width), lambda b, t: (b * nt + t, 0))
    return pl.pallas_call(
        _mix_prompt_kernel,
        grid=(batch, nt),
        in_specs=[
            _resident((N_KEYS, WINDOW)),
            smem, smem,
            tok(D_MODEL),
            _resident((1, D_MODEL)),
            _resident(w.shape),
            _resident((n_lb, D_MODEL)),
            _resident((1, D_MODEL)),
        ],
        out_specs=[
            tok(D_MODEL), tok(D_MODEL), tok(2 * D_MODEL),
            pl.BlockSpec((None, A_HEADS, A_DK, A_DV), lambda b, t: (b, 0, 0, 0)),
            pl.BlockSpec((None, WINDOW, kvw), lambda b, t: (b, 0, 0)),
        ],
        out_shape=[
            jax.ShapeDtypeStruct((batch * seq, D_MODEL), BF16),
            jax.ShapeDtypeStruct((batch * seq, D_MODEL), BF16),
            jax.ShapeDtypeStruct((batch * seq, 2 * D_MODEL), BF16),
            jax.ShapeDtypeStruct((batch, A_HEADS, A_DK, A_DV), F32),
            jax.ShapeDtypeStruct((batch, WINDOW, kvw), F32),
        ],
        scratch_shapes=[
            pltpu.VMEM((B_KV_HEADS, 2 * N_KEYS, PAIRS_PER_KV * WINDOW), F32),
            pltpu.VMEM((A_HEADS, A_DV, A_DK), F32),
            pltpu.VMEM((WINDOW, kvw), F32),
        ],
        compiler_params=pltpu.CompilerParams(
            dimension_semantics=("arbitrary", "arbitrary"), vmem_limit_bytes=VMEM_LIMIT),
        name="mix_prompt",
    )(bucket, rel_bias, sink, x, g_pre, w, hgrn_lb, hgrn_norm)


def _outproj_kernel(oa_ref, ob_ref, ga_ref, gb_ref, x_ref, p_ref, wpa_ref, wpb_ref, wo_ref, gpost_ref,
                    wple_ref, wg_ref, y_ref):
    a = jnp.dot(oa_ref[...], wpa_ref[...], preferred_element_type=F32)
    b = jnp.dot(ob_ref[...], wpb_ref[...], preferred_element_type=F32)
    m = _sigmoid(ga_ref[...].astype(F32)) * a + _sigmoid(gb_ref[...].astype(F32)) * b
    y = jnp.dot(m.astype(BF16), wo_ref[...], preferred_element_type=F32)
    y = y * lax.rsqrt(jnp.mean(y * y, axis=-1, keepdims=True) + EPS) * gpost_ref[...]
    x1 = x_ref[...] + y
    gate = _sigmoid(jnp.dot(x1.astype(BF16), wg_ref[...], preferred_element_type=F32))
    e = jnp.dot(p_ref[...].astype(BF16), wple_ref[...], preferred_element_type=F32) * gate
    y_ref[...] = x1 + e


def _outproj(oa, ob, gates, ga_col, gb_col, x, p, wpa, wpb, wo, gpost, wple, wg, tm):
    n = x.shape[0]
    ple = p.shape[1]
    tok = lambda w, c=0: pl.BlockSpec((tm, w), lambda i, c=c: (i, c))
    return pl.pallas_call(
        _outproj_kernel,
        grid=(n // tm,),
        in_specs=[
            tok(D_MODEL), tok(D_MODEL), tok(D_MODEL, ga_col), tok(D_MODEL, gb_col), tok(D_MODEL), tok(ple),
            _resident((D_MODEL, D_MODEL)), _resident((D_MODEL, D_MODEL)), _resident((D_MODEL, D_MODEL)),
            _resident((1, D_MODEL)), _resident((ple, D_MODEL)), _resident((D_MODEL, D_MODEL)),
        ],
        out_specs=tok(D_MODEL),
        out_shape=jax.ShapeDtypeStruct((n, D_MODEL), F32),
        compiler_params=pltpu.CompilerParams(
            dimension_semantics=("arbitrary",), vmem_limit_bytes=VMEM_LIMIT),
        name="outproj",
    )(oa, ob, gates, gates, x, p, wpa, wpb, wo, gpost, wple, wg)


def _rel_bucket(rel):
    n = jnp.maximum(rel, 0)
    max_exact = REL_BUCKETS // 2
    nf = jnp.maximum(n, 1).astype(F32)
    large = max_exact + (jnp.log(nf / max_exact) / jnp.log(jnp.float32(REL_MAX_DIST / max_exact))
                         * (REL_BUCKETS - max_exact)).astype(jnp.int32)
    large = jnp.minimum(large, REL_BUCKETS - 1)
    return jnp.where(n < max_exact, n, large)


def _bucket_table(q_pos, k_pos, k_valid):
    rel = q_pos[:, None] - k_pos[None, :]
    ok = (rel >= 0) & (rel < WINDOW) & k_valid[None, :]
    return jnp.where(ok, _rel_bucket(rel), -1).astype(jnp.int32)


TM_PROJ = 512
SWA_BLOCKS = 4
SAMPLE_BB = 4
SWA_SAMPLE_BB = 8


def _layer(xp, xs, s_hgrn, win_k, win_v, pp, ps, norm_pre, w_in, hgrn_lb, hgrn_norm, attn_sink,
           rel_bias, w_pa, w_pb, w_o, norm_post, w_ple, w_ple_gate):
    batch, seq, _ = xp.shape
    dbatch, t_len, _ = xs.shape
    kw = B_KV_HEADS * B_HD

    w16 = w_in.astype(BF16)
    g_pre = norm_pre.reshape(1, D_MODEL)
    g_post = norm_post.reshape(1, D_MODEL)
    g_hgrn = hgrn_norm.reshape(1, D_MODEL)
    wpa, wpb, wo = w_pa.astype(BF16), w_pb.astype(BF16), w_o.astype(BF16)
    wple, wg = w_ple.astype(BF16), w_ple_gate.astype(BF16)

    xp2 = xp.reshape(batch * seq, D_MODEL)
    xs2 = xs.reshape(dbatch * t_len, D_MODEL)
    k_all = jnp.arange(N_KEYS)
    bucket_p = _bucket_table(jnp.arange(WINDOW) + WINDOW, k_all, jnp.ones((N_KEYS,), bool)).T
    rows_s = SWA_SAMPLE_BB * t_len
    bucket_s = _bucket_table(WINDOW + jnp.arange(rows_s) % t_len, k_all, k_all < WINDOW + t_len).T

    oa_p, ob_p, gab_p, st_p, kvwin_p = _mix_prompt(bucket_p, rel_bias, attn_sink, xp2, g_pre, w16,
                                                   hgrn_lb, g_hgrn, batch, seq)

    pm_s, f_s, kv_s = _inproj(xs2, g_pre, w16, min(TM_PROJ, dbatch * t_len))
    oa_s, st_s = _hgrn_sample(pm_s, f_s, hgrn_lb, g_hgrn, s_hgrn, dbatch, t_len, SAMPLE_BB)
    ob_s, nk_s, nv_s = _swa_sample(bucket_s, rel_bias, attn_sink, pm_s, kv_s,
                                   win_k.reshape(dbatch, WINDOW, kw), win_v.reshape(dbatch, WINDOW, kw),
                                   dbatch, t_len, SWA_SAMPLE_BB)

    y_p = _outproj(oa_p, ob_p, gab_p, 0, 1, xp2, pp.reshape(batch * seq, -1),
                   wpa, wpb, wo, g_post, wple, wg, TM_PROJ)
    y_s = _outproj(oa_s, ob_s, pm_s, G_GA, G_GB, xs2, ps.reshape(dbatch * t_len, -1),
                   wpa, wpb, wo, g_post, wple, wg, min(TM_PROJ, dbatch * t_len))

    k_win_p = kvwin_p[:, :, 0:kw].reshape(batch, WINDOW, B_KV_HEADS, B_HD)
    v_win_p = kvwin_p[:, :, kw:].reshape(batch, WINDOW, B_KV_HEADS, B_HD)
    return (y_p.reshape(batch, seq, D_MODEL), y_s.reshape(dbatch, t_len, D_MODEL), st_p, st_s,
            k_win_p, v_win_p,
            nk_s.reshape(dbatch, WINDOW, B_KV_HEADS, B_HD), nv_s.reshape(dbatch, WINDOW, B_KV_HEADS, B_HD))


def kernel(x_prompt, x_sample, state_hgrn, cache_swa_k, cache_swa_v, p_prompt, p_sample, norm_pre, w_in,
           hgrn_lb, hgrn_norm, attn_sink, rel_bias, w_pa, w_pb, w_o, norm_post, w_ple, w_ple_gate):
    depth = w_in.shape[0]
    assert depth == 1, "the forget-gate lower bound is implemented for a single layer"
    xp, xs = x_prompt, x_sample
    outs = []
    for l in range(depth):
        res = _layer(xp, xs, state_hgrn[l], cache_swa_k[l], cache_swa_v[l], p_prompt[l], p_sample[l],
                     norm_pre[l], w_in[l], hgrn_lb, hgrn_norm[l], attn_sink[l], rel_bias,
                     w_pa[l], w_pb[l], w_o[l], norm_post[l], w_ple[l], w_ple_gate[l])
        xp, xs = res[0], res[1]
        outs.append(res[2:])
    stack = lambda i: jnp.stack([o[i] for o in outs])
    return (xp, xs, stack(0), stack(1), stack(2), stack(3), stack(4), stack(5))
```

```python
import functools
import math

import jax
import jax.numpy as jnp
import numpy as np
from jax import lax
from jax.experimental import pallas as pl
from jax.experimental.pallas import tpu as pltpu

F32 = jnp.float32
BF16 = jnp.bfloat16

D_MODEL = 1024
A_HEADS = 8
A_DK = 128
A_DV = 128
A_CHUNK = 64
A_SUB = 16
B_HEADS = 16
B_KV_HEADS = 2
B_HD = 64
WINDOW = 128
REL_BUCKETS = 32
REL_MAX_DIST = 128
EPS = 1e-6
NEG = float("-inf")

G_AQ, G_AI, G_AOG, G_AZ, G_BQ, G_BZ, G_GA, G_GB = range(8)
N_GROUPS = 8

VMEM_LIMIT = 56 * 1024 * 1024


def _sigmoid(x):
    return 1.0 / (1.0 + jnp.exp(-x))


def _silu(x):
    return x * _sigmoid(x)


def _resident(shape):
    nd = len(shape)
    return pl.BlockSpec(shape, lambda *_: (0,) * nd, pipeline_mode=pl.Buffered(1))


IN_OFFS = (0, 1024, 2048, 3072, 4096, 5120, 6144, 6272, 6400, 7424, 8448, 9472)
SLAB_SEGS = (0, 2, 3, 4, 5, 8, 9, 10)
SEG_AF, SEG_BK, SEG_BV = 1, 6, 7


def _inproj_kernel(x_ref, g_ref, w_ref, pm_ref, f_ref, kv_ref):
    x = x_ref[...]
    ms = jnp.mean(x * x, axis=-1, keepdims=True)
    u = (x * lax.rsqrt(ms + EPS) * g_ref[...]).astype(BF16)
    proj = lambda lo, hi: jnp.dot(u, w_ref[:, lo:hi], preferred_element_type=F32)
    for c, seg in enumerate(SLAB_SEGS):
        pm_ref[:, c * D_MODEL:(c + 1) * D_MODEL] = proj(IN_OFFS[seg], IN_OFFS[seg + 1]).astype(BF16)
    f_ref[...] = proj(IN_OFFS[SEG_AF], IN_OFFS[SEG_AF + 1])
    kv_ref[...] = proj(IN_OFFS[SEG_BK], IN_OFFS[SEG_BV + 1])


def _inproj(x, g, w, tm):
    n = x.shape[0]
    nm = N_GROUPS * D_MODEL
    nkv = IN_OFFS[SEG_BV + 1] - IN_OFFS[SEG_BK]
    return pl.pallas_call(
        _inproj_kernel,
        grid=(n // tm,),
        in_specs=[
            pl.BlockSpec((tm, D_MODEL), lambda i: (i, 0)),
            _resident((1, D_MODEL)),
            _resident(w.shape),
        ],
        out_specs=[
            pl.BlockSpec((tm, nm), lambda i: (i, 0)),
            pl.BlockSpec((tm, D_MODEL), lambda i: (i, 0)),
            pl.BlockSpec((tm, nkv), lambda i: (i, 0)),
        ],
        out_shape=[
            jax.ShapeDtypeStruct((n, nm), BF16),
            jax.ShapeDtypeStruct((n, D_MODEL), F32),
            jax.ShapeDtypeStruct((n, nkv), F32),
        ],
        compiler_params=pltpu.CompilerParams(
            dimension_semantics=("arbitrary",), vmem_limit_bytes=VMEM_LIMIT),
        name="inproj",
    )(x, g, w)


def _lower_bound(lb_ref):
    l = lb_ref[...]
    m = jnp.max(l, axis=0, keepdims=True)
    e = jnp.exp(l - m)
    return e[0:1, :] / jnp.sum(e, axis=0, keepdims=True)


def _group_cumsum(x, row, period, shifts):
    pos = row & (period - 1)
    for sh in shifts:
        x = x + jnp.where(pos >= sh, pltpu.roll(x, sh, axis=0), 0.0)
    return x


def _hgrn_finish(o, g, og_pre, z_pre):
    o = o * lax.rsqrt(jnp.mean(o * o, axis=-1, keepdims=True) + EPS)
    return o * g * _sigmoid(og_pre) * _silu(z_pre)


def _dot_nt(a, b):
    return lax.dot_general(a, b, (((1,), (1,)), ((), ())), preferred_element_type=F32)


def _dot_tn(a, b):
    return lax.dot_general(a, b, (((0,), (0,)), ((), ())), preferred_element_type=F32)


def _hgrn_sample_kernel(q_ref, v_ref, og_ref, z_ref, f_ref, lb_ref, g_ref, s0_ref, oa_ref, s_ref,
                        *, bb, t_len):
    rows = bb * t_len
    lb_all = _lower_bound(lb_ref)
    row = lax.broadcasted_iota(jnp.int32, (rows, A_DK), 0)
    pos = row & (t_len - 1)
    grp = row >> (t_len.bit_length() - 1)
    shifts = tuple(1 << i for i in range((t_len - 1).bit_length()))
    for h in range(A_HEADS):
        ls = slice(h * A_DK, (h + 1) * A_DK)
        lb = lb_all[:, ls]
        fg = lb + (1.0 - lb) * _sigmoid(f_ref[:, ls])
        logf = jnp.log(fg)
        k = 1.0 - fg
        b = _group_cumsum(logf, row, t_len, shifts)
        q = _silu(q_ref[:, ls].astype(F32))
        v = v_ref[:, ls].astype(F32)

        o = jnp.zeros((rows, A_DV), F32)
        for d in range(t_len):
            if d == 0:
                a = jnp.sum(q * k, axis=-1, keepdims=True)
                o = o + a * v
            else:
                w = jnp.exp(jnp.where(pos >= d, b - pltpu.roll(b, d, axis=0), 0.0))
                a = jnp.sum(q * pltpu.roll(k, d, axis=0) * w, axis=-1, keepdims=True)
                o = o + jnp.where(pos >= d, a * pltpu.roll(v, d, axis=0), 0.0)

        b_last = b
        for d in range(1, t_len):
            b_last = jnp.where(pos == t_len - 1 - d, pltpu.roll(b, rows - d, axis=0), b_last)
        e_last = jnp.exp(b_last)
        qs = q * jnp.exp(b)
        kd = k * jnp.exp(b_last - b)
        v16 = v.astype(BF16)
        for j in range(bb):
            s0 = s0_ref[j, h]
            mine = grp == j
            o = o + jnp.dot(jnp.where(mine, qs, 0.0).astype(BF16), s0.astype(BF16),
                            preferred_element_type=F32)
            u = _dot_tn(jnp.where(mine, kd, 0.0).astype(BF16), v16)
            e_row = e_last[j * t_len:j * t_len + 1, :]
            e_col = jnp.broadcast_to(e_row, (A_DK, A_DK)).T
            s_ref[j, h] = e_col * s0 + u

        oa = _hgrn_finish(o, g_ref[:, ls], og_ref[:, ls].astype(F32), z_ref[:, ls].astype(F32))
        oa_ref[:, ls] = oa.astype(BF16)


def _hgrn_sample(pm, f, hgrn_lb, hgrn_norm, s0, batch, t_len, bb):
    rows = bb * t_len
    col = lambda grp: pl.BlockSpec((rows, D_MODEL), lambda i, grp=grp: (i, grp))
    n_lb = hgrn_lb.shape[0]
    st_spec = pl.BlockSpec((bb, A_HEADS, A_DK, A_DV), lambda i: (i, 0, 0, 0))
    return pl.pallas_call(
        functools.partial(_hgrn_sample_kernel, bb=bb, t_len=t_len),
        grid=(batch // bb,),
        in_specs=[
            col(G_AQ), col(G_AI), col(G_AOG), col(G_AZ),
            pl.BlockSpec((rows, D_MODEL), lambda i: (i, 0)),
            _resident((n_lb, D_MODEL)),
            _resident((1, D_MODEL)),
            st_spec,
        ],
        out_specs=[pl.BlockSpec((rows, D_MODEL), lambda i: (i, 0)), st_spec],
        out_shape=[
            jax.ShapeDtypeStruct((batch * t_len, D_MODEL), BF16),
            jax.ShapeDtypeStruct((batch, A_HEADS, A_DK, A_DV), F32),
        ],
        compiler_params=pltpu.CompilerParams(
            dimension_semantics=("arbitrary",), vmem_limit_bytes=VMEM_LIMIT),
        name="hgrn_sample",
    )(pm, pm, pm, pm, f, hgrn_lb, hgrn_norm, s0)


N_PAIRS = B_HEADS // 2
PAIRS_PER_KV = N_PAIRS // B_KV_HEADS
N_KEYS = 2 * WINDOW


def _build_bias(bucket_ref, relb_ref, bias_ref, r):
    bucket = bucket_ref[...]
    for h in range(B_HEADS):
        def body(kb, acc, h=h):
            return jnp.where(bucket == kb, relb_ref[kb, h], acc)
        tab = lax.fori_loop(0, REL_BUCKETS, body, jnp.full(bucket.shape, NEG, F32))
        pair, parity = divmod(h, 2)
        kv, pp = divmod(pair, PAIRS_PER_KV)
        bias_ref[kv, parity * N_KEYS:(parity + 1) * N_KEYS, pp * r:(pp + 1) * r] = tab


def _swa_keys(kk):
    lo = lax.broadcasted_iota(jnp.int32, (N_KEYS, 2 * B_HD), 1) < B_HD
    kk_sw = pltpu.roll(kk, B_HD, axis=1)
    slabs = []
    for kv in range(B_KV_HEADS):
        k_lo, k_hi = (kk, kk_sw) if kv == 0 else (kk_sw, kk)
        slabs.append(jnp.concatenate([jnp.where(lo, k_lo, 0.0), jnp.where(lo, 0.0, k_hi)], axis=0).astype(BF16))
    return slabs


def _swa_attend(logits, vals_t, col_masks, sink_ref, kv, r):
    width = PAIRS_PER_KV * r
    lane_pp = lax.broadcasted_iota(jnp.int32, (1, width), 1) >> (r.bit_length() - 1)
    zero_half = jnp.zeros((B_HD, N_KEYS), BF16)
    acc = None
    for parity in range(2):
        sink = jnp.zeros((1, width), F32)
        for pp in range(PAIRS_PER_KV):
            sink = jnp.where(lane_pp == pp, sink_ref[(kv * PAIRS_PER_KV + pp) * 2 + parity], sink)
        l = logits[parity * N_KEYS:(parity + 1) * N_KEYS, :]
        m = jnp.maximum(jnp.max(l, axis=0, keepdims=True), sink)
        p = jnp.exp(l - m)
        denom = jnp.sum(p, axis=0, keepdims=True) + jnp.exp(sink - m)
        p16 = p.astype(BF16)
        pv = None
        for v_t, mask in zip(vals_t, col_masks):
            v_kv = v_t[kv * B_HD:(kv + 1) * B_HD, :]
            lhs = jnp.concatenate([v_kv, zero_half] if parity == 0 else [zero_half, v_kv], axis=0)
            part = jnp.dot(lhs, p16, preferred_element_type=F32)
            pv = part if mask is None else jnp.where(mask, part, 0.0 if pv is None else pv)
        pv = pv / denom
        acc = pv if acc is None else acc + pv
    return acc.T


def _swa_sample_kernel(bucket_ref, relb_ref, sink_ref, q_ref, z_ref, kvn_ref, ck_ref, cv_ref,
                       ob_ref, nk_ref, nv_ref, bias_ref, *, bb, t_len):
    rows = bb * t_len
    width = PAIRS_PER_KV * rows

    @pl.when(pl.program_id(0) == 0)
    def _():
        _build_bias(bucket_ref, relb_ref, bias_ref, rows)

    kw = B_KV_HEADS * B_HD
    t_shift = t_len.bit_length() - 1
    row = lax.broadcasted_iota(jnp.int32, (rows, kw), 0)
    seq_of_col = lambda n: (lax.broadcasted_iota(jnp.int32, (n, width), 1) & (rows - 1)) >> t_shift
    seq_l = seq_of_col(2 * N_KEYS)
    seq_v = seq_of_col(2 * B_HD)
    q_pairs = [q_ref[:, p * 128:(p + 1) * 128] * (B_HD ** -0.5) for p in range(N_PAIRS)]
    q_kv = [jnp.concatenate(q_pairs[kv * PAIRS_PER_KV:(kv + 1) * PAIRS_PER_KV], axis=0)
            for kv in range(B_KV_HEADS)]
    kn = kvn_ref[:, 0:kw]
    vn = kvn_ref[:, kw:2 * kw]
    pad = jnp.zeros((N_KEYS - WINDOW - rows, kw), F32)
    logits = [None] * B_KV_HEADS
    vals_t, col_masks = [], []
    for j in range(bb):
        sel = row < t_len
        kj = jnp.where(sel, kn if j == 0 else pltpu.roll(kn, rows - j * t_len, axis=0), 0.0)
        vj = jnp.where(sel, vn if j == 0 else pltpu.roll(vn, rows - j * t_len, axis=0), 0.0)
        kk = jnp.concatenate([ck_ref[j], kj, pad], axis=0)
        vv = jnp.concatenate([cv_ref[j], vj, pad], axis=0)
        keys = _swa_keys(kk)
        for kv in range(B_KV_HEADS):
            lg = _dot_nt(keys[kv], q_kv[kv])
            logits[kv] = jnp.where(seq_l == j, lg, 0.0 if logits[kv] is None else logits[kv])
        vals_t.append(vv.T.astype(BF16))
        col_masks.append(seq_v == j)
        nk_ref[j] = pltpu.roll(kk, N_KEYS - t_len, axis=0)[0:WINDOW, :]
        nv_ref[j] = pltpu.roll(vv, N_KEYS - t_len, axis=0)[0:WINDOW, :]
    for kv in range(B_KV_HEADS):
        acc_t = _swa_attend(logits[kv] + bias_ref[kv], vals_t, col_masks, sink_ref, kv, rows)
        for pp in range(PAIRS_PER_KV):
            p = kv * PAIRS_PER_KV + pp
            z = z_ref[:, p * 128:(p + 1) * 128].astype(F32)
            ob_ref[:, p * 128:(p + 1) * 128] = (acc_t[pp * rows:(pp + 1) * rows, :] * _silu(z)).astype(BF16)


def _swa_sample(bucket, rel_bias, sink, pm, kvn, cache_k, cache_v, batch, t_len, bb):
    rows = bb * t_len
    kvw = kvn.shape[1]
    kw = kvw // 2
    smem = pl.BlockSpec(memory_space=pltpu.SMEM)
    cache_spec = pl.BlockSpec((bb, WINDOW, kw), lambda i: (i, 0, 0))
    return pl.pallas_call(
        functools.partial(_swa_sample_kernel, bb=bb, t_len=t_len),
        grid=(batch // bb,),
        in_specs=[
            _resident((N_KEYS, rows)),
            smem, smem,
            pl.BlockSpec((rows, D_MODEL), lambda i: (i, G_BQ)),
            pl.BlockSpec((rows, D_MODEL), lambda i: (i, G_BZ)),
            pl.BlockSpec((rows, kvw), lambda i: (i, 0)),
            cache_spec, cache_spec,
        ],
        out_specs=[pl.BlockSpec((rows, D_MODEL), lambda i: (i, 0)), cache_spec, cache_spec],
        out_shape=[
            jax.ShapeDtypeStruct((batch * t_len, D_MODEL), BF16),
            jax.ShapeDtypeStruct((batch, WINDOW, kw), F32),
            jax.ShapeDtypeStruct((batch, WINDOW, kw), F32),
        ],
        scratch_shapes=[pltpu.VMEM((B_KV_HEADS, 2 * N_KEYS, PAIRS_PER_KV * rows), F32)],
        compiler_params=pltpu.CompilerParams(
            dimension_semantics=("arbitrary",), vmem_limit_bytes=VMEM_LIMIT),
        name="swa_sample",
    )(bucket, rel_bias, sink, pm, pm, kvn, cache_k, cache_v)


MIX_TQ = 512
HGRN_SLAB = 256


class _HgrnConsts:
    def __init__(self, slab):
        shift = A_CHUNK.bit_length() - 1
        self.row = lax.broadcasted_iota(jnp.int32, (slab, A_DK), 0)
        rs = lax.broadcasted_iota(jnp.int32, (slab, slab), 0)
        cs = lax.broadcasted_iota(jnp.int32, (slab, slab), 1)
        self.same_chunk_causal = jnp.logical_and((rs >> shift) == (cs >> shift), cs <= rs)
        self.zero_row = jnp.zeros((1, A_DK), F32)
        self.zero_blk = jnp.zeros((A_SUB, A_DK), BF16)
        self.zero_chunk = jnp.zeros((A_CHUNK, A_DK), BF16)


def _hgrn_slab(f_pre, q_pre, v, lb, st, c):
    slab = f_pre.shape[0]
    n_sub = A_CHUNK // A_SUB
    sc = slab // A_CHUNK
    rep = lambda r, n: jnp.broadcast_to(r, (n, A_DK))
    fg = lb + (1.0 - lb) * _sigmoid(f_pre)
    logf = jnp.log(fg)
    k = 1.0 - fg
    b = _group_cumsum(logf, c.row, A_CHUNK, (1, 2, 4, 8, 16, 32))
    q = _silu(q_pre)

    ends = [b[A_SUB * j + A_SUB - 1:A_SUB * (j + 1), :] for j in range(sc * n_sub)]
    endrow = jnp.concatenate([rep(e, A_SUB) for e in ends], axis=0)
    prevrow = jnp.concatenate(
        [rep(c.zero_row if j % n_sub == 0 else ends[j - 1], A_SUB) for j in range(sc * n_sub)], axis=0)
    tot = [ends[n_sub * ch + n_sub - 1] for ch in range(sc)]
    lastrow = jnp.concatenate([rep(t, A_CHUNK) for t in tot], axis=0)
    kend = k * jnp.exp(endrow - b)
    qd = q * jnp.exp(b - prevrow)
    qs = qd * jnp.exp(prevrow)
    kd = kend * jnp.exp(lastrow - endrow)
    kdiag16 = (k * jnp.exp(prevrow - b)).astype(BF16)
    qd16 = qd.astype(BF16)
    kend16 = kend.astype(BF16)
    qs16 = qs.astype(BF16)
    kd16 = kd.astype(BF16)

    blk = lambda a, j: a[A_SUB * j:A_SUB * (j + 1), :]
    chk = lambda a, ch: a[A_CHUNK * ch:A_CHUNK * (ch + 1), :]

    q_groups, k_groups = [], []
    for i in range(n_sub):
        q_groups.append(jnp.concatenate(
            [blk(qd16, j) if j % n_sub == i else c.zero_blk for j in range(sc * n_sub)], axis=0))
        pieces = []
        for j in range(sc * n_sub):
            ch, jj = divmod(j, n_sub)
            if jj == i:
                pieces.append(blk(kdiag16, j))
            elif jj == i - 1:
                pieces.append(blk(kend16, j))
            elif jj < i:
                pieces.append((blk(kend, j) * jnp.exp(ends[ch * n_sub + i - 1] - ends[j])).astype(BF16))
            else:
                pieces.append(c.zero_blk)
        k_groups.append(jnp.concatenate(pieces, axis=0))
    att = _dot_nt(jnp.concatenate(q_groups, axis=1), jnp.concatenate(k_groups, axis=1))
    att = jnp.where(c.same_chunk_causal, att, 0.0)

    if sc > 1:
        q_groups, k_groups = [], []
        for ch in range(1, sc):
            q_groups.append(jnp.concatenate(
                [chk(qs16, c2) if c2 == ch else c.zero_chunk for c2 in range(sc)], axis=0))
            pieces = []
            for c2 in range(sc):
                if c2 == ch - 1:
                    pieces.append(chk(kd16, c2))
                elif c2 < ch:
                    pieces.append((chk(kd, c2) * jnp.exp(sum(tot[c2 + 1:ch]))).astype(BF16))
                else:
                    pieces.append(c.zero_chunk)
            k_groups.append(jnp.concatenate(pieces, axis=0))
        att = att + _dot_nt(jnp.concatenate(q_groups, axis=1), jnp.concatenate(k_groups, axis=1))
    o = jnp.dot(att.astype(BF16), v, preferred_element_type=F32)

    q0 = jnp.concatenate(
        [chk(qs16, 0)] + [(chk(qs, ch) * jnp.exp(sum(tot[:ch]))).astype(BF16) for ch in range(1, sc)], axis=0)
    o = o + _dot_nt(q0, st.astype(BF16))

    k1 = jnp.concatenate(
        [(chk(kd, ch) * jnp.exp(sum(tot[ch + 1:]))).astype(BF16) for ch in range(sc - 1)]
        + [chk(kd16, sc - 1)], axis=0)
    st = st * jnp.exp(sum(tot)) + _dot_tn(v, k1)
    return o, st


def _hgrn_blockwise(stage_ref, oraw_ref, ls, lb, st):
    n_blk = stage_ref.shape[1] // A_SUB
    row = lax.broadcasted_iota(jnp.int32, (A_SUB, A_DK), 0)
    shifts = tuple(1 << i for i in range((A_SUB - 1).bit_length()))

    def body(i, st):
        rows = pl.ds(pl.multiple_of(i * A_SUB, A_SUB), A_SUB)
        fg = lb + (1.0 - lb) * _sigmoid(stage_ref[1, rows, ls])
        k = 1.0 - fg
        b = _group_cumsum(jnp.log(fg), row, A_SUB, shifts)
        q = _silu(stage_ref[0, rows, ls])
        v = stage_ref[2, rows, ls]
        o = _dot_nt((q * jnp.exp(b)).astype(BF16), st.astype(BF16))
        o = o + jnp.sum(q * k, axis=-1, keepdims=True) * v
        for d in range(1, A_SUB):
            ok = row >= d
            w = jnp.exp(jnp.where(ok, b - pltpu.roll(b, d, axis=0), 0.0))
            a = jnp.sum(q * pltpu.roll(k, d, axis=0) * w, axis=-1, keepdims=True)
            o = o + jnp.where(ok, a * pltpu.roll(v, d, axis=0), 0.0)
        oraw_ref[rows, ls] = o
        b_end = b[A_SUB - 1:A_SUB, :]
        kd = (k * jnp.exp(b_end - b)).astype(BF16)
        return st * jnp.exp(b_end) + _dot_tn(v.astype(BF16), kd)

    return lax.fori_loop(0, n_blk, body, st)


LB_SAFE = math.exp(-80.0 / A_SUB)


def _mix_prompt_kernel(bucket_ref, relb_ref, sink_ref, x_ref, gpre_ref, w_ref, lb_ref, gh_ref,
                       oa_ref, ob_ref, gab_ref, st_ref, kvwin_ref,
                       bias_ref, st_scr, kvprev_scr, stage_scr, oraw_scr):
    tile = pl.program_id(1)
    at_start = tile == 0

    @pl.when(jnp.logical_and(pl.program_id(0) == 0, at_start))
    def _():
        _build_bias(bucket_ref, relb_ref, bias_ref, WINDOW)

    @pl.when(at_start)
    def _():
        st_scr[...] = jnp.zeros(st_scr.shape, F32)
        kvprev_scr[...] = jnp.zeros(kvprev_scr.shape, F32)

    x = x_ref[...]
    ms = jnp.mean(x * x, axis=-1, keepdims=True)
    u = (x * lax.rsqrt(ms + EPS) * gpre_ref[...]).astype(BF16)
    proj = lambda seg, lo, hi: jnp.dot(u, w_ref[:, IN_OFFS[seg] + lo:IN_OFFS[seg] + hi],
                                       preferred_element_type=F32)

    lb_all = _lower_bound(lb_ref)
    consts = _HgrnConsts(HGRN_SLAB)
    pair_w = 2 * A_DK
    n_phase = A_HEADS // 2
    n_blocks = MIX_TQ // WINDOW
    assert n_blocks == n_phase
    gate_w = D_MODEL // n_phase
    kw = B_KV_HEADS * B_HD

    def hgrn_proj(hp):
        lo, hi = hp * pair_w, (hp + 1) * pair_w
        return tuple(proj(seg, lo, hi) for seg in (0, SEG_AF, 2, 3, 4))

    def hgrn_pair(hp, q2, f2, v2, og2, z2):
        for hh in range(2):
            h = 2 * hp + hh
            ls = slice(hh * A_DK, (hh + 1) * A_DK)
            hs = slice(h * A_DK, (h + 1) * A_DK)
            st = st_scr[h]
            outs = []
            for s0 in range(0, MIX_TQ, HGRN_SLAB):
                rs = slice(s0, s0 + HGRN_SLAB)
                o, st = _hgrn_slab(f2[rs, ls], q2[rs, ls], v2[rs, ls].astype(BF16), lb_all[:, hs], st, consts)
                outs.append(o)
            st_scr[h] = st
            st_ref[h] = st.T
            oa = _hgrn_finish(jnp.concatenate(outs, axis=0), gh_ref[:, hs], og2[:, ls], z2[:, ls])
            oa_ref[:, hs] = oa.astype(BF16)

    key = lax.broadcasted_iota(jnp.int32, (2 * N_KEYS, PAIRS_PER_KV * WINDOW), 0) & (N_KEYS - 1)
    no_prev = jnp.logical_and(key < WINDOW, at_start)

    def swa_block(j, kv, bq, bz):
        r0 = j * WINDOW
        if j == 0:
            kvj = jnp.concatenate([kvprev_scr[...], kv[0:WINDOW, :]], axis=0)
        else:
            kvj = kv[r0 - WINDOW:r0 + WINDOW, :]
        keys = _swa_keys(kvj[:, 0:kw])
        vals_t = [kvj[:, kw:2 * kw].T.astype(BF16)]
        for kvh in range(B_KV_HEADS):
            q = jnp.concatenate([bq[r0:r0 + WINDOW, p * 128:(p + 1) * 128]
                                 for p in range(kvh * PAIRS_PER_KV, (kvh + 1) * PAIRS_PER_KV)], axis=0)
            logits = _dot_nt(keys[kvh], q) + bias_ref[kvh]
            if j == 0:
                logits = jnp.where(no_prev, NEG, logits)
            acc_t = _swa_attend(logits, vals_t, [None], sink_ref, kvh, WINDOW)
            for pp in range(PAIRS_PER_KV):
                cs = slice((kvh * PAIRS_PER_KV + pp) * 128, (kvh * PAIRS_PER_KV + pp + 1) * 128)
                ob_ref[r0:r0 + WINDOW, cs] = (
                    acc_t[pp * WINDOW:(pp + 1) * WINDOW, :] * _silu(bz[r0:r0 + WINDOW, cs])).astype(BF16)

    def hgrn_pair_blockwise(hp, q2, f2, v2, og2, z2):
        stage_scr[0] = q2
        stage_scr[1] = f2
        stage_scr[2] = v2
        for hh in range(2):
            h = 2 * hp + hh
            ls = slice(hh * A_DK, (hh + 1) * A_DK)
            hs = slice(h * A_DK, (h + 1) * A_DK)
            st = _hgrn_blockwise(stage_scr, oraw_scr, ls, lb_all[:, hs], st_scr[h])
            st_scr[h] = st
            st_ref[h] = st.T
            oa = _hgrn_finish(oraw_scr[:, ls], gh_ref[:, hs], og2[:, ls], z2[:, ls])
            oa_ref[:, hs] = oa.astype(BF16)

    def gate_proj(ph):
        for seg, base in ((9, 0), (10, D_MODEL)):
            gab_ref[:, base + ph * gate_w:base + (ph + 1) * gate_w] = proj(
                seg, ph * gate_w, (ph + 1) * gate_w).astype(BF16)

    def swa_proj():
        kv = proj(SEG_BK, 0, 2 * kw)
        bq = (proj(5, 0, D_MODEL) * (B_HD ** -0.5)).astype(BF16)
        bz = proj(8, 0, D_MODEL)
        return kv, bq, bz

    def keep_window(kv):
        kvprev_scr[...] = kv[MIX_TQ - WINDOW:, :]
        kvwin_ref[...] = kv[MIX_TQ - WINDOW:, :]

    slab_safe = jnp.min(lb_all) >= LB_SAFE

    @pl.when(slab_safe)
    def _():
        kv, bq, bz = swa_proj()
        nxt = hgrn_proj(0)
        for ph in range(n_phase):
            cur = nxt
            if ph + 1 < n_phase:
                nxt = hgrn_proj(ph + 1)
            gate_proj(ph)
            hgrn_pair(ph, *cur)
            swa_block(ph, kv, bq, bz)
        keep_window(kv)

    @pl.when(jnp.logical_not(slab_safe))
    def _():
        for ph in range(n_phase):
            hgrn_pair_blockwise(ph, *hgrn_proj(ph))
            gate_proj(ph)
        kv, bq, bz = swa_proj()
        for j in range(n_blocks):
            swa_block(j, kv, bq, bz)
        keep_window(kv)


def _mix_prompt(bucket, rel_bias, sink, x, g_pre, w, hgrn_lb, hgrn_norm, batch, seq):
    nt = seq // MIX_TQ
    n_lb = hgrn_lb.shape[0]
    kvw = 2 * B_KV_HEADS * B_HD
    smem = pl.BlockSpec(memory_space=pltpu.SMEM)
    tok = lambda width: pl.BlockSpec((MIX_TQ, width), lambda b, t: (b * nt + t, 0))
    return pl.pallas_call(
        _mix_prompt_kernel,
        grid=(batch, nt),
        in_specs=[
            _resident((N_KEYS, WINDOW)),
            smem, smem,
            tok(D_MODEL),
            _resident((1, D_MODEL)),
            _resident(w.shape),
            _resident((n_lb, D_MODEL)),
            _resident((1, D_MODEL)),
        ],
        out_specs=[
            tok(D_MODEL), tok(D_MODEL), tok(2 * D_MODEL),
            pl.BlockSpec((None, A_HEADS, A_DK, A_DV), lambda b, t: (b, 0, 0, 0)),
            pl.BlockSpec((None, WINDOW, kvw), lambda b, t: (b, 0, 0)),
        ],
        out_shape=[
            jax.ShapeDtypeStruct((batch * seq, D_MODEL), BF16),
            jax.ShapeDtypeStruct((batch * seq, D_MODEL), BF16),
            jax.ShapeDtypeStruct((batch * seq, 2 * D_MODEL), BF16),
            jax.ShapeDtypeStruct((batch, A_HEADS, A_DK, A_DV), F32),
            jax.ShapeDtypeStruct((batch, WINDOW, kvw), F32),
        ],
        scratch_shapes=[
            pltpu.VMEM((B_KV_HEADS, 2 * N_KEYS, PAIRS_PER_KV * WINDOW), F32),
            pltpu.VMEM((A_HEADS, A_DV, A_DK), F32),
            pltpu.VMEM((WINDOW, kvw), F32),
            pltpu.VMEM((3, MIX_TQ, 2 * A_DK), F32),
            pltpu.VMEM((MIX_TQ, 2 * A_DK), F32),
        ],
        compiler_params=pltpu.CompilerParams(
            dimension_semantics=("arbitrary", "arbitrary"), vmem_limit_bytes=VMEM_LIMIT),
        name="mix_prompt",
    )(bucket, rel_bias, sink, x, g_pre, w, hgrn_lb, hgrn_norm)


def _outproj_kernel(oa_ref, ob_ref, ga_ref, gb_ref, x_ref, p_ref, wpa_ref, wpb_ref, wo_ref, gpost_ref,
                    wple_ref, wg_ref, y_ref):
    a = jnp.dot(oa_ref[...], wpa_ref[...], preferred_element_type=F32)
    b = jnp.dot(ob_ref[...], wpb_ref[...], preferred_element_type=F32)
    m = _sigmoid(ga_ref[...].astype(F32)) * a + _sigmoid(gb_ref[...].astype(F32)) * b
    y = jnp.dot(m.astype(BF16), wo_ref[...], preferred_element_type=F32)
    y = y * lax.rsqrt(jnp.mean(y * y, axis=-1, keepdims=True) + EPS) * gpost_ref[...]
    x1 = x_ref[...] + y
    gate = _sigmoid(jnp.dot(x1.astype(BF16), wg_ref[...], preferred_element_type=F32))
    e = jnp.dot(p_ref[...].astype(BF16), wple_ref[...], preferred_element_type=F32) * gate
    y_ref[...] = x1 + e


def _outproj(oa, ob, gates, ga_col, gb_col, x, p, wpa, wpb, wo, gpost, wple, wg, tm):
    n = x.shape[0]
    ple = p.shape[1]
    tok = lambda w, c=0: pl.BlockSpec((tm, w), lambda i, c=c: (i, c))
    return pl.pallas_call(
        _outproj_kernel,
        grid=(n // tm,),
        in_specs=[
            tok(D_MODEL), tok(D_MODEL), tok(D_MODEL, ga_col), tok(D_MODEL, gb_col), tok(D_MODEL), tok(ple),
            _resident((D_MODEL, D_MODEL)), _resident((D_MODEL, D_MODEL)), _resident((D_MODEL, D_MODEL)),
            _resident((1, D_MODEL)), _resident((ple, D_MODEL)), _resident((D_MODEL, D_MODEL)),
        ],
        out_specs=tok(D_MODEL),
        out_shape=jax.ShapeDtypeStruct((n, D_MODEL), F32),
        compiler_params=pltpu.CompilerParams(
            dimension_semantics=("arbitrary",), vmem_limit_bytes=VMEM_LIMIT),
        name="outproj",
    )(oa, ob, gates, gates, x, p, wpa, wpb, wo, gpost, wple, wg)


def _rel_bucket(rel):
    n = np.maximum(rel, 0)
    max_exact = REL_BUCKETS // 2
    nf = np.maximum(n, 1).astype(np.float32)
    scaled = (np.log(nf / np.float32(max_exact)) / np.float32(math.log(REL_MAX_DIST / max_exact))
              * np.float32(REL_BUCKETS - max_exact))
    frac = scaled - np.floor(scaled)
    inside = (n > max_exact) & (n < REL_MAX_DIST)
    assert np.all((frac[inside] > 1e-3) & (frac[inside] < 1.0 - 1e-3))
    large = np.minimum(max_exact + scaled.astype(np.int32), REL_BUCKETS - 1)
    return np.where(n < max_exact, n, large)


def _bucket_table(q_pos, k_pos, k_valid):
    rel = q_pos[:, None] - k_pos[None, :]
    ok = (rel >= 0) & (rel < WINDOW) & k_valid[None, :]
    return jnp.asarray(np.where(ok, _rel_bucket(rel), -1).astype(np.int32).T)


TM_PROJ = 512
SAMPLE_BB = 4
SWA_SAMPLE_BB = 8


def _layer(xp, xs, s_hgrn, win_k, win_v, pp, ps, norm_pre, w_in, hgrn_lb, hgrn_norm, attn_sink,
           rel_bias, w_pa, w_pb, w_o, norm_post, w_ple, w_ple_gate):
    batch, seq, _ = xp.shape
    dbatch, t_len, _ = xs.shape
    kw = B_KV_HEADS * B_HD

    w16 = w_in.astype(BF16)
    g_pre = norm_pre.reshape(1, D_MODEL)
    g_post = norm_post.reshape(1, D_MODEL)
    g_hgrn = hgrn_norm.reshape(1, D_MODEL)
    wpa, wpb, wo = w_pa.astype(BF16), w_pb.astype(BF16), w_o.astype(BF16)
    wple, wg = w_ple.astype(BF16), w_ple_gate.astype(BF16)

    xp2 = xp.reshape(batch * seq, D_MODEL)
    xs2 = xs.reshape(dbatch * t_len, D_MODEL)
    k_all = np.arange(N_KEYS)
    bucket_p = _bucket_table(np.arange(WINDOW) + WINDOW, k_all, np.ones((N_KEYS,), bool))
    rows_s = SWA_SAMPLE_BB * t_len
    bucket_s = _bucket_table(WINDOW + np.arange(rows_s) % t_len, k_all, k_all < WINDOW + t_len)

    oa_p, ob_p, gab_p, st_p, kvwin_p = _mix_prompt(bucket_p, rel_bias, attn_sink, xp2, g_pre, w16,
                                                   hgrn_lb, g_hgrn, batch, seq)

    pm_s, f_s, kv_s = _inproj(xs2, g_pre, w16, min(TM_PROJ, dbatch * t_len))
    oa_s, st_s = _hgrn_sample(pm_s, f_s, hgrn_lb, g_hgrn, s_hgrn, dbatch, t_len, SAMPLE_BB)
    ob_s, nk_s, nv_s = _swa_sample(bucket_s, rel_bias, attn_sink, pm_s, kv_s,
                                   win_k.reshape(dbatch, WINDOW, kw), win_v.reshape(dbatch, WINDOW, kw),
                                   dbatch, t_len, SWA_SAMPLE_BB)

    y_p = _outproj(oa_p, ob_p, gab_p, 0, 1, xp2, pp.reshape(batch * seq, -1),
                   wpa, wpb, wo, g_post, wple, wg, TM_PROJ)
    y_s = _outproj(oa_s, ob_s, pm_s, G_GA, G_GB, xs2, ps.reshape(dbatch * t_len, -1),
                   wpa, wpb, wo, g_post, wple, wg, min(TM_PROJ, dbatch * t_len))

    k_win_p = kvwin_p[:, :, 0:kw].reshape(batch, WINDOW, B_KV_HEADS, B_HD)
    v_win_p = kvwin_p[:, :, kw:].reshape(batch, WINDOW, B_KV_HEADS, B_HD)
    return (y_p.reshape(batch, seq, D_MODEL), y_s.reshape(dbatch, t_len, D_MODEL), st_p, st_s,
            k_win_p, v_win_p,
            nk_s.reshape(dbatch, WINDOW, B_KV_HEADS, B_HD), nv_s.reshape(dbatch, WINDOW, B_KV_HEADS, B_HD))


def kernel(x_prompt, x_sample, state_hgrn, cache_swa_k, cache_swa_v, p_prompt, p_sample, norm_pre, w_in,
           hgrn_lb, hgrn_norm, attn_sink, rel_bias, w_pa, w_pb, w_o, norm_post, w_ple, w_ple_gate):
    depth = w_in.shape[0]
    assert depth == 1, "the forget-gate lower bound is implemented for a single layer"
    xp, xs = x_prompt, x_sample
    outs = []
    for l in range(depth):
        res = _layer(xp, xs, state_hgrn[l], cache_swa_k[l], cache_swa_v[l], p_prompt[l], p_sample[l],
                     norm_pre[l], w_in[l], hgrn_lb, hgrn_norm[l], attn_sink[l], rel_bias,
                     w_pa[l], w_pb[l], w_o[l], norm_post[l], w_ple[l], w_ple_gate[l])
        xp, xs = res[0], res[1]
        outs.append(res[2:])
    stack = lambda i: jnp.stack([o[i] for o in outs])
    return (xp, xs, stack(0), stack(1), stack(2), stack(3), stack(4), stack(5))
```

```python
import functools
import math

import jax
import jax.numpy as jnp
import numpy as np
from jax import lax
from jax.experimental import pallas as pl
from jax.experimental.pallas import tpu as pltpu

F32 = jnp.float32
BF16 = jnp.bfloat16

D_MODEL = 1024
A_HEADS = 8
A_DK = 128
A_DV = 128
A_CHUNK = 64
A_SUB = 16
B_HEADS = 16
B_KV_HEADS = 2
B_HD = 64
WINDOW = 128
REL_BUCKETS = 32
REL_MAX_DIST = 128
EPS = 1e-6
NEG = float("-inf")

G_AQ, G_AI, G_AOG, G_AZ, G_BQ, G_BZ, G_GA, G_GB = range(8)
N_GROUPS = 8

VMEM_LIMIT = 56 * 1024 * 1024


def _sigmoid(x):
    return 1.0 / (1.0 + jnp.exp(-x))


def _silu(x):
    return x * _sigmoid(x)


def _resident(shape):
    nd = len(shape)
    return pl.BlockSpec(shape, lambda *_: (0,) * nd, pipeline_mode=pl.Buffered(1))


IN_OFFS = (0, 1024, 2048, 3072, 4096, 5120, 6144, 6272, 6400, 7424, 8448, 9472)
SLAB_SEGS = (0, 2, 3, 4, 5, 8, 9, 10)
SEG_AF, SEG_BK, SEG_BV = 1, 6, 7


def _inproj_kernel(x_ref, g_ref, w_ref, pm_ref, f_ref, kv_ref):
    x = x_ref[...]
    ms = jnp.mean(x * x, axis=-1, keepdims=True)
    u = (x * lax.rsqrt(ms + EPS) * g_ref[...]).astype(BF16)
    proj = lambda lo, hi: jnp.dot(u, w_ref[:, lo:hi], preferred_element_type=F32)
    for c, seg in enumerate(SLAB_SEGS):
        pm_ref[:, c * D_MODEL:(c + 1) * D_MODEL] = proj(IN_OFFS[seg], IN_OFFS[seg + 1]).astype(BF16)
    f_ref[...] = proj(IN_OFFS[SEG_AF], IN_OFFS[SEG_AF + 1])
    kv_ref[...] = proj(IN_OFFS[SEG_BK], IN_OFFS[SEG_BV + 1])


def _inproj(x, g, w, tm):
    n = x.shape[0]
    nm = N_GROUPS * D_MODEL
    nkv = IN_OFFS[SEG_BV + 1] - IN_OFFS[SEG_BK]
    return pl.pallas_call(
        _inproj_kernel,
        grid=(n // tm,),
        in_specs=[
            pl.BlockSpec((tm, D_MODEL), lambda i: (i, 0)),
            _resident((1, D_MODEL)),
            _resident(w.shape),
        ],
        out_specs=[
            pl.BlockSpec((tm, nm), lambda i: (i, 0)),
            pl.BlockSpec((tm, D_MODEL), lambda i: (i, 0)),
            pl.BlockSpec((tm, nkv), lambda i: (i, 0)),
        ],
        out_shape=[
            jax.ShapeDtypeStruct((n, nm), BF16),
            jax.ShapeDtypeStruct((n, D_MODEL), F32),
            jax.ShapeDtypeStruct((n, nkv), F32),
        ],
        compiler_params=pltpu.CompilerParams(
            dimension_semantics=("arbitrary",), vmem_limit_bytes=VMEM_LIMIT),
        name="inproj",
    )(x, g, w)


def _lower_bound(lb_ref):
    l = lb_ref[...]
    m = jnp.max(l, axis=0, keepdims=True)
    e = jnp.exp(l - m)
    return e[0:1, :] / jnp.sum(e, axis=0, keepdims=True)


def _group_cumsum(x, row, period, shifts):
    pos = row & (period - 1)
    for sh in shifts:
        x = x + jnp.where(pos >= sh, pltpu.roll(x, sh, axis=0), 0.0)
    return x


def _hgrn_finish(o, g, og_pre, z_pre):
    o = o * lax.rsqrt(jnp.mean(o * o, axis=-1, keepdims=True) + EPS)
    return o * g * _sigmoid(og_pre) * _silu(z_pre)


def _dot_nt(a, b):
    return lax.dot_general(a, b, (((1,), (1,)), ((), ())), preferred_element_type=F32)


def _dot_tn(a, b):
    return lax.dot_general(a, b, (((0,), (0,)), ((), ())), preferred_element_type=F32)


def _hgrn_sample_kernel(q_ref, v_ref, og_ref, z_ref, f_ref, lb_ref, g_ref, s0_ref, oa_ref, s_ref,
                        *, bb, t_len):
    rows = bb * t_len
    lb_all = _lower_bound(lb_ref)
    row = lax.broadcasted_iota(jnp.int32, (rows, A_DK), 0)
    pos = row & (t_len - 1)
    grp = row >> (t_len.bit_length() - 1)
    shifts = tuple(1 << i for i in range((t_len - 1).bit_length()))
    n_tail = A_DK - rows
    sel_row = lax.broadcasted_iota(jnp.int32, (n_tail, 2 * bb * A_DV), 0)
    sel_blk = lax.broadcasted_iota(jnp.int32, (n_tail, 2 * bb * A_DV), 1) >> (A_DV.bit_length() - 1)
    picks = jnp.logical_and(sel_row < 2 * bb, sel_blk == bb + (sel_row & (bb - 1)))
    rhs_tail = jnp.where(picks, 1.0, 0.0).astype(BF16)
    lhs_pad = jnp.zeros((n_tail - 2 * bb, A_DK), F32)
    v_pad = jnp.zeros((rows, bb * A_DV), F32)
    for h in range(A_HEADS):
        ls = slice(h * A_DK, (h + 1) * A_DK)
        lb = lb_all[:, ls]
        fg = lb + (1.0 - lb) * _sigmoid(f_ref[:, ls])
        logf = jnp.log(fg)
        k = 1.0 - fg
        b = _group_cumsum(logf, row, t_len, shifts)
        q = _silu(q_ref[:, ls].astype(F32))
        v = v_ref[:, ls].astype(F32)

        o = jnp.zeros((rows, A_DV), F32)
        for d in range(t_len):
            if d == 0:
                a = jnp.sum(q * k, axis=-1, keepdims=True)
                o = o + a * v
            else:
                w = jnp.exp(jnp.where(pos >= d, b - pltpu.roll(b, d, axis=0), 0.0))
                a = jnp.sum(q * pltpu.roll(k, d, axis=0) * w, axis=-1, keepdims=True)
                o = o + jnp.where(pos >= d, a * pltpu.roll(v, d, axis=0), 0.0)

        b_last = b
        for d in range(1, t_len):
            b_last = jnp.where(pos == t_len - 1 - d, pltpu.roll(b, rows - d, axis=0), b_last)
        e_last = jnp.exp(b_last)
        qs = q * jnp.exp(b)
        kd = k * jnp.exp(b_last - b)
        e_rows = jnp.concatenate([e_last[j * t_len:j * t_len + 1, :] for j in range(bb)], axis=0)
        e_hi = e_rows.astype(BF16).astype(F32)
        lhs_t = jnp.concatenate([kd, e_hi, e_rows - e_hi, lhs_pad], axis=0).T.astype(BF16)
        v_rows = jnp.concatenate([jnp.where(grp == j, v, 0.0) for j in range(bb)] + [v_pad], axis=1)
        res = jnp.dot(lhs_t, jnp.concatenate([v_rows.astype(BF16), rhs_tail], axis=0),
                      preferred_element_type=F32)
        for j in range(bb):
            s0 = s0_ref[j, h]
            o = o + jnp.dot(jnp.where(grp == j, qs, 0.0).astype(BF16), s0.astype(BF16),
                            preferred_element_type=F32)
            s_ref[j, h] = (res[:, (bb + j) * A_DV:(bb + j + 1) * A_DV] * s0
                           + res[:, j * A_DV:(j + 1) * A_DV])

        oa = _hgrn_finish(o, g_ref[:, ls], og_ref[:, ls].astype(F32), z_ref[:, ls].astype(F32))
        oa_ref[:, ls] = oa.astype(BF16)


def _hgrn_sample(pm, f, hgrn_lb, hgrn_norm, s0, batch, t_len, bb):
    rows = bb * t_len
    col = lambda grp: pl.BlockSpec((rows, D_MODEL), lambda i, grp=grp: (i, grp))
    n_lb = hgrn_lb.shape[0]
    st_spec = pl.BlockSpec((bb, A_HEADS, A_DK, A_DV), lambda i: (i, 0, 0, 0))
    return pl.pallas_call(
        functools.partial(_hgrn_sample_kernel, bb=bb, t_len=t_len),
        grid=(batch // bb,),
        in_specs=[
            col(G_AQ), col(G_AI), col(G_AOG), col(G_AZ),
            pl.BlockSpec((rows, D_MODEL), lambda i: (i, 0)),
            _resident((n_lb, D_MODEL)),
            _resident((1, D_MODEL)),
            st_spec,
        ],
        out_specs=[pl.BlockSpec((rows, D_MODEL), lambda i: (i, 0)), st_spec],
        out_shape=[
            jax.ShapeDtypeStruct((batch * t_len, D_MODEL), BF16),
            jax.ShapeDtypeStruct((batch, A_HEADS, A_DK, A_DV), F32),
        ],
        compiler_params=pltpu.CompilerParams(
            dimension_semantics=("arbitrary",), vmem_limit_bytes=VMEM_LIMIT),
        name="hgrn_sample",
    )(pm, pm, pm, pm, f, hgrn_lb, hgrn_norm, s0)


N_PAIRS = B_HEADS // 2
PAIRS_PER_KV = N_PAIRS // B_KV_HEADS
N_KEYS = 2 * WINDOW


def _build_bias(bucket_ref, relb_ref, bias_ref, r):
    bucket = bucket_ref[...]
    for h in range(B_HEADS):
        def body(kb, acc, h=h):
            return jnp.where(bucket == kb, relb_ref[kb, h], acc)
        tab = lax.fori_loop(0, REL_BUCKETS, body, jnp.full(bucket.shape, NEG, F32))
        pair, parity = divmod(h, 2)
        kv, pp = divmod(pair, PAIRS_PER_KV)
        bias_ref[kv, parity * N_KEYS:(parity + 1) * N_KEYS, pp * r:(pp + 1) * r] = tab


def _swa_keys(kk):
    lo = lax.broadcasted_iota(jnp.int32, (N_KEYS, 2 * B_HD), 1) < B_HD
    kk_sw = pltpu.roll(kk, B_HD, axis=1)
    slabs = []
    for kv in range(B_KV_HEADS):
        k_lo, k_hi = (kk, kk_sw) if kv == 0 else (kk_sw, kk)
        slabs.append(jnp.concatenate([jnp.where(lo, k_lo, 0.0), jnp.where(lo, 0.0, k_hi)], axis=0).astype(BF16))
    return slabs


def _swa_attend(logits, vals_t, col_masks, sink_ref, kv, r):
    width = PAIRS_PER_KV * r
    lane_pp = lax.broadcasted_iota(jnp.int32, (1, width), 1) >> (r.bit_length() - 1)
    halves = []
    for parity in range(2):
        sink = jnp.zeros((1, width), F32)
        for pp in range(PAIRS_PER_KV):
            sink = jnp.where(lane_pp == pp, sink_ref[(kv * PAIRS_PER_KV + pp) * 2 + parity], sink)
        l = logits[parity * N_KEYS:(parity + 1) * N_KEYS, :]
        m = jnp.maximum(jnp.max(l, axis=0, keepdims=True), sink)
        p = jnp.exp(l - m)
        denom = jnp.sum(p, axis=0, keepdims=True) + jnp.exp(sink - m)
        p16 = p.astype(BF16)
        pv = None
        for v_t, mask in zip(vals_t, col_masks):
            part = jnp.dot(v_t[kv * B_HD:(kv + 1) * B_HD, :], p16, preferred_element_type=F32)
            pv = part if mask is None else jnp.where(mask, part, 0.0 if pv is None else pv)
        halves.append(pv / denom)
    return jnp.concatenate(halves, axis=0).T


def _swa_sample_kernel(bucket_ref, relb_ref, sink_ref, q_ref, z_ref, kvn_ref, ck_ref, cv_ref,
                       ob_ref, nk_ref, nv_ref, bias_ref, *, bb, t_len):
    rows = bb * t_len
    width = PAIRS_PER_KV * rows

    @pl.when(pl.program_id(0) == 0)
    def _():
        _build_bias(bucket_ref, relb_ref, bias_ref, rows)

    kw = B_KV_HEADS * B_HD
    t_shift = t_len.bit_length() - 1
    row = lax.broadcasted_iota(jnp.int32, (rows, kw), 0)
    seq_of_col = lambda n: (lax.broadcasted_iota(jnp.int32, (n, width), 1) & (rows - 1)) >> t_shift
    seq_l = seq_of_col(2 * N_KEYS)
    seq_v = seq_of_col(B_HD)
    q_pairs = [q_ref[:, p * 128:(p + 1) * 128] * (B_HD ** -0.5) for p in range(N_PAIRS)]
    q_kv = [jnp.concatenate(q_pairs[kv * PAIRS_PER_KV:(kv + 1) * PAIRS_PER_KV], axis=0)
            for kv in range(B_KV_HEADS)]
    kn = kvn_ref[:, 0:kw]
    vn = kvn_ref[:, kw:2 * kw]
    pad = jnp.zeros((N_KEYS - WINDOW - rows, kw), F32)
    logits = [None] * B_KV_HEADS
    vals_t, col_masks = [], []
    for j in range(bb):
        sel = row < t_len
        kj = jnp.where(sel, kn if j == 0 else pltpu.roll(kn, rows - j * t_len, axis=0), 0.0)
        vj = jnp.where(sel, vn if j == 0 else pltpu.roll(vn, rows - j * t_len, axis=0), 0.0)
        kk = jnp.concatenate([ck_ref[j], kj, pad], axis=0)
        vv = jnp.concatenate([cv_ref[j], vj, pad], axis=0)
        keys = _swa_keys(kk)
        for kv in range(B_KV_HEADS):
            lg = _dot_nt(keys[kv], q_kv[kv])
            logits[kv] = jnp.where(seq_l == j, lg, 0.0 if logits[kv] is None else logits[kv])
        vals_t.append(vv.T.astype(BF16))
        col_masks.append(seq_v == j)
        nk_ref[j] = pltpu.roll(kk, N_KEYS - t_len, axis=0)[0:WINDOW, :]
        nv_ref[j] = pltpu.roll(vv, N_KEYS - t_len, axis=0)[0:WINDOW, :]
    for kv in range(B_KV_HEADS):
        acc_t = _swa_attend(logits[kv] + bias_ref[kv], vals_t, col_masks, sink_ref, kv, rows)
        for pp in range(PAIRS_PER_KV):
            p = kv * PAIRS_PER_KV + pp
            z = z_ref[:, p * 128:(p + 1) * 128].astype(F32)
            ob_ref[:, p * 128:(p + 1) * 128] = (acc_t[pp * rows:(pp + 1) * rows, :] * _silu(z)).astype(BF16)


def _swa_sample(bucket, rel_bias, sink, pm, kvn, cache_k, cache_v, batch, t_len, bb):
    rows = bb * t_len
    kvw = kvn.shape[1]
    kw = kvw // 2
    smem = pl.BlockSpec(memory_space=pltpu.SMEM)
    cache_spec = pl.BlockSpec((bb, WINDOW, kw), lambda i: (i, 0, 0))
    return pl.pallas_call(
        functools.partial(_swa_sample_kernel, bb=bb, t_len=t_len),
        grid=(batch // bb,),
        in_specs=[
            _resident((N_KEYS, rows)),
            smem, smem,
            pl.BlockSpec((rows, D_MODEL), lambda i: (i, G_BQ)),
            pl.BlockSpec((rows, D_MODEL), lambda i: (i, G_BZ)),
            pl.BlockSpec((rows, kvw), lambda i: (i, 0)),
            cache_spec, cache_spec,
        ],
        out_specs=[pl.BlockSpec((rows, D_MODEL), lambda i: (i, 0)), cache_spec, cache_spec],
        out_shape=[
            jax.ShapeDtypeStruct((batch * t_len, D_MODEL), BF16),
            jax.ShapeDtypeStruct((batch, WINDOW, kw), F32),
            jax.ShapeDtypeStruct((batch, WINDOW, kw), F32),
        ],
        scratch_shapes=[pltpu.VMEM((B_KV_HEADS, 2 * N_KEYS, PAIRS_PER_KV * rows), F32)],
        compiler_params=pltpu.CompilerParams(
            dimension_semantics=("arbitrary",), vmem_limit_bytes=VMEM_LIMIT),
        name="swa_sample",
    )(bucket, rel_bias, sink, pm, pm, kvn, cache_k, cache_v)


MIX_TQ = 512
HGRN_SLAB = 256


class _HgrnConsts:
    def __init__(self, slab):
        shift = A_CHUNK.bit_length() - 1
        self.row = lax.broadcasted_iota(jnp.int32, (slab, A_DK), 0)
        rs = lax.broadcasted_iota(jnp.int32, (slab, slab), 0)
        cs = lax.broadcasted_iota(jnp.int32, (slab, slab), 1)
        self.same_chunk_causal = jnp.logical_and((rs >> shift) == (cs >> shift), cs <= rs)
        self.zero_row = jnp.zeros((1, A_DK), F32)
        self.zero_blk = jnp.zeros((A_SUB, A_DK), BF16)
        self.zero_chunk = jnp.zeros((A_CHUNK, A_DK), BF16)


def _hgrn_slab(f_pre, q_pre, v, lb, st, c):
    slab = f_pre.shape[0]
    n_sub = A_CHUNK // A_SUB
    sc = slab // A_CHUNK
    rep = lambda r, n: jnp.broadcast_to(r, (n, A_DK))
    fg = lb + (1.0 - lb) * _sigmoid(f_pre)
    logf = jnp.log(fg)
    k = 1.0 - fg
    b = _group_cumsum(logf, c.row, A_CHUNK, (1, 2, 4, 8, 16, 32))
    q = _silu(q_pre)

    ends = [b[A_SUB * j + A_SUB - 1:A_SUB * (j + 1), :] for j in range(sc * n_sub)]
    endrow = jnp.concatenate([rep(e, A_SUB) for e in ends], axis=0)
    prevrow = jnp.concatenate(
        [rep(c.zero_row if j % n_sub == 0 else ends[j - 1], A_SUB) for j in range(sc * n_sub)], axis=0)
    tot = [ends[n_sub * ch + n_sub - 1] for ch in range(sc)]
    lastrow = jnp.concatenate([rep(t, A_CHUNK) for t in tot], axis=0)
    kend = k * jnp.exp(endrow - b)
    qd = q * jnp.exp(b - prevrow)
    qs = qd * jnp.exp(prevrow)
    kd = kend * jnp.exp(lastrow - endrow)
    kdiag16 = (k * jnp.exp(prevrow - b)).astype(BF16)
    qd16 = qd.astype(BF16)
    kend16 = kend.astype(BF16)
    qs16 = qs.astype(BF16)
    kd16 = kd.astype(BF16)

    blk = lambda a, j: a[A_SUB * j:A_SUB * (j + 1), :]
    chk = lambda a, ch: a[A_CHUNK * ch:A_CHUNK * (ch + 1), :]

    q_groups, k_groups = [], []
    for i in range(n_sub):
        q_groups.append(jnp.concatenate(
            [blk(qd16, j) if j % n_sub == i else c.zero_blk for j in range(sc * n_sub)], axis=0))
        pieces = []
        for j in range(sc * n_sub):
            ch, jj = divmod(j, n_sub)
            if jj == i:
                pieces.append(blk(kdiag16, j))
            elif jj == i - 1:
                pieces.append(blk(kend16, j))
            elif jj < i:
                pieces.append((blk(kend, j) * jnp.exp(ends[ch * n_sub + i - 1] - ends[j])).astype(BF16))
            else:
                pieces.append(c.zero_blk)
        k_groups.append(jnp.concatenate(pieces, axis=0))
    att = _dot_nt(jnp.concatenate(q_groups, axis=1), jnp.concatenate(k_groups, axis=1))
    att = jnp.where(c.same_chunk_causal, att, 0.0)

    if sc > 1:
        q_groups, k_groups = [], []
        for ch in range(1, sc):
            q_groups.append(jnp.concatenate(
                [chk(qs16, c2) if c2 == ch else c.zero_chunk for c2 in range(sc)], axis=0))
            pieces = []
            for c2 in range(sc):
                if c2 == ch - 1:
                    pieces.append(chk(kd16, c2))
                elif c2 < ch:
                    pieces.append((chk(kd, c2) * jnp.exp(sum(tot[c2 + 1:ch]))).astype(BF16))
                else:
                    pieces.append(c.zero_chunk)
            k_groups.append(jnp.concatenate(pieces, axis=0))
        att = att + _dot_nt(jnp.concatenate(q_groups, axis=1), jnp.concatenate(k_groups, axis=1))
    o = jnp.dot(att.astype(BF16), v, preferred_element_type=F32)

    q0 = jnp.concatenate(
        [chk(qs16, 0)] + [(chk(qs, ch) * jnp.exp(sum(tot[:ch]))).astype(BF16) for ch in range(1, sc)], axis=0)
    o = o + _dot_nt(q0, st.astype(BF16))

    k1 = jnp.concatenate(
        [(chk(kd, ch) * jnp.exp(sum(tot[ch + 1:]))).astype(BF16) for ch in range(sc - 1)]
        + [chk(kd16, sc - 1)], axis=0)
    st = st * jnp.exp(sum(tot)) + _dot_tn(v, k1)
    return o, st


def _hgrn_blockwise(stage_ref, oraw_ref, ls, lb, st):
    n_blk = stage_ref.shape[1] // A_SUB
    row = lax.broadcasted_iota(jnp.int32, (A_SUB, A_DK), 0)
    shifts = tuple(1 << i for i in range((A_SUB - 1).bit_length()))

    def body(i, st):
        rows = pl.ds(pl.multiple_of(i * A_SUB, A_SUB), A_SUB)
        fg = lb + (1.0 - lb) * _sigmoid(stage_ref[1, rows, ls])
        k = 1.0 - fg
        b = _group_cumsum(jnp.log(fg), row, A_SUB, shifts)
        q = _silu(stage_ref[0, rows, ls])
        v = stage_ref[2, rows, ls]
        o = _dot_nt((q * jnp.exp(b)).astype(BF16), st.astype(BF16))
        o = o + jnp.sum(q * k, axis=-1, keepdims=True) * v
        for d in range(1, A_SUB):
            ok = row >= d
            w = jnp.exp(jnp.where(ok, b - pltpu.roll(b, d, axis=0), 0.0))
            a = jnp.sum(q * pltpu.roll(k, d, axis=0) * w, axis=-1, keepdims=True)
            o = o + jnp.where(ok, a * pltpu.roll(v, d, axis=0), 0.0)
        oraw_ref[rows, ls] = o
        b_end = b[A_SUB - 1:A_SUB, :]
        kd = (k * jnp.exp(b_end - b)).astype(BF16)
        return st * jnp.exp(b_end) + _dot_tn(v.astype(BF16), kd)

    return lax.fori_loop(0, n_blk, body, st)


LB_SAFE = math.exp(-80.0 / A_SUB)


def _mix_prompt_kernel(bucket_ref, relb_ref, sink_ref, x_ref, gpre_ref, w_ref, lb_ref, gh_ref,
                       oa_ref, ob_ref, gab_ref, st_ref, kvwin_ref,
                       bias_ref, st_scr, kvprev_scr, stage_scr, oraw_scr):
    tile = pl.program_id(1)
    at_start = tile == 0

    @pl.when(jnp.logical_and(pl.program_id(0) == 0, at_start))
    def _():
        _build_bias(bucket_ref, relb_ref, bias_ref, WINDOW)

    @pl.when(at_start)
    def _():
        st_scr[...] = jnp.zeros(st_scr.shape, F32)
        kvprev_scr[...] = jnp.zeros(kvprev_scr.shape, F32)

    x = x_ref[...]
    ms = jnp.mean(x * x, axis=-1, keepdims=True)
    u = (x * lax.rsqrt(ms + EPS) * gpre_ref[...]).astype(BF16)
    proj = lambda seg, lo, hi: jnp.dot(u, w_ref[:, IN_OFFS[seg] + lo:IN_OFFS[seg] + hi],
                                       preferred_element_type=F32)

    lb_all = _lower_bound(lb_ref)
    consts = _HgrnConsts(HGRN_SLAB)
    pair_w = 2 * A_DK
    n_phase = A_HEADS // 2
    n_blocks = MIX_TQ // WINDOW
    assert n_blocks == n_phase
    gate_w = D_MODEL // n_phase
    kw = B_KV_HEADS * B_HD

    def hgrn_proj(hp):
        lo, hi = hp * pair_w, (hp + 1) * pair_w
        return tuple(proj(seg, lo, hi) for seg in (0, SEG_AF, 2, 3, 4))

    def hgrn_pair(hp, q2, f2, v2, og2, z2):
        for hh in range(2):
            h = 2 * hp + hh
            ls = slice(hh * A_DK, (hh + 1) * A_DK)
            hs = slice(h * A_DK, (h + 1) * A_DK)
            st = st_scr[h]
            outs = []
            for s0 in range(0, MIX_TQ, HGRN_SLAB):
                rs = slice(s0, s0 + HGRN_SLAB)
                o, st = _hgrn_slab(f2[rs, ls], q2[rs, ls], v2[rs, ls].astype(BF16), lb_all[:, hs], st, consts)
                outs.append(o)
            st_scr[h] = st
            st_ref[h] = st.T
            oa = _hgrn_finish(jnp.concatenate(outs, axis=0), gh_ref[:, hs], og2[:, ls], z2[:, ls])
            oa_ref[:, hs] = oa.astype(BF16)

    key = lax.broadcasted_iota(jnp.int32, (2 * N_KEYS, PAIRS_PER_KV * WINDOW), 0) & (N_KEYS - 1)
    no_prev = jnp.logical_and(key < WINDOW, at_start)

    def swa_block(j, kv, bq, bz):
        r0 = j * WINDOW
        if j == 0:
            kvj = jnp.concatenate([kvprev_scr[...], kv[0:WINDOW, :]], axis=0)
        else:
            kvj = kv[r0 - WINDOW:r0 + WINDOW, :]
        keys = _swa_keys(kvj[:, 0:kw])
        vals_t = [kvj[:, kw:2 * kw].T.astype(BF16)]
        for kvh in range(B_KV_HEADS):
            q = jnp.concatenate([bq[r0:r0 + WINDOW, p * 128:(p + 1) * 128]
                                 for p in range(kvh * PAIRS_PER_KV, (kvh + 1) * PAIRS_PER_KV)], axis=0)
            logits = _dot_nt(keys[kvh], q) + bias_ref[kvh]
            if j == 0:
                logits = jnp.where(no_prev, NEG, logits)
            acc_t = _swa_attend(logits, vals_t, [None], sink_ref, kvh, WINDOW)
            for pp in range(PAIRS_PER_KV):
                cs = slice((kvh * PAIRS_PER_KV + pp) * 128, (kvh * PAIRS_PER_KV + pp + 1) * 128)
                ob_ref[r0:r0 + WINDOW, cs] = (
                    acc_t[pp * WINDOW:(pp + 1) * WINDOW, :] * _silu(bz[r0:r0 + WINDOW, cs])).astype(BF16)

    def hgrn_pair_blockwise(hp, q2, f2, v2, og2, z2):
        stage_scr[0] = q2
        stage_scr[1] = f2
        stage_scr[2] = v2
        for hh in range(2):
            h = 2 * hp + hh
            ls = slice(hh * A_DK, (hh + 1) * A_DK)
            hs = slice(h * A_DK, (h + 1) * A_DK)
            st = _hgrn_blockwise(stage_scr, oraw_scr, ls, lb_all[:, hs], st_scr[h])
            st_scr[h] = st
            st_ref[h] = st.T
            oa = _hgrn_finish(oraw_scr[:, ls], gh_ref[:, hs], og2[:, ls], z2[:, ls])
            oa_ref[:, hs] = oa.astype(BF16)

    def gate_proj(ph):
        for seg, base in ((9, 0), (10, D_MODEL)):
            gab_ref[:, base + ph * gate_w:base + (ph + 1) * gate_w] = proj(
                seg, ph * gate_w, (ph + 1) * gate_w).astype(BF16)

    def swa_proj():
        kv = proj(SEG_BK, 0, 2 * kw)
        bq = (proj(5, 0, D_MODEL) * (B_HD ** -0.5)).astype(BF16)
        bz = proj(8, 0, D_MODEL)
        return kv, bq, bz

    def keep_window(kv):
        kvprev_scr[...] = kv[MIX_TQ - WINDOW:, :]
        kvwin_ref[...] = kv[MIX_TQ - WINDOW:, :]

    slab_safe = jnp.min(lb_all) >= LB_SAFE

    @pl.when(slab_safe)
    def _():
        kv, bq, bz = swa_proj()
        nxt = hgrn_proj(0)
        for ph in range(n_phase):
            cur = nxt
            if ph + 1 < n_phase:
                nxt = hgrn_proj(ph + 1)
            gate_proj(ph)
            hgrn_pair(ph, *cur)
            swa_block(ph, kv, bq, bz)
        keep_window(kv)

    @pl.when(jnp.logical_not(slab_safe))
    def _():
        for ph in range(n_phase):
            hgrn_pair_blockwise(ph, *hgrn_proj(ph))
            gate_proj(ph)
        kv, bq, bz = swa_proj()
        for j in range(n_blocks):
            swa_block(j, kv, bq, bz)
        keep_window(kv)


def _mix_prompt(bucket, rel_bias, sink, x, g_pre, w, hgrn_lb, hgrn_norm, batch, seq):
    nt = seq // MIX_TQ
    n_lb = hgrn_lb.shape[0]
    kvw = 2 * B_KV_HEADS * B_HD
    smem = pl.BlockSpec(memory_space=pltpu.SMEM)
    tok = lambda width: pl.BlockSpec((MIX_TQ, width), lambda b, t: (b * nt + t, 0))
    return pl.pallas_call(
        _mix_prompt_kernel,
        grid=(batch, nt),
        in_specs=[
            _resident((N_KEYS, WINDOW)),
            smem, smem,
            tok(D_MODEL),
            _resident((1, D_MODEL)),
            _resident(w.shape),
            _resident((n_lb, D_MODEL)),
            _resident((1, D_MODEL)),
        ],
        out_specs=[
            tok(D_MODEL), tok(D_MODEL), tok(2 * D_MODEL),
            pl.BlockSpec((None, A_HEADS, A_DK, A_DV), lambda b, t: (b, 0, 0, 0)),
            pl.BlockSpec((None, WINDOW, kvw), lambda b, t: (b, 0, 0)),
        ],
        out_shape=[
            jax.ShapeDtypeStruct((batch * seq, D_MODEL), BF16),
            jax.ShapeDtypeStruct((batch * seq, D_MODEL), BF16),
            jax.ShapeDtypeStruct((batch * seq, 2 * D_MODEL), BF16),
            jax.ShapeDtypeStruct((batch, A_HEADS, A_DK, A_DV), F32),
            jax.ShapeDtypeStruct((batch, WINDOW, kvw), F32),
        ],
        scratch_shapes=[
            pltpu.VMEM((B_KV_HEADS, 2 * N_KEYS, PAIRS_PER_KV * WINDOW), F32),
            pltpu.VMEM((A_HEADS, A_DV, A_DK), F32),
            pltpu.VMEM((WINDOW, kvw), F32),
            pltpu.VMEM((3, MIX_TQ, 2 * A_DK), F32),
            pltpu.VMEM((MIX_TQ, 2 * A_DK), F32),
        ],
        compiler_params=pltpu.CompilerParams(
            dimension_semantics=("arbitrary", "arbitrary"), vmem_limit_bytes=VMEM_LIMIT),
        name="mix_prompt",
    )(bucket, rel_bias, sink, x, g_pre, w, hgrn_lb, hgrn_norm)


def _outproj_kernel(oa_ref, ob_ref, ga_ref, gb_ref, x_ref, p_ref, wpa_ref, wpb_ref, wo_ref, gpost_ref,
                    wple_ref, wg_ref, y_ref):
    a = jnp.dot(oa_ref[...], wpa_ref[...], preferred_element_type=F32)
    b = jnp.dot(ob_ref[...], wpb_ref[...], preferred_element_type=F32)
    m = _sigmoid(ga_ref[...].astype(F32)) * a + _sigmoid(gb_ref[...].astype(F32)) * b
    y = jnp.dot(m.astype(BF16), wo_ref[...], preferred_element_type=F32)
    y = y * lax.rsqrt(jnp.mean(y * y, axis=-1, keepdims=True) + EPS) * gpost_ref[...]
    x1 = x_ref[...] + y
    gate = _sigmoid(jnp.dot(x1.astype(BF16), wg_ref[...], preferred_element_type=F32))
    e = jnp.dot(p_ref[...].astype(BF16), wple_ref[...], preferred_element_type=F32) * gate
    y_ref[...] = x1 + e


def _outproj(oa, ob, gates, ga_col, gb_col, x, p, wpa, wpb, wo, gpost, wple, wg, tm):
    n = x.shape[0]
    ple = p.shape[1]
    tok = lambda w, c=0: pl.BlockSpec((tm, w), lambda i, c=c: (i, c))
    return pl.pallas_call(
        _outproj_kernel,
        grid=(n // tm,),
        in_specs=[
            tok(D_MODEL), tok(D_MODEL), tok(D_MODEL, ga_col), tok(D_MODEL, gb_col), tok(D_MODEL), tok(ple),
            _resident((D_MODEL, D_MODEL)), _resident((D_MODEL, D_MODEL)), _resident((D_MODEL, D_MODEL)),
            _resident((1, D_MODEL)), _resident((ple, D_MODEL)), _resident((D_MODEL, D_MODEL)),
        ],
        out_specs=tok(D_MODEL),
        out_shape=jax.ShapeDtypeStruct((n, D_MODEL), F32),
        compiler_params=pltpu.CompilerParams(
            dimension_semantics=("arbitrary",), vmem_limit_bytes=VMEM_LIMIT),
        name="outproj",
    )(oa, ob, gates, gates, x, p, wpa, wpb, wo, gpost, wple, wg)


def _rel_bucket(rel):
    n = np.maximum(rel, 0)
    max_exact = REL_BUCKETS // 2
    nf = np.maximum(n, 1).astype(np.float32)
    scaled = (np.log(nf / np.float32(max_exact)) / np.float32(math.log(REL_MAX_DIST / max_exact))
              * np.float32(REL_BUCKETS - max_exact))
    frac = scaled - np.floor(scaled)
    inside = (n > max_exact) & (n < REL_MAX_DIST)
    assert np.all((frac[inside] > 1e-3) & (frac[inside] < 1.0 - 1e-3))
    large = np.minimum(max_exact + scaled.astype(np.int32), REL_BUCKETS - 1)
    return np.where(n < max_exact, n, large)


def _bucket_table(q_pos, k_pos, k_valid):
    rel = q_pos[:, None] - k_pos[None, :]
    ok = (rel >= 0) & (rel < WINDOW) & k_valid[None, :]
    return jnp.asarray(np.where(ok, _rel_bucket(rel), -1).astype(np.int32).T)


TM_PROJ = 512
SAMPLE_BB = 4
SWA_SAMPLE_BB = 8


def _layer(xp, xs, s_hgrn, win_k, win_v, pp, ps, norm_pre, w_in, hgrn_lb, hgrn_norm, attn_sink,
           rel_bias, w_pa, w_pb, w_o, norm_post, w_ple, w_ple_gate):
    batch, seq, _ = xp.shape
    dbatch, t_len, _ = xs.shape
    kw = B_KV_HEADS * B_HD

    w16 = w_in.astype(BF16)
    g_pre = norm_pre.reshape(1, D_MODEL)
    g_post = norm_post.reshape(1, D_MODEL)
    g_hgrn = hgrn_norm.reshape(1, D_MODEL)
    wpa, wpb, wo = w_pa.astype(BF16), w_pb.astype(BF16), w_o.astype(BF16)
    wple, wg = w_ple.astype(BF16), w_ple_gate.astype(BF16)

    xp2 = xp.reshape(batch * seq, D_MODEL)
    xs2 = xs.reshape(dbatch * t_len, D_MODEL)
    k_all = np.arange(N_KEYS)
    bucket_p = _bucket_table(np.arange(WINDOW) + WINDOW, k_all, np.ones((N_KEYS,), bool))
    rows_s = SWA_SAMPLE_BB * t_len
    bucket_s = _bucket_table(WINDOW + np.arange(rows_s) % t_len, k_all, k_all < WINDOW + t_len)

    oa_p, ob_p, gab_p, st_p, kvwin_p = _mix_prompt(bucket_p, rel_bias, attn_sink, xp2, g_pre, w16,
                                                   hgrn_lb, g_hgrn, batch, seq)

    pm_s, f_s, kv_s = _inproj(xs2, g_pre, w16, min(TM_PROJ, dbatch * t_len))
    oa_s, st_s = _hgrn_sample(pm_s, f_s, hgrn_lb, g_hgrn, s_hgrn, dbatch, t_len, SAMPLE_BB)
    ob_s, nk_s, nv_s = _swa_sample(bucket_s, rel_bias, attn_sink, pm_s, kv_s,
                                   win_k.reshape(dbatch, WINDOW, kw), win_v.reshape(dbatch, WINDOW, kw),
                                   dbatch, t_len, SWA_SAMPLE_BB)

    y_p = _outproj(oa_p, ob_p, gab_p, 0, 1, xp2, pp.reshape(batch * seq, -1),
                   wpa, wpb, wo, g_post, wple, wg, TM_PROJ)
    y_s = _outproj(oa_s, ob_s, pm_s, G_GA, G_GB, xs2, ps.reshape(dbatch * t_len, -1),
                   wpa, wpb, wo, g_post, wple, wg, min(TM_PROJ, dbatch * t_len))

    k_win_p = kvwin_p[:, :, 0:kw].reshape(batch, WINDOW, B_KV_HEADS, B_HD)
    v_win_p = kvwin_p[:, :, kw:].reshape(batch, WINDOW, B_KV_HEADS, B_HD)
    return (y_p.reshape(batch, seq, D_MODEL), y_s.reshape(dbatch, t_len, D_MODEL), st_p, st_s,
            k_win_p, v_win_p,
            nk_s.reshape(dbatch, WINDOW, B_KV_HEADS, B_HD), nv_s.reshape(dbatch, WINDOW, B_KV_HEADS, B_HD))


def kernel(x_prompt, x_sample, state_hgrn, cache_swa_k, cache_swa_v, p_prompt, p_sample, norm_pre, w_in,
           hgrn_lb, hgrn_norm, attn_sink, rel_bias, w_pa, w_pb, w_o, norm_post, w_ple, w_ple_gate):
    depth = w_in.shape[0]
    assert depth == 1, "the forget-gate lower bound is implemented for a single layer"
    xp, xs = x_prompt, x_sample
    outs = []
    for l in range(depth):
        res = _layer(xp, xs, state_hgrn[l], cache_swa_k[l], cache_swa_v[l], p_prompt[l], p_sample[l],
                     norm_pre[l], w_in[l], hgrn_lb, hgrn_norm[l], attn_sink[l], rel_bias,
                     w_pa[l], w_pb[l], w_o[l], norm_post[l], w_ple[l], w_ple_gate[l])
        xp, xs = res[0], res[1]
        outs.append(res[2:])
    stack = lambda i: jnp.stack([o[i] for o in outs])
    return (xp, xs, stack(0), stack(1), stack(2), stack(3), stack(4), stack(5))
```

```python
import functools
import math

import jax
import jax.numpy as jnp
import numpy as np
from jax import lax
from jax.experimental import pallas as pl
from jax.experimental.pallas import tpu as pltpu

F32 = jnp.float32
BF16 = jnp.bfloat16

D_MODEL = 1024
A_HEADS = 8
A_DK = 128
A_DV = 128
A_CHUNK = 64
A_SUB = 16
B_HEADS = 16
B_KV_HEADS = 2
B_HD = 64
WINDOW = 128
REL_BUCKETS = 32
REL_MAX_DIST = 128
EPS = 1e-6
NEG = float("-inf")

G_AQ, G_AI, G_AOG, G_AZ, G_BQ, G_BZ, G_GA, G_GB = range(8)
N_GROUPS = 8

VMEM_LIMIT = 56 * 1024 * 1024


def _sigmoid(x):
    return 1.0 / (1.0 + jnp.exp(-x))


def _silu(x):
    return x * _sigmoid(x)


def _resident(shape):
    nd = len(shape)
    return pl.BlockSpec(shape, lambda *_: (0,) * nd, pipeline_mode=pl.Buffered(1))


IN_OFFS = (0, 1024, 2048, 3072, 4096, 5120, 6144, 6272, 6400, 7424, 8448, 9472)
SLAB_SEGS = (0, 2, 3, 4, 5, 8, 9, 10)
SEG_AF, SEG_BK, SEG_BV = 1, 6, 7


def _inproj_kernel(x_ref, g_ref, w_ref, pm_ref, f_ref, kv_ref):
    x = x_ref[...]
    ms = jnp.mean(x * x, axis=-1, keepdims=True)
    u = (x * lax.rsqrt(ms + EPS) * g_ref[...]).astype(BF16)
    proj = lambda lo, hi: jnp.dot(u, w_ref[:, lo:hi], preferred_element_type=F32)
    for c, seg in enumerate(SLAB_SEGS):
        pm_ref[:, c * D_MODEL:(c + 1) * D_MODEL] = proj(IN_OFFS[seg], IN_OFFS[seg + 1]).astype(BF16)
    f_ref[...] = proj(IN_OFFS[SEG_AF], IN_OFFS[SEG_AF + 1])
    kv_ref[...] = proj(IN_OFFS[SEG_BK], IN_OFFS[SEG_BV + 1])


def _inproj(x, g, w, tm):
    n = x.shape[0]
    nm = N_GROUPS * D_MODEL
    nkv = IN_OFFS[SEG_BV + 1] - IN_OFFS[SEG_BK]
    return pl.pallas_call(
        _inproj_kernel,
        grid=(n // tm,),
        in_specs=[
            pl.BlockSpec((tm, D_MODEL), lambda i: (i, 0)),
            _resident((1, D_MODEL)),
            _resident(w.shape),
        ],
        out_specs=[
            pl.BlockSpec((tm, nm), lambda i: (i, 0)),
            pl.BlockSpec((tm, D_MODEL), lambda i: (i, 0)),
            pl.BlockSpec((tm, nkv), lambda i: (i, 0)),
        ],
        out_shape=[
            jax.ShapeDtypeStruct((n, nm), BF16),
            jax.ShapeDtypeStruct((n, D_MODEL), F32),
            jax.ShapeDtypeStruct((n, nkv), F32),
        ],
        compiler_params=pltpu.CompilerParams(
            dimension_semantics=("arbitrary",), vmem_limit_bytes=VMEM_LIMIT),
        name="inproj",
    )(x, g, w)


def _lower_bound(lb_ref):
    l = lb_ref[...]
    m = jnp.max(l, axis=0, keepdims=True)
    e = jnp.exp(l - m)
    return e[0:1, :] / jnp.sum(e, axis=0, keepdims=True)


def _group_cumsum(x, row, period, shifts):
    pos = row & (period - 1)
    for sh in shifts:
        x = x + jnp.where(pos >= sh, pltpu.roll(x, sh, axis=0), 0.0)
    return x


def _hgrn_finish(o, g, og_pre, z_pre):
    o = o * lax.rsqrt(jnp.mean(o * o, axis=-1, keepdims=True) + EPS)
    return o * g * _sigmoid(og_pre) * _silu(z_pre)


def _dot_nt(a, b):
    return lax.dot_general(a, b, (((1,), (1,)), ((), ())), preferred_element_type=F32)


def _dot_tn(a, b):
    return lax.dot_general(a, b, (((0,), (0,)), ((), ())), preferred_element_type=F32)


def _hgrn_sample_kernel(q_ref, v_ref, og_ref, z_ref, f_ref, lb_ref, g_ref, s0_ref, oa_ref, s_ref,
                        *, bb, t_len):
    rows = bb * t_len
    lb_all = _lower_bound(lb_ref)
    row = lax.broadcasted_iota(jnp.int32, (rows, A_DK), 0)
    pos = row & (t_len - 1)
    grp = row >> (t_len.bit_length() - 1)
    shifts = tuple(1 << i for i in range((t_len - 1).bit_length()))
    n_tail = A_DK - rows
    sel_row = lax.broadcasted_iota(jnp.int32, (n_tail, 2 * bb * A_DV), 0)
    sel_blk = lax.broadcasted_iota(jnp.int32, (n_tail, 2 * bb * A_DV), 1) >> (A_DV.bit_length() - 1)
    picks = jnp.logical_and(sel_row < 2 * bb, sel_blk == bb + (sel_row & (bb - 1)))
    rhs_tail = jnp.where(picks, 1.0, 0.0).astype(BF16)
    lhs_pad = jnp.zeros((n_tail - 2 * bb, A_DK), F32)
    v_pad = jnp.zeros((rows, bb * A_DV), F32)
    for h in range(A_HEADS):
        ls = slice(h * A_DK, (h + 1) * A_DK)
        lb = lb_all[:, ls]
        fg = lb + (1.0 - lb) * _sigmoid(f_ref[:, ls])
        logf = jnp.log(fg)
        k = 1.0 - fg
        b = _group_cumsum(logf, row, t_len, shifts)
        q = _silu(q_ref[:, ls].astype(F32))
        v = v_ref[:, ls].astype(F32)

        o = jnp.zeros((rows, A_DV), F32)
        for d in range(t_len):
            if d == 0:
                a = jnp.sum(q * k, axis=-1, keepdims=True)
                o = o + a * v
            else:
                w = jnp.exp(jnp.where(pos >= d, b - pltpu.roll(b, d, axis=0), 0.0))
                a = jnp.sum(q * pltpu.roll(k, d, axis=0) * w, axis=-1, keepdims=True)
                o = o + jnp.where(pos >= d, a * pltpu.roll(v, d, axis=0), 0.0)

        b_last = b
        for d in range(1, t_len):
            b_last = jnp.where(pos == t_len - 1 - d, pltpu.roll(b, rows - d, axis=0), b_last)
        e_last = jnp.exp(b_last)
        qs = q * jnp.exp(b)
        kd = k * jnp.exp(b_last - b)
        e_rows = jnp.concatenate([e_last[j * t_len:j * t_len + 1, :] for j in range(bb)], axis=0)
        e_hi = e_rows.astype(BF16).astype(F32)
        lhs_t = jnp.concatenate([kd, e_hi, e_rows - e_hi, lhs_pad], axis=0).T.astype(BF16)
        v_rows = jnp.concatenate([jnp.where(grp == j, v, 0.0) for j in range(bb)] + [v_pad], axis=1)
        res = jnp.dot(lhs_t, jnp.concatenate([v_rows.astype(BF16), rhs_tail], axis=0),
                      preferred_element_type=F32)
        for j in range(bb):
            s0 = s0_ref[j, h]
            o = o + jnp.dot(jnp.where(grp == j, qs, 0.0).astype(BF16), s0.astype(BF16),
                            preferred_element_type=F32)
            s_ref[j, h] = (res[:, (bb + j) * A_DV:(bb + j + 1) * A_DV] * s0
                           + res[:, j * A_DV:(j + 1) * A_DV])

        oa = _hgrn_finish(o, g_ref[:, ls], og_ref[:, ls].astype(F32), z_ref[:, ls].astype(F32))
        oa_ref[:, ls] = oa.astype(BF16)


def _hgrn_sample(pm, f, hgrn_lb, hgrn_norm, s0, batch, t_len, bb):
    rows = bb * t_len
    col = lambda grp: pl.BlockSpec((rows, D_MODEL), lambda i, grp=grp: (i, grp))
    n_lb = hgrn_lb.shape[0]
    st_spec = pl.BlockSpec((bb, A_HEADS, A_DK, A_DV), lambda i: (i, 0, 0, 0))
    return pl.pallas_call(
        functools.partial(_hgrn_sample_kernel, bb=bb, t_len=t_len),
        grid=(batch // bb,),
        in_specs=[
            col(G_AQ), col(G_AI), col(G_AOG), col(G_AZ),
            pl.BlockSpec((rows, D_MODEL), lambda i: (i, 0)),
            _resident((n_lb, D_MODEL)),
            _resident((1, D_MODEL)),
            st_spec,
        ],
        out_specs=[pl.BlockSpec((rows, D_MODEL), lambda i: (i, 0)), st_spec],
        out_shape=[
            jax.ShapeDtypeStruct((batch * t_len, D_MODEL), BF16),
            jax.ShapeDtypeStruct((batch, A_HEADS, A_DK, A_DV), F32),
        ],
        compiler_params=pltpu.CompilerParams(
            dimension_semantics=("arbitrary",), vmem_limit_bytes=VMEM_LIMIT),
        name="hgrn_sample",
    )(pm, pm, pm, pm, f, hgrn_lb, hgrn_norm, s0)


N_PAIRS = B_HEADS // 2
PAIRS_PER_KV = N_PAIRS // B_KV_HEADS
N_KEYS = 2 * WINDOW


def _build_bias(bucket_ref, relb_ref, bias_ref, r):
    bucket = bucket_ref[...]
    for h in range(B_HEADS):
        def body(kb, acc, h=h):
            return jnp.where(bucket == kb, relb_ref[kb, h], acc)
        tab = lax.fori_loop(0, REL_BUCKETS, body, jnp.full(bucket.shape, NEG, F32))
        pair, parity = divmod(h, 2)
        kv, pp = divmod(pair, PAIRS_PER_KV)
        bias_ref[kv, parity * N_KEYS:(parity + 1) * N_KEYS, pp * r:(pp + 1) * r] = tab


def _swa_keys(kk):
    lo = lax.broadcasted_iota(jnp.int32, (N_KEYS, 2 * B_HD), 1) < B_HD
    kk_sw = pltpu.roll(kk, B_HD, axis=1)
    slabs = []
    for kv in range(B_KV_HEADS):
        k_lo, k_hi = (kk, kk_sw) if kv == 0 else (kk_sw, kk)
        slabs.append(jnp.concatenate([jnp.where(lo, k_lo, 0.0), jnp.where(lo, 0.0, k_hi)], axis=0).astype(BF16))
    return slabs


def _swa_attend(logits, vals_t, col_masks, sink_ref, kv, r):
    width = PAIRS_PER_KV * r
    lane_pp = lax.broadcasted_iota(jnp.int32, (1, width), 1) >> (r.bit_length() - 1)
    halves = []
    for parity in range(2):
        sink = jnp.zeros((1, width), F32)
        for pp in range(PAIRS_PER_KV):
            sink = jnp.where(lane_pp == pp, sink_ref[(kv * PAIRS_PER_KV + pp) * 2 + parity], sink)
        l = logits[parity * N_KEYS:(parity + 1) * N_KEYS, :]
        m = jnp.maximum(jnp.max(l, axis=0, keepdims=True), sink)
        p = jnp.exp(l - m)
        denom = jnp.sum(p, axis=0, keepdims=True) + jnp.exp(sink - m)
        p16 = p.astype(BF16)
        v_rows = jnp.concatenate([v_t[kv * B_HD:(kv + 1) * B_HD, :] for v_t in vals_t], axis=0)
        pv_all = jnp.dot(v_rows, p16, preferred_element_type=F32)
        pv = None
        for n, mask in enumerate(col_masks):
            part = pv_all[n * B_HD:(n + 1) * B_HD, :]
            pv = part if mask is None else jnp.where(mask, part, 0.0 if pv is None else pv)
        halves.append(pv / denom)
    return jnp.concatenate(halves, axis=0).T


def _swa_sample_kernel(bucket_ref, relb_ref, sink_ref, q_ref, z_ref, kvn_ref, ck_ref, cv_ref,
                       ob_ref, nk_ref, nv_ref, bias_ref, *, bb, t_len):
    rows = bb * t_len
    width = PAIRS_PER_KV * rows

    @pl.when(pl.program_id(0) == 0)
    def _():
        _build_bias(bucket_ref, relb_ref, bias_ref, rows)

    kw = B_KV_HEADS * B_HD
    t_shift = t_len.bit_length() - 1
    row = lax.broadcasted_iota(jnp.int32, (rows, kw), 0)
    seq_of_col = lambda n: (lax.broadcasted_iota(jnp.int32, (n, width), 1) & (rows - 1)) >> t_shift
    seq_l = seq_of_col(2 * N_KEYS)
    seq_v = seq_of_col(B_HD)
    q_pairs = [q_ref[:, p * 128:(p + 1) * 128] * (B_HD ** -0.5) for p in range(N_PAIRS)]
    q_kv = [jnp.concatenate(q_pairs[kv * PAIRS_PER_KV:(kv + 1) * PAIRS_PER_KV], axis=0)
            for kv in range(B_KV_HEADS)]
    kn = kvn_ref[:, 0:kw]
    vn = kvn_ref[:, kw:2 * kw]
    pad = jnp.zeros((N_KEYS - WINDOW - rows, kw), F32)
    logits = [None] * B_KV_HEADS
    vals_t, col_masks = [], []
    for j in range(bb):
        sel = row < t_len
        kj = jnp.where(sel, kn if j == 0 else pltpu.roll(kn, rows - j * t_len, axis=0), 0.0)
        vj = jnp.where(sel, vn if j == 0 else pltpu.roll(vn, rows - j * t_len, axis=0), 0.0)
        kk = jnp.concatenate([ck_ref[j], kj, pad], axis=0)
        vv = jnp.concatenate([cv_ref[j], vj, pad], axis=0)
        keys = _swa_keys(kk)
        for kv in range(B_KV_HEADS):
            lg = _dot_nt(keys[kv], q_kv[kv])
            logits[kv] = jnp.where(seq_l == j, lg, 0.0 if logits[kv] is None else logits[kv])
        vals_t.append(vv.T.astype(BF16))
        col_masks.append(seq_v == j)
        nk_ref[j] = pltpu.roll(kk, N_KEYS - t_len, axis=0)[0:WINDOW, :]
        nv_ref[j] = pltpu.roll(vv, N_KEYS - t_len, axis=0)[0:WINDOW, :]
    for kv in range(B_KV_HEADS):
        acc_t = _swa_attend(logits[kv] + bias_ref[kv], vals_t, col_masks, sink_ref, kv, rows)
        for pp in range(PAIRS_PER_KV):
            p = kv * PAIRS_PER_KV + pp
            z = z_ref[:, p * 128:(p + 1) * 128].astype(F32)
            ob_ref[:, p * 128:(p + 1) * 128] = (acc_t[pp * rows:(pp + 1) * rows, :] * _silu(z)).astype(BF16)


def _swa_sample(bucket, rel_bias, sink, pm, kvn, cache_k, cache_v, batch, t_len, bb):
    rows = bb * t_len
    kvw = kvn.shape[1]
    kw = kvw // 2
    smem = pl.BlockSpec(memory_space=pltpu.SMEM)
    cache_spec = pl.BlockSpec((bb, WINDOW, kw), lambda i: (i, 0, 0))
    return pl.pallas_call(
        functools.partial(_swa_sample_kernel, bb=bb, t_len=t_len),
        grid=(batch // bb,),
        in_specs=[
            _resident((N_KEYS, rows)),
            smem, smem,
            pl.BlockSpec((rows, D_MODEL), lambda i: (i, G_BQ)),
            pl.BlockSpec((rows, D_MODEL), lambda i: (i, G_BZ)),
            pl.BlockSpec((rows, kvw), lambda i: (i, 0)),
            cache_spec, cache_spec,
        ],
        out_specs=[pl.BlockSpec((rows, D_MODEL), lambda i: (i, 0)), cache_spec, cache_spec],
        out_shape=[
            jax.ShapeDtypeStruct((batch * t_len, D_MODEL), BF16),
            jax.ShapeDtypeStruct((batch, WINDOW, kw), F32),
            jax.ShapeDtypeStruct((batch, WINDOW, kw), F32),
        ],
        scratch_shapes=[pltpu.VMEM((B_KV_HEADS, 2 * N_KEYS, PAIRS_PER_KV * rows), F32)],
        compiler_params=pltpu.CompilerParams(
            dimension_semantics=("arbitrary",), vmem_limit_bytes=VMEM_LIMIT),
        name="swa_sample",
    )(bucket, rel_bias, sink, pm, pm, kvn, cache_k, cache_v)


MIX_TQ = 512
HGRN_SLAB = 256


class _HgrnConsts:
    def __init__(self, slab):
        shift = A_CHUNK.bit_length() - 1
        self.row = lax.broadcasted_iota(jnp.int32, (slab, A_DK), 0)
        rs = lax.broadcasted_iota(jnp.int32, (slab, slab), 0)
        cs = lax.broadcasted_iota(jnp.int32, (slab, slab), 1)
        self.same_chunk_causal = jnp.logical_and((rs >> shift) == (cs >> shift), cs <= rs)
        self.zero_row = jnp.zeros((1, A_DK), F32)
        self.zero_blk = jnp.zeros((A_SUB, A_DK), BF16)
        self.zero_chunk = jnp.zeros((A_CHUNK, A_DK), BF16)


def _hgrn_slab(f_pre, q_pre, v, lb, st, c):
    slab = f_pre.shape[0]
    n_sub = A_CHUNK // A_SUB
    sc = slab // A_CHUNK
    rep = lambda r, n: jnp.broadcast_to(r, (n, A_DK))
    fg = lb + (1.0 - lb) * _sigmoid(f_pre)
    logf = jnp.log(fg)
    k = 1.0 - fg
    b = _group_cumsum(logf, c.row, A_CHUNK, (1, 2, 4, 8, 16, 32))
    q = _silu(q_pre)

    ends = [b[A_SUB * j + A_SUB - 1:A_SUB * (j + 1), :] for j in range(sc * n_sub)]
    endrow = jnp.concatenate([rep(e, A_SUB) for e in ends], axis=0)
    prevrow = jnp.concatenate(
        [rep(c.zero_row if j % n_sub == 0 else ends[j - 1], A_SUB) for j in range(sc * n_sub)], axis=0)
    tot = [ends[n_sub * ch + n_sub - 1] for ch in range(sc)]
    lastrow = jnp.concatenate([rep(t, A_CHUNK) for t in tot], axis=0)
    kend = k * jnp.exp(endrow - b)
    qd = q * jnp.exp(b - prevrow)
    qs = qd * jnp.exp(prevrow)
    kd = kend * jnp.exp(lastrow - endrow)
    kdiag16 = (k * jnp.exp(prevrow - b)).astype(BF16)
    qd16 = qd.astype(BF16)
    kend16 = kend.astype(BF16)
    qs16 = qs.astype(BF16)
    kd16 = kd.astype(BF16)

    blk = lambda a, j: a[A_SUB * j:A_SUB * (j + 1), :]
    chk = lambda a, ch: a[A_CHUNK * ch:A_CHUNK * (ch + 1), :]

    q_groups, k_groups = [], []
    for i in range(n_sub):
        q_groups.append(jnp.concatenate(
            [blk(qd16, j) if j % n_sub == i else c.zero_blk for j in range(sc * n_sub)], axis=0))
        pieces = []
        for j in range(sc * n_sub):
            ch, jj = divmod(j, n_sub)
            if jj == i:
                pieces.append(blk(kdiag16, j))
            elif jj == i - 1:
                pieces.append(blk(kend16, j))
            elif jj < i:
                pieces.append((blk(kend, j) * jnp.exp(ends[ch * n_sub + i - 1] - ends[j])).astype(BF16))
            else:
                pieces.append(c.zero_blk)
        k_groups.append(jnp.concatenate(pieces, axis=0))
    att = _dot_nt(jnp.concatenate(q_groups, axis=1), jnp.concatenate(k_groups, axis=1))
    att = jnp.where(c.same_chunk_causal, att, 0.0)

    if sc > 1:
        q_groups, k_groups = [], []
        for ch in range(1, sc):
            q_groups.append(jnp.concatenate(
                [chk(qs16, c2) if c2 == ch else c.zero_chunk for c2 in range(sc)], axis=0))
            pieces = []
            for c2 in range(sc):
                if c2 == ch - 1:
                    pieces.append(chk(kd16, c2))
                elif c2 < ch:
                    pieces.append((chk(kd, c2) * jnp.exp(sum(tot[c2 + 1:ch]))).astype(BF16))
                else:
                    pieces.append(c.zero_chunk)
            k_groups.append(jnp.concatenate(pieces, axis=0))
        att = att + _dot_nt(jnp.concatenate(q_groups, axis=1), jnp.concatenate(k_groups, axis=1))
    o = jnp.dot(att.astype(BF16), v, preferred_element_type=F32)

    q0 = jnp.concatenate(
        [chk(qs16, 0)] + [(chk(qs, ch) * jnp.exp(sum(tot[:ch]))).astype(BF16) for ch in range(1, sc)], axis=0)
    o = o + _dot_nt(q0, st.astype(BF16))

    k1 = jnp.concatenate(
        [(chk(kd, ch) * jnp.exp(sum(tot[ch + 1:]))).astype(BF16) for ch in range(sc - 1)]
        + [chk(kd16, sc - 1)], axis=0)
    st = st * jnp.exp(sum(tot)) + _dot_tn(v, k1)
    return o, st


def _hgrn_blockwise(stage_ref, oraw_ref, ls, lb, st):
    n_blk = stage_ref.shape[1] // A_SUB
    row = lax.broadcasted_iota(jnp.int32, (A_SUB, A_DK), 0)
    shifts = tuple(1 << i for i in range((A_SUB - 1).bit_length()))

    def body(i, st):
        rows = pl.ds(pl.multiple_of(i * A_SUB, A_SUB), A_SUB)
        fg = lb + (1.0 - lb) * _sigmoid(stage_ref[1, rows, ls])
        k = 1.0 - fg
        b = _group_cumsum(jnp.log(fg), row, A_SUB, shifts)
        q = _silu(stage_ref[0, rows, ls])
        v = stage_ref[2, rows, ls]
        o = _dot_nt((q * jnp.exp(b)).astype(BF16), st.astype(BF16))
        o = o + jnp.sum(q * k, axis=-1, keepdims=True) * v
        for d in range(1, A_SUB):
            ok = row >= d
            w = jnp.exp(jnp.where(ok, b - pltpu.roll(b, d, axis=0), 0.0))
            a = jnp.sum(q * pltpu.roll(k, d, axis=0) * w, axis=-1, keepdims=True)
            o = o + jnp.where(ok, a * pltpu.roll(v, d, axis=0), 0.0)
        oraw_ref[rows, ls] = o
        b_end = b[A_SUB - 1:A_SUB, :]
        kd = (k * jnp.exp(b_end - b)).astype(BF16)
        return st * jnp.exp(b_end) + _dot_tn(v.astype(BF16), kd)

    return lax.fori_loop(0, n_blk, body, st)


LB_SAFE = math.exp(-80.0 / A_SUB)


def _mix_prompt_kernel(bucket_ref, relb_ref, sink_ref, x_ref, gpre_ref, w_ref, lb_ref, gh_ref,
                       oa_ref, ob_ref, gab_ref, st_ref, kvwin_ref,
                       bias_ref, st_scr, kvprev_scr, stage_scr, oraw_scr):
    tile = pl.program_id(1)
    at_start = tile == 0

    @pl.when(jnp.logical_and(pl.program_id(0) == 0, at_start))
    def _():
        _build_bias(bucket_ref, relb_ref, bias_ref, WINDOW)

    @pl.when(at_start)
    def _():
        st_scr[...] = jnp.zeros(st_scr.shape, F32)
        kvprev_scr[...] = jnp.zeros(kvprev_scr.shape, F32)

    x = x_ref[...]
    ms = jnp.mean(x * x, axis=-1, keepdims=True)
    u = (x * lax.rsqrt(ms + EPS) * gpre_ref[...]).astype(BF16)
    proj = lambda seg, lo, hi: jnp.dot(u, w_ref[:, IN_OFFS[seg] + lo:IN_OFFS[seg] + hi],
                                       preferred_element_type=F32)

    lb_all = _lower_bound(lb_ref)
    consts = _HgrnConsts(HGRN_SLAB)
    pair_w = 2 * A_DK
    n_phase = A_HEADS // 2
    n_blocks = MIX_TQ // WINDOW
    assert n_blocks == n_phase
    gate_w = D_MODEL // n_phase
    kw = B_KV_HEADS * B_HD

    def hgrn_proj(hp):
        lo, hi = hp * pair_w, (hp + 1) * pair_w
        return tuple(proj(seg, lo, hi) for seg in (0, SEG_AF, 2, 3, 4))

    def hgrn_pair(hp, q2, f2, v2, og2, z2):
        for hh in range(2):
            h = 2 * hp + hh
            ls = slice(hh * A_DK, (hh + 1) * A_DK)
            hs = slice(h * A_DK, (h + 1) * A_DK)
            st = st_scr[h]
            outs = []
            for s0 in range(0, MIX_TQ, HGRN_SLAB):
                rs = slice(s0, s0 + HGRN_SLAB)
                o, st = _hgrn_slab(f2[rs, ls], q2[rs, ls], v2[rs, ls].astype(BF16), lb_all[:, hs], st, consts)
                outs.append(o)
            st_scr[h] = st
            st_ref[h] = st.T
            oa = _hgrn_finish(jnp.concatenate(outs, axis=0), gh_ref[:, hs], og2[:, ls], z2[:, ls])
            oa_ref[:, hs] = oa.astype(BF16)

    key = lax.broadcasted_iota(jnp.int32, (2 * N_KEYS, PAIRS_PER_KV * WINDOW), 0) & (N_KEYS - 1)
    no_prev = jnp.logical_and(key < WINDOW, at_start)

    def swa_block(j, kv, bq, bz):
        r0 = j * WINDOW
        if j == 0:
            kvj = jnp.concatenate([kvprev_scr[...], kv[0:WINDOW, :]], axis=0)
        else:
            kvj = kv[r0 - WINDOW:r0 + WINDOW, :]
        keys = _swa_keys(kvj[:, 0:kw])
        vals_t = [kvj[:, kw:2 * kw].T.astype(BF16)]
        for kvh in range(B_KV_HEADS):
            q = jnp.concatenate([bq[r0:r0 + WINDOW, p * 128:(p + 1) * 128]
                                 for p in range(kvh * PAIRS_PER_KV, (kvh + 1) * PAIRS_PER_KV)], axis=0)
            logits = _dot_nt(keys[kvh], q) + bias_ref[kvh]
            if j == 0:
                logits = jnp.where(no_prev, NEG, logits)
            acc_t = _swa_attend(logits, vals_t, [None], sink_ref, kvh, WINDOW)
            for pp in range(PAIRS_PER_KV):
                cs = slice((kvh * PAIRS_PER_KV + pp) * 128, (kvh * PAIRS_PER_KV + pp + 1) * 128)
                ob_ref[r0:r0 + WINDOW, cs] = (
                    acc_t[pp * WINDOW:(pp + 1) * WINDOW, :] * _silu(bz[r0:r0 + WINDOW, cs])).astype(BF16)

    def hgrn_pair_blockwise(hp, q2, f2, v2, og2, z2):
        stage_scr[0] = q2
        stage_scr[1] = f2
        stage_scr[2] = v2
        for hh in range(2):
            h = 2 * hp + hh
            ls = slice(hh * A_DK, (hh + 1) * A_DK)
            hs = slice(h * A_DK, (h + 1) * A_DK)
            st = _hgrn_blockwise(stage_scr, oraw_scr, ls, lb_all[:, hs], st_scr[h])
            st_scr[h] = st
            st_ref[h] = st.T
            oa = _hgrn_finish(oraw_scr[:, ls], gh_ref[:, hs], og2[:, ls], z2[:, ls])
            oa_ref[:, hs] = oa.astype(BF16)

    def gate_proj(ph):
        for seg, base in ((9, 0), (10, D_MODEL)):
            gab_ref[:, base + ph * gate_w:base + (ph + 1) * gate_w] = proj(
                seg, ph * gate_w, (ph + 1) * gate_w).astype(BF16)

    def swa_proj():
        kv = proj(SEG_BK, 0, 2 * kw)
        bq = (proj(5, 0, D_MODEL) * (B_HD ** -0.5)).astype(BF16)
        bz = proj(8, 0, D_MODEL)
        return kv, bq, bz

    def keep_window(kv):
        kvprev_scr[...] = kv[MIX_TQ - WINDOW:, :]
        kvwin_ref[...] = kv[MIX_TQ - WINDOW:, :]

    slab_safe = jnp.min(lb_all) >= LB_SAFE

    @pl.when(slab_safe)
    def _():
        kv, bq, bz = swa_proj()
        nxt = hgrn_proj(0)
        for ph in range(n_phase):
            cur = nxt
            if ph + 1 < n_phase:
                nxt = hgrn_proj(ph + 1)
            gate_proj(ph)
            hgrn_pair(ph, *cur)
            swa_block(ph, kv, bq, bz)
        keep_window(kv)

    @pl.when(jnp.logical_not(slab_safe))
    def _():
        for ph in range(n_phase):
            hgrn_pair_blockwise(ph, *hgrn_proj(ph))
            gate_proj(ph)
        kv, bq, bz = swa_proj()
        for j in range(n_blocks):
            swa_block(j, kv, bq, bz)
        keep_window(kv)


def _mix_prompt(bucket, rel_bias, sink, x, g_pre, w, hgrn_lb, hgrn_norm, batch, seq):
    nt = seq // MIX_TQ
    n_lb = hgrn_lb.shape[0]
    kvw = 2 * B_KV_HEADS * B_HD
    smem = pl.BlockSpec(memory_space=pltpu.SMEM)
    tok = lambda width: pl.BlockSpec((MIX_TQ, width), lambda b, t: (b * nt + t, 0))
    return pl.pallas_call(
        _mix_prompt_kernel,
        grid=(batch, nt),
        in_specs=[
            _resident((N_KEYS, WINDOW)),
            smem, smem,
            tok(D_MODEL),
            _resident((1, D_MODEL)),
            _resident(w.shape),
            _resident((n_lb, D_MODEL)),
            _resident((1, D_MODEL)),
        ],
        out_specs=[
            tok(D_MODEL), tok(D_MODEL), tok(2 * D_MODEL),
            pl.BlockSpec((None, A_HEADS, A_DK, A_DV), lambda b, t: (b, 0, 0, 0)),
            pl.BlockSpec((None, WINDOW, kvw), lambda b, t: (b, 0, 0)),
        ],
        out_shape=[
            jax.ShapeDtypeStruct((batch * seq, D_MODEL), BF16),
            jax.ShapeDtypeStruct((batch * seq, D_MODEL), BF16),
            jax.ShapeDtypeStruct((batch * seq, 2 * D_MODEL), BF16),
            jax.ShapeDtypeStruct((batch, A_HEADS, A_DK, A_DV), F32),
            jax.ShapeDtypeStruct((batch, WINDOW, kvw), F32),
        ],
        scratch_shapes=[
            pltpu.VMEM((B_KV_HEADS, 2 * N_KEYS, PAIRS_PER_KV * WINDOW), F32),
            pltpu.VMEM((A_HEADS, A_DV, A_DK), F32),
            pltpu.VMEM((WINDOW, kvw), F32),
            pltpu.VMEM((3, MIX_TQ, 2 * A_DK), F32),
            pltpu.VMEM((MIX_TQ, 2 * A_DK), F32),
        ],
        compiler_params=pltpu.CompilerParams(
            dimension_semantics=("arbitrary", "arbitrary"), vmem_limit_bytes=VMEM_LIMIT),
        name="mix_prompt",
    )(bucket, rel_bias, sink, x, g_pre, w, hgrn_lb, hgrn_norm)


def _outproj_kernel(oa_ref, ob_ref, ga_ref, gb_ref, x_ref, p_ref, wpa_ref, wpb_ref, wo_ref, gpost_ref,
                    wple_ref, wg_ref, y_ref):
    a = jnp.dot(oa_ref[...], wpa_ref[...], preferred_element_type=F32)
    b = jnp.dot(ob_ref[...], wpb_ref[...], preferred_element_type=F32)
    m = _sigmoid(ga_ref[...].astype(F32)) * a + _sigmoid(gb_ref[...].astype(F32)) * b
    y = jnp.dot(m.astype(BF16), wo_ref[...], preferred_element_type=F32)
    y = y * lax.rsqrt(jnp.mean(y * y, axis=-1, keepdims=True) + EPS) * gpost_ref[...]
    x1 = x_ref[...] + y
    gate = _sigmoid(jnp.dot(x1.astype(BF16), wg_ref[...], preferred_element_type=F32))
    e = jnp.dot(p_ref[...].astype(BF16), wple_ref[...], preferred_element_type=F32) * gate
    y_ref[...] = x1 + e


def _outproj(oa, ob, gates, ga_col, gb_col, x, p, wpa, wpb, wo, gpost, wple, wg, tm):
    n = x.shape[0]
    ple = p.shape[1]
    tok = lambda w, c=0: pl.BlockSpec((tm, w), lambda i, c=c: (i, c))
    return pl.pallas_call(
        _outproj_kernel,
        grid=(n // tm,),
        in_specs=[
            tok(D_MODEL), tok(D_MODEL), tok(D_MODEL, ga_col), tok(D_MODEL, gb_col), tok(D_MODEL), tok(ple),
            _resident((D_MODEL, D_MODEL)), _resident((D_MODEL, D_MODEL)), _resident((D_MODEL, D_MODEL)),
            _resident((1, D_MODEL)), _resident((ple, D_MODEL)), _resident((D_MODEL, D_MODEL)),
        ],
        out_specs=tok(D_MODEL),
        out_shape=jax.ShapeDtypeStruct((n, D_MODEL), F32),
        compiler_params=pltpu.CompilerParams(
            dimension_semantics=("arbitrary",), vmem_limit_bytes=VMEM_LIMIT),
        name="outproj",
    )(oa, ob, gates, gates, x, p, wpa, wpb, wo, gpost, wple, wg)


def _rel_bucket(rel):
    n = np.maximum(rel, 0)
    max_exact = REL_BUCKETS // 2
    nf = np.maximum(n, 1).astype(np.float32)
    scaled = (np.log(nf / np.float32(max_exact)) / np.float32(math.log(REL_MAX_DIST / max_exact))
              * np.float32(REL_BUCKETS - max_exact))
    frac = scaled - np.floor(scaled)
    inside = (n > max_exact) & (n < REL_MAX_DIST)
    assert np.all((frac[inside] > 1e-3) & (frac[inside] < 1.0 - 1e-3))
    large = np.minimum(max_exact + scaled.astype(np.int32), REL_BUCKETS - 1)
    return np.where(n < max_exact, n, large)


def _bucket_table(q_pos, k_pos, k_valid):
    rel = q_pos[:, None] - k_pos[None, :]
    ok = (rel >= 0) & (rel < WINDOW) & k_valid[None, :]
    return jnp.asarray(np.where(ok, _rel_bucket(rel), -1).astype(np.int32).T)


TM_PROJ = 512
SAMPLE_BB = 8
SWA_SAMPLE_BB = 8


def _layer(xp, xs, s_hgrn, win_k, win_v, pp, ps, norm_pre, w_in, hgrn_lb, hgrn_norm, attn_sink,
           rel_bias, w_pa, w_pb, w_o, norm_post, w_ple, w_ple_gate):
    batch, seq, _ = xp.shape
    dbatch, t_len, _ = xs.shape
    kw = B_KV_HEADS * B_HD

    w16 = w_in.astype(BF16)
    g_pre = norm_pre.reshape(1, D_MODEL)
    g_post = norm_post.reshape(1, D_MODEL)
    g_hgrn = hgrn_norm.reshape(1, D_MODEL)
    wpa, wpb, wo = w_pa.astype(BF16), w_pb.astype(BF16), w_o.astype(BF16)
    wple, wg = w_ple.astype(BF16), w_ple_gate.astype(BF16)

    xp2 = xp.reshape(batch * seq, D_MODEL)
    xs2 = xs.reshape(dbatch * t_len, D_MODEL)
    k_all = np.arange(N_KEYS)
    bucket_p = _bucket_table(np.arange(WINDOW) + WINDOW, k_all, np.ones((N_KEYS,), bool))
    rows_s = SWA_SAMPLE_BB * t_len
    bucket_s = _bucket_table(WINDOW + np.arange(rows_s) % t_len, k_all, k_all < WINDOW + t_len)

    oa_p, ob_p, gab_p, st_p, kvwin_p = _mix_prompt(bucket_p, rel_bias, attn_sink, xp2, g_pre, w16,
                                                   hgrn_lb, g_hgrn, batch, seq)

    pm_s, f_s, kv_s = _inproj(xs2, g_pre, w16, min(TM_PROJ, dbatch * t_len))
    oa_s, st_s = _hgrn_sample(pm_s, f_s, hgrn_lb, g_hgrn, s_hgrn, dbatch, t_len, SAMPLE_BB)
    ob_s, nk_s, nv_s = _swa_sample(bucket_s, rel_bias, attn_sink, pm_s, kv_s,
                                   win_k.reshape(dbatch, WINDOW, kw), win_v.reshape(dbatch, WINDOW, kw),
                                   dbatch, t_len, SWA_SAMPLE_BB)

    y_p = _outproj(oa_p, ob_p, gab_p, 0, 1, xp2, pp.reshape(batch * seq, -1),
                   wpa, wpb, wo, g_post, wple, wg, TM_PROJ)
    y_s = _outproj(oa_s, ob_s, pm_s, G_GA, G_GB, xs2, ps.reshape(dbatch * t_len, -1),
                   wpa, wpb, wo, g_post, wple, wg, min(TM_PROJ, dbatch * t_len))

    k_win_p = kvwin_p[:, :, 0:kw].reshape(batch, WINDOW, B_KV_HEADS, B_HD)
    v_win_p = kvwin_p[:, :, kw:].reshape(batch, WINDOW, B_KV_HEADS, B_HD)
    return (y_p.reshape(batch, seq, D_MODEL), y_s.reshape(dbatch, t_len, D_MODEL), st_p, st_s,
            k_win_p, v_win_p,
            nk_s.reshape(dbatch, WINDOW, B_KV_HEADS, B_HD), nv_s.reshape(dbatch, WINDOW, B_KV_HEADS, B_HD))


def kernel(x_prompt, x_sample, state_hgrn, cache_swa_k, cache_swa_v, p_prompt, p_sample, norm_pre, w_in,
           hgrn_lb, hgrn_norm, attn_sink, rel_bias, w_pa, w_pb, w_o, norm_post, w_ple, w_ple_gate):
    depth = w_in.shape[0]
    assert depth == 1, "the forget-gate lower bound is implemented for a single layer"
    xp, xs = x_prompt, x_sample
    outs = []
    for l in range(depth):
        res = _layer(xp, xs, state_hgrn[l], cache_swa_k[l], cache_swa_v[l], p_prompt[l], p_sample[l],
                     norm_pre[l], w_in[l], hgrn_lb, hgrn_norm[l], attn_sink[l], rel_bias,
                     w_pa[l], w_pb[l], w_o[l], norm_post[l], w_ple[l], w_ple_gate[l])
        xp, xs = res[0], res[1]
        outs.append(res[2:])
    stack = lambda i: jnp.stack([o[i] for o in outs])
    return (xp, xs, stack(0), stack(1), stack(2), stack(3), stack(4), stack(5))
```

```python
import functools
import math

import jax
import jax.numpy as jnp
import numpy as np
from jax import lax
from jax.experimental import pallas as pl
from jax.experimental.pallas import tpu as pltpu

F32 = jnp.float32
BF16 = jnp.bfloat16

D_MODEL = 1024
A_HEADS = 8
A_DK = 128
A_DV = 128
A_CHUNK = 64
A_SUB = 16
B_HEADS = 16
B_KV_HEADS = 2
B_HD = 64
WINDOW = 128
REL_BUCKETS = 32
REL_MAX_DIST = 128
EPS = 1e-6
NEG = float("-inf")

G_AQ, G_AI, G_AOG, G_AZ, G_BQ, G_BZ, G_GA, G_GB = range(8)
N_GROUPS = 8

VMEM_LIMIT = 56 * 1024 * 1024


def _sigmoid(x):
    return 1.0 / (1.0 + jnp.exp(-x))


def _silu(x):
    return x * _sigmoid(x)


def _resident(shape):
    nd = len(shape)
    return pl.BlockSpec(shape, lambda *_: (0,) * nd, pipeline_mode=pl.Buffered(1))


IN_OFFS = (0, 1024, 2048, 3072, 4096, 5120, 6144, 6272, 6400, 7424, 8448, 9472)
SLAB_SEGS = (0, 2, 3, 4, 5, 8, 9, 10)
SEG_AF, SEG_BK, SEG_BV = 1, 6, 7


def _inproj_kernel(x_ref, g_ref, w_ref, pm_ref, f_ref, kv_ref):
    x = x_ref[...]
    ms = jnp.mean(x * x, axis=-1, keepdims=True)
    u = (x * lax.rsqrt(ms + EPS) * g_ref[...]).astype(BF16)
    proj = lambda lo, hi: jnp.dot(u, w_ref[:, lo:hi], preferred_element_type=F32)
    for c, seg in enumerate(SLAB_SEGS):
        pm_ref[:, c * D_MODEL:(c + 1) * D_MODEL] = proj(IN_OFFS[seg], IN_OFFS[seg + 1]).astype(BF16)
    f_ref[...] = proj(IN_OFFS[SEG_AF], IN_OFFS[SEG_AF + 1])
    kv_ref[...] = proj(IN_OFFS[SEG_BK], IN_OFFS[SEG_BV + 1])


def _inproj(x, g, w, tm):
    n = x.shape[0]
    nm = N_GROUPS * D_MODEL
    nkv = IN_OFFS[SEG_BV + 1] - IN_OFFS[SEG_BK]
    return pl.pallas_call(
        _inproj_kernel,
        grid=(n // tm,),
        in_specs=[
            pl.BlockSpec((tm, D_MODEL), lambda i: (i, 0)),
            _resident((1, D_MODEL)),
            _resident(w.shape),
        ],
        out_specs=[
            pl.BlockSpec((tm, nm), lambda i: (i, 0)),
            pl.BlockSpec((tm, D_MODEL), lambda i: (i, 0)),
            pl.BlockSpec((tm, nkv), lambda i: (i, 0)),
        ],
        out_shape=[
            jax.ShapeDtypeStruct((n, nm), BF16),
            jax.ShapeDtypeStruct((n, D_MODEL), F32),
            jax.ShapeDtypeStruct((n, nkv), F32),
        ],
        compiler_params=pltpu.CompilerParams(
            dimension_semantics=("arbitrary",), vmem_limit_bytes=VMEM_LIMIT),
        name="inproj",
    )(x, g, w)


def _lower_bound(lb_ref):
    l = lb_ref[...]
    m = jnp.max(l, axis=0, keepdims=True)
    e = jnp.exp(l - m)
    return e[0:1, :] / jnp.sum(e, axis=0, keepdims=True)


def _group_cumsum(x, row, period, shifts):
    pos = row & (period - 1)
    for sh in shifts:
        x = x + jnp.where(pos >= sh, pltpu.roll(x, sh, axis=0), 0.0)
    return x


def _hgrn_finish(o, g, og_pre, z_pre):
    o = o * lax.rsqrt(jnp.mean(o * o, axis=-1, keepdims=True) + EPS)
    return o * g * _sigmoid(og_pre) * _silu(z_pre)


def _dot_nt(a, b):
    return lax.dot_general(a, b, (((1,), (1,)), ((), ())), preferred_element_type=F32)


def _dot_tn(a, b):
    return lax.dot_general(a, b, (((0,), (0,)), ((), ())), preferred_element_type=F32)


def _hgrn_sample_kernel(q_ref, v_ref, og_ref, z_ref, f_ref, lb_ref, g_ref, s0_ref, oa_ref, s_ref,
                        *, bb, t_len):
    rows = bb * t_len
    lb_all = _lower_bound(lb_ref)
    row = lax.broadcasted_iota(jnp.int32, (rows, A_DK), 0)
    pos = row & (t_len - 1)
    grp = row >> (t_len.bit_length() - 1)
    shifts = tuple(1 << i for i in range((t_len - 1).bit_length()))
    n_tail = A_DK - rows
    sel_row = lax.broadcasted_iota(jnp.int32, (n_tail, 2 * bb * A_DV), 0)
    sel_blk = lax.broadcasted_iota(jnp.int32, (n_tail, 2 * bb * A_DV), 1) >> (A_DV.bit_length() - 1)
    picks = jnp.logical_and(sel_row < 2 * bb, sel_blk == bb + (sel_row & (bb - 1)))
    rhs_tail = jnp.where(picks, 1.0, 0.0).astype(BF16)
    lhs_pad = jnp.zeros((n_tail - 2 * bb, A_DK), F32)
    v_pad = jnp.zeros((rows, bb * A_DV), F32)
    for h in range(A_HEADS):
        ls = slice(h * A_DK, (h + 1) * A_DK)
        lb = lb_all[:, ls]
        fg = lb + (1.0 - lb) * _sigmoid(f_ref[:, ls])
        logf = jnp.log(fg)
        k = 1.0 - fg
        b = _group_cumsum(logf, row, t_len, shifts)
        q = _silu(q_ref[:, ls].astype(F32))
        v = v_ref[:, ls].astype(F32)

        o = jnp.zeros((rows, A_DV), F32)
        for d in range(t_len):
            if d == 0:
                a = jnp.sum(q * k, axis=-1, keepdims=True)
                o = o + a * v
            else:
                w = jnp.exp(jnp.where(pos >= d, b - pltpu.roll(b, d, axis=0), 0.0))
                a = jnp.sum(q * pltpu.roll(k, d, axis=0) * w, axis=-1, keepdims=True)
                o = o + jnp.where(pos >= d, a * pltpu.roll(v, d, axis=0), 0.0)

        b_last = b
        for d in range(1, t_len):
            b_last = jnp.where(pos == t_len - 1 - d, pltpu.roll(b, rows - d, axis=0), b_last)
        e_last = jnp.exp(b_last)
        qs = q * jnp.exp(b)
        kd = k * jnp.exp(b_last - b)
        e_rows = jnp.concatenate([e_last[j * t_len:j * t_len + 1, :] for j in range(bb)], axis=0)
        e_hi = e_rows.astype(BF16).astype(F32)
        lhs_t = jnp.concatenate([kd, e_hi, e_rows - e_hi, lhs_pad], axis=0).T.astype(BF16)
        v_rows = jnp.concatenate([jnp.where(grp == j, v, 0.0) for j in range(bb)] + [v_pad], axis=1)
        res = jnp.dot(lhs_t, jnp.concatenate([v_rows.astype(BF16), rhs_tail], axis=0),
                      preferred_element_type=F32)
        for j in range(bb):
            s0 = s0_ref[j, h]
            o = o + jnp.dot(jnp.where(grp == j, qs, 0.0).astype(BF16), s0.astype(BF16),
                            preferred_element_type=F32)
            s_ref[j, h] = (res[:, (bb + j) * A_DV:(bb + j + 1) * A_DV] * s0
                           + res[:, j * A_DV:(j + 1) * A_DV])

        oa = _hgrn_finish(o, g_ref[:, ls], og_ref[:, ls].astype(F32), z_ref[:, ls].astype(F32))
        oa_ref[:, ls] = oa.astype(BF16)


N_PAIRS = B_HEADS // 2
PAIRS_PER_KV = N_PAIRS // B_KV_HEADS
N_KEYS = 2 * WINDOW


def _build_bias(bucket_ref, relb_ref, bias_ref, r):
    bucket = bucket_ref[...]
    for h in range(B_HEADS):
        def body(kb, acc, h=h):
            return jnp.where(bucket == kb, relb_ref[kb, h], acc)
        tab = lax.fori_loop(0, REL_BUCKETS, body, jnp.full(bucket.shape, NEG, F32))
        pair, parity = divmod(h, 2)
        kv, pp = divmod(pair, PAIRS_PER_KV)
        bias_ref[kv, parity * N_KEYS:(parity + 1) * N_KEYS, pp * r:(pp + 1) * r] = tab


def _swa_keys(kk):
    lo = lax.broadcasted_iota(jnp.int32, (N_KEYS, 2 * B_HD), 1) < B_HD
    kk_sw = pltpu.roll(kk, B_HD, axis=1)
    slabs = []
    for kv in range(B_KV_HEADS):
        k_lo, k_hi = (kk, kk_sw) if kv == 0 else (kk_sw, kk)
        slabs.append(jnp.concatenate([jnp.where(lo, k_lo, 0.0), jnp.where(lo, 0.0, k_hi)], axis=0).astype(BF16))
    return slabs


def _swa_attend(logits, vals_t, col_masks, sink_ref, kv, r):
    width = PAIRS_PER_KV * r
    lane_pp = lax.broadcasted_iota(jnp.int32, (1, width), 1) >> (r.bit_length() - 1)
    halves = []
    for parity in range(2):
        sink = jnp.zeros((1, width), F32)
        for pp in range(PAIRS_PER_KV):
            sink = jnp.where(lane_pp == pp, sink_ref[(kv * PAIRS_PER_KV + pp) * 2 + parity], sink)
        l = logits[parity * N_KEYS:(parity + 1) * N_KEYS, :]
        m = jnp.maximum(jnp.max(l, axis=0, keepdims=True), sink)
        p = jnp.exp(l - m)
        denom = jnp.sum(p, axis=0, keepdims=True) + jnp.exp(sink - m)
        p16 = p.astype(BF16)
        v_rows = jnp.concatenate([v_t[kv * B_HD:(kv + 1) * B_HD, :] for v_t in vals_t], axis=0)
        pv_all = jnp.dot(v_rows, p16, preferred_element_type=F32)
        pv = None
        for n, mask in enumerate(col_masks):
            part = pv_all[n * B_HD:(n + 1) * B_HD, :]
            pv = part if mask is None else jnp.where(mask, part, 0.0 if pv is None else pv)
        halves.append(pv / denom)
    return jnp.concatenate(halves, axis=0).T


def _swa_sample_kernel(bucket_ref, relb_ref, sink_ref, q_ref, z_ref, kvn_ref, ck_ref, cv_ref,
                       ob_ref, nk_ref, nv_ref, bias_ref, *, bb, t_len):
    rows = bb * t_len
    width = PAIRS_PER_KV * rows

    @pl.when(pl.program_id(0) == 0)
    def _():
        _build_bias(bucket_ref, relb_ref, bias_ref, rows)

    kw = B_KV_HEADS * B_HD
    t_shift = t_len.bit_length() - 1
    row = lax.broadcasted_iota(jnp.int32, (rows, kw), 0)
    seq_of_col = lambda n: (lax.broadcasted_iota(jnp.int32, (n, width), 1) & (rows - 1)) >> t_shift
    seq_l = seq_of_col(2 * N_KEYS)
    seq_v = seq_of_col(B_HD)
    q_pairs = [q_ref[:, p * 128:(p + 1) * 128] * (B_HD ** -0.5) for p in range(N_PAIRS)]
    q_kv = [jnp.concatenate(q_pairs[kv * PAIRS_PER_KV:(kv + 1) * PAIRS_PER_KV], axis=0)
            for kv in range(B_KV_HEADS)]
    kn = kvn_ref[:, 0:kw]
    vn = kvn_ref[:, kw:2 * kw]
    pad = jnp.zeros((N_KEYS - WINDOW - rows, kw), F32)
    logits = [None] * B_KV_HEADS
    vals_t, col_masks = [], []
    for j in range(bb):
        sel = row < t_len
        kj = jnp.where(sel, kn if j == 0 else pltpu.roll(kn, rows - j * t_len, axis=0), 0.0)
        vj = jnp.where(sel, vn if j == 0 else pltpu.roll(vn, rows - j * t_len, axis=0), 0.0)
        kk = jnp.concatenate([ck_ref[j], kj, pad], axis=0)
        vv = jnp.concatenate([cv_ref[j], vj, pad], axis=0)
        keys = _swa_keys(kk)
        for kv in range(B_KV_HEADS):
            lg = _dot_nt(keys[kv], q_kv[kv])
            logits[kv] = jnp.where(seq_l == j, lg, 0.0 if logits[kv] is None else logits[kv])
        vals_t.append(vv.T.astype(BF16))
        col_masks.append(seq_v == j)
        nk_ref[j] = pltpu.roll(kk, N_KEYS - t_len, axis=0)[0:WINDOW, :]
        nv_ref[j] = pltpu.roll(vv, N_KEYS - t_len, axis=0)[0:WINDOW, :]
    for kv in range(B_KV_HEADS):
        acc_t = _swa_attend(logits[kv] + bias_ref[kv], vals_t, col_masks, sink_ref, kv, rows)
        for pp in range(PAIRS_PER_KV):
            p = kv * PAIRS_PER_KV + pp
            z = z_ref[:, p * 128:(p + 1) * 128].astype(F32)
            ob_ref[:, p * 128:(p + 1) * 128] = (acc_t[pp * rows:(pp + 1) * rows, :] * _silu(z)).astype(BF16)


def _mix_sample_kernel(bucket_ref, relb_ref, sink_ref, aq_ref, ai_ref, aog_ref, az_ref, f_ref, lb_ref, g_ref,
                       s0_ref, bq_ref, bz_ref, kvn_ref, ck_ref, cv_ref,
                       oa_ref, s_ref, ob_ref, nk_ref, nv_ref, bias_ref, *, bb, t_len):
    _hgrn_sample_kernel(aq_ref, ai_ref, aog_ref, az_ref, f_ref, lb_ref, g_ref, s0_ref, oa_ref, s_ref,
                        bb=bb, t_len=t_len)
    _swa_sample_kernel(bucket_ref, relb_ref, sink_ref, bq_ref, bz_ref, kvn_ref, ck_ref, cv_ref,
                       ob_ref, nk_ref, nv_ref, bias_ref, bb=bb, t_len=t_len)


def _mix_sample(bucket, rel_bias, sink, pm, f, kvn, hgrn_lb, hgrn_norm, s0, cache_k, cache_v,
                batch, t_len, bb):
    rows = bb * t_len
    kvw = kvn.shape[1]
    kw = kvw // 2
    n_lb = hgrn_lb.shape[0]
    smem = pl.BlockSpec(memory_space=pltpu.SMEM)
    col = lambda grp: pl.BlockSpec((rows, D_MODEL), lambda i, grp=grp: (i, grp))
    tok = pl.BlockSpec((rows, D_MODEL), lambda i: (i, 0))
    st_spec = pl.BlockSpec((bb, A_HEADS, A_DK, A_DV), lambda i: (i, 0, 0, 0))
    cache_spec = pl.BlockSpec((bb, WINDOW, kw), lambda i: (i, 0, 0))
    return pl.pallas_call(
        functools.partial(_mix_sample_kernel, bb=bb, t_len=t_len),
        grid=(batch // bb,),
        in_specs=[
            _resident((N_KEYS, rows)),
            smem, smem,
            col(G_AQ), col(G_AI), col(G_AOG), col(G_AZ),
            tok,
            _resident((n_lb, D_MODEL)),
            _resident((1, D_MODEL)),
            st_spec,
            col(G_BQ), col(G_BZ),
            pl.BlockSpec((rows, kvw), lambda i: (i, 0)),
            cache_spec, cache_spec,
        ],
        out_specs=[tok, st_spec, tok, cache_spec, cache_spec],
        out_shape=[
            jax.ShapeDtypeStruct((batch * t_len, D_MODEL), BF16),
            jax.ShapeDtypeStruct((batch, A_HEADS, A_DK, A_DV), F32),
            jax.ShapeDtypeStruct((batch * t_len, D_MODEL), BF16),
            jax.ShapeDtypeStruct((batch, WINDOW, kw), F32),
            jax.ShapeDtypeStruct((batch, WINDOW, kw), F32),
        ],
        scratch_shapes=[pltpu.VMEM((B_KV_HEADS, 2 * N_KEYS, PAIRS_PER_KV * rows), F32)],
        compiler_params=pltpu.CompilerParams(
            dimension_semantics=("arbitrary",), vmem_limit_bytes=VMEM_LIMIT),
        name="mix_sample",
    )(bucket, rel_bias, sink, pm, pm, pm, pm, f, hgrn_lb, hgrn_norm, s0, pm, pm, kvn, cache_k, cache_v)


MIX_TQ = 512
HGRN_SLAB = 256


class _HgrnConsts:
    def __init__(self, slab):
        shift = A_CHUNK.bit_length() - 1
        self.row = lax.broadcasted_iota(jnp.int32, (slab, A_DK), 0)
        rs = lax.broadcasted_iota(jnp.int32, (slab, slab), 0)
        cs = lax.broadcasted_iota(jnp.int32, (slab, slab), 1)
        self.same_chunk_causal = jnp.logical_and((rs >> shift) == (cs >> shift), cs <= rs)
        self.zero_row = jnp.zeros((1, A_DK), F32)
        self.zero_blk = jnp.zeros((A_SUB, A_DK), BF16)
        self.zero_chunk = jnp.zeros((A_CHUNK, A_DK), BF16)


def _hgrn_slab(f_pre, q_pre, v, lb, st, c):
    slab = f_pre.shape[0]
    n_sub = A_CHUNK // A_SUB
    sc = slab // A_CHUNK
    rep = lambda r, n: jnp.broadcast_to(r, (n, A_DK))
    fg = lb + (1.0 - lb) * _sigmoid(f_pre)
    logf = jnp.log(fg)
    k = 1.0 - fg
    b = _group_cumsum(logf, c.row, A_CHUNK, (1, 2, 4, 8, 16, 32))
    q = _silu(q_pre)

    ends = [b[A_SUB * j + A_SUB - 1:A_SUB * (j + 1), :] for j in range(sc * n_sub)]
    endrow = jnp.concatenate([rep(e, A_SUB) for e in ends], axis=0)
    prevrow = jnp.concatenate(
        [rep(c.zero_row if j % n_sub == 0 else ends[j - 1], A_SUB) for j in range(sc * n_sub)], axis=0)
    tot = [ends[n_sub * ch + n_sub - 1] for ch in range(sc)]
    lastrow = jnp.concatenate([rep(t, A_CHUNK) for t in tot], axis=0)
    kend = k * jnp.exp(endrow - b)
    qd = q * jnp.exp(b - prevrow)
    qs = qd * jnp.exp(prevrow)
    kd = kend * jnp.exp(lastrow - endrow)
    kdiag16 = (k * jnp.exp(prevrow - b)).astype(BF16)
    qd16 = qd.astype(BF16)
    kend16 = kend.astype(BF16)
    qs16 = qs.astype(BF16)
    kd16 = kd.astype(BF16)

    blk = lambda a, j: a[A_SUB * j:A_SUB * (j + 1), :]
    chk = lambda a, ch: a[A_CHUNK * ch:A_CHUNK * (ch + 1), :]

    q_groups, k_groups = [], []
    for i in range(n_sub):
        q_groups.append(jnp.concatenate(
            [blk(qd16, j) if j % n_sub == i else c.zero_blk for j in range(sc * n_sub)], axis=0))
        pieces = []
        for j in range(sc * n_sub):
            ch, jj = divmod(j, n_sub)
            if jj == i:
                pieces.append(blk(kdiag16, j))
            elif jj == i - 1:
                pieces.append(blk(kend16, j))
            elif jj < i:
                pieces.append((blk(kend, j) * jnp.exp(ends[ch * n_sub + i - 1] - ends[j])).astype(BF16))
            else:
                pieces.append(c.zero_blk)
        k_groups.append(jnp.concatenate(pieces, axis=0))
    att = _dot_nt(jnp.concatenate(q_groups, axis=1), jnp.concatenate(k_groups, axis=1))
    att = jnp.where(c.same_chunk_causal, att, 0.0)

    if sc > 1:
        q_groups, k_groups = [], []
        for ch in range(1, sc):
            q_groups.append(jnp.concatenate(
                [chk(qs16, c2) if c2 == ch else c.zero_chunk for c2 in range(sc)], axis=0))
            pieces = []
            for c2 in range(sc):
                if c2 == ch - 1:
                    pieces.append(chk(kd16, c2))
                elif c2 < ch:
                    pieces.append((chk(kd, c2) * jnp.exp(sum(tot[c2 + 1:ch]))).astype(BF16))
                else:
                    pieces.append(c.zero_chunk)
            k_groups.append(jnp.concatenate(pieces, axis=0))
        att = att + _dot_nt(jnp.concatenate(q_groups, axis=1), jnp.concatenate(k_groups, axis=1))
    o = jnp.dot(att.astype(BF16), v, preferred_element_type=F32)

    q0 = jnp.concatenate(
        [chk(qs16, 0)] + [(chk(qs, ch) * jnp.exp(sum(tot[:ch]))).astype(BF16) for ch in range(1, sc)], axis=0)
    o = o + _dot_nt(q0, st.astype(BF16))

    k1 = jnp.concatenate(
        [(chk(kd, ch) * jnp.exp(sum(tot[ch + 1:]))).astype(BF16) for ch in range(sc - 1)]
        + [chk(kd16, sc - 1)], axis=0)
    st = st * jnp.exp(sum(tot)) + _dot_tn(v, k1)
    return o, st


def _hgrn_blockwise(stage_ref, oraw_ref, ls, lb, st):
    n_blk = stage_ref.shape[1] // A_SUB
    row = lax.broadcasted_iota(jnp.int32, (A_SUB, A_DK), 0)
    shifts = tuple(1 << i for i in range((A_SUB - 1).bit_length()))

    def body(i, st):
        rows = pl.ds(pl.multiple_of(i * A_SUB, A_SUB), A_SUB)
        fg = lb + (1.0 - lb) * _sigmoid(stage_ref[1, rows, ls])
        k = 1.0 - fg
        b = _group_cumsum(jnp.log(fg), row, A_SUB, shifts)
        q = _silu(stage_ref[0, rows, ls])
        v = stage_ref[2, rows, ls]
        o = _dot_nt((q * jnp.exp(b)).astype(BF16), st.astype(BF16))
        o = o + jnp.sum(q * k, axis=-1, keepdims=True) * v
        for d in range(1, A_SUB):
            ok = row >= d
            w = jnp.exp(jnp.where(ok, b - pltpu.roll(b, d, axis=0), 0.0))
            a = jnp.sum(q * pltpu.roll(k, d, axis=0) * w, axis=-1, keepdims=True)
            o = o + jnp.where(ok, a * pltpu.roll(v, d, axis=0), 0.0)
        oraw_ref[rows, ls] = o
        b_end = b[A_SUB - 1:A_SUB, :]
        kd = (k * jnp.exp(b_end - b)).astype(BF16)
        return st * jnp.exp(b_end) + _dot_tn(v.astype(BF16), kd)

    return lax.fori_loop(0, n_blk, body, st)


LB_SAFE = math.exp(-80.0 / A_SUB)


def _mix_prompt_kernel(bucket_ref, relb_ref, sink_ref, x_ref, gpre_ref, w_ref, lb_ref, gh_ref,
                       oa_ref, ob_ref, gab_ref, st_ref, kvwin_ref,
                       bias_ref, st_scr, kvprev_scr, stage_scr, oraw_scr):
    tile = pl.program_id(1)
    at_start = tile == 0

    @pl.when(jnp.logical_and(pl.program_id(0) == 0, at_start))
    def _():
        _build_bias(bucket_ref, relb_ref, bias_ref, WINDOW)

    @pl.when(at_start)
    def _():
        st_scr[...] = jnp.zeros(st_scr.shape, F32)
        kvprev_scr[...] = jnp.zeros(kvprev_scr.shape, F32)

    x = x_ref[...]
    ms = jnp.mean(x * x, axis=-1, keepdims=True)
    u = (x * lax.rsqrt(ms + EPS) * gpre_ref[...]).astype(BF16)
    proj = lambda seg, lo, hi: jnp.dot(u, w_ref[:, IN_OFFS[seg] + lo:IN_OFFS[seg] + hi],
                                       preferred_element_type=F32)

    lb_all = _lower_bound(lb_ref)
    consts = _HgrnConsts(HGRN_SLAB)
    pair_w = 2 * A_DK
    n_phase = A_HEADS // 2
    n_blocks = MIX_TQ // WINDOW
    assert n_blocks == n_phase
    gate_w = D_MODEL // n_phase
    kw = B_KV_HEADS * B_HD

    def hgrn_proj(hp):
        lo, hi = hp * pair_w, (hp + 1) * pair_w
        return tuple(proj(seg, lo, hi) for seg in (0, SEG_AF, 2, 3, 4))

    def hgrn_pair(hp, q2, f2, v2, og2, z2):
        for hh in range(2):
            h = 2 * hp + hh
            ls = slice(hh * A_DK, (hh + 1) * A_DK)
            hs = slice(h * A_DK, (h + 1) * A_DK)
            st = st_scr[h]
            outs = []
            for s0 in range(0, MIX_TQ, HGRN_SLAB):
                rs = slice(s0, s0 + HGRN_SLAB)
                o, st = _hgrn_slab(f2[rs, ls], q2[rs, ls], v2[rs, ls].astype(BF16), lb_all[:, hs], st, consts)
                outs.append(o)
            st_scr[h] = st
            st_ref[h] = st.T
            oa = _hgrn_finish(jnp.concatenate(outs, axis=0), gh_ref[:, hs], og2[:, ls], z2[:, ls])
            oa_ref[:, hs] = oa.astype(BF16)

    key = lax.broadcasted_iota(jnp.int32, (2 * N_KEYS, PAIRS_PER_KV * WINDOW), 0) & (N_KEYS - 1)
    no_prev = jnp.logical_and(key < WINDOW, at_start)

    def swa_block(j, kv, bq, bz):
        r0 = j * WINDOW
        if j == 0:
            kvj = jnp.concatenate([kvprev_scr[...], kv[0:WINDOW, :]], axis=0)
        else:
            kvj = kv[r0 - WINDOW:r0 + WINDOW, :]
        keys = _swa_keys(kvj[:, 0:kw])
        vals_t = [kvj[:, kw:2 * kw].T.astype(BF16)]
        for kvh in range(B_KV_HEADS):
            q = jnp.concatenate([bq[r0:r0 + WINDOW, p * 128:(p + 1) * 128]
                                 for p in range(kvh * PAIRS_PER_KV, (kvh + 1) * PAIRS_PER_KV)], axis=0)
            logits = _dot_nt(keys[kvh], q) + bias_ref[kvh]
            if j == 0:
                logits = jnp.where(no_prev, NEG, logits)
            acc_t = _swa_attend(logits, vals_t, [None], sink_ref, kvh, WINDOW)
            for pp in range(PAIRS_PER_KV):
                cs = slice((kvh * PAIRS_PER_KV + pp) * 128, (kvh * PAIRS_PER_KV + pp + 1) * 128)
                ob_ref[r0:r0 + WINDOW, cs] = (
                    acc_t[pp * WINDOW:(pp + 1) * WINDOW, :] * _silu(bz[r0:r0 + WINDOW, cs])).astype(BF16)

    def hgrn_pair_blockwise(hp, q2, f2, v2, og2, z2):
        stage_scr[0] = q2
        stage_scr[1] = f2
        stage_scr[2] = v2
        for hh in range(2):
            h = 2 * hp + hh
            ls = slice(hh * A_DK, (hh + 1) * A_DK)
            hs = slice(h * A_DK, (h + 1) * A_DK)
            st = _hgrn_blockwise(stage_scr, oraw_scr, ls, lb_all[:, hs], st_scr[h])
            st_scr[h] = st
            st_ref[h] = st.T
            oa = _hgrn_finish(oraw_scr[:, ls], gh_ref[:, hs], og2[:, ls], z2[:, ls])
            oa_ref[:, hs] = oa.astype(BF16)

    def gate_proj(ph):
        for seg, base in ((9, 0), (10, D_MODEL)):
            gab_ref[:, base + ph * gate_w:base + (ph + 1) * gate_w] = proj(
                seg, ph * gate_w, (ph + 1) * gate_w).astype(BF16)

    def swa_proj():
        kv = proj(SEG_BK, 0, 2 * kw)
        bq = (proj(5, 0, D_MODEL) * (B_HD ** -0.5)).astype(BF16)
        bz = proj(8, 0, D_MODEL)
        return kv, bq, bz

    def keep_window(kv):
        kvprev_scr[...] = kv[MIX_TQ - WINDOW:, :]
        kvwin_ref[...] = kv[MIX_TQ - WINDOW:, :]

    slab_safe = jnp.min(lb_all) >= LB_SAFE

    @pl.when(slab_safe)
    def _():
        kv, bq, bz = swa_proj()
        nxt = hgrn_proj(0)
        for ph in range(n_phase):
            cur = nxt
            if ph + 1 < n_phase:
                nxt = hgrn_proj(ph + 1)
            gate_proj(ph)
            hgrn_pair(ph, *cur)
            swa_block(ph, kv, bq, bz)
        keep_window(kv)

    @pl.when(jnp.logical_not(slab_safe))
    def _():
        for ph in range(n_phase):
            hgrn_pair_blockwise(ph, *hgrn_proj(ph))
            gate_proj(ph)
        kv, bq, bz = swa_proj()
        for j in range(n_blocks):
            swa_block(j, kv, bq, bz)
        keep_window(kv)


def _mix_prompt(bucket, rel_bias, sink, x, g_pre, w, hgrn_lb, hgrn_norm, batch, seq):
    nt = seq // MIX_TQ
    n_lb = hgrn_lb.shape[0]
    kvw = 2 * B_KV_HEADS * B_HD
    smem = pl.BlockSpec(memory_space=pltpu.SMEM)
    tok = lambda width: pl.BlockSpec((MIX_TQ, width), lambda b, t: (b * nt + t, 0))
    return pl.pallas_call(
        _mix_prompt_kernel,
        grid=(batch, nt),
        in_specs=[
            _resident((N_KEYS, WINDOW)),
            smem, smem,
            tok(D_MODEL),
            _resident((1, D_MODEL)),
            _resident(w.shape),
            _resident((n_lb, D_MODEL)),
            _resident((1, D_MODEL)),
        ],
        out_specs=[
            tok(D_MODEL), tok(D_MODEL), tok(2 * D_MODEL),
            pl.BlockSpec((None, A_HEADS, A_DK, A_DV), lambda b, t: (b, 0, 0, 0)),
            pl.BlockSpec((None, WINDOW, kvw), lambda b, t: (b, 0, 0)),
        ],
        out_shape=[
            jax.ShapeDtypeStruct((batch * seq, D_MODEL), BF16),
            jax.ShapeDtypeStruct((batch * seq, D_MODEL), BF16),
            jax.ShapeDtypeStruct((batch * seq, 2 * D_MODEL), BF16),
            jax.ShapeDtypeStruct((batch, A_HEADS, A_DK, A_DV), F32),
            jax.ShapeDtypeStruct((batch, WINDOW, kvw), F32),
        ],
        scratch_shapes=[
            pltpu.VMEM((B_KV_HEADS, 2 * N_KEYS, PAIRS_PER_KV * WINDOW), F32),
            pltpu.VMEM((A_HEADS, A_DV, A_DK), F32),
            pltpu.VMEM((WINDOW, kvw), F32),
            pltpu.VMEM((3, MIX_TQ, 2 * A_DK), F32),
            pltpu.VMEM((MIX_TQ, 2 * A_DK), F32),
        ],
        compiler_params=pltpu.CompilerParams(
            dimension_semantics=("arbitrary", "arbitrary"), vmem_limit_bytes=VMEM_LIMIT),
        name="mix_prompt",
    )(bucket, rel_bias, sink, x, g_pre, w, hgrn_lb, hgrn_norm)


def _outproj_kernel(oa_ref, ob_ref, ga_ref, gb_ref, x_ref, p_ref, wpa_ref, wpb_ref, wo_ref, gpost_ref,
                    wple_ref, wg_ref, y_ref):
    a = jnp.dot(oa_ref[...], wpa_ref[...], preferred_element_type=F32)
    b = jnp.dot(ob_ref[...], wpb_ref[...], preferred_element_type=F32)
    m = _sigmoid(ga_ref[...].astype(F32)) * a + _sigmoid(gb_ref[...].astype(F32)) * b
    y = jnp.dot(m.astype(BF16), wo_ref[...], preferred_element_type=F32)
    y = y * lax.rsqrt(jnp.mean(y * y, axis=-1, keepdims=True) + EPS) * gpost_ref[...]
    x1 = x_ref[...] + y
    gate = _sigmoid(jnp.dot(x1.astype(BF16), wg_ref[...], preferred_element_type=F32))
    e = jnp.dot(p_ref[...].astype(BF16), wple_ref[...], preferred_element_type=F32) * gate
    y_ref[...] = x1 + e


def _outproj(oa, ob, gates, ga_col, gb_col, x, p, wpa, wpb, wo, gpost, wple, wg, tm):
    n = x.shape[0]
    ple = p.shape[1]
    tok = lambda w, c=0: pl.BlockSpec((tm, w), lambda i, c=c: (i, c))
    return pl.pallas_call(
        _outproj_kernel,
        grid=(n // tm,),
        in_specs=[
            tok(D_MODEL), tok(D_MODEL), tok(D_MODEL, ga_col), tok(D_MODEL, gb_col), tok(D_MODEL), tok(ple),
            _resident((D_MODEL, D_MODEL)), _resident((D_MODEL, D_MODEL)), _resident((D_MODEL, D_MODEL)),
            _resident((1, D_MODEL)), _resident((ple, D_MODEL)), _resident((D_MODEL, D_MODEL)),
        ],
        out_specs=tok(D_MODEL),
        out_shape=jax.ShapeDtypeStruct((n, D_MODEL), F32),
        compiler_params=pltpu.CompilerParams(
            dimension_semantics=("arbitrary",), vmem_limit_bytes=VMEM_LIMIT),
        name="outproj",
    )(oa, ob, gates, gates, x, p, wpa, wpb, wo, gpost, wple, wg)


def _rel_bucket(rel):
    n = np.maximum(rel, 0)
    max_exact = REL_BUCKETS // 2
    nf = np.maximum(n, 1).astype(np.float32)
    scaled = (np.log(nf / np.float32(max_exact)) / np.float32(math.log(REL_MAX_DIST / max_exact))
              * np.float32(REL_BUCKETS - max_exact))
    frac = scaled - np.floor(scaled)
    inside = (n > max_exact) & (n < REL_MAX_DIST)
    assert np.all((frac[inside] > 1e-3) & (frac[inside] < 1.0 - 1e-3))
    large = np.minimum(max_exact + scaled.astype(np.int32), REL_BUCKETS - 1)
    return np.where(n < max_exact, n, large)


def _bucket_table(q_pos, k_pos, k_valid):
    rel = q_pos[:, None] - k_pos[None, :]
    ok = (rel >= 0) & (rel < WINDOW) & k_valid[None, :]
    return jnp.asarray(np.where(ok, _rel_bucket(rel), -1).astype(np.int32).T)


TM_PROJ = 512
SAMPLE_BB = 8


def _layer(xp, xs, s_hgrn, win_k, win_v, pp, ps, norm_pre, w_in, hgrn_lb, hgrn_norm, attn_sink,
           rel_bias, w_pa, w_pb, w_o, norm_post, w_ple, w_ple_gate):
    batch, seq, _ = xp.shape
    dbatch, t_len, _ = xs.shape
    kw = B_KV_HEADS * B_HD

    w16 = w_in.astype(BF16)
    g_pre = norm_pre.reshape(1, D_MODEL)
    g_post = norm_post.reshape(1, D_MODEL)
    g_hgrn = hgrn_norm.reshape(1, D_MODEL)
    wpa, wpb, wo = w_pa.astype(BF16), w_pb.astype(BF16), w_o.astype(BF16)
    wple, wg = w_ple.astype(BF16), w_ple_gate.astype(BF16)

    xp2 = xp.reshape(batch * seq, D_MODEL)
    xs2 = xs.reshape(dbatch * t_len, D_MODEL)
    k_all = np.arange(N_KEYS)
    bucket_p = _bucket_table(np.arange(WINDOW) + WINDOW, k_all, np.ones((N_KEYS,), bool))
    rows_s = SAMPLE_BB * t_len
    bucket_s = _bucket_table(WINDOW + np.arange(rows_s) % t_len, k_all, k_all < WINDOW + t_len)

    oa_p, ob_p, gab_p, st_p, kvwin_p = _mix_prompt(bucket_p, rel_bias, attn_sink, xp2, g_pre, w16,
                                                   hgrn_lb, g_hgrn, batch, seq)

    pm_s, f_s, kv_s = _inproj(xs2, g_pre, w16, min(TM_PROJ, dbatch * t_len))
    oa_s, st_s, ob_s, nk_s, nv_s = _mix_sample(bucket_s, rel_bias, attn_sink, pm_s, f_s, kv_s, hgrn_lb, g_hgrn,
                                               s_hgrn, win_k.reshape(dbatch, WINDOW, kw),
                                               win_v.reshape(dbatch, WINDOW, kw), dbatch, t_len, SAMPLE_BB)

    y_p = _outproj(oa_p, ob_p, gab_p, 0, 1, xp2, pp.reshape(batch * seq, -1),
                   wpa, wpb, wo, g_post, wple, wg, TM_PROJ)
    y_s = _outproj(oa_s, ob_s, pm_s, G_GA, G_GB, xs2, ps.reshape(dbatch * t_len, -1),
                   wpa, wpb, wo, g_post, wple, wg, min(TM_PROJ, dbatch * t_len))

    k_win_p = kvwin_p[:, :, 0:kw].reshape(batch, WINDOW, B_KV_HEADS, B_HD)
    v_win_p = kvwin_p[:, :, kw:].reshape(batch, WINDOW, B_KV_HEADS, B_HD)
    return (y_p.reshape(batch, seq, D_MODEL), y_s.reshape(dbatch, t_len, D_MODEL), st_p, st_s,
            k_win_p, v_win_p,
            nk_s.reshape(dbatch, WINDOW, B_KV_HEADS, B_HD), nv_s.reshape(dbatch, WINDOW, B_KV_HEADS, B_HD))


def kernel(x_prompt, x_sample, state_hgrn, cache_swa_k, cache_swa_v, p_prompt, p_sample, norm_pre, w_in,
           hgrn_lb, hgrn_norm, attn_sink, rel_bias, w_pa, w_pb, w_o, norm_post, w_ple, w_ple_gate):
    depth = w_in.shape[0]
    assert depth == 1, "the forget-gate lower bound is implemented for a single layer"
    xp, xs = x_prompt, x_sample
    outs = []
    for l in range(depth):
        res = _layer(xp, xs, state_hgrn[l], cache_swa_k[l], cache_swa_v[l], p_prompt[l], p_sample[l],
                     norm_pre[l], w_in[l], hgrn_lb, hgrn_norm[l], attn_sink[l], rel_bias,
                     w_pa[l], w_pb[l], w_o[l], norm_post[l], w_ple[l], w_ple_gate[l])
        xp, xs = res[0], res[1]
        outs.append(res[2:])
    stack = lambda i: jnp.stack([o[i] for o in outs])
    return (xp, xs, stack(0), stack(1), stack(2), stack(3), stack(4), stack(5))
```

```python
import functools
import math

import jax
import jax.numpy as jnp
import numpy as np
from jax import lax
from jax.experimental import pallas as pl
from jax.experimental.pallas import tpu as pltpu

F32 = jnp.float32
BF16 = jnp.bfloat16

D_MODEL = 1024
A_HEADS = 8
A_DK = 128
A_DV = 128
A_CHUNK = 64
A_SUB = 16
B_HEADS = 16
B_KV_HEADS = 2
B_HD = 64
WINDOW = 128
REL_BUCKETS = 32
REL_MAX_DIST = 128
EPS = 1e-6
NEG = float("-inf")
LOG2E = math.log2(math.e)
Q_SCALE = B_HD ** -0.5 * LOG2E

G_AQ, G_AI, G_AOG, G_AZ, G_BQ, G_BZ, G_GA, G_GB = range(8)
N_GROUPS = 8

VMEM_LIMIT = 56 * 1024 * 1024


def _sigmoid(x):
    return 1.0 / (1.0 + jnp.exp(-x))


def _silu(x):
    return x * _sigmoid(x)


def _resident(shape):
    nd = len(shape)
    return pl.BlockSpec(shape, lambda *_: (0,) * nd, pipeline_mode=pl.Buffered(1))


IN_OFFS = (0, 1024, 2048, 3072, 4096, 5120, 6144, 6272, 6400, 7424, 8448, 9472)
SLAB_SEGS = (0, 2, 3, 4, 5, 8, 9, 10)
SEG_AF, SEG_BK, SEG_BV = 1, 6, 7


def _inproj_kernel(x_ref, g_ref, w_ref, pm_ref, f_ref, kv_ref):
    x = x_ref[...]
    ms = jnp.mean(x * x, axis=-1, keepdims=True)
    u = (x * lax.rsqrt(ms + EPS) * g_ref[...]).astype(BF16)
    proj = lambda lo, hi: jnp.dot(u, w_ref[:, lo:hi], preferred_element_type=F32)
    for c, seg in enumerate(SLAB_SEGS):
        pm_ref[:, c * D_MODEL:(c + 1) * D_MODEL] = proj(IN_OFFS[seg], IN_OFFS[seg + 1]).astype(BF16)
    f_ref[...] = proj(IN_OFFS[SEG_AF], IN_OFFS[SEG_AF + 1])
    kv_ref[...] = proj(IN_OFFS[SEG_BK], IN_OFFS[SEG_BV + 1])


def _inproj(x, g, w, tm):
    n = x.shape[0]
    nm = N_GROUPS * D_MODEL
    nkv = IN_OFFS[SEG_BV + 1] - IN_OFFS[SEG_BK]
    return pl.pallas_call(
        _inproj_kernel,
        grid=(n // tm,),
        in_specs=[
            pl.BlockSpec((tm, D_MODEL), lambda i: (i, 0)),
            _resident((1, D_MODEL)),
            _resident(w.shape),
        ],
        out_specs=[
            pl.BlockSpec((tm, nm), lambda i: (i, 0)),
            pl.BlockSpec((tm, D_MODEL), lambda i: (i, 0)),
            pl.BlockSpec((tm, nkv), lambda i: (i, 0)),
        ],
        out_shape=[
            jax.ShapeDtypeStruct((n, nm), BF16),
            jax.ShapeDtypeStruct((n, D_MODEL), F32),
            jax.ShapeDtypeStruct((n, nkv), F32),
        ],
        compiler_params=pltpu.CompilerParams(
            dimension_semantics=("arbitrary",), vmem_limit_bytes=VMEM_LIMIT),
        name="inproj",
    )(x, g, w)


def _lower_bound(lb_ref):
    l = lb_ref[...]
    m = jnp.max(l, axis=0, keepdims=True)
    e = jnp.exp(l - m)
    return e[0:1, :] / jnp.sum(e, axis=0, keepdims=True)


def _group_cumsum(x, row, period, shifts):
    pos = row & (period - 1)
    for sh in shifts:
        x = x + jnp.where(pos >= sh, pltpu.roll(x, sh, axis=0), 0.0)
    return x


def _hgrn_finish(o, g, og_pre, z_pre):
    o = o * lax.rsqrt(jnp.mean(o * o, axis=-1, keepdims=True) + EPS)
    return o * g * _sigmoid(og_pre) * _silu(z_pre)


def _dot_nt(a, b):
    return lax.dot_general(a, b, (((1,), (1,)), ((), ())), preferred_element_type=F32)


def _dot_tn(a, b):
    return lax.dot_general(a, b, (((0,), (0,)), ((), ())), preferred_element_type=F32)


def _hgrn_sample_kernel(q_ref, v_ref, og_ref, z_ref, f_ref, lb_ref, g_ref, s0_ref, oa_ref, s_ref,
                        *, bb, t_len):
    rows = bb * t_len
    lb_all = _lower_bound(lb_ref)
    row = lax.broadcasted_iota(jnp.int32, (rows, A_DK), 0)
    pos = row & (t_len - 1)
    grp = row >> (t_len.bit_length() - 1)
    shifts = tuple(1 << i for i in range((t_len - 1).bit_length()))
    n_tail = A_DK - rows
    sel_row = lax.broadcasted_iota(jnp.int32, (n_tail, 2 * bb * A_DV), 0)
    sel_blk = lax.broadcasted_iota(jnp.int32, (n_tail, 2 * bb * A_DV), 1) >> (A_DV.bit_length() - 1)
    picks = jnp.logical_and(sel_row < 2 * bb, sel_blk == bb + (sel_row & (bb - 1)))
    rhs_tail = jnp.where(picks, 1.0, 0.0).astype(BF16)
    lhs_pad = jnp.zeros((n_tail - 2 * bb, A_DK), F32)
    v_pad = jnp.zeros((rows, bb * A_DV), F32)
    for h in range(A_HEADS):
        ls = slice(h * A_DK, (h + 1) * A_DK)
        lb = lb_all[:, ls]
        fg = lb + (1.0 - lb) * _sigmoid(f_ref[:, ls])
        logf = jnp.log(fg)
        k = 1.0 - fg
        b = _group_cumsum(logf, row, t_len, shifts)
        q = _silu(q_ref[:, ls].astype(F32))
        v = v_ref[:, ls].astype(F32)

        o = jnp.zeros((rows, A_DV), F32)
        for d in range(t_len):
            if d == 0:
                a = jnp.sum(q * k, axis=-1, keepdims=True)
                o = o + a * v
            else:
                w = jnp.exp(jnp.where(pos >= d, b - pltpu.roll(b, d, axis=0), 0.0))
                a = jnp.sum(q * pltpu.roll(k, d, axis=0) * w, axis=-1, keepdims=True)
                o = o + jnp.where(pos >= d, a * pltpu.roll(v, d, axis=0), 0.0)

        b_last = b
        for d in range(1, t_len):
            b_last = jnp.where(pos == t_len - 1 - d, pltpu.roll(b, rows - d, axis=0), b_last)
        e_last = jnp.exp(b_last)
        qs = q * jnp.exp(b)
        kd = k * jnp.exp(b_last - b)
        e_rows = jnp.concatenate([e_last[j * t_len:j * t_len + 1, :] for j in range(bb)], axis=0)
        e_hi = e_rows.astype(BF16).astype(F32)
        lhs_t = jnp.concatenate([kd, e_hi, e_rows - e_hi, lhs_pad], axis=0).T.astype(BF16)
        v_rows = jnp.concatenate([jnp.where(grp == j, v, 0.0) for j in range(bb)] + [v_pad], axis=1)
        res = jnp.dot(lhs_t, jnp.concatenate([v_rows.astype(BF16), rhs_tail], axis=0),
                      preferred_element_type=F32)
        for j in range(bb):
            s0 = s0_ref[j, h]
            o = o + jnp.dot(jnp.where(grp == j, qs, 0.0).astype(BF16), s0.astype(BF16),
                            preferred_element_type=F32)
            s_ref[j, h] = (res[:, (bb + j) * A_DV:(bb + j + 1) * A_DV] * s0
                           + res[:, j * A_DV:(j + 1) * A_DV])

        oa = _hgrn_finish(o, g_ref[:, ls], og_ref[:, ls].astype(F32), z_ref[:, ls].astype(F32))
        oa_ref[:, ls] = oa.astype(BF16)


N_PAIRS = B_HEADS // 2
PAIRS_PER_KV = N_PAIRS // B_KV_HEADS
N_KEYS = 2 * WINDOW


def _build_bias(bucket_ref, relb_ref, bias_ref, r):
    bucket = bucket_ref[...]
    for h in range(B_HEADS):
        def body(kb, acc, h=h):
            return jnp.where(bucket == kb, relb_ref[kb, h] * LOG2E, acc)
        tab = lax.fori_loop(0, REL_BUCKETS, body, jnp.full(bucket.shape, NEG, F32))
        pair, parity = divmod(h, 2)
        kv, pp = divmod(pair, PAIRS_PER_KV)
        bias_ref[kv, parity * N_KEYS:(parity + 1) * N_KEYS, pp * r:(pp + 1) * r] = tab


def _swa_keys(kk):
    lo = lax.broadcasted_iota(jnp.int32, (N_KEYS, 2 * B_HD), 1) < B_HD
    kk_sw = pltpu.roll(kk, B_HD, axis=1)
    slabs = []
    for kv in range(B_KV_HEADS):
        k_lo, k_hi = (kk, kk_sw) if kv == 0 else (kk_sw, kk)
        slabs.append(jnp.concatenate([jnp.where(lo, k_lo, 0.0), jnp.where(lo, 0.0, k_hi)], axis=0).astype(BF16))
    return slabs


def _swa_attend(logits, vals_t, col_masks, sink_ref, kv, r):
    width = PAIRS_PER_KV * r
    lane_pp = lax.broadcasted_iota(jnp.int32, (1, width), 1) >> (r.bit_length() - 1)
    halves = []
    for parity in range(2):
        sink = jnp.zeros((1, width), F32)
        for pp in range(PAIRS_PER_KV):
            sink = jnp.where(lane_pp == pp, sink_ref[(kv * PAIRS_PER_KV + pp) * 2 + parity] * LOG2E, sink)
        l = logits[parity * N_KEYS:(parity + 1) * N_KEYS, :]
        m = jnp.maximum(jnp.max(l, axis=0, keepdims=True), sink)
        p = jnp.exp2(l - m)
        denom = jnp.sum(p, axis=0, keepdims=True) + jnp.exp2(sink - m)
        p16 = p.astype(BF16)
        v_rows = jnp.concatenate([v_t[kv * B_HD:(kv + 1) * B_HD, :] for v_t in vals_t], axis=0)
        pv_all = jnp.dot(v_rows, p16, preferred_element_type=F32)
        pv = None
        for n, mask in enumerate(col_masks):
            part = pv_all[n * B_HD:(n + 1) * B_HD, :]
            pv = part if mask is None else jnp.where(mask, part, 0.0 if pv is None else pv)
        halves.append(pv / denom)
    return jnp.concatenate(halves, axis=0).T


def _swa_sample_kernel(bucket_ref, relb_ref, sink_ref, q_ref, z_ref, kvn_ref, ck_ref, cv_ref,
                       ob_ref, nk_ref, nv_ref, bias_ref, *, bb, t_len):
    rows = bb * t_len
    width = PAIRS_PER_KV * rows

    @pl.when(pl.program_id(0) == 0)
    def _():
        _build_bias(bucket_ref, relb_ref, bias_ref, rows)

    kw = B_KV_HEADS * B_HD
    t_shift = t_len.bit_length() - 1
    row = lax.broadcasted_iota(jnp.int32, (rows, kw), 0)
    seq_of_col = lambda n: (lax.broadcasted_iota(jnp.int32, (n, width), 1) & (rows - 1)) >> t_shift
    seq_l = seq_of_col(2 * N_KEYS)
    seq_v = seq_of_col(B_HD)
    q_pairs = [(q_ref[:, p * 128:(p + 1) * 128].astype(F32) * Q_SCALE).astype(BF16) for p in range(N_PAIRS)]
    q_kv = [jnp.concatenate(q_pairs[kv * PAIRS_PER_KV:(kv + 1) * PAIRS_PER_KV], axis=0)
            for kv in range(B_KV_HEADS)]
    kn = kvn_ref[:, 0:kw]
    vn = kvn_ref[:, kw:2 * kw]
    pad = jnp.zeros((N_KEYS - WINDOW - rows, kw), F32)
    logits = [None] * B_KV_HEADS
    vals_t, col_masks = [], []
    for j in range(bb):
        sel = row < t_len
        kj = jnp.where(sel, kn if j == 0 else pltpu.roll(kn, rows - j * t_len, axis=0), 0.0)
        vj = jnp.where(sel, vn if j == 0 else pltpu.roll(vn, rows - j * t_len, axis=0), 0.0)
        kk = jnp.concatenate([ck_ref[j], kj, pad], axis=0)
        vv = jnp.concatenate([cv_ref[j], vj, pad], axis=0)
        keys = _swa_keys(kk)
        for kv in range(B_KV_HEADS):
            lg = _dot_nt(keys[kv], q_kv[kv])
            logits[kv] = jnp.where(seq_l == j, lg, 0.0 if logits[kv] is None else logits[kv])
        vals_t.append(vv.T.astype(BF16))
        col_masks.append(seq_v == j)
        nk_ref[j] = pltpu.roll(kk, N_KEYS - t_len, axis=0)[0:WINDOW, :]
        nv_ref[j] = pltpu.roll(vv, N_KEYS - t_len, axis=0)[0:WINDOW, :]
    for kv in range(B_KV_HEADS):
        acc_t = _swa_attend(logits[kv] + bias_ref[kv], vals_t, col_masks, sink_ref, kv, rows)
        for pp in range(PAIRS_PER_KV):
            p = kv * PAIRS_PER_KV + pp
            z = z_ref[:, p * 128:(p + 1) * 128].astype(F32)
            ob_ref[:, p * 128:(p + 1) * 128] = (acc_t[pp * rows:(pp + 1) * rows, :] * _silu(z)).astype(BF16)


def _mix_sample_kernel(bucket_ref, relb_ref, sink_ref, aq_ref, ai_ref, aog_ref, az_ref, f_ref, lb_ref, g_ref,
                       s0_ref, bq_ref, bz_ref, kvn_ref, ck_ref, cv_ref,
                       oa_ref, s_ref, ob_ref, nk_ref, nv_ref, bias_ref, *, bb, t_len):
    _hgrn_sample_kernel(aq_ref, ai_ref, aog_ref, az_ref, f_ref, lb_ref, g_ref, s0_ref, oa_ref, s_ref,
                        bb=bb, t_len=t_len)
    _swa_sample_kernel(bucket_ref, relb_ref, sink_ref, bq_ref, bz_ref, kvn_ref, ck_ref, cv_ref,
                       ob_ref, nk_ref, nv_ref, bias_ref, bb=bb, t_len=t_len)


def _mix_sample(bucket, rel_bias, sink, pm, f, kvn, hgrn_lb, hgrn_norm, s0, cache_k, cache_v,
                batch, t_len, bb):
    rows = bb * t_len
    kvw = kvn.shape[1]
    kw = kvw // 2
    n_lb = hgrn_lb.shape[0]
    smem = pl.BlockSpec(memory_space=pltpu.SMEM)
    col = lambda grp: pl.BlockSpec((rows, D_MODEL), lambda i, grp=grp: (i, grp))
    tok = pl.BlockSpec((rows, D_MODEL), lambda i: (i, 0))
    st_spec = pl.BlockSpec((bb, A_HEADS, A_DK, A_DV), lambda i: (i, 0, 0, 0))
    cache_spec = pl.BlockSpec((bb, WINDOW, kw), lambda i: (i, 0, 0))
    return pl.pallas_call(
        functools.partial(_mix_sample_kernel, bb=bb, t_len=t_len),
        grid=(batch // bb,),
        in_specs=[
            _resident((N_KEYS, rows)),
            smem, smem,
            col(G_AQ), col(G_AI), col(G_AOG), col(G_AZ),
            tok,
            _resident((n_lb, D_MODEL)),
            _resident((1, D_MODEL)),
            st_spec,
            col(G_BQ), col(G_BZ),
            pl.BlockSpec((rows, kvw), lambda i: (i, 0)),
            cache_spec, cache_spec,
        ],
        out_specs=[tok, st_spec, tok, cache_spec, cache_spec],
        out_shape=[
            jax.ShapeDtypeStruct((batch * t_len, D_MODEL), BF16),
            jax.ShapeDtypeStruct((batch, A_HEADS, A_DK, A_DV), F32),
            jax.ShapeDtypeStruct((batch * t_len, D_MODEL), BF16),
            jax.ShapeDtypeStruct((batch, WINDOW, kw), F32),
            jax.ShapeDtypeStruct((batch, WINDOW, kw), F32),
        ],
        scratch_shapes=[pltpu.VMEM((B_KV_HEADS, 2 * N_KEYS, PAIRS_PER_KV * rows), F32)],
        compiler_params=pltpu.CompilerParams(
            dimension_semantics=("arbitrary",), vmem_limit_bytes=VMEM_LIMIT),
        name="mix_sample",
    )(bucket, rel_bias, sink, pm, pm, pm, pm, f, hgrn_lb, hgrn_norm, s0, pm, pm, kvn, cache_k, cache_v)


MIX_TQ = 512
HGRN_SLAB = 256


class _HgrnConsts:
    def __init__(self, slab):
        shift = A_CHUNK.bit_length() - 1
        self.row = lax.broadcasted_iota(jnp.int32, (slab, A_DK), 0)
        rs = lax.broadcasted_iota(jnp.int32, (slab, slab), 0)
        cs = lax.broadcasted_iota(jnp.int32, (slab, slab), 1)
        self.same_chunk_causal = jnp.logical_and((rs >> shift) == (cs >> shift), cs <= rs)
        self.zero_row = jnp.zeros((1, A_DK), F32)
        self.zero_blk = jnp.zeros((A_SUB, A_DK), BF16)
        self.zero_chunk = jnp.zeros((A_CHUNK, A_DK), BF16)


def _hgrn_slab(f_pre, q_pre, v, lb, st, c):
    slab = f_pre.shape[0]
    n_sub = A_CHUNK // A_SUB
    sc = slab // A_CHUNK
    rep = lambda r, n: jnp.broadcast_to(r, (n, A_DK))
    fg = lb + (1.0 - lb) * _sigmoid(f_pre)
    logf = jnp.log(fg)
    k = 1.0 - fg
    b = _group_cumsum(logf, c.row, A_CHUNK, (1, 2, 4, 8, 16, 32))
    q = _silu(q_pre)

    ends = [b[A_SUB * j + A_SUB - 1:A_SUB * (j + 1), :] for j in range(sc * n_sub)]
    endrow = jnp.concatenate([rep(e, A_SUB) for e in ends], axis=0)
    prevrow = jnp.concatenate(
        [rep(c.zero_row if j % n_sub == 0 else ends[j - 1], A_SUB) for j in range(sc * n_sub)], axis=0)
    tot = [ends[n_sub * ch + n_sub - 1] for ch in range(sc)]
    lastrow = jnp.concatenate([rep(t, A_CHUNK) for t in tot], axis=0)
    kend = k * jnp.exp(endrow - b)
    qd = q * jnp.exp(b - prevrow)
    qs = qd * jnp.exp(prevrow)
    kd = kend * jnp.exp(lastrow - endrow)
    kdiag16 = (k * jnp.exp(prevrow - b)).astype(BF16)
    qd16 = qd.astype(BF16)
    kend16 = kend.astype(BF16)
    qs16 = qs.astype(BF16)
    kd16 = kd.astype(BF16)

    blk = lambda a, j: a[A_SUB * j:A_SUB * (j + 1), :]
    chk = lambda a, ch: a[A_CHUNK * ch:A_CHUNK * (ch + 1), :]

    q_groups, k_groups = [], []
    for i in range(n_sub):
        q_groups.append(jnp.concatenate(
            [blk(qd16, j) if j % n_sub == i else c.zero_blk for j in range(sc * n_sub)], axis=0))
        pieces = []
        for j in range(sc * n_sub):
            ch, jj = divmod(j, n_sub)
            if jj == i:
                pieces.append(blk(kdiag16, j))
            elif jj == i - 1:
                pieces.append(blk(kend16, j))
            elif jj < i:
                pieces.append((blk(kend, j) * jnp.exp(ends[ch * n_sub + i - 1] - ends[j])).astype(BF16))
            else:
                pieces.append(c.zero_blk)
        k_groups.append(jnp.concatenate(pieces, axis=0))
    att = _dot_nt(jnp.concatenate(q_groups, axis=1), jnp.concatenate(k_groups, axis=1))
    att = jnp.where(c.same_chunk_causal, att, 0.0)

    if sc > 1:
        q_groups, k_groups = [], []
        for ch in range(1, sc):
            q_groups.append(jnp.concatenate(
                [chk(qs16, c2) if c2 == ch else c.zero_chunk for c2 in range(sc)], axis=0))
            pieces = []
            for c2 in range(sc):
                if c2 == ch - 1:
                    pieces.append(chk(kd16, c2))
                elif c2 < ch:
                    pieces.append((chk(kd, c2) * jnp.exp(sum(tot[c2 + 1:ch]))).astype(BF16))
                else:
                    pieces.append(c.zero_chunk)
            k_groups.append(jnp.concatenate(pieces, axis=0))
        att = att + _dot_nt(jnp.concatenate(q_groups, axis=1), jnp.concatenate(k_groups, axis=1))
    o = jnp.dot(att.astype(BF16), v, preferred_element_type=F32)

    q0 = jnp.concatenate(
        [chk(qs16, 0)] + [(chk(qs, ch) * jnp.exp(sum(tot[:ch]))).astype(BF16) for ch in range(1, sc)], axis=0)
    o = o + _dot_nt(q0, st.astype(BF16))

    k1 = jnp.concatenate(
        [(chk(kd, ch) * jnp.exp(sum(tot[ch + 1:]))).astype(BF16) for ch in range(sc - 1)]
        + [chk(kd16, sc - 1)], axis=0)
    st = st * jnp.exp(sum(tot)) + _dot_tn(v, k1)
    return o, st


def _hgrn_blockwise(stage_ref, oraw_ref, ls, lb, st):
    n_blk = stage_ref.shape[1] // A_SUB
    row = lax.broadcasted_iota(jnp.int32, (A_SUB, A_DK), 0)
    shifts = tuple(1 << i for i in range((A_SUB - 1).bit_length()))

    def body(i, st):
        rows = pl.ds(pl.multiple_of(i * A_SUB, A_SUB), A_SUB)
        fg = lb + (1.0 - lb) * _sigmoid(stage_ref[1, rows, ls])
        k = 1.0 - fg
        b = _group_cumsum(jnp.log(fg), row, A_SUB, shifts)
        q = _silu(stage_ref[0, rows, ls])
        v = stage_ref[2, rows, ls]
        o = _dot_nt((q * jnp.exp(b)).astype(BF16), st.astype(BF16))
        o = o + jnp.sum(q * k, axis=-1, keepdims=True) * v
        for d in range(1, A_SUB):
            ok = row >= d
            w = jnp.exp(jnp.where(ok, b - pltpu.roll(b, d, axis=0), 0.0))
            a = jnp.sum(q * pltpu.roll(k, d, axis=0) * w, axis=-1, keepdims=True)
            o = o + jnp.where(ok, a * pltpu.roll(v, d, axis=0), 0.0)
        oraw_ref[rows, ls] = o
        b_end = b[A_SUB - 1:A_SUB, :]
        kd = (k * jnp.exp(b_end - b)).astype(BF16)
        return st * jnp.exp(b_end) + _dot_tn(v.astype(BF16), kd)

    return lax.fori_loop(0, n_blk, body, st)


LB_SAFE = math.exp(-80.0 / A_SUB)


def _mix_prompt_kernel(bucket_ref, relb_ref, sink_ref, x_ref, gpre_ref, w_ref, lb_ref, gh_ref,
                       oa_ref, ob_ref, gab_ref, st_ref, kvwin_ref,
                       bias_ref, st_scr, kvprev_scr, stage_scr, oraw_scr):
    tile = pl.program_id(1)
    at_start = tile == 0

    @pl.when(jnp.logical_and(pl.program_id(0) == 0, at_start))
    def _():
        _build_bias(bucket_ref, relb_ref, bias_ref, WINDOW)

    @pl.when(at_start)
    def _():
        st_scr[...] = jnp.zeros(st_scr.shape, F32)
        kvprev_scr[...] = jnp.zeros(kvprev_scr.shape, F32)

    x = x_ref[...]
    ms = jnp.mean(x * x, axis=-1, keepdims=True)
    u = (x * lax.rsqrt(ms + EPS) * gpre_ref[...]).astype(BF16)
    proj = lambda seg, lo, hi: jnp.dot(u, w_ref[:, IN_OFFS[seg] + lo:IN_OFFS[seg] + hi],
                                       preferred_element_type=F32)

    lb_all = _lower_bound(lb_ref)
    consts = _HgrnConsts(HGRN_SLAB)
    pair_w = 2 * A_DK
    n_phase = A_HEADS // 2
    n_blocks = MIX_TQ // WINDOW
    assert n_blocks == n_phase
    gate_w = D_MODEL // n_phase
    kw = B_KV_HEADS * B_HD

    def hgrn_proj(hp):
        lo, hi = hp * pair_w, (hp + 1) * pair_w
        return tuple(proj(seg, lo, hi) for seg in (0, SEG_AF, 2, 3, 4))

    def hgrn_pair(hp, q2, f2, v2, og2, z2):
        for hh in range(2):
            h = 2 * hp + hh
            ls = slice(hh * A_DK, (hh + 1) * A_DK)
            hs = slice(h * A_DK, (h + 1) * A_DK)
            st = st_scr[h]
            outs = []
            for s0 in range(0, MIX_TQ, HGRN_SLAB):
                rs = slice(s0, s0 + HGRN_SLAB)
                o, st = _hgrn_slab(f2[rs, ls], q2[rs, ls], v2[rs, ls].astype(BF16), lb_all[:, hs], st, consts)
                outs.append(o)
            st_scr[h] = st
            st_ref[h] = st.T
            oa = _hgrn_finish(jnp.concatenate(outs, axis=0), gh_ref[:, hs], og2[:, ls], z2[:, ls])
            oa_ref[:, hs] = oa.astype(BF16)

    key = lax.broadcasted_iota(jnp.int32, (2 * N_KEYS, PAIRS_PER_KV * WINDOW), 0) & (N_KEYS - 1)
    no_prev = jnp.logical_and(key < WINDOW, at_start)

    def swa_block(j, kv, bq, bz):
        r0 = j * WINDOW
        if j == 0:
            kvj = jnp.concatenate([kvprev_scr[...], kv[0:WINDOW, :]], axis=0)
        else:
            kvj = kv[r0 - WINDOW:r0 + WINDOW, :]
        keys = _swa_keys(kvj[:, 0:kw])
        vals_t = [kvj[:, kw:2 * kw].T.astype(BF16)]
        for kvh in range(B_KV_HEADS):
            q = jnp.concatenate([bq[r0:r0 + WINDOW, p * 128:(p + 1) * 128]
                                 for p in range(kvh * PAIRS_PER_KV, (kvh + 1) * PAIRS_PER_KV)], axis=0)
            logits = _dot_nt(keys[kvh], q) + bias_ref[kvh]
            if j == 0:
                logits = jnp.where(no_prev, NEG, logits)
            acc_t = _swa_attend(logits, vals_t, [None], sink_ref, kvh, WINDOW)
            for pp in range(PAIRS_PER_KV):
                cs = slice((kvh * PAIRS_PER_KV + pp) * 128, (kvh * PAIRS_PER_KV + pp + 1) * 128)
                ob_ref[r0:r0 + WINDOW, cs] = (
                    acc_t[pp * WINDOW:(pp + 1) * WINDOW, :] * _silu(bz[r0:r0 + WINDOW, cs])).astype(BF16)

    def hgrn_pair_blockwise(hp, q2, f2, v2, og2, z2):
        stage_scr[0] = q2
        stage_scr[1] = f2
        stage_scr[2] = v2
        for hh in range(2):
            h = 2 * hp + hh
            ls = slice(hh * A_DK, (hh + 1) * A_DK)
            hs = slice(h * A_DK, (h + 1) * A_DK)
            st = _hgrn_blockwise(stage_scr, oraw_scr, ls, lb_all[:, hs], st_scr[h])
            st_scr[h] = st
            st_ref[h] = st.T
            oa = _hgrn_finish(oraw_scr[:, ls], gh_ref[:, hs], og2[:, ls], z2[:, ls])
            oa_ref[:, hs] = oa.astype(BF16)

    def gate_proj(ph):
        for seg, base in ((9, 0), (10, D_MODEL)):
            gab_ref[:, base + ph * gate_w:base + (ph + 1) * gate_w] = proj(
                seg, ph * gate_w, (ph + 1) * gate_w).astype(BF16)

    def swa_proj():
        kv = proj(SEG_BK, 0, 2 * kw)
        bq = (proj(5, 0, D_MODEL) * Q_SCALE).astype(BF16)
        bz = proj(8, 0, D_MODEL)
        return kv, bq, bz

    def keep_window(kv):
        kvprev_scr[...] = kv[MIX_TQ - WINDOW:, :]
        kvwin_ref[...] = kv[MIX_TQ - WINDOW:, :]

    slab_safe = jnp.min(lb_all) >= LB_SAFE

    @pl.when(slab_safe)
    def _():
        kv, bq, bz = swa_proj()
        nxt = hgrn_proj(0)
        for ph in range(n_phase):
            cur = nxt
            if ph + 1 < n_phase:
                nxt = hgrn_proj(ph + 1)
            gate_proj(ph)
            hgrn_pair(ph, *cur)
            swa_block(ph, kv, bq, bz)
        keep_window(kv)

    @pl.when(jnp.logical_not(slab_safe))
    def _():
        for ph in range(n_phase):
            hgrn_pair_blockwise(ph, *hgrn_proj(ph))
            gate_proj(ph)
        kv, bq, bz = swa_proj()
        for j in range(n_blocks):
            swa_block(j, kv, bq, bz)
        keep_window(kv)


def _mix_prompt(bucket, rel_bias, sink, x, g_pre, w, hgrn_lb, hgrn_norm, batch, seq):
    nt = seq // MIX_TQ
    n_lb = hgrn_lb.shape[0]
    kvw = 2 * B_KV_HEADS * B_HD
    smem = pl.BlockSpec(memory_space=pltpu.SMEM)
    tok = lambda width: pl.BlockSpec((MIX_TQ, width), lambda b, t: (b * nt + t, 0))
    return pl.pallas_call(
        _mix_prompt_kernel,
        grid=(batch, nt),
        in_specs=[
            _resident((N_KEYS, WINDOW)),
            smem, smem,
            tok(D_MODEL),
            _resident((1, D_MODEL)),
            _resident(w.shape),
            _resident((n_lb, D_MODEL)),
            _resident((1, D_MODEL)),
        ],
        out_specs=[
            tok(D_MODEL), tok(D_MODEL), tok(2 * D_MODEL),
            pl.BlockSpec((None, A_HEADS, A_DK, A_DV), lambda b, t: (b, 0, 0, 0)),
            pl.BlockSpec((None, WINDOW, kvw), lambda b, t: (b, 0, 0)),
        ],
        out_shape=[
            jax.ShapeDtypeStruct((batch * seq, D_MODEL), BF16),
            jax.ShapeDtypeStruct((batch * seq, D_MODEL), BF16),
            jax.ShapeDtypeStruct((batch * seq, 2 * D_MODEL), BF16),
            jax.ShapeDtypeStruct((batch, A_HEADS, A_DK, A_DV), F32),
            jax.ShapeDtypeStruct((batch, WINDOW, kvw), F32),
        ],
        scratch_shapes=[
            pltpu.VMEM((B_KV_HEADS, 2 * N_KEYS, PAIRS_PER_KV * WINDOW), F32),
            pltpu.VMEM((A_HEADS, A_DV, A_DK), F32),
            pltpu.VMEM((WINDOW, kvw), F32),
            pltpu.VMEM((3, MIX_TQ, 2 * A_DK), F32),
            pltpu.VMEM((MIX_TQ, 2 * A_DK), F32),
        ],
        compiler_params=pltpu.CompilerParams(
            dimension_semantics=("arbitrary", "arbitrary"), vmem_limit_bytes=VMEM_LIMIT),
        name="mix_prompt",
    )(bucket, rel_bias, sink, x, g_pre, w, hgrn_lb, hgrn_norm)


def _outproj_kernel(oa_ref, ob_ref, ga_ref, gb_ref, x_ref, p_ref, wpa_ref, wpb_ref, wo_ref, gpost_ref,
                    wple_ref, wg_ref, y_ref):
    a = jnp.dot(oa_ref[...], wpa_ref[...], preferred_element_type=F32)
    b = jnp.dot(ob_ref[...], wpb_ref[...], preferred_element_type=F32)
    m = _sigmoid(ga_ref[...].astype(F32)) * a + _sigmoid(gb_ref[...].astype(F32)) * b
    y = jnp.dot(m.astype(BF16), wo_ref[...], preferred_element_type=F32)
    y = y * lax.rsqrt(jnp.mean(y * y, axis=-1, keepdims=True) + EPS) * gpost_ref[...]
    x1 = x_ref[...] + y
    gate = _sigmoid(jnp.dot(x1.astype(BF16), wg_ref[...], preferred_element_type=F32))
    e = jnp.dot(p_ref[...].astype(BF16), wple_ref[...], preferred_element_type=F32) * gate
    y_ref[...] = x1 + e


def _outproj(oa, ob, gates, ga_col, gb_col, x, p, wpa, wpb, wo, gpost, wple, wg, tm):
    n = x.shape[0]
    ple = p.shape[1]
    tok = lambda w, c=0: pl.BlockSpec((tm, w), lambda i, c=c: (i, c))
    return pl.pallas_call(
        _outproj_kernel,
        grid=(n // tm,),
        in_specs=[
            tok(D_MODEL), tok(D_MODEL), tok(D_MODEL, ga_col), tok(D_MODEL, gb_col), tok(D_MODEL), tok(ple),
            _resident((D_MODEL, D_MODEL)), _resident((D_MODEL, D_MODEL)), _resident((D_MODEL, D_MODEL)),
            _resident((1, D_MODEL)), _resident((ple, D_MODEL)), _resident((D_MODEL, D_MODEL)),
        ],
        out_specs=tok(D_MODEL),
        out_shape=jax.ShapeDtypeStruct((n, D_MODEL), F32),
        compiler_params=pltpu.CompilerParams(
            dimension_semantics=("arbitrary",), vmem_limit_bytes=VMEM_LIMIT),
        name="outproj",
    )(oa, ob, gates, gates, x, p, wpa, wpb, wo, gpost, wple, wg)


def _rel_bucket(rel):
    n = np.maximum(rel, 0)
    max_exact = REL_BUCKETS // 2
    nf = np.maximum(n, 1).astype(np.float32)
    scaled = (np.log(nf / np.float32(max_exact)) / np.float32(math.log(REL_MAX_DIST / max_exact))
              * np.float32(REL_BUCKETS - max_exact))
    frac = scaled - np.floor(scaled)
    inside = (n > max_exact) & (n < REL_MAX_DIST)
    assert np.all((frac[inside] > 1e-3) & (frac[inside] < 1.0 - 1e-3))
    large = np.minimum(max_exact + scaled.astype(np.int32), REL_BUCKETS - 1)
    return np.where(n < max_exact, n, large)


def _bucket_table(q_pos, k_pos, k_valid):
    rel = q_pos[:, None] - k_pos[None, :]
    ok = (rel >= 0) & (rel < WINDOW) & k_valid[None, :]
    return jnp.asarray(np.where(ok, _rel_bucket(rel), -1).astype(np.int32).T)


TM_PROJ = 512
SAMPLE_BB = 8


def _layer(xp, xs, s_hgrn, win_k, win_v, pp, ps, norm_pre, w_in, hgrn_lb, hgrn_norm, attn_sink,
           rel_bias, w_pa, w_pb, w_o, norm_post, w_ple, w_ple_gate):
    batch, seq, _ = xp.shape
    dbatch, t_len, _ = xs.shape
    kw = B_KV_HEADS * B_HD

    w16 = w_in.astype(BF16)
    g_pre = norm_pre.reshape(1, D_MODEL)
    g_post = norm_post.reshape(1, D_MODEL)
    g_hgrn = hgrn_norm.reshape(1, D_MODEL)
    wpa, wpb, wo = w_pa.astype(BF16), w_pb.astype(BF16), w_o.astype(BF16)
    wple, wg = w_ple.astype(BF16), w_ple_gate.astype(BF16)

    xp2 = xp.reshape(batch * seq, D_MODEL)
    xs2 = xs.reshape(dbatch * t_len, D_MODEL)
    k_all = np.arange(N_KEYS)
    bucket_p = _bucket_table(np.arange(WINDOW) + WINDOW, k_all, np.ones((N_KEYS,), bool))
    rows_s = SAMPLE_BB * t_len
    bucket_s = _bucket_table(WINDOW + np.arange(rows_s) % t_len, k_all, k_all < WINDOW + t_len)

    oa_p, ob_p, gab_p, st_p, kvwin_p = _mix_prompt(bucket_p, rel_bias, attn_sink, xp2, g_pre, w16,
                                                   hgrn_lb, g_hgrn, batch, seq)

    pm_s, f_s, kv_s = _inproj(xs2, g_pre, w16, min(TM_PROJ, dbatch * t_len))
    oa_s, st_s, ob_s, nk_s, nv_s = _mix_sample(bucket_s, rel_bias, attn_sink, pm_s, f_s, kv_s, hgrn_lb, g_hgrn,
                                               s_hgrn, win_k.reshape(dbatch, WINDOW, kw),
                                               win_v.reshape(dbatch, WINDOW, kw), dbatch, t_len, SAMPLE_BB)

    y_p = _outproj(oa_p, ob_p, gab_p, 0, 1, xp2, pp.reshape(batch * seq, -1),
                   wpa, wpb, wo, g_post, wple, wg, TM_PROJ)
    y_s = _outproj(oa_s, ob_s, pm_s, G_GA, G_GB, xs2, ps.reshape(dbatch * t_len, -1),
                   wpa, wpb, wo, g_post, wple, wg, min(TM_PROJ, dbatch * t_len))

    k_win_p = kvwin_p[:, :, 0:kw].reshape(batch, WINDOW, B_KV_HEADS, B_HD)
    v_win_p = kvwin_p[:, :, kw:].reshape(batch, WINDOW, B_KV_HEADS, B_HD)
    return (y_p.reshape(batch, seq, D_MODEL), y_s.reshape(dbatch, t_len, D_MODEL), st_p, st_s,
            k_win_p, v_win_p,
            nk_s.reshape(dbatch, WINDOW, B_KV_HEADS, B_HD), nv_s.reshape(dbatch, WINDOW, B_KV_HEADS, B_HD))


def kernel(x_prompt, x_sample, state_hgrn, cache_swa_k, cache_swa_v, p_prompt, p_sample, norm_pre, w_in,
           hgrn_lb, hgrn_norm, attn_sink, rel_bias, w_pa, w_pb, w_o, norm_post, w_ple, w_ple_gate):
    depth = w_in.shape[0]
    assert depth == 1, "the forget-gate lower bound is implemented for a single layer"
    xp, xs = x_prompt, x_sample
    outs = []
    for l in range(depth):
        res = _layer(xp, xs, state_hgrn[l], cache_swa_k[l], cache_swa_v[l], p_prompt[l], p_sample[l],
                     norm_pre[l], w_in[l], hgrn_lb, hgrn_norm[l], attn_sink[l], rel_bias,
                     w_pa[l], w_pb[l], w_o[l], norm_post[l], w_ple[l], w_ple_gate[l])
        xp, xs = res[0], res[1]
        outs.append(res[2:])
    stack = lambda i: jnp.stack([o[i] for o in outs])
    return (xp, xs, stack(0), stack(1), stack(2), stack(3), stack(4), stack(5))
```

```python
import functools
import math

import jax
import jax.numpy as jnp
import numpy as np
from jax import lax
from jax.experimental import pallas as pl
from jax.experimental.pallas import tpu as pltpu

F32 = jnp.float32
BF16 = jnp.bfloat16

D_MODEL = 1024
A_HEADS = 8
A_DK = 128
A_DV = 128
A_CHUNK = 64
A_SUB = 16
B_HEADS = 16
B_KV_HEADS = 2
B_HD = 64
WINDOW = 128
REL_BUCKETS = 32
REL_MAX_DIST = 128
EPS = 1e-6
NEG = float("-inf")
LOG2E = math.log2(math.e)
Q_SCALE = B_HD ** -0.5 * LOG2E

G_AQ, G_AI, G_AOG, G_AZ, G_BQ, G_BZ, G_GA, G_GB = range(8)
N_GROUPS = 8

VMEM_LIMIT = 56 * 1024 * 1024


def _sigmoid(x):
    return 1.0 / (1.0 + jnp.exp2(x * -LOG2E))


def _silu(x):
    return x * _sigmoid(x)


def _resident(shape):
    nd = len(shape)
    return pl.BlockSpec(shape, lambda *_: (0,) * nd, pipeline_mode=pl.Buffered(1))


IN_OFFS = (0, 1024, 2048, 3072, 4096, 5120, 6144, 6272, 6400, 7424, 8448, 9472)
SLAB_SEGS = (0, 2, 3, 4, 5, 8, 9, 10)
SEG_AF, SEG_BK, SEG_BV = 1, 6, 7


def _inproj_kernel(x_ref, g_ref, w_ref, pm_ref, f_ref, kv_ref):
    x = x_ref[...]
    ms = jnp.mean(x * x, axis=-1, keepdims=True)
    u = (x * lax.rsqrt(ms + EPS) * g_ref[...]).astype(BF16)
    proj = lambda lo, hi: jnp.dot(u, w_ref[:, lo:hi], preferred_element_type=F32)
    for c, seg in enumerate(SLAB_SEGS):
        pm_ref[:, c * D_MODEL:(c + 1) * D_MODEL] = proj(IN_OFFS[seg], IN_OFFS[seg + 1]).astype(BF16)
    f_ref[...] = proj(IN_OFFS[SEG_AF], IN_OFFS[SEG_AF + 1])
    kv_ref[...] = proj(IN_OFFS[SEG_BK], IN_OFFS[SEG_BV + 1])


def _inproj(x, g, w, tm):
    n = x.shape[0]
    nm = N_GROUPS * D_MODEL
    nkv = IN_OFFS[SEG_BV + 1] - IN_OFFS[SEG_BK]
    return pl.pallas_call(
        _inproj_kernel,
        grid=(n // tm,),
        in_specs=[
            pl.BlockSpec((tm, D_MODEL), lambda i: (i, 0)),
            _resident((1, D_MODEL)),
            _resident(w.shape),
        ],
        out_specs=[
            pl.BlockSpec((tm, nm), lambda i: (i, 0)),
            pl.BlockSpec((tm, D_MODEL), lambda i: (i, 0)),
            pl.BlockSpec((tm, nkv), lambda i: (i, 0)),
        ],
        out_shape=[
            jax.ShapeDtypeStruct((n, nm), BF16),
            jax.ShapeDtypeStruct((n, D_MODEL), F32),
            jax.ShapeDtypeStruct((n, nkv), F32),
        ],
        compiler_params=pltpu.CompilerParams(
            dimension_semantics=("arbitrary",), vmem_limit_bytes=VMEM_LIMIT),
        name="inproj",
    )(x, g, w)


def _lower_bound(lb_ref):
    l = lb_ref[...]
    m = jnp.max(l, axis=0, keepdims=True)
    e = jnp.exp(l - m)
    return e[0:1, :] / jnp.sum(e, axis=0, keepdims=True)


def _group_cumsum(x, row, period, shifts):
    pos = row & (period - 1)
    for sh in shifts:
        x = x + jnp.where(pos >= sh, pltpu.roll(x, sh, axis=0), 0.0)
    return x


def _hgrn_finish(o, g, og_pre, z_pre):
    o = o * lax.rsqrt(jnp.mean(o * o, axis=-1, keepdims=True) + EPS)
    return o * g * _sigmoid(og_pre) * _silu(z_pre)


def _dot_nt(a, b):
    return lax.dot_general(a, b, (((1,), (1,)), ((), ())), preferred_element_type=F32)


def _dot_tn(a, b):
    return lax.dot_general(a, b, (((0,), (0,)), ((), ())), preferred_element_type=F32)


def _hgrn_sample_kernel(q_ref, v_ref, og_ref, z_ref, f_ref, lb_ref, g_ref, s0_ref, oa_ref, s_ref,
                        *, bb, t_len):
    rows = bb * t_len
    lb_all = _lower_bound(lb_ref)
    row = lax.broadcasted_iota(jnp.int32, (rows, A_DK), 0)
    pos = row & (t_len - 1)
    grp = row >> (t_len.bit_length() - 1)
    shifts = tuple(1 << i for i in range((t_len - 1).bit_length()))
    n_tail = A_DK - rows
    sel_row = lax.broadcasted_iota(jnp.int32, (n_tail, 2 * bb * A_DV), 0)
    sel_blk = lax.broadcasted_iota(jnp.int32, (n_tail, 2 * bb * A_DV), 1) >> (A_DV.bit_length() - 1)
    picks = jnp.logical_and(sel_row < 2 * bb, sel_blk == bb + (sel_row & (bb - 1)))
    rhs_tail = jnp.where(picks, 1.0, 0.0).astype(BF16)
    lhs_pad = jnp.zeros((n_tail - 2 * bb, A_DK), F32)
    v_pad = jnp.zeros((rows, bb * A_DV), F32)
    for h in range(A_HEADS):
        ls = slice(h * A_DK, (h + 1) * A_DK)
        lb = lb_all[:, ls]
        fg = lb + (1.0 - lb) * _sigmoid(f_ref[:, ls])
        logf = jnp.log2(fg)
        k = 1.0 - fg
        b = _group_cumsum(logf, row, t_len, shifts)
        q = _silu(q_ref[:, ls].astype(F32))
        v = v_ref[:, ls].astype(F32)

        o = jnp.zeros((rows, A_DV), F32)
        for d in range(t_len):
            if d == 0:
                a = jnp.sum(q * k, axis=-1, keepdims=True)
                o = o + a * v
            else:
                w = jnp.exp2(jnp.where(pos >= d, b - pltpu.roll(b, d, axis=0), 0.0))
                a = jnp.sum(q * pltpu.roll(k, d, axis=0) * w, axis=-1, keepdims=True)
                o = o + jnp.where(pos >= d, a * pltpu.roll(v, d, axis=0), 0.0)

        b_last = b
        for d in range(1, t_len):
            b_last = jnp.where(pos == t_len - 1 - d, pltpu.roll(b, rows - d, axis=0), b_last)
        e_last = jnp.exp2(b_last)
        qs = q * jnp.exp2(b)
        kd = k * jnp.exp2(b_last - b)
        e_rows = jnp.concatenate([e_last[j * t_len:j * t_len + 1, :] for j in range(bb)], axis=0)
        e_hi = e_rows.astype(BF16).astype(F32)
        lhs_t = jnp.concatenate([kd, e_hi, e_rows - e_hi, lhs_pad], axis=0).T.astype(BF16)
        v_rows = jnp.concatenate([jnp.where(grp == j, v, 0.0) for j in range(bb)] + [v_pad], axis=1)
        res = jnp.dot(lhs_t, jnp.concatenate([v_rows.astype(BF16), rhs_tail], axis=0),
                      preferred_element_type=F32)
        for j in range(bb):
            s0 = s0_ref[j, h]
            o = o + jnp.dot(jnp.where(grp == j, qs, 0.0).astype(BF16), s0.astype(BF16),
                            preferred_element_type=F32)
            s_ref[j, h] = (res[:, (bb + j) * A_DV:(bb + j + 1) * A_DV] * s0
                           + res[:, j * A_DV:(j + 1) * A_DV])

        oa = _hgrn_finish(o, g_ref[:, ls], og_ref[:, ls].astype(F32), z_ref[:, ls].astype(F32))
        oa_ref[:, ls] = oa.astype(BF16)


N_PAIRS = B_HEADS // 2
PAIRS_PER_KV = N_PAIRS // B_KV_HEADS
N_KEYS = 2 * WINDOW


def _build_bias(bucket_ref, relb_ref, bias_ref, r):
    bucket = bucket_ref[...]
    for h in range(B_HEADS):
        def body(kb, acc, h=h):
            return jnp.where(bucket == kb, relb_ref[kb, h] * LOG2E, acc)
        tab = lax.fori_loop(0, REL_BUCKETS, body, jnp.full(bucket.shape, NEG, F32))
        pair, parity = divmod(h, 2)
        kv, pp = divmod(pair, PAIRS_PER_KV)
        bias_ref[kv, parity * N_KEYS:(parity + 1) * N_KEYS, pp * r:(pp + 1) * r] = tab


def _swa_keys(kk):
    lo = lax.broadcasted_iota(jnp.int32, (N_KEYS, 2 * B_HD), 1) < B_HD
    kk_sw = pltpu.roll(kk, B_HD, axis=1)
    slabs = []
    for kv in range(B_KV_HEADS):
        k_lo, k_hi = (kk, kk_sw) if kv == 0 else (kk_sw, kk)
        slabs.append(jnp.concatenate([jnp.where(lo, k_lo, 0.0), jnp.where(lo, 0.0, k_hi)], axis=0).astype(BF16))
    return slabs


def _swa_attend(logits, vals_t, col_masks, sink_ref, kv, r):
    width = PAIRS_PER_KV * r
    lane_pp = lax.broadcasted_iota(jnp.int32, (1, width), 1) >> (r.bit_length() - 1)
    halves = []
    for parity in range(2):
        sink = jnp.zeros((1, width), F32)
        for pp in range(PAIRS_PER_KV):
            sink = jnp.where(lane_pp == pp, sink_ref[(kv * PAIRS_PER_KV + pp) * 2 + parity] * LOG2E, sink)
        l = logits[parity * N_KEYS:(parity + 1) * N_KEYS, :]
        m = jnp.maximum(jnp.max(l, axis=0, keepdims=True), sink)
        p = jnp.exp2(l - m)
        denom = jnp.sum(p, axis=0, keepdims=True) + jnp.exp2(sink - m)
        p16 = p.astype(BF16)
        v_rows = jnp.concatenate([v_t[kv * B_HD:(kv + 1) * B_HD, :] for v_t in vals_t], axis=0)
        pv_all = jnp.dot(v_rows, p16, preferred_element_type=F32)
        pv = None
        for n, mask in enumerate(col_masks):
            part = pv_all[n * B_HD:(n + 1) * B_HD, :]
            pv = part if mask is None else jnp.where(mask, part, 0.0 if pv is None else pv)
        halves.append(pv / denom)
    return jnp.concatenate(halves, axis=0).T


def _swa_sample_kernel(bucket_ref, relb_ref, sink_ref, q_ref, z_ref, kvn_ref, ck_ref, cv_ref,
                       ob_ref, nk_ref, nv_ref, bias_ref, *, bb, t_len):
    rows = bb * t_len
    width = PAIRS_PER_KV * rows

    @pl.when(pl.program_id(0) == 0)
    def _():
        _build_bias(bucket_ref, relb_ref, bias_ref, rows)

    kw = B_KV_HEADS * B_HD
    t_shift = t_len.bit_length() - 1
    row = lax.broadcasted_iota(jnp.int32, (rows, kw), 0)
    seq_of_col = lambda n: (lax.broadcasted_iota(jnp.int32, (n, width), 1) & (rows - 1)) >> t_shift
    seq_l = seq_of_col(2 * N_KEYS)
    seq_v = seq_of_col(B_HD)
    q_pairs = [(q_ref[:, p * 128:(p + 1) * 128].astype(F32) * Q_SCALE).astype(BF16) for p in range(N_PAIRS)]
    q_kv = [jnp.concatenate(q_pairs[kv * PAIRS_PER_KV:(kv + 1) * PAIRS_PER_KV], axis=0)
            for kv in range(B_KV_HEADS)]
    kn = kvn_ref[:, 0:kw]
    vn = kvn_ref[:, kw:2 * kw]
    pad = jnp.zeros((N_KEYS - WINDOW - rows, kw), F32)
    logits = [None] * B_KV_HEADS
    vals_t, col_masks = [], []
    for j in range(bb):
        sel = row < t_len
        kj = jnp.where(sel, kn if j == 0 else pltpu.roll(kn, rows - j * t_len, axis=0), 0.0)
        vj = jnp.where(sel, vn if j == 0 else pltpu.roll(vn, rows - j * t_len, axis=0), 0.0)
        kk = jnp.concatenate([ck_ref[j], kj, pad], axis=0)
        vv = jnp.concatenate([cv_ref[j], vj, pad], axis=0)
        keys = _swa_keys(kk)
        for kv in range(B_KV_HEADS):
            lg = _dot_nt(keys[kv], q_kv[kv])
            logits[kv] = jnp.where(seq_l == j, lg, 0.0 if logits[kv] is None else logits[kv])
        vals_t.append(vv.T.astype(BF16))
        col_masks.append(seq_v == j)
        nk_ref[j] = pltpu.roll(kk, N_KEYS - t_len, axis=0)[0:WINDOW, :]
        nv_ref[j] = pltpu.roll(vv, N_KEYS - t_len, axis=0)[0:WINDOW, :]
    for kv in range(B_KV_HEADS):
        acc_t = _swa_attend(logits[kv] + bias_ref[kv], vals_t, col_masks, sink_ref, kv, rows)
        for pp in range(PAIRS_PER_KV):
            p = kv * PAIRS_PER_KV + pp
            z = z_ref[:, p * 128:(p + 1) * 128].astype(F32)
            ob_ref[:, p * 128:(p + 1) * 128] = (acc_t[pp * rows:(pp + 1) * rows, :] * _silu(z)).astype(BF16)


def _mix_sample_kernel(bucket_ref, relb_ref, sink_ref, aq_ref, ai_ref, aog_ref, az_ref, f_ref, lb_ref, g_ref,
                       s0_ref, bq_ref, bz_ref, kvn_ref, ck_ref, cv_ref,
                       oa_ref, s_ref, ob_ref, nk_ref, nv_ref, bias_ref, *, bb, t_len):
    _hgrn_sample_kernel(aq_ref, ai_ref, aog_ref, az_ref, f_ref, lb_ref, g_ref, s0_ref, oa_ref, s_ref,
                        bb=bb, t_len=t_len)
    _swa_sample_kernel(bucket_ref, relb_ref, sink_ref, bq_ref, bz_ref, kvn_ref, ck_ref, cv_ref,
                       ob_ref, nk_ref, nv_ref, bias_ref, bb=bb, t_len=t_len)


def _mix_sample(bucket, rel_bias, sink, pm, f, kvn, hgrn_lb, hgrn_norm, s0, cache_k, cache_v,
                batch, t_len, bb):
    rows = bb * t_len
    kvw = kvn.shape[1]
    kw = kvw // 2
    n_lb = hgrn_lb.shape[0]
    smem = pl.BlockSpec(memory_space=pltpu.SMEM)
    col = lambda grp: pl.BlockSpec((rows, D_MODEL), lambda i, grp=grp: (i, grp))
    tok = pl.BlockSpec((rows, D_MODEL), lambda i: (i, 0))
    st_spec = pl.BlockSpec((bb, A_HEADS, A_DK, A_DV), lambda i: (i, 0, 0, 0))
    cache_spec = pl.BlockSpec((bb, WINDOW, kw), lambda i: (i, 0, 0))
    return pl.pallas_call(
        functools.partial(_mix_sample_kernel, bb=bb, t_len=t_len),
        grid=(batch // bb,),
        in_specs=[
            _resident((N_KEYS, rows)),
            smem, smem,
            col(G_AQ), col(G_AI), col(G_AOG), col(G_AZ),
            tok,
            _resident((n_lb, D_MODEL)),
            _resident((1, D_MODEL)),
            st_spec,
            col(G_BQ), col(G_BZ),
            pl.BlockSpec((rows, kvw), lambda i: (i, 0)),
            cache_spec, cache_spec,
        ],
        out_specs=[tok, st_spec, tok, cache_spec, cache_spec],
        out_shape=[
            jax.ShapeDtypeStruct((batch * t_len, D_MODEL), BF16),
            jax.ShapeDtypeStruct((batch, A_HEADS, A_DK, A_DV), F32),
            jax.ShapeDtypeStruct((batch * t_len, D_MODEL), BF16),
            jax.ShapeDtypeStruct((batch, WINDOW, kw), F32),
            jax.ShapeDtypeStruct((batch, WINDOW, kw), F32),
        ],
        scratch_shapes=[pltpu.VMEM((B_KV_HEADS, 2 * N_KEYS, PAIRS_PER_KV * rows), F32)],
        compiler_params=pltpu.CompilerParams(
            dimension_semantics=("arbitrary",), vmem_limit_bytes=VMEM_LIMIT),
        name="mix_sample",
    )(bucket, rel_bias, sink, pm, pm, pm, pm, f, hgrn_lb, hgrn_norm, s0, pm, pm, kvn, cache_k, cache_v)


MIX_TQ = 512
HGRN_SLAB = 256


class _HgrnConsts:
    def __init__(self, slab):
        shift = A_CHUNK.bit_length() - 1
        self.row = lax.broadcasted_iota(jnp.int32, (slab, A_DK), 0)
        rs = lax.broadcasted_iota(jnp.int32, (slab, slab), 0)
        cs = lax.broadcasted_iota(jnp.int32, (slab, slab), 1)
        self.same_chunk_causal = jnp.logical_and((rs >> shift) == (cs >> shift), cs <= rs)
        self.zero_row = jnp.zeros((1, A_DK), F32)
        self.zero_blk = jnp.zeros((A_SUB, A_DK), BF16)
        self.zero_chunk = jnp.zeros((A_CHUNK, A_DK), BF16)


def _hgrn_slab(f_pre, q_pre, v, lb, st, c):
    slab = f_pre.shape[0]
    n_sub = A_CHUNK // A_SUB
    sc = slab // A_CHUNK
    rep = lambda r, n: jnp.broadcast_to(r, (n, A_DK))
    fg = lb + (1.0 - lb) * _sigmoid(f_pre)
    logf = jnp.log2(fg)
    k = 1.0 - fg
    b = _group_cumsum(logf, c.row, A_CHUNK, (1, 2, 4, 8, 16, 32))
    q = _silu(q_pre)

    ends = [b[A_SUB * j + A_SUB - 1:A_SUB * (j + 1), :] for j in range(sc * n_sub)]
    endrow = jnp.concatenate([rep(e, A_SUB) for e in ends], axis=0)
    prevrow = jnp.concatenate(
        [rep(c.zero_row if j % n_sub == 0 else ends[j - 1], A_SUB) for j in range(sc * n_sub)], axis=0)
    tot = [ends[n_sub * ch + n_sub - 1] for ch in range(sc)]
    lastrow = jnp.concatenate([rep(t, A_CHUNK) for t in tot], axis=0)
    kend = k * jnp.exp2(endrow - b)
    qd = q * jnp.exp2(b - prevrow)
    qs = qd * jnp.exp2(prevrow)
    kd = kend * jnp.exp2(lastrow - endrow)
    kdiag16 = (k * jnp.exp2(prevrow - b)).astype(BF16)
    qd16 = qd.astype(BF16)
    kend16 = kend.astype(BF16)
    qs16 = qs.astype(BF16)
    kd16 = kd.astype(BF16)

    blk = lambda a, j: a[A_SUB * j:A_SUB * (j + 1), :]
    chk = lambda a, ch: a[A_CHUNK * ch:A_CHUNK * (ch + 1), :]

    q_groups, k_groups = [], []
    for i in range(n_sub):
        q_groups.append(jnp.concatenate(
            [blk(qd16, j) if j % n_sub == i else c.zero_blk for j in range(sc * n_sub)], axis=0))
        pieces = []
        for j in range(sc * n_sub):
            ch, jj = divmod(j, n_sub)
            if jj == i:
                pieces.append(blk(kdiag16, j))
            elif jj == i - 1:
                pieces.append(blk(kend16, j))
            elif jj < i:
                pieces.append((blk(kend, j) * jnp.exp2(ends[ch * n_sub + i - 1] - ends[j])).astype(BF16))
            else:
                pieces.append(c.zero_blk)
        k_groups.append(jnp.concatenate(pieces, axis=0))
    att = _dot_nt(jnp.concatenate(q_groups, axis=1), jnp.concatenate(k_groups, axis=1))
    att = jnp.where(c.same_chunk_causal, att, 0.0)

    if sc > 1:
        q_groups, k_groups = [], []
        for ch in range(1, sc):
            q_groups.append(jnp.concatenate(
                [chk(qs16, c2) if c2 == ch else c.zero_chunk for c2 in range(sc)], axis=0))
            pieces = []
            for c2 in range(sc):
                if c2 == ch - 1:
                    pieces.append(chk(kd16, c2))
                elif c2 < ch:
                    pieces.append((chk(kd, c2) * jnp.exp2(sum(tot[c2 + 1:ch]))).astype(BF16))
                else:
                    pieces.append(c.zero_chunk)
            k_groups.append(jnp.concatenate(pieces, axis=0))
        att = att + _dot_nt(jnp.concatenate(q_groups, axis=1), jnp.concatenate(k_groups, axis=1))
    o = jnp.dot(att.astype(BF16), v, preferred_element_type=F32)

    q0 = jnp.concatenate(
        [chk(qs16, 0)] + [(chk(qs, ch) * jnp.exp2(sum(tot[:ch]))).astype(BF16) for ch in range(1, sc)], axis=0)
    o = o + _dot_nt(q0, st.astype(BF16))

    k1 = jnp.concatenate(
        [(chk(kd, ch) * jnp.exp2(sum(tot[ch + 1:]))).astype(BF16) for ch in range(sc - 1)]
        + [chk(kd16, sc - 1)], axis=0)
    st = st * jnp.exp2(sum(tot)) + _dot_tn(v, k1)
    return o, st


def _hgrn_blockwise(stage_ref, oraw_ref, ls, lb, st):
    n_blk = stage_ref.shape[1] // A_SUB
    row = lax.broadcasted_iota(jnp.int32, (A_SUB, A_DK), 0)
    shifts = tuple(1 << i for i in range((A_SUB - 1).bit_length()))

    def body(i, st):
        rows = pl.ds(pl.multiple_of(i * A_SUB, A_SUB), A_SUB)
        fg = lb + (1.0 - lb) * _sigmoid(stage_ref[1, rows, ls])
        k = 1.0 - fg
        b = _group_cumsum(jnp.log2(fg), row, A_SUB, shifts)
        q = _silu(stage_ref[0, rows, ls])
        v = stage_ref[2, rows, ls]
        o = _dot_nt((q * jnp.exp2(b)).astype(BF16), st.astype(BF16))
        o = o + jnp.sum(q * k, axis=-1, keepdims=True) * v
        for d in range(1, A_SUB):
            ok = row >= d
            w = jnp.exp2(jnp.where(ok, b - pltpu.roll(b, d, axis=0), 0.0))
            a = jnp.sum(q * pltpu.roll(k, d, axis=0) * w, axis=-1, keepdims=True)
            o = o + jnp.where(ok, a * pltpu.roll(v, d, axis=0), 0.0)
        oraw_ref[rows, ls] = o
        b_end = b[A_SUB - 1:A_SUB, :]
        kd = (k * jnp.exp2(b_end - b)).astype(BF16)
        return st * jnp.exp2(b_end) + _dot_tn(v.astype(BF16), kd)

    return lax.fori_loop(0, n_blk, body, st)


LB_SAFE = math.exp(-80.0 / A_SUB)


def _mix_prompt_kernel(bucket_ref, relb_ref, sink_ref, x_ref, gpre_ref, w_ref, lb_ref, gh_ref,
                       oa_ref, ob_ref, gab_ref, st_ref, kvwin_ref,
                       bias_ref, st_scr, kvprev_scr, stage_scr, oraw_scr):
    tile = pl.program_id(1)
    at_start = tile == 0

    @pl.when(jnp.logical_and(pl.program_id(0) == 0, at_start))
    def _():
        _build_bias(bucket_ref, relb_ref, bias_ref, WINDOW)

    @pl.when(at_start)
    def _():
        st_scr[...] = jnp.zeros(st_scr.shape, F32)
        kvprev_scr[...] = jnp.zeros(kvprev_scr.shape, F32)

    x = x_ref[...]
    ms = jnp.mean(x * x, axis=-1, keepdims=True)
    u = (x * lax.rsqrt(ms + EPS) * gpre_ref[...]).astype(BF16)
    proj = lambda seg, lo, hi: jnp.dot(u, w_ref[:, IN_OFFS[seg] + lo:IN_OFFS[seg] + hi],
                                       preferred_element_type=F32)

    lb_all = _lower_bound(lb_ref)
    consts = _HgrnConsts(HGRN_SLAB)
    pair_w = 2 * A_DK
    n_phase = A_HEADS // 2
    n_blocks = MIX_TQ // WINDOW
    assert n_blocks == n_phase
    gate_w = D_MODEL // n_phase
    kw = B_KV_HEADS * B_HD

    def hgrn_proj(hp):
        lo, hi = hp * pair_w, (hp + 1) * pair_w
        return tuple(proj(seg, lo, hi) for seg in (0, SEG_AF, 2, 3, 4))

    def hgrn_pair(hp, q2, f2, v2, og2, z2):
        for hh in range(2):
            h = 2 * hp + hh
            ls = slice(hh * A_DK, (hh + 1) * A_DK)
            hs = slice(h * A_DK, (h + 1) * A_DK)
            st = st_scr[h]
            outs = []
            for s0 in range(0, MIX_TQ, HGRN_SLAB):
                rs = slice(s0, s0 + HGRN_SLAB)
                o, st = _hgrn_slab(f2[rs, ls], q2[rs, ls], v2[rs, ls].astype(BF16), lb_all[:, hs], st, consts)
                outs.append(o)
            st_scr[h] = st
            st_ref[h] = st.T
            oa = _hgrn_finish(jnp.concatenate(outs, axis=0), gh_ref[:, hs], og2[:, ls], z2[:, ls])
            oa_ref[:, hs] = oa.astype(BF16)

    key = lax.broadcasted_iota(jnp.int32, (2 * N_KEYS, PAIRS_PER_KV * WINDOW), 0) & (N_KEYS - 1)
    no_prev = jnp.logical_and(key < WINDOW, at_start)

    def swa_block(j, kv, bq, bz):
        r0 = j * WINDOW
        if j == 0:
            kvj = jnp.concatenate([kvprev_scr[...], kv[0:WINDOW, :]], axis=0)
        else:
            kvj = kv[r0 - WINDOW:r0 + WINDOW, :]
        keys = _swa_keys(kvj[:, 0:kw])
        vals_t = [kvj[:, kw:2 * kw].T.astype(BF16)]
        for kvh in range(B_KV_HEADS):
            q = jnp.concatenate([bq[r0:r0 + WINDOW, p * 128:(p + 1) * 128]
                                 for p in range(kvh * PAIRS_PER_KV, (kvh + 1) * PAIRS_PER_KV)], axis=0)
            logits = _dot_nt(keys[kvh], q) + bias_ref[kvh]
            if j == 0:
                logits = jnp.where(no_prev, NEG, logits)
            acc_t = _swa_attend(logits, vals_t, [None], sink_ref, kvh, WINDOW)
            for pp in range(PAIRS_PER_KV):
                cs = slice((kvh * PAIRS_PER_KV + pp) * 128, (kvh * PAIRS_PER_KV + pp + 1) * 128)
                ob_ref[r0:r0 + WINDOW, cs] = (
                    acc_t[pp * WINDOW:(pp + 1) * WINDOW, :] * _silu(bz[r0:r0 + WINDOW, cs])).astype(BF16)

    def hgrn_pair_blockwise(hp, q2, f2, v2, og2, z2):
        stage_scr[0] = q2
        stage_scr[1] = f2
        stage_scr[2] = v2
        for hh in range(2):
            h = 2 * hp + hh
            ls = slice(hh * A_DK, (hh + 1) * A_DK)
            hs = slice(h * A_DK, (h + 1) * A_DK)
            st = _hgrn_blockwise(stage_scr, oraw_scr, ls, lb_all[:, hs], st_scr[h])
            st_scr[h] = st
            st_ref[h] = st.T
            oa = _hgrn_finish(oraw_scr[:, ls], gh_ref[:, hs], og2[:, ls], z2[:, ls])
            oa_ref[:, hs] = oa.astype(BF16)

    def gate_proj(ph):
        for seg, base in ((9, 0), (10, D_MODEL)):
            gab_ref[:, base + ph * gate_w:base + (ph + 1) * gate_w] = proj(
                seg, ph * gate_w, (ph + 1) * gate_w).astype(BF16)

    def swa_proj():
        kv = proj(SEG_BK, 0, 2 * kw)
        bq = (proj(5, 0, D_MODEL) * Q_SCALE).astype(BF16)
        bz = proj(8, 0, D_MODEL)
        return kv, bq, bz

    def keep_window(kv):
        kvprev_scr[...] = kv[MIX_TQ - WINDOW:, :]
        kvwin_ref[...] = kv[MIX_TQ - WINDOW:, :]

    slab_safe = jnp.min(lb_all) >= LB_SAFE

    @pl.when(slab_safe)
    def _():
        kv, bq, bz = swa_proj()
        nxt = hgrn_proj(0)
        for ph in range(n_phase):
            cur = nxt
            if ph + 1 < n_phase:
                nxt = hgrn_proj(ph + 1)
            gate_proj(ph)
            hgrn_pair(ph, *cur)
            swa_block(ph, kv, bq, bz)
        keep_window(kv)

    @pl.when(jnp.logical_not(slab_safe))
    def _():
        for ph in range(n_phase):
            hgrn_pair_blockwise(ph, *hgrn_proj(ph))
            gate_proj(ph)
        kv, bq, bz = swa_proj()
        for j in range(n_blocks):
            swa_block(j, kv, bq, bz)
        keep_window(kv)


def _mix_prompt(bucket, rel_bias, sink, x, g_pre, w, hgrn_lb, hgrn_norm, batch, seq):
    nt = seq // MIX_TQ
    n_lb = hgrn_lb.shape[0]
    kvw = 2 * B_KV_HEADS * B_HD
    smem = pl.BlockSpec(memory_space=pltpu.SMEM)
    tok = lambda width: pl.BlockSpec((MIX_TQ, width), lambda b, t: (b * nt + t, 0))
    return pl.pallas_call(
        _mix_prompt_kernel,
        grid=(batch, nt),
        in_specs=[
            _resident((N_KEYS, WINDOW)),
            smem, smem,
            tok(D_MODEL),
            _resident((1, D_MODEL)),
            _resident(w.shape),
            _resident((n_lb, D_MODEL)),
            _resident((1, D_MODEL)),
        ],
        out_specs=[
            tok(D_MODEL), tok(D_MODEL), tok(2 * D_MODEL),
            pl.BlockSpec((None, A_HEADS, A_DK, A_DV), lambda b, t: (b, 0, 0, 0)),
            pl.BlockSpec((None, WINDOW, kvw), lambda b, t: (b, 0, 0)),
        ],
        out_shape=[
            jax.ShapeDtypeStruct((batch * seq, D_MODEL), BF16),
            jax.ShapeDtypeStruct((batch * seq, D_MODEL), BF16),
            jax.ShapeDtypeStruct((batch * seq, 2 * D_MODEL), BF16),
            jax.ShapeDtypeStruct((batch, A_HEADS, A_DK, A_DV), F32),
            jax.ShapeDtypeStruct((batch, WINDOW, kvw), F32),
        ],
        scratch_shapes=[
            pltpu.VMEM((B_KV_HEADS, 2 * N_KEYS, PAIRS_PER_KV * WINDOW), F32),
            pltpu.VMEM((A_HEADS, A_DV, A_DK), F32),
            pltpu.VMEM((WINDOW, kvw), F32),
            pltpu.VMEM((3, MIX_TQ, 2 * A_DK), F32),
            pltpu.VMEM((MIX_TQ, 2 * A_DK), F32),
        ],
        compiler_params=pltpu.CompilerParams(
            dimension_semantics=("arbitrary", "arbitrary"), vmem_limit_bytes=VMEM_LIMIT),
        name="mix_prompt",
    )(bucket, rel_bias, sink, x, g_pre, w, hgrn_lb, hgrn_norm)


def _outproj_tile(oa_ref, ob_ref, ga_ref, gb_ref, x_ref, p_ref, wpa_ref, wpb_ref, wo_ref, gpost_ref,
                  wple_ref, wg_ref, y_ref):
    a = jnp.dot(oa_ref[...], wpa_ref[...], preferred_element_type=F32)
    b = jnp.dot(ob_ref[...], wpb_ref[...], preferred_element_type=F32)
    m = _sigmoid(ga_ref[...].astype(F32)) * a + _sigmoid(gb_ref[...].astype(F32)) * b
    y = jnp.dot(m.astype(BF16), wo_ref[...], preferred_element_type=F32)
    y = y * lax.rsqrt(jnp.mean(y * y, axis=-1, keepdims=True) + EPS) * gpost_ref[...]
    x1 = x_ref[...] + y
    gate = _sigmoid(jnp.dot(x1.astype(BF16), wg_ref[...], preferred_element_type=F32))
    e = jnp.dot(p_ref[...].astype(BF16), wple_ref[...], preferred_element_type=F32) * gate
    y_ref[...] = x1 + e


def _outproj_kernel(*refs, n_prompt):
    prompt_in, sample_in, weights, (y_p_ref, y_s_ref) = refs[0:6], refs[6:12], refs[12:18], refs[18:20]
    i = pl.program_id(0)

    @pl.when(i < n_prompt)
    def _():
        _outproj_tile(*prompt_in, *weights, y_p_ref)

    @pl.when(i >= n_prompt)
    def _():
        _outproj_tile(*sample_in, *weights, y_s_ref)


def _outproj(prompt, sample, wpa, wpb, wo, gpost, wple, wg, tm):
    n_p = prompt[5].shape[0] // tm
    n_s = sample[5].shape[0] // tm
    ple = prompt[6].shape[1]

    def specs(args, row, once):
        oa, ob, gates, ga_col, gb_col, x, p = args
        mode = dict(pipeline_mode=pl.Buffered(1)) if once else {}
        tok = lambda w, c=0: pl.BlockSpec((tm, w), lambda i, c=c: (row(i), c), **mode)
        return ([tok(D_MODEL), tok(D_MODEL), tok(D_MODEL, ga_col), tok(D_MODEL, gb_col), tok(D_MODEL), tok(ple)],
                [oa, ob, gates, gates, x, p])

    row_p = lambda i: jnp.minimum(i, n_p - 1)
    row_s = lambda i: jnp.maximum(i - n_p, 0)
    specs_p, args_p = specs(prompt, row_p, False)
    specs_s, args_s = specs(sample, row_s, n_s == 1)
    return pl.pallas_call(
        functools.partial(_outproj_kernel, n_prompt=n_p),
        grid=(n_p + n_s,),
        in_specs=specs_p + specs_s + [
            _resident((D_MODEL, D_MODEL)), _resident((D_MODEL, D_MODEL)), _resident((D_MODEL, D_MODEL)),
            _resident((1, D_MODEL)), _resident((ple, D_MODEL)), _resident((D_MODEL, D_MODEL)),
        ],
        out_specs=[pl.BlockSpec((tm, D_MODEL), lambda i: (row_p(i), 0)),
                   pl.BlockSpec((tm, D_MODEL), lambda i: (row_s(i), 0))],
        out_shape=[jax.ShapeDtypeStruct((n_p * tm, D_MODEL), F32),
                   jax.ShapeDtypeStruct((n_s * tm, D_MODEL), F32)],
        compiler_params=pltpu.CompilerParams(
            dimension_semantics=("arbitrary",), vmem_limit_bytes=VMEM_LIMIT),
        name="outproj",
    )(*args_p, *args_s, wpa, wpb, wo, gpost, wple, wg)


def _rel_bucket(rel):
    n = np.maximum(rel, 0)
    max_exact = REL_BUCKETS // 2
    nf = np.maximum(n, 1).astype(np.float32)
    scaled = (np.log(nf / np.float32(max_exact)) / np.float32(math.log(REL_MAX_DIST / max_exact))
              * np.float32(REL_BUCKETS - max_exact))
    frac = scaled - np.floor(scaled)
    inside = (n > max_exact) & (n < REL_MAX_DIST)
    assert np.all((frac[inside] > 1e-3) & (frac[inside] < 1.0 - 1e-3))
    large = np.minimum(max_exact + scaled.astype(np.int32), REL_BUCKETS - 1)
    return np.where(n < max_exact, n, large)


def _bucket_table(q_pos, k_pos, k_valid):
    rel = q_pos[:, None] - k_pos[None, :]
    ok = (rel >= 0) & (rel < WINDOW) & k_valid[None, :]
    return jnp.asarray(np.where(ok, _rel_bucket(rel), -1).astype(np.int32).T)


TM_PROJ = 512
SAMPLE_BB = 8


def _layer(xp, xs, s_hgrn, win_k, win_v, pp, ps, norm_pre, w_in, hgrn_lb, hgrn_norm, attn_sink,
           rel_bias, w_pa, w_pb, w_o, norm_post, w_ple, w_ple_gate):
    batch, seq, _ = xp.shape
    dbatch, t_len, _ = xs.shape
    kw = B_KV_HEADS * B_HD

    w16 = w_in.astype(BF16)
    g_pre = norm_pre.reshape(1, D_MODEL)
    g_post = norm_post.reshape(1, D_MODEL)
    g_hgrn = hgrn_norm.reshape(1, D_MODEL)
    wpa, wpb, wo = w_pa.astype(BF16), w_pb.astype(BF16), w_o.astype(BF16)
    wple, wg = w_ple.astype(BF16), w_ple_gate.astype(BF16)

    xp2 = xp.reshape(batch * seq, D_MODEL)
    xs2 = xs.reshape(dbatch * t_len, D_MODEL)
    k_all = np.arange(N_KEYS)
    bucket_p = _bucket_table(np.arange(WINDOW) + WINDOW, k_all, np.ones((N_KEYS,), bool))
    rows_s = SAMPLE_BB * t_len
    bucket_s = _bucket_table(WINDOW + np.arange(rows_s) % t_len, k_all, k_all < WINDOW + t_len)

    oa_p, ob_p, gab_p, st_p, kvwin_p = _mix_prompt(bucket_p, rel_bias, attn_sink, xp2, g_pre, w16,
                                                   hgrn_lb, g_hgrn, batch, seq)

    pm_s, f_s, kv_s = _inproj(xs2, g_pre, w16, min(TM_PROJ, dbatch * t_len))
    oa_s, st_s, ob_s, nk_s, nv_s = _mix_sample(bucket_s, rel_bias, attn_sink, pm_s, f_s, kv_s, hgrn_lb, g_hgrn,
                                               s_hgrn, win_k.reshape(dbatch, WINDOW, kw),
                                               win_v.reshape(dbatch, WINDOW, kw), dbatch, t_len, SAMPLE_BB)

    assert (dbatch * t_len) % TM_PROJ == 0
    y_p, y_s = _outproj((oa_p, ob_p, gab_p, 0, 1, xp2, pp.reshape(batch * seq, -1)),
                        (oa_s, ob_s, pm_s, G_GA, G_GB, xs2, ps.reshape(dbatch * t_len, -1)),
                        wpa, wpb, wo, g_post, wple, wg, TM_PROJ)

    k_win_p = kvwin_p[:, :, 0:kw].reshape(batch, WINDOW, B_KV_HEADS, B_HD)
    v_win_p = kvwin_p[:, :, kw:].reshape(batch, WINDOW, B_KV_HEADS, B_HD)
    return (y_p.reshape(batch, seq, D_MODEL), y_s.reshape(dbatch, t_len, D_MODEL), st_p, st_s,
            k_win_p, v_win_p,
            nk_s.reshape(dbatch, WINDOW, B_KV_HEADS, B_HD), nv_s.reshape(dbatch, WINDOW, B_KV_HEADS, B_HD))


def kernel(x_prompt, x_sample, state_hgrn, cache_swa_k, cache_swa_v, p_prompt, p_sample, norm_pre, w_in,
           hgrn_lb, hgrn_norm, attn_sink, rel_bias, w_pa, w_pb, w_o, norm_post, w_ple, w_ple_gate):
    depth = w_in.shape[0]
    assert depth == 1, "the forget-gate lower bound is implemented for a single layer"
    xp, xs = x_prompt, x_sample
    outs = []
    for l in range(depth):
        res = _layer(xp, xs, state_hgrn[l], cache_swa_k[l], cache_swa_v[l], p_prompt[l], p_sample[l],
                     norm_pre[l], w_in[l], hgrn_lb, hgrn_norm[l], attn_sink[l], rel_bias,
                     w_pa[l], w_pb[l], w_o[l], norm_post[l], w_ple[l], w_ple_gate[l])
        xp, xs = res[0], res[1]
        outs.append(res[2:])
    stack = lambda i: jnp.stack([o[i] for o in outs])
    return (xp, xs, stack(0), stack(1), stack(2), stack(3), stack(4), stack(5))
```

```python
import functools
import math

import jax
import jax.numpy as jnp
import numpy as np
from jax import lax
from jax.experimental import pallas as pl
from jax.experimental.pallas import tpu as pltpu

F32 = jnp.float32
BF16 = jnp.bfloat16

D_MODEL = 1024
A_HEADS = 8
A_DK = 128
A_DV = 128
A_CHUNK = 64
A_SUB = 16
B_HEADS = 16
B_KV_HEADS = 2
B_HD = 64
WINDOW = 128
REL_BUCKETS = 32
REL_MAX_DIST = 128
EPS = 1e-6
NEG = float("-inf")
LOG2E = math.log2(math.e)
Q_SCALE = B_HD ** -0.5 * LOG2E

G_AQ, G_AI, G_AOG, G_AZ, G_BQ, G_BZ, G_GA, G_GB = range(8)
N_GROUPS = 8

VMEM_LIMIT = 56 * 1024 * 1024


def _sigmoid(x):
    return 0.5 * jnp.tanh(0.5 * x) + 0.5


def _silu(x):
    h = 0.5 * x
    return h * jnp.tanh(h) + h


def _resident(shape):
    nd = len(shape)
    return pl.BlockSpec(shape, lambda *_: (0,) * nd, pipeline_mode=pl.Buffered(1))


IN_OFFS = (0, 1024, 2048, 3072, 4096, 5120, 6144, 6272, 6400, 7424, 8448, 9472)
SLAB_SEGS = (0, 2, 3, 4, 5, 8, 9, 10)
SEG_AF, SEG_BK, SEG_BV = 1, 6, 7


def _inproj_kernel(x_ref, g_ref, w_ref, pm_ref, f_ref, kv_ref):
    x = x_ref[...]
    ms = jnp.mean(x * x, axis=-1, keepdims=True)
    u = (x * lax.rsqrt(ms + EPS) * g_ref[...]).astype(BF16)
    proj = lambda lo, hi: jnp.dot(u, w_ref[:, lo:hi], preferred_element_type=F32)
    for c, seg in enumerate(SLAB_SEGS):
        pm_ref[:, c * D_MODEL:(c + 1) * D_MODEL] = proj(IN_OFFS[seg], IN_OFFS[seg + 1]).astype(BF16)
    f_ref[...] = proj(IN_OFFS[SEG_AF], IN_OFFS[SEG_AF + 1])
    kv_ref[...] = proj(IN_OFFS[SEG_BK], IN_OFFS[SEG_BV + 1])


def _inproj(x, g, w, tm):
    n = x.shape[0]
    nm = N_GROUPS * D_MODEL
    nkv = IN_OFFS[SEG_BV + 1] - IN_OFFS[SEG_BK]
    return pl.pallas_call(
        _inproj_kernel,
        grid=(n // tm,),
        in_specs=[
            pl.BlockSpec((tm, D_MODEL), lambda i: (i, 0)),
            _resident((1, D_MODEL)),
            _resident(w.shape),
        ],
        out_specs=[
            pl.BlockSpec((tm, nm), lambda i: (i, 0)),
            pl.BlockSpec((tm, D_MODEL), lambda i: (i, 0)),
            pl.BlockSpec((tm, nkv), lambda i: (i, 0)),
        ],
        out_shape=[
            jax.ShapeDtypeStruct((n, nm), BF16),
            jax.ShapeDtypeStruct((n, D_MODEL), F32),
            jax.ShapeDtypeStruct((n, nkv), F32),
        ],
        compiler_params=pltpu.CompilerParams(
            dimension_semantics=("arbitrary",), vmem_limit_bytes=VMEM_LIMIT),
        name="inproj",
    )(x, g, w)


def _lower_bound(lb_ref):
    l = lb_ref[...]
    m = jnp.max(l, axis=0, keepdims=True)
    e = jnp.exp(l - m)
    return e[0:1, :] / jnp.sum(e, axis=0, keepdims=True)


def _forget_gate(f_pre, lb):
    return lb + (1.0 - lb) / (1.0 + jnp.exp(-f_pre))


def _group_cumsum(x, row, period, shifts):
    pos = row & (period - 1)
    for sh in shifts:
        x = x + jnp.where(pos >= sh, pltpu.roll(x, sh, axis=0), 0.0)
    return x


def _hgrn_finish(o, g, og_pre, z_pre):
    o = o * lax.rsqrt(jnp.mean(o * o, axis=-1, keepdims=True) + EPS)
    return o * g * _sigmoid(og_pre) * _silu(z_pre)


def _dot_nt(a, b):
    return lax.dot_general(a, b, (((1,), (1,)), ((), ())), preferred_element_type=F32)


def _dot_tn(a, b):
    return lax.dot_general(a, b, (((0,), (0,)), ((), ())), preferred_element_type=F32)


def _hgrn_sample_kernel(q_ref, v_ref, og_ref, z_ref, f_ref, lb_ref, g_ref, s0_ref, oa_ref, s_ref,
                        *, bb, t_len):
    rows = bb * t_len
    lb_all = _lower_bound(lb_ref)
    row = lax.broadcasted_iota(jnp.int32, (rows, A_DK), 0)
    pos = row & (t_len - 1)
    grp = row >> (t_len.bit_length() - 1)
    shifts = tuple(1 << i for i in range((t_len - 1).bit_length()))
    n_tail = A_DK - rows
    sel_row = lax.broadcasted_iota(jnp.int32, (n_tail, 2 * bb * A_DV), 0)
    sel_blk = lax.broadcasted_iota(jnp.int32, (n_tail, 2 * bb * A_DV), 1) >> (A_DV.bit_length() - 1)
    picks = jnp.logical_and(sel_row < 2 * bb, sel_blk == bb + (sel_row & (bb - 1)))
    rhs_tail = jnp.where(picks, 1.0, 0.0).astype(BF16)
    lhs_pad = jnp.zeros((n_tail - 2 * bb, A_DK), F32)
    v_pad = jnp.zeros((rows, bb * A_DV), F32)
    for h in range(A_HEADS):
        ls = slice(h * A_DK, (h + 1) * A_DK)
        lb = lb_all[:, ls]
        fg = _forget_gate(f_ref[:, ls], lb)
        logf = jnp.log(fg)
        k = 1.0 - fg
        b = _group_cumsum(logf, row, t_len, shifts)
        q = _silu(q_ref[:, ls].astype(F32))
        v = v_ref[:, ls].astype(F32)

        o = jnp.zeros((rows, A_DV), F32)
        for d in range(t_len):
            if d == 0:
                a = jnp.sum(q * k, axis=-1, keepdims=True)
                o = o + a * v
            else:
                w = jnp.exp(jnp.where(pos >= d, b - pltpu.roll(b, d, axis=0), 0.0))
                a = jnp.sum(q * pltpu.roll(k, d, axis=0) * w, axis=-1, keepdims=True)
                o = o + jnp.where(pos >= d, a * pltpu.roll(v, d, axis=0), 0.0)

        b_last = b
        for d in range(1, t_len):
            b_last = jnp.where(pos == t_len - 1 - d, pltpu.roll(b, rows - d, axis=0), b_last)
        e_last = jnp.exp(b_last)
        qs = q * jnp.exp(b)
        kd = k * jnp.exp(b_last - b)
        e_rows = jnp.concatenate([e_last[j * t_len:j * t_len + 1, :] for j in range(bb)], axis=0)
        e_hi = e_rows.astype(BF16).astype(F32)
        lhs_t = jnp.concatenate([kd, e_hi, e_rows - e_hi, lhs_pad], axis=0).T.astype(BF16)
        v_rows = jnp.concatenate([jnp.where(grp == j, v, 0.0) for j in range(bb)] + [v_pad], axis=1)
        res = jnp.dot(lhs_t, jnp.concatenate([v_rows.astype(BF16), rhs_tail], axis=0),
                      preferred_element_type=F32)
        for j in range(bb):
            s0 = s0_ref[j, h]
            o = o + jnp.dot(jnp.where(grp == j, qs, 0.0).astype(BF16), s0.astype(BF16),
                            preferred_element_type=F32)
            s_ref[j, h] = (res[:, (bb + j) * A_DV:(bb + j + 1) * A_DV] * s0
                           + res[:, j * A_DV:(j + 1) * A_DV])

        oa = _hgrn_finish(o, g_ref[:, ls], og_ref[:, ls].astype(F32), z_ref[:, ls].astype(F32))
        oa_ref[:, ls] = oa.astype(BF16)


N_PAIRS = B_HEADS // 2
PAIRS_PER_KV = N_PAIRS // B_KV_HEADS
N_KEYS = 2 * WINDOW


def _build_bias(bucket_ref, relb_ref, bias_ref, r):
    bucket = bucket_ref[...]
    for h in range(B_HEADS):
        def body(kb, acc, h=h):
            return jnp.where(bucket == kb, relb_ref[kb, h] * LOG2E, acc)
        tab = lax.fori_loop(0, REL_BUCKETS, body, jnp.full(bucket.shape, NEG, F32))
        pair, parity = divmod(h, 2)
        kv, pp = divmod(pair, PAIRS_PER_KV)
        bias_ref[kv, parity * N_KEYS:(parity + 1) * N_KEYS, pp * r:(pp + 1) * r] = tab


def _swa_keys(kk):
    lo = lax.broadcasted_iota(jnp.int32, (N_KEYS, 2 * B_HD), 1) < B_HD
    kk_sw = pltpu.roll(kk, B_HD, axis=1)
    slabs = []
    for kv in range(B_KV_HEADS):
        k_lo, k_hi = (kk, kk_sw) if kv == 0 else (kk_sw, kk)
        slabs.append(jnp.concatenate([jnp.where(lo, k_lo, 0.0), jnp.where(lo, 0.0, k_hi)], axis=0).astype(BF16))
    return slabs


def _swa_attend(logits, vals_t, col_masks, sink_ref, kv, r):
    width = PAIRS_PER_KV * r
    lane_pp = lax.broadcasted_iota(jnp.int32, (1, width), 1) >> (r.bit_length() - 1)
    halves = []
    for parity in range(2):
        sink = jnp.zeros((1, width), F32)
        for pp in range(PAIRS_PER_KV):
            sink = jnp.where(lane_pp == pp, sink_ref[(kv * PAIRS_PER_KV + pp) * 2 + parity] * LOG2E, sink)
        l = logits[parity * N_KEYS:(parity + 1) * N_KEYS, :]
        m = jnp.maximum(jnp.max(l, axis=0, keepdims=True), sink)
        p = jnp.exp2(l - m)
        denom = jnp.sum(p, axis=0, keepdims=True) + jnp.exp2(sink - m)
        p16 = p.astype(BF16)
        v_rows = jnp.concatenate([v_t[kv * B_HD:(kv + 1) * B_HD, :] for v_t in vals_t], axis=0)
        pv_all = jnp.dot(v_rows, p16, preferred_element_type=F32)
        pv = None
        for n, mask in enumerate(col_masks):
            part = pv_all[n * B_HD:(n + 1) * B_HD, :]
            pv = part if mask is None else jnp.where(mask, part, 0.0 if pv is None else pv)
        halves.append(pv / denom)
    return jnp.concatenate(halves, axis=0).T


def _swa_sample_kernel(bucket_ref, relb_ref, sink_ref, q_ref, z_ref, kvn_ref, ck_ref, cv_ref,
                       ob_ref, nk_ref, nv_ref, bias_ref, *, bb, t_len):
    rows = bb * t_len
    width = PAIRS_PER_KV * rows

    @pl.when(pl.program_id(0) == 0)
    def _():
        _build_bias(bucket_ref, relb_ref, bias_ref, rows)

    kw = B_KV_HEADS * B_HD
    t_shift = t_len.bit_length() - 1
    row = lax.broadcasted_iota(jnp.int32, (rows, kw), 0)
    seq_of_col = lambda n: (lax.broadcasted_iota(jnp.int32, (n, width), 1) & (rows - 1)) >> t_shift
    seq_l = seq_of_col(2 * N_KEYS)
    seq_v = seq_of_col(B_HD)
    q_pairs = [(q_ref[:, p * 128:(p + 1) * 128].astype(F32) * Q_SCALE).astype(BF16) for p in range(N_PAIRS)]
    q_kv = [jnp.concatenate(q_pairs[kv * PAIRS_PER_KV:(kv + 1) * PAIRS_PER_KV], axis=0)
            for kv in range(B_KV_HEADS)]
    kn = kvn_ref[:, 0:kw]
    vn = kvn_ref[:, kw:2 * kw]
    pad = jnp.zeros((N_KEYS - WINDOW - rows, kw), F32)
    logits = [None] * B_KV_HEADS
    vals_t, col_masks = [], []
    for j in range(bb):
        sel = row < t_len
        kj = jnp.where(sel, kn if j == 0 else pltpu.roll(kn, rows - j * t_len, axis=0), 0.0)
        vj = jnp.where(sel, vn if j == 0 else pltpu.roll(vn, rows - j * t_len, axis=0), 0.0)
        kk = jnp.concatenate([ck_ref[j], kj, pad], axis=0)
        vv = jnp.concatenate([cv_ref[j], vj, pad], axis=0)
        keys = _swa_keys(kk)
        for kv in range(B_KV_HEADS):
            lg = _dot_nt(keys[kv], q_kv[kv])
            logits[kv] = jnp.where(seq_l == j, lg, 0.0 if logits[kv] is None else logits[kv])
        vals_t.append(vv.T.astype(BF16))
        col_masks.append(seq_v == j)
        nk_ref[j] = pltpu.roll(kk, N_KEYS - t_len, axis=0)[0:WINDOW, :]
        nv_ref[j] = pltpu.roll(vv, N_KEYS - t_len, axis=0)[0:WINDOW, :]
    for kv in range(B_KV_HEADS):
        acc_t = _swa_attend(logits[kv] + bias_ref[kv], vals_t, col_masks, sink_ref, kv, rows)
        for pp in range(PAIRS_PER_KV):
            p = kv * PAIRS_PER_KV + pp
            z = z_ref[:, p * 128:(p + 1) * 128].astype(F32)
            ob_ref[:, p * 128:(p + 1) * 128] = (acc_t[pp * rows:(pp + 1) * rows, :] * _silu(z)).astype(BF16)


def _mix_sample_kernel(bucket_ref, relb_ref, sink_ref, aq_ref, ai_ref, aog_ref, az_ref, f_ref, lb_ref, g_ref,
                       s0_ref, bq_ref, bz_ref, kvn_ref, ck_ref, cv_ref,
                       oa_ref, s_ref, ob_ref, nk_ref, nv_ref, bias_ref, *, bb, t_len):
    _hgrn_sample_kernel(aq_ref, ai_ref, aog_ref, az_ref, f_ref, lb_ref, g_ref, s0_ref, oa_ref, s_ref,
                        bb=bb, t_len=t_len)
    _swa_sample_kernel(bucket_ref, relb_ref, sink_ref, bq_ref, bz_ref, kvn_ref, ck_ref, cv_ref,
                       ob_ref, nk_ref, nv_ref, bias_ref, bb=bb, t_len=t_len)


def _mix_sample(bucket, rel_bias, sink, pm, f, kvn, hgrn_lb, hgrn_norm, s0, cache_k, cache_v,
                batch, t_len, bb):
    rows = bb * t_len
    kvw = kvn.shape[1]
    kw = kvw // 2
    n_lb = hgrn_lb.shape[0]
    smem = pl.BlockSpec(memory_space=pltpu.SMEM)
    col = lambda grp: pl.BlockSpec((rows, D_MODEL), lambda i, grp=grp: (i, grp))
    tok = pl.BlockSpec((rows, D_MODEL), lambda i: (i, 0))
    st_spec = pl.BlockSpec((bb, A_HEADS, A_DK, A_DV), lambda i: (i, 0, 0, 0))
    cache_spec = pl.BlockSpec((bb, WINDOW, kw), lambda i: (i, 0, 0))
    return pl.pallas_call(
        functools.partial(_mix_sample_kernel, bb=bb, t_len=t_len),
        grid=(batch // bb,),
        in_specs=[
            _resident((N_KEYS, rows)),
            smem, smem,
            col(G_AQ), col(G_AI), col(G_AOG), col(G_AZ),
            tok,
            _resident((n_lb, D_MODEL)),
            _resident((1, D_MODEL)),
            st_spec,
            col(G_BQ), col(G_BZ),
            pl.BlockSpec((rows, kvw), lambda i: (i, 0)),
            cache_spec, cache_spec,
        ],
        out_specs=[tok, st_spec, tok, cache_spec, cache_spec],
        out_shape=[
            jax.ShapeDtypeStruct((batch * t_len, D_MODEL), BF16),
            jax.ShapeDtypeStruct((batch, A_HEADS, A_DK, A_DV), F32),
            jax.ShapeDtypeStruct((batch * t_len, D_MODEL), BF16),
            jax.ShapeDtypeStruct((batch, WINDOW, kw), F32),
            jax.ShapeDtypeStruct((batch, WINDOW, kw), F32),
        ],
        scratch_shapes=[pltpu.VMEM((B_KV_HEADS, 2 * N_KEYS, PAIRS_PER_KV * rows), F32)],
        compiler_params=pltpu.CompilerParams(
            dimension_semantics=("arbitrary",), vmem_limit_bytes=VMEM_LIMIT),
        name="mix_sample",
    )(bucket, rel_bias, sink, pm, pm, pm, pm, f, hgrn_lb, hgrn_norm, s0, pm, pm, kvn, cache_k, cache_v)


MIX_TQ = 512
HGRN_SLAB = 256


class _HgrnConsts:
    def __init__(self, slab):
        shift = A_CHUNK.bit_length() - 1
        self.row = lax.broadcasted_iota(jnp.int32, (slab, A_DK), 0)
        rs = lax.broadcasted_iota(jnp.int32, (slab, slab), 0)
        cs = lax.broadcasted_iota(jnp.int32, (slab, slab), 1)
        self.same_chunk_causal = jnp.logical_and((rs >> shift) == (cs >> shift), cs <= rs)
        self.zero_row = jnp.zeros((1, A_DK), F32)
        self.zero_blk = jnp.zeros((A_SUB, A_DK), BF16)
        self.zero_chunk = jnp.zeros((A_CHUNK, A_DK), BF16)


def _hgrn_slab(f_pre, q_pre, v, lb, st, c):
    slab = f_pre.shape[0]
    n_sub = A_CHUNK // A_SUB
    sc = slab // A_CHUNK
    rep = lambda r, n: jnp.broadcast_to(r, (n, A_DK))
    fg = _forget_gate(f_pre, lb)
    logf = jnp.log(fg)
    k = 1.0 - fg
    b = _group_cumsum(logf, c.row, A_CHUNK, (1, 2, 4, 8, 16, 32))
    q = _silu(q_pre)

    ends = [b[A_SUB * j + A_SUB - 1:A_SUB * (j + 1), :] for j in range(sc * n_sub)]
    endrow = jnp.concatenate([rep(e, A_SUB) for e in ends], axis=0)
    prevrow = jnp.concatenate(
        [rep(c.zero_row if j % n_sub == 0 else ends[j - 1], A_SUB) for j in range(sc * n_sub)], axis=0)
    tot = [ends[n_sub * ch + n_sub - 1] for ch in range(sc)]
    lastrow = jnp.concatenate([rep(t, A_CHUNK) for t in tot], axis=0)
    kend = k * jnp.exp(endrow - b)
    qd = q * jnp.exp(b - prevrow)
    qs = qd * jnp.exp(prevrow)
    kd = kend * jnp.exp(lastrow - endrow)
    kdiag16 = (k * jnp.exp(prevrow - b)).astype(BF16)
    qd16 = qd.astype(BF16)
    kend16 = kend.astype(BF16)
    qs16 = qs.astype(BF16)
    kd16 = kd.astype(BF16)

    blk = lambda a, j: a[A_SUB * j:A_SUB * (j + 1), :]
    chk = lambda a, ch: a[A_CHUNK * ch:A_CHUNK * (ch + 1), :]

    q_groups, k_groups = [], []
    for i in range(n_sub):
        q_groups.append(jnp.concatenate(
            [blk(qd16, j) if j % n_sub == i else c.zero_blk for j in range(sc * n_sub)], axis=0))
        pieces = []
        for j in range(sc * n_sub):
            ch, jj = divmod(j, n_sub)
            if jj == i:
                pieces.append(blk(kdiag16, j))
            elif jj == i - 1:
                pieces.append(blk(kend16, j))
            elif jj < i:
                pieces.append((blk(kend, j) * jnp.exp(ends[ch * n_sub + i - 1] - ends[j])).astype(BF16))
            else:
                pieces.append(c.zero_blk)
        k_groups.append(jnp.concatenate(pieces, axis=0))
    att = _dot_nt(jnp.concatenate(q_groups, axis=1), jnp.concatenate(k_groups, axis=1))
    att = jnp.where(c.same_chunk_causal, att, 0.0)

    if sc > 1:
        q_groups, k_groups = [], []
        for ch in range(1, sc):
            q_groups.append(jnp.concatenate(
                [chk(qs16, c2) if c2 == ch else c.zero_chunk for c2 in range(sc)], axis=0))
            pieces = []
            for c2 in range(sc):
                if c2 == ch - 1:
                    pieces.append(chk(kd16, c2))
                elif c2 < ch:
                    pieces.append((chk(kd, c2) * jnp.exp(sum(tot[c2 + 1:ch]))).astype(BF16))
                else:
                    pieces.append(c.zero_chunk)
            k_groups.append(jnp.concatenate(pieces, axis=0))
        att = att + _dot_nt(jnp.concatenate(q_groups, axis=1), jnp.concatenate(k_groups, axis=1))
    o = jnp.dot(att.astype(BF16), v, preferred_element_type=F32)

    q0 = jnp.concatenate(
        [chk(qs16, 0)] + [(chk(qs, ch) * jnp.exp(sum(tot[:ch]))).astype(BF16) for ch in range(1, sc)], axis=0)
    o = o + _dot_nt(q0, st.astype(BF16))

    k1 = jnp.concatenate(
        [(chk(kd, ch) * jnp.exp(sum(tot[ch + 1:]))).astype(BF16) for ch in range(sc - 1)]
        + [chk(kd16, sc - 1)], axis=0)
    st = st * jnp.exp(sum(tot)) + _dot_tn(v, k1)
    return o, st


def _hgrn_blockwise(stage_ref, oraw_ref, ls, lb, st):
    n_blk = stage_ref.shape[1] // A_SUB
    row = lax.broadcasted_iota(jnp.int32, (A_SUB, A_DK), 0)
    shifts = tuple(1 << i for i in range((A_SUB - 1).bit_length()))

    def body(i, st):
        rows = pl.ds(pl.multiple_of(i * A_SUB, A_SUB), A_SUB)
        fg = _forget_gate(stage_ref[1, rows, ls], lb)
        k = 1.0 - fg
        b = _group_cumsum(jnp.log(fg), row, A_SUB, shifts)
        q = _silu(stage_ref[0, rows, ls])
        v = stage_ref[2, rows, ls]
        o = _dot_nt((q * jnp.exp(b)).astype(BF16), st.astype(BF16))
        o = o + jnp.sum(q * k, axis=-1, keepdims=True) * v
        for d in range(1, A_SUB):
            ok = row >= d
            w = jnp.exp(jnp.where(ok, b - pltpu.roll(b, d, axis=0), 0.0))
            a = jnp.sum(q * pltpu.roll(k, d, axis=0) * w, axis=-1, keepdims=True)
            o = o + jnp.where(ok, a * pltpu.roll(v, d, axis=0), 0.0)
        oraw_ref[rows, ls] = o
        b_end = b[A_SUB - 1:A_SUB, :]
        kd = (k * jnp.exp(b_end - b)).astype(BF16)
        return st * jnp.exp(b_end) + _dot_tn(v.astype(BF16), kd)

    return lax.fori_loop(0, n_blk, body, st)


LB_SAFE = math.exp(-80.0 / A_SUB)


def _mix_prompt_kernel(bucket_ref, relb_ref, sink_ref, x_ref, gpre_ref, w_ref, lb_ref, gh_ref,
                       oa_ref, ob_ref, gab_ref, st_ref, kvwin_ref,
                       bias_ref, st_scr, kvprev_scr, stage_scr, oraw_scr):
    tile = pl.program_id(1)
    at_start = tile == 0

    @pl.when(jnp.logical_and(pl.program_id(0) == 0, at_start))
    def _():
        _build_bias(bucket_ref, relb_ref, bias_ref, WINDOW)

    @pl.when(at_start)
    def _():
        st_scr[...] = jnp.zeros(st_scr.shape, F32)
        kvprev_scr[...] = jnp.zeros(kvprev_scr.shape, F32)

    x = x_ref[...]
    ms = jnp.mean(x * x, axis=-1, keepdims=True)
    u = (x * lax.rsqrt(ms + EPS) * gpre_ref[...]).astype(BF16)
    proj = lambda seg, lo, hi: jnp.dot(u, w_ref[:, IN_OFFS[seg] + lo:IN_OFFS[seg] + hi],
                                       preferred_element_type=F32)

    lb_all = _lower_bound(lb_ref)
    consts = _HgrnConsts(HGRN_SLAB)
    pair_w = 2 * A_DK
    n_phase = A_HEADS // 2
    n_blocks = MIX_TQ // WINDOW
    assert n_blocks == n_phase
    gate_w = D_MODEL // n_phase
    kw = B_KV_HEADS * B_HD

    def hgrn_proj(hp):
        lo, hi = hp * pair_w, (hp + 1) * pair_w
        return tuple(proj(seg, lo, hi) for seg in (0, SEG_AF, 2, 3, 4))

    def hgrn_pair(hp, q2, f2, v2, og2, z2):
        for hh in range(2):
            h = 2 * hp + hh
            ls = slice(hh * A_DK, (hh + 1) * A_DK)
            hs = slice(h * A_DK, (h + 1) * A_DK)
            st = st_scr[h]
            outs = []
            for s0 in range(0, MIX_TQ, HGRN_SLAB):
                rs = slice(s0, s0 + HGRN_SLAB)
                o, st = _hgrn_slab(f2[rs, ls], q2[rs, ls], v2[rs, ls].astype(BF16), lb_all[:, hs], st, consts)
                outs.append(o)
            st_scr[h] = st
            st_ref[h] = st.T
            oa = _hgrn_finish(jnp.concatenate(outs, axis=0), gh_ref[:, hs], og2[:, ls], z2[:, ls])
            oa_ref[:, hs] = oa.astype(BF16)

    key = lax.broadcasted_iota(jnp.int32, (2 * N_KEYS, PAIRS_PER_KV * WINDOW), 0) & (N_KEYS - 1)
    no_prev = jnp.logical_and(key < WINDOW, at_start)

    def swa_block(j, kv, bq, bz):
        r0 = j * WINDOW
        if j == 0:
            kvj = jnp.concatenate([kvprev_scr[...], kv[0:WINDOW, :]], axis=0)
        else:
            kvj = kv[r0 - WINDOW:r0 + WINDOW, :]
        keys = _swa_keys(kvj[:, 0:kw])
        vals_t = [kvj[:, kw:2 * kw].T.astype(BF16)]
        for kvh in range(B_KV_HEADS):
            q = jnp.concatenate([bq[r0:r0 + WINDOW, p * 128:(p + 1) * 128]
                                 for p in range(kvh * PAIRS_PER_KV, (kvh + 1) * PAIRS_PER_KV)], axis=0)
            logits = _dot_nt(keys[kvh], q) + bias_ref[kvh]
            if j == 0:
                logits = jnp.where(no_prev, NEG, logits)
            acc_t = _swa_attend(logits, vals_t, [None], sink_ref, kvh, WINDOW)
            for pp in range(PAIRS_PER_KV):
                cs = slice((kvh * PAIRS_PER_KV + pp) * 128, (kvh * PAIRS_PER_KV + pp + 1) * 128)
                ob_ref[r0:r0 + WINDOW, cs] = (
                    acc_t[pp * WINDOW:(pp + 1) * WINDOW, :] * _silu(bz[r0:r0 + WINDOW, cs])).astype(BF16)

    def hgrn_pair_blockwise(hp, q2, f2, v2, og2, z2):
        stage_scr[0] = q2
        stage_scr[1] = f2
        stage_scr[2] = v2
        for hh in range(2):
            h = 2 * hp + hh
            ls = slice(hh * A_DK, (hh + 1) * A_DK)
            hs = slice(h * A_DK, (h + 1) * A_DK)
            st = _hgrn_blockwise(stage_scr, oraw_scr, ls, lb_all[:, hs], st_scr[h])
            st_scr[h] = st
            st_ref[h] = st.T
            oa = _hgrn_finish(oraw_scr[:, ls], gh_ref[:, hs], og2[:, ls], z2[:, ls])
            oa_ref[:, hs] = oa.astype(BF16)

    def gate_proj(ph):
        for seg, base in ((9, 0), (10, D_MODEL)):
            gab_ref[:, base + ph * gate_w:base + (ph + 1) * gate_w] = proj(
                seg, ph * gate_w, (ph + 1) * gate_w).astype(BF16)

    def swa_proj():
        kv = proj(SEG_BK, 0, 2 * kw)
        bq = (proj(5, 0, D_MODEL) * Q_SCALE).astype(BF16)
        bz = proj(8, 0, D_MODEL)
        return kv, bq, bz

    def keep_window(kv):
        kvprev_scr[...] = kv[MIX_TQ - WINDOW:, :]
        kvwin_ref[...] = kv[MIX_TQ - WINDOW:, :]

    slab_safe = jnp.min(lb_all) >= LB_SAFE

    @pl.when(slab_safe)
    def _():
        kv, bq, bz = swa_proj()
        nxt = hgrn_proj(0)
        for ph in range(n_phase):
            cur = nxt
            if ph + 1 < n_phase:
                nxt = hgrn_proj(ph + 1)
            gate_proj(ph)
            hgrn_pair(ph, *cur)
            swa_block(ph, kv, bq, bz)
        keep_window(kv)

    @pl.when(jnp.logical_not(slab_safe))
    def _():
        for ph in range(n_phase):
            hgrn_pair_blockwise(ph, *hgrn_proj(ph))
            gate_proj(ph)
        kv, bq, bz = swa_proj()
        for j in range(n_blocks):
            swa_block(j, kv, bq, bz)
        keep_window(kv)


def _mix_prompt(bucket, rel_bias, sink, x, g_pre, w, hgrn_lb, hgrn_norm, batch, seq):
    nt = seq // MIX_TQ
    n_lb = hgrn_lb.shape[0]
    kvw = 2 * B_KV_HEADS * B_HD
    smem = pl.BlockSpec(memory_space=pltpu.SMEM)
    tok = lambda width: pl.BlockSpec((MIX_TQ, width), lambda b, t: (b * nt + t, 0))
    return pl.pallas_call(
        _mix_prompt_kernel,
        grid=(batch, nt),
        in_specs=[
            _resident((N_KEYS, WINDOW)),
            smem, smem,
            tok(D_MODEL),
            _resident((1, D_MODEL)),
            _resident(w.shape),
            _resident((n_lb, D_MODEL)),
            _resident((1, D_MODEL)),
        ],
        out_specs=[
            tok(D_MODEL), tok(D_MODEL), tok(2 * D_MODEL),
            pl.BlockSpec((None, A_HEADS, A_DK, A_DV), lambda b, t: (b, 0, 0, 0)),
            pl.BlockSpec((None, WINDOW, kvw), lambda b, t: (b, 0, 0)),
        ],
        out_shape=[
            jax.ShapeDtypeStruct((batch * seq, D_MODEL), BF16),
            jax.ShapeDtypeStruct((batch * seq, D_MODEL), BF16),
            jax.ShapeDtypeStruct((batch * seq, 2 * D_MODEL), BF16),
            jax.ShapeDtypeStruct((batch, A_HEADS, A_DK, A_DV), F32),
            jax.ShapeDtypeStruct((batch, WINDOW, kvw), F32),
        ],
        scratch_shapes=[
            pltpu.VMEM((B_KV_HEADS, 2 * N_KEYS, PAIRS_PER_KV * WINDOW), F32),
            pltpu.VMEM((A_HEADS, A_DV, A_DK), F32),
            pltpu.VMEM((WINDOW, kvw), F32),
            pltpu.VMEM((3, MIX_TQ, 2 * A_DK), F32),
            pltpu.VMEM((MIX_TQ, 2 * A_DK), F32),
        ],
        compiler_params=pltpu.CompilerParams(
            dimension_semantics=("arbitrary", "arbitrary"), vmem_limit_bytes=VMEM_LIMIT),
        name="mix_prompt",
    )(bucket, rel_bias, sink, x, g_pre, w, hgrn_lb, hgrn_norm)


def _outproj_kernel(oa_ref, ob_ref, ga_ref, gb_ref, x_ref, p_ref, wpa_ref, wpb_ref, wo_ref, gpost_ref,
                    wple_ref, wg_ref, y_ref):
    a = jnp.dot(oa_ref[...], wpa_ref[...], preferred_element_type=F32)
    b = jnp.dot(ob_ref[...], wpb_ref[...], preferred_element_type=F32)
    m = _sigmoid(ga_ref[...].astype(F32)) * a + _sigmoid(gb_ref[...].astype(F32)) * b
    y = jnp.dot(m.astype(BF16), wo_ref[...], preferred_element_type=F32)
    y = y * lax.rsqrt(jnp.mean(y * y, axis=-1, keepdims=True) + EPS) * gpost_ref[...]
    x1 = x_ref[...] + y
    gate = _sigmoid(jnp.dot(x1.astype(BF16), wg_ref[...], preferred_element_type=F32))
    e = jnp.dot(p_ref[...].astype(BF16), wple_ref[...], preferred_element_type=F32) * gate
    y_ref[...] = x1 + e


def _outproj(oa, ob, gates, ga_col, gb_col, x, p, wpa, wpb, wo, gpost, wple, wg, tm):
    n = x.shape[0]
    ple = p.shape[1]
    tok = lambda w, c=0: pl.BlockSpec((tm, w), lambda i, c=c: (i, c))
    return pl.pallas_call(
        _outproj_kernel,
        grid=(n // tm,),
        in_specs=[
            tok(D_MODEL), tok(D_MODEL), tok(D_MODEL, ga_col), tok(D_MODEL, gb_col), tok(D_MODEL), tok(ple),
            _resident((D_MODEL, D_MODEL)), _resident((D_MODEL, D_MODEL)), _resident((D_MODEL, D_MODEL)),
            _resident((1, D_MODEL)), _resident((ple, D_MODEL)), _resident((D_MODEL, D_MODEL)),
        ],
        out_specs=tok(D_MODEL),
        out_shape=jax.ShapeDtypeStruct((n, D_MODEL), F32),
        compiler_params=pltpu.CompilerParams(
            dimension_semantics=("arbitrary",), vmem_limit_bytes=VMEM_LIMIT),
        name="outproj",
    )(oa, ob, gates, gates, x, p, wpa, wpb, wo, gpost, wple, wg)


def _rel_bucket(rel):
    n = np.maximum(rel, 0)
    max_exact = REL_BUCKETS // 2
    nf = np.maximum(n, 1).astype(np.float32)
    scaled = (np.log(nf / np.float32(max_exact)) / np.float32(math.log(REL_MAX_DIST / max_exact))
              * np.float32(REL_BUCKETS - max_exact))
    frac = scaled - np.floor(scaled)
    inside = (n > max_exact) & (n < REL_MAX_DIST)
    assert np.all((frac[inside] > 1e-3) & (frac[inside] < 1.0 - 1e-3))
    large = np.minimum(max_exact + scaled.astype(np.int32), REL_BUCKETS - 1)
    return np.where(n < max_exact, n, large)


def _bucket_table(q_pos, k_pos, k_valid):
    rel = q_pos[:, None] - k_pos[None, :]
    ok = (rel >= 0) & (rel < WINDOW) & k_valid[None, :]
    return jnp.asarray(np.where(ok, _rel_bucket(rel), -1).astype(np.int32).T)


TM_PROJ = 512
SAMPLE_BB = 8


def _layer(xp, xs, s_hgrn, win_k, win_v, pp, ps, norm_pre, w_in, hgrn_lb, hgrn_norm, attn_sink,
           rel_bias, w_pa, w_pb, w_o, norm_post, w_ple, w_ple_gate):
    batch, seq, _ = xp.shape
    dbatch, t_len, _ = xs.shape
    kw = B_KV_HEADS * B_HD

    w16 = w_in.astype(BF16)
    g_pre = norm_pre.reshape(1, D_MODEL)
    g_post = norm_post.reshape(1, D_MODEL)
    g_hgrn = hgrn_norm.reshape(1, D_MODEL)
    wpa, wpb, wo = w_pa.astype(BF16), w_pb.astype(BF16), w_o.astype(BF16)
    wple, wg = w_ple.astype(BF16), w_ple_gate.astype(BF16)

    xp2 = xp.reshape(batch * seq, D_MODEL)
    xs2 = xs.reshape(dbatch * t_len, D_MODEL)
    k_all = np.arange(N_KEYS)
    bucket_p = _bucket_table(np.arange(WINDOW) + WINDOW, k_all, np.ones((N_KEYS,), bool))
    rows_s = SAMPLE_BB * t_len
    bucket_s = _bucket_table(WINDOW + np.arange(rows_s) % t_len, k_all, k_all < WINDOW + t_len)

    oa_p, ob_p, gab_p, st_p, kvwin_p = _mix_prompt(bucket_p, rel_bias, attn_sink, xp2, g_pre, w16,
                                                   hgrn_lb, g_hgrn, batch, seq)

    pm_s, f_s, kv_s = _inproj(xs2, g_pre, w16, min(TM_PROJ, dbatch * t_len))
    oa_s, st_s, ob_s, nk_s, nv_s = _mix_sample(bucket_s, rel_bias, attn_sink, pm_s, f_s, kv_s, hgrn_lb, g_hgrn,
                                               s_hgrn, win_k.reshape(dbatch, WINDOW, kw),
                                               win_v.reshape(dbatch, WINDOW, kw), dbatch, t_len, SAMPLE_BB)

    y_p = _outproj(oa_p, ob_p, gab_p, 0, 1, xp2, pp.reshape(batch * seq, -1),
                   wpa, wpb, wo, g_post, wple, wg, TM_PROJ)
    y_s = _outproj(oa_s, ob_s, pm_s, G_GA, G_GB, xs2, ps.reshape(dbatch * t_len, -1),
                   wpa, wpb, wo, g_post, wple, wg, min(TM_PROJ, dbatch * t_len))

    k_win_p = kvwin_p[:, :, 0:kw].reshape(batch, WINDOW, B_KV_HEADS, B_HD)
    v_win_p = kvwin_p[:, :, kw:].reshape(batch, WINDOW, B_KV_HEADS, B_HD)
    return (y_p.reshape(batch, seq, D_MODEL), y_s.reshape(dbatch, t_len, D_MODEL), st_p, st_s,
            k_win_p, v_win_p,
            nk_s.reshape(dbatch, WINDOW, B_KV_HEADS, B_HD), nv_s.reshape(dbatch, WINDOW, B_KV_HEADS, B_HD))


def kernel(x_prompt, x_sample, state_hgrn, cache_swa_k, cache_swa_v, p_prompt, p_sample, norm_pre, w_in,
           hgrn_lb, hgrn_norm, attn_sink, rel_bias, w_pa, w_pb, w_o, norm_post, w_ple, w_ple_gate):
    depth = w_in.shape[0]
    assert depth == 1, "the forget-gate lower bound is implemented for a single layer"
    xp, xs = x_prompt, x_sample
    outs = []
    for l in range(depth):
        res = _layer(xp, xs, state_hgrn[l], cache_swa_k[l], cache_swa_v[l], p_prompt[l], p_sample[l],
                     norm_pre[l], w_in[l], hgrn_lb, hgrn_norm[l], attn_sink[l], rel_bias,
                     w_pa[l], w_pb[l], w_o[l], norm_post[l], w_ple[l], w_ple_gate[l])
        xp, xs = res[0], res[1]
        outs.append(res[2:])
    stack = lambda i: jnp.stack([o[i] for o in outs])
    return (xp, xs, stack(0), stack(1), stack(2), stack(3), stack(4), stack(5))
```

```python
import functools
import math

import jax
import jax.numpy as jnp
import numpy as np
from jax import lax
from jax.experimental import pallas as pl
from jax.experimental.pallas import tpu as pltpu

F32 = jnp.float32
BF16 = jnp.bfloat16

D_MODEL = 1024
A_HEADS = 8
A_DK = 128
A_DV = 128
A_CHUNK = 64
A_SUB = 16
B_HEADS = 16
B_KV_HEADS = 2
B_HD = 64
WINDOW = 128
REL_BUCKETS = 32
REL_MAX_DIST = 128
EPS = 1e-6
NEG = float("-inf")
LOG2E = math.log2(math.e)
Q_SCALE = B_HD ** -0.5 * LOG2E

G_AQ, G_AI, G_AOG, G_AZ, G_BQ, G_BZ, G_GA, G_GB = range(8)
N_GROUPS = 8

VMEM_LIMIT = 56 * 1024 * 1024


def _sigmoid(x):
    return 0.5 * jnp.tanh(0.5 * x) + 0.5


def _silu(x):
    h = 0.5 * x
    return h * jnp.tanh(h) + h


def _resident(shape):
    nd = len(shape)
    return pl.BlockSpec(shape, lambda *_: (0,) * nd, pipeline_mode=pl.Buffered(1))


IN_OFFS = (0, 1024, 2048, 3072, 4096, 5120, 6144, 6272, 6400, 7424, 8448, 9472)
SLAB_SEGS = (0, 2, 3, 4, 5, 8, 9, 10)
SEG_AF, SEG_BK, SEG_BV = 1, 6, 7


def _inproj_kernel(x_ref, g_ref, w_ref, pm_ref, f_ref, kv_ref):
    x = x_ref[...]
    ms = jnp.mean(x * x, axis=-1, keepdims=True)
    u = (x * lax.rsqrt(ms + EPS) * g_ref[...]).astype(BF16)
    proj = lambda lo, hi: jnp.dot(u, w_ref[:, lo:hi], preferred_element_type=F32)
    for c, seg in enumerate(SLAB_SEGS):
        pm_ref[:, c * D_MODEL:(c + 1) * D_MODEL] = proj(IN_OFFS[seg], IN_OFFS[seg + 1]).astype(BF16)
    f_ref[...] = proj(IN_OFFS[SEG_AF], IN_OFFS[SEG_AF + 1])
    kv_ref[...] = proj(IN_OFFS[SEG_BK], IN_OFFS[SEG_BV + 1])


def _inproj(x, g, w, tm):
    n = x.shape[0]
    nm = N_GROUPS * D_MODEL
    nkv = IN_OFFS[SEG_BV + 1] - IN_OFFS[SEG_BK]
    return pl.pallas_call(
        _inproj_kernel,
        grid=(n // tm,),
        in_specs=[
            pl.BlockSpec((tm, D_MODEL), lambda i: (i, 0)),
            _resident((1, D_MODEL)),
            _resident(w.shape),
        ],
        out_specs=[
            pl.BlockSpec((tm, nm), lambda i: (i, 0)),
            pl.BlockSpec((tm, D_MODEL), lambda i: (i, 0)),
            pl.BlockSpec((tm, nkv), lambda i: (i, 0)),
        ],
        out_shape=[
            jax.ShapeDtypeStruct((n, nm), BF16),
            jax.ShapeDtypeStruct((n, D_MODEL), F32),
            jax.ShapeDtypeStruct((n, nkv), F32),
        ],
        compiler_params=pltpu.CompilerParams(
            dimension_semantics=("arbitrary",), vmem_limit_bytes=VMEM_LIMIT),
        name="inproj",
    )(x, g, w)


def _lower_bound(lb_ref):
    l = lb_ref[...]
    m = jnp.max(l, axis=0, keepdims=True)
    e = jnp.exp(l - m)
    return e[0:1, :] / jnp.sum(e, axis=0, keepdims=True)


def _forget_gate(f_pre, lb):
    return lb + (1.0 - lb) / (1.0 + jnp.exp(-f_pre))


def _group_cumsum(x, row, period, shifts):
    pos = row & (period - 1)
    for sh in shifts:
        x = x + jnp.where(pos >= sh, pltpu.roll(x, sh, axis=0), 0.0)
    return x


def _hgrn_finish(o, g, og_pre, z_pre):
    o = o * lax.rsqrt(jnp.mean(o * o, axis=-1, keepdims=True) + EPS)
    return o * g * _sigmoid(og_pre) * _silu(z_pre)


def _dot_nt(a, b):
    return lax.dot_general(a, b, (((1,), (1,)), ((), ())), preferred_element_type=F32)


def _dot_tn(a, b):
    return lax.dot_general(a, b, (((0,), (0,)), ((), ())), preferred_element_type=F32)


def _hgrn_sample_kernel(q_ref, v_ref, og_ref, z_ref, f_ref, lb_ref, g_ref, s0_ref, oa_ref, s_ref,
                        *, bb, t_len):
    rows = bb * t_len
    lb_all = _lower_bound(lb_ref)
    row = lax.broadcasted_iota(jnp.int32, (rows, A_DK), 0)
    pos = row & (t_len - 1)
    grp = row >> (t_len.bit_length() - 1)
    shifts = tuple(1 << i for i in range((t_len - 1).bit_length()))
    n_tail = A_DK - rows
    sel_row = lax.broadcasted_iota(jnp.int32, (n_tail, 2 * bb * A_DV), 0)
    sel_blk = lax.broadcasted_iota(jnp.int32, (n_tail, 2 * bb * A_DV), 1) >> (A_DV.bit_length() - 1)
    picks = jnp.logical_and(sel_row < 2 * bb, sel_blk == bb + (sel_row & (bb - 1)))
    rhs_tail = jnp.where(picks, 1.0, 0.0).astype(BF16)
    lhs_pad = jnp.zeros((n_tail - 2 * bb, A_DK), F32)
    v_pad = jnp.zeros((rows, bb * A_DV), F32)
    for h in range(A_HEADS):
        ls = slice(h * A_DK, (h + 1) * A_DK)
        lb = lb_all[:, ls]
        fg = _forget_gate(f_ref[:, ls], lb)
        logf = jnp.log(fg)
        k = 1.0 - fg
        b = _group_cumsum(logf, row, t_len, shifts)
        q = _silu(q_ref[:, ls].astype(F32))
        v = v_ref[:, ls].astype(F32)

        o = jnp.zeros((rows, A_DV), F32)
        for d in range(t_len):
            if d == 0:
                a = jnp.sum(q * k, axis=-1, keepdims=True)
                o = o + a * v
            else:
                w = jnp.exp(jnp.where(pos >= d, b - pltpu.roll(b, d, axis=0), 0.0))
                a = jnp.sum(q * pltpu.roll(k, d, axis=0) * w, axis=-1, keepdims=True)
                o = o + jnp.where(pos >= d, a * pltpu.roll(v, d, axis=0), 0.0)

        b_last = b
        for d in range(1, t_len):
            b_last = jnp.where(pos == t_len - 1 - d, pltpu.roll(b, rows - d, axis=0), b_last)
        e_last = jnp.exp(b_last)
        qs = q * jnp.exp(b)
        kd = k * jnp.exp(b_last - b)
        e_rows = jnp.concatenate([e_last[j * t_len:j * t_len + 1, :] for j in range(bb)], axis=0)
        e_hi = e_rows.astype(BF16).astype(F32)
        lhs_t = jnp.concatenate([kd, e_hi, e_rows - e_hi, lhs_pad], axis=0).T.astype(BF16)
        v_rows = jnp.concatenate([jnp.where(grp == j, v, 0.0) for j in range(bb)] + [v_pad], axis=1)
        res = jnp.dot(lhs_t, jnp.concatenate([v_rows.astype(BF16), rhs_tail], axis=0),
                      preferred_element_type=F32)
        for j in range(bb):
            s0 = s0_ref[j, h]
            o = o + jnp.dot(jnp.where(grp == j, qs, 0.0).astype(BF16), s0.astype(BF16),
                            preferred_element_type=F32)
            s_ref[j, h] = (res[:, (bb + j) * A_DV:(bb + j + 1) * A_DV] * s0
                           + res[:, j * A_DV:(j + 1) * A_DV])

        oa = _hgrn_finish(o, g_ref[:, ls], og_ref[:, ls].astype(F32), z_ref[:, ls].astype(F32))
        oa_ref[:, ls] = oa.astype(BF16)


N_PAIRS = B_HEADS // 2
PAIRS_PER_KV = N_PAIRS // B_KV_HEADS
N_KEYS = 2 * WINDOW


def _build_bias(bucket_ref, relb_ref, bias_ref, r):
    bucket = bucket_ref[...]
    for h in range(B_HEADS):
        def body(kb, acc, h=h):
            return jnp.where(bucket == kb, relb_ref[kb, h] * LOG2E, acc)
        tab = lax.fori_loop(0, REL_BUCKETS, body, jnp.full(bucket.shape, NEG, F32))
        pair, parity = divmod(h, 2)
        kv, pp = divmod(pair, PAIRS_PER_KV)
        bias_ref[kv, parity * N_KEYS:(parity + 1) * N_KEYS, pp * r:(pp + 1) * r] = tab


def _swa_keys(kk):
    lo = lax.broadcasted_iota(jnp.int32, (N_KEYS, 2 * B_HD), 1) < B_HD
    kk_sw = pltpu.roll(kk, B_HD, axis=1)
    slabs = []
    for kv in range(B_KV_HEADS):
        k_lo, k_hi = (kk, kk_sw) if kv == 0 else (kk_sw, kk)
        slabs.append(jnp.concatenate([jnp.where(lo, k_lo, 0.0), jnp.where(lo, 0.0, k_hi)], axis=0).astype(BF16))
    return slabs


def _swa_attend(logits, vals_t, col_masks, sink_ref, kv, r):
    width = PAIRS_PER_KV * r
    lane_pp = lax.broadcasted_iota(jnp.int32, (1, width), 1) >> (r.bit_length() - 1)
    halves = []
    for parity in range(2):
        sink = jnp.zeros((1, width), F32)
        for pp in range(PAIRS_PER_KV):
            sink = jnp.where(lane_pp == pp, sink_ref[(kv * PAIRS_PER_KV + pp) * 2 + parity] * LOG2E, sink)
        l = logits[parity * N_KEYS:(parity + 1) * N_KEYS, :]
        m = jnp.maximum(jnp.max(l, axis=0, keepdims=True), sink)
        p = jnp.exp2(l - m)
        denom = jnp.sum(p, axis=0, keepdims=True) + jnp.exp2(sink - m)
        p16 = p.astype(BF16)
        v_rows = jnp.concatenate([v_t[kv * B_HD:(kv + 1) * B_HD, :] for v_t in vals_t], axis=0)
        pv_all = jnp.dot(v_rows, p16, preferred_element_type=F32)
        pv = None
        for n, mask in enumerate(col_masks):
            part = pv_all[n * B_HD:(n + 1) * B_HD, :]
            pv = part if mask is None else jnp.where(mask, part, 0.0 if pv is None else pv)
        halves.append(pv / denom)
    return jnp.concatenate(halves, axis=0).T


def _swa_sample_kernel(bucket_ref, relb_ref, sink_ref, q_ref, z_ref, kvn_ref, ck_ref, cv_ref,
                       ob_ref, nk_ref, nv_ref, bias_ref, *, bb, t_len):
    rows = bb * t_len
    width = PAIRS_PER_KV * rows

    @pl.when(pl.program_id(0) == 0)
    def _():
        _build_bias(bucket_ref, relb_ref, bias_ref, rows)

    kw = B_KV_HEADS * B_HD
    t_shift = t_len.bit_length() - 1
    row = lax.broadcasted_iota(jnp.int32, (rows, kw), 0)
    seq_of_col = lambda n: (lax.broadcasted_iota(jnp.int32, (n, width), 1) & (rows - 1)) >> t_shift
    seq_l = seq_of_col(2 * N_KEYS)
    seq_v = seq_of_col(B_HD)
    q_pairs = [(q_ref[:, p * 128:(p + 1) * 128].astype(F32) * Q_SCALE).astype(BF16) for p in range(N_PAIRS)]
    q_kv = [jnp.concatenate(q_pairs[kv * PAIRS_PER_KV:(kv + 1) * PAIRS_PER_KV], axis=0)
            for kv in range(B_KV_HEADS)]
    kn = kvn_ref[:, 0:kw]
    vn = kvn_ref[:, kw:2 * kw]
    n_new = N_KEYS - WINDOW
    assert n_new == WINDOW
    pad = jnp.zeros((n_new - rows, kw), F32)
    kn_t = jnp.concatenate([kn, pad], axis=0).T
    vn_t = jnp.concatenate([vn, pad], axis=0).T
    lane = lax.broadcasted_iota(jnp.int32, (kw, n_new), 1)
    logits = [None] * B_KV_HEADS
    vals_t, col_masks = [], []
    for j in range(bb):
        sel = row < t_len
        kj = jnp.where(sel, kn if j == 0 else pltpu.roll(kn, rows - j * t_len, axis=0), 0.0)
        kj_t = jnp.where(lane < t_len, kn_t if j == 0 else pltpu.roll(kn_t, n_new - j * t_len, axis=1), 0.0)
        vj_t = jnp.where(lane < t_len, vn_t if j == 0 else pltpu.roll(vn_t, n_new - j * t_len, axis=1), 0.0)
        ck_t = ck_ref[j]
        cv_t = cv_ref[j]
        keys = _swa_keys(jnp.concatenate([ck_t.T, kj, pad], axis=0))
        for kv in range(B_KV_HEADS):
            lg = _dot_nt(keys[kv], q_kv[kv])
            logits[kv] = jnp.where(seq_l == j, lg, 0.0 if logits[kv] is None else logits[kv])
        vals_t.append(jnp.concatenate([cv_t, vj_t], axis=1).astype(BF16))
        col_masks.append(seq_v == j)
        keep = lane < WINDOW - t_len
        nk_ref[j] = jnp.where(keep, pltpu.roll(ck_t, WINDOW - t_len, axis=1),
                              pltpu.roll(kj_t, WINDOW - t_len, axis=1))
        nv_ref[j] = jnp.where(keep, pltpu.roll(cv_t, WINDOW - t_len, axis=1),
                              pltpu.roll(vj_t, WINDOW - t_len, axis=1))
    for kv in range(B_KV_HEADS):
        acc_t = _swa_attend(logits[kv] + bias_ref[kv], vals_t, col_masks, sink_ref, kv, rows)
        for pp in range(PAIRS_PER_KV):
            p = kv * PAIRS_PER_KV + pp
            z = z_ref[:, p * 128:(p + 1) * 128].astype(F32)
            ob_ref[:, p * 128:(p + 1) * 128] = (acc_t[pp * rows:(pp + 1) * rows, :] * _silu(z)).astype(BF16)


def _mix_sample_kernel(bucket_ref, relb_ref, sink_ref, aq_ref, ai_ref, aog_ref, az_ref, f_ref, lb_ref, g_ref,
                       s0_ref, bq_ref, bz_ref, kvn_ref, ck_ref, cv_ref,
                       oa_ref, s_ref, ob_ref, nk_ref, nv_ref, bias_ref, *, bb, t_len):
    _hgrn_sample_kernel(aq_ref, ai_ref, aog_ref, az_ref, f_ref, lb_ref, g_ref, s0_ref, oa_ref, s_ref,
                        bb=bb, t_len=t_len)
    _swa_sample_kernel(bucket_ref, relb_ref, sink_ref, bq_ref, bz_ref, kvn_ref, ck_ref, cv_ref,
                       ob_ref, nk_ref, nv_ref, bias_ref, bb=bb, t_len=t_len)


def _mix_sample(bucket, rel_bias, sink, pm, f, kvn, hgrn_lb, hgrn_norm, s0, cache_k, cache_v,
                batch, t_len, bb):
    rows = bb * t_len
    kvw = kvn.shape[1]
    kw = kvw // 2
    n_lb = hgrn_lb.shape[0]
    smem = pl.BlockSpec(memory_space=pltpu.SMEM)
    col = lambda grp: pl.BlockSpec((rows, D_MODEL), lambda i, grp=grp: (i, grp))
    tok = pl.BlockSpec((rows, D_MODEL), lambda i: (i, 0))
    st_spec = pl.BlockSpec((bb, A_HEADS, A_DK, A_DV), lambda i: (i, 0, 0, 0))
    cache_spec = pl.BlockSpec((bb, kw, WINDOW), lambda i: (i, 0, 0))
    return pl.pallas_call(
        functools.partial(_mix_sample_kernel, bb=bb, t_len=t_len),
        grid=(batch // bb,),
        in_specs=[
            _resident((N_KEYS, rows)),
            smem, smem,
            col(G_AQ), col(G_AI), col(G_AOG), col(G_AZ),
            tok,
            _resident((n_lb, D_MODEL)),
            _resident((1, D_MODEL)),
            st_spec,
            col(G_BQ), col(G_BZ),
            pl.BlockSpec((rows, kvw), lambda i: (i, 0)),
            cache_spec, cache_spec,
        ],
        out_specs=[tok, st_spec, tok, cache_spec, cache_spec],
        out_shape=[
            jax.ShapeDtypeStruct((batch * t_len, D_MODEL), BF16),
            jax.ShapeDtypeStruct((batch, A_HEADS, A_DK, A_DV), F32),
            jax.ShapeDtypeStruct((batch * t_len, D_MODEL), BF16),
            jax.ShapeDtypeStruct((batch, kw, WINDOW), F32),
            jax.ShapeDtypeStruct((batch, kw, WINDOW), F32),
        ],
        scratch_shapes=[pltpu.VMEM((B_KV_HEADS, 2 * N_KEYS, PAIRS_PER_KV * rows), F32)],
        compiler_params=pltpu.CompilerParams(
            dimension_semantics=("arbitrary",), vmem_limit_bytes=VMEM_LIMIT),
        name="mix_sample",
    )(bucket, rel_bias, sink, pm, pm, pm, pm, f, hgrn_lb, hgrn_norm, s0, pm, pm, kvn, cache_k, cache_v)


MIX_TQ = 512
HGRN_SLAB = 256


class _HgrnConsts:
    def __init__(self, slab):
        shift = A_CHUNK.bit_length() - 1
        self.row = lax.broadcasted_iota(jnp.int32, (slab, A_DK), 0)
        rs = lax.broadcasted_iota(jnp.int32, (slab, slab), 0)
        cs = lax.broadcasted_iota(jnp.int32, (slab, slab), 1)
        self.same_chunk_causal = jnp.logical_and((rs >> shift) == (cs >> shift), cs <= rs)
        self.zero_row = jnp.zeros((1, A_DK), F32)
        self.zero_blk = jnp.zeros((A_SUB, A_DK), BF16)
        self.zero_chunk = jnp.zeros((A_CHUNK, A_DK), BF16)


def _hgrn_slab(f_pre, q_pre, v, lb, st, c):
    slab = f_pre.shape[0]
    n_sub = A_CHUNK // A_SUB
    sc = slab // A_CHUNK
    rep = lambda r, n: jnp.broadcast_to(r, (n, A_DK))
    fg = _forget_gate(f_pre, lb)
    logf = jnp.log(fg)
    k = 1.0 - fg
    b = _group_cumsum(logf, c.row, A_CHUNK, (1, 2, 4, 8, 16, 32))
    q = _silu(q_pre)

    ends = [b[A_SUB * j + A_SUB - 1:A_SUB * (j + 1), :] for j in range(sc * n_sub)]
    endrow = jnp.concatenate([rep(e, A_SUB) for e in ends], axis=0)
    prevrow = jnp.concatenate(
        [rep(c.zero_row if j % n_sub == 0 else ends[j - 1], A_SUB) for j in range(sc * n_sub)], axis=0)
    tot = [ends[n_sub * ch + n_sub - 1] for ch in range(sc)]
    lastrow = jnp.concatenate([rep(t, A_CHUNK) for t in tot], axis=0)
    kend = k * jnp.exp(endrow - b)
    qd = q * jnp.exp(b - prevrow)
    qs = qd * jnp.exp(prevrow)
    kd = kend * jnp.exp(lastrow - endrow)
    kdiag16 = (k * jnp.exp(prevrow - b)).astype(BF16)
    qd16 = qd.astype(BF16)
    kend16 = kend.astype(BF16)
    qs16 = qs.astype(BF16)
    kd16 = kd.astype(BF16)

    blk = lambda a, j: a[A_SUB * j:A_SUB * (j + 1), :]
    chk = lambda a, ch: a[A_CHUNK * ch:A_CHUNK * (ch + 1), :]

    q_groups, k_groups = [], []
    for i in range(n_sub):
        q_groups.append(jnp.concatenate(
            [blk(qd16, j) if j % n_sub == i else c.zero_blk for j in range(sc * n_sub)], axis=0))
        pieces = []
        for j in range(sc * n_sub):
            ch, jj = divmod(j, n_sub)
            if jj == i:
                pieces.append(blk(kdiag16, j))
            elif jj == i - 1:
                pieces.append(blk(kend16, j))
            elif jj < i:
                pieces.append((blk(kend, j) * jnp.exp(ends[ch * n_sub + i - 1] - ends[j])).astype(BF16))
            else:
                pieces.append(c.zero_blk)
        k_groups.append(jnp.concatenate(pieces, axis=0))
    att = _dot_nt(jnp.concatenate(q_groups, axis=1), jnp.concatenate(k_groups, axis=1))
    att = jnp.where(c.same_chunk_causal, att, 0.0)

    if sc > 1:
        q_groups, k_groups = [], []
        for ch in range(1, sc):
            q_groups.append(jnp.concatenate(
                [chk(qs16, c2) if c2 == ch else c.zero_chunk for c2 in range(sc)], axis=0))
            pieces = []
            for c2 in range(sc):
                if c2 == ch - 1:
                    pieces.append(chk(kd16, c2))
                elif c2 < ch:
                    pieces.append((chk(kd, c2) * jnp.exp(sum(tot[c2 + 1:ch]))).astype(BF16))
                else:
                    pieces.append(c.zero_chunk)
            k_groups.append(jnp.concatenate(pieces, axis=0))
        att = att + _dot_nt(jnp.concatenate(q_groups, axis=1), jnp.concatenate(k_groups, axis=1))
    o = jnp.dot(att.astype(BF16), v, preferred_element_type=F32)

    q0 = jnp.concatenate(
        [chk(qs16, 0)] + [(chk(qs, ch) * jnp.exp(sum(tot[:ch]))).astype(BF16) for ch in range(1, sc)], axis=0)
    o = o + _dot_nt(q0, st.astype(BF16))

    k1 = jnp.concatenate(
        [(chk(kd, ch) * jnp.exp(sum(tot[ch + 1:]))).astype(BF16) for ch in range(sc - 1)]
        + [chk(kd16, sc - 1)], axis=0)
    st = st * jnp.exp(sum(tot)) + _dot_tn(v, k1)
    return o, st


def _hgrn_blockwise(stage_ref, oraw_ref, ls, lb, st):
    n_blk = stage_ref.shape[1] // A_SUB
    row = lax.broadcasted_iota(jnp.int32, (A_SUB, A_DK), 0)
    shifts = tuple(1 << i for i in range((A_SUB - 1).bit_length()))

    def body(i, st):
        rows = pl.ds(pl.multiple_of(i * A_SUB, A_SUB), A_SUB)
        fg = _forget_gate(stage_ref[1, rows, ls], lb)
        k = 1.0 - fg
        b = _group_cumsum(jnp.log(fg), row, A_SUB, shifts)
        q = _silu(stage_ref[0, rows, ls])
        v = stage_ref[2, rows, ls]
        o = _dot_nt((q * jnp.exp(b)).astype(BF16), st.astype(BF16))
        o = o + jnp.sum(q * k, axis=-1, keepdims=True) * v
        for d in range(1, A_SUB):
            ok = row >= d
            w = jnp.exp(jnp.where(ok, b - pltpu.roll(b, d, axis=0), 0.0))
            a = jnp.sum(q * pltpu.roll(k, d, axis=0) * w, axis=-1, keepdims=True)
            o = o + jnp.where(ok, a * pltpu.roll(v, d, axis=0), 0.0)
        oraw_ref[rows, ls] = o
        b_end = b[A_SUB - 1:A_SUB, :]
        kd = (k * jnp.exp(b_end - b)).astype(BF16)
        return st * jnp.exp(b_end) + _dot_tn(v.astype(BF16), kd)

    return lax.fori_loop(0, n_blk, body, st)


LB_SAFE = math.exp(-80.0 / A_SUB)


def _mix_prompt_kernel(bucket_ref, relb_ref, sink_ref, x_ref, gpre_ref, w_ref, lb_ref, gh_ref,
                       oa_ref, ob_ref, gab_ref, st_ref, kvwin_ref,
                       bias_ref, st_scr, kvprev_scr, stage_scr, oraw_scr):
    tile = pl.program_id(1)
    at_start = tile == 0

    @pl.when(jnp.logical_and(pl.program_id(0) == 0, at_start))
    def _():
        _build_bias(bucket_ref, relb_ref, bias_ref, WINDOW)

    @pl.when(at_start)
    def _():
        st_scr[...] = jnp.zeros(st_scr.shape, F32)
        kvprev_scr[...] = jnp.zeros(kvprev_scr.shape, F32)

    x = x_ref[...]
    ms = jnp.mean(x * x, axis=-1, keepdims=True)
    u = (x * lax.rsqrt(ms + EPS) * gpre_ref[...]).astype(BF16)
    proj = lambda seg, lo, hi: jnp.dot(u, w_ref[:, IN_OFFS[seg] + lo:IN_OFFS[seg] + hi],
                                       preferred_element_type=F32)

    lb_all = _lower_bound(lb_ref)
    consts = _HgrnConsts(HGRN_SLAB)
    pair_w = 2 * A_DK
    n_phase = A_HEADS // 2
    n_blocks = MIX_TQ // WINDOW
    assert n_blocks == n_phase
    gate_w = D_MODEL // n_phase
    kw = B_KV_HEADS * B_HD

    def hgrn_proj(hp):
        lo, hi = hp * pair_w, (hp + 1) * pair_w
        return tuple(proj(seg, lo, hi) for seg in (0, SEG_AF, 2, 3, 4))

    def hgrn_pair(hp, q2, f2, v2, og2, z2):
        for hh in range(2):
            h = 2 * hp + hh
            ls = slice(hh * A_DK, (hh + 1) * A_DK)
            hs = slice(h * A_DK, (h + 1) * A_DK)
            st = st_scr[h]
            outs = []
            for s0 in range(0, MIX_TQ, HGRN_SLAB):
                rs = slice(s0, s0 + HGRN_SLAB)
                o, st = _hgrn_slab(f2[rs, ls], q2[rs, ls], v2[rs, ls].astype(BF16), lb_all[:, hs], st, consts)
                outs.append(o)
            st_scr[h] = st
            st_ref[h] = st.T
            oa = _hgrn_finish(jnp.concatenate(outs, axis=0), gh_ref[:, hs], og2[:, ls], z2[:, ls])
            oa_ref[:, hs] = oa.astype(BF16)

    key = lax.broadcasted_iota(jnp.int32, (2 * N_KEYS, PAIRS_PER_KV * WINDOW), 0) & (N_KEYS - 1)
    no_prev = jnp.logical_and(key < WINDOW, at_start)

    def swa_block(j, kv, bq, bz):
        r0 = j * WINDOW
        if j == 0:
            kvj = jnp.concatenate([kvprev_scr[...], kv[0:WINDOW, :]], axis=0)
        else:
            kvj = kv[r0 - WINDOW:r0 + WINDOW, :]
        keys = _swa_keys(kvj[:, 0:kw])
        vals_t = [kvj[:, kw:2 * kw].T.astype(BF16)]
        for kvh in range(B_KV_HEADS):
            q = jnp.concatenate([bq[r0:r0 + WINDOW, p * 128:(p + 1) * 128]
                                 for p in range(kvh * PAIRS_PER_KV, (kvh + 1) * PAIRS_PER_KV)], axis=0)
            logits = _dot_nt(keys[kvh], q) + bias_ref[kvh]
            if j == 0:
                logits = jnp.where(no_prev, NEG, logits)
            acc_t = _swa_attend(logits, vals_t, [None], sink_ref, kvh, WINDOW)
            for pp in range(PAIRS_PER_KV):
                cs = slice((kvh * PAIRS_PER_KV + pp) * 128, (kvh * PAIRS_PER_KV + pp + 1) * 128)
                ob_ref[r0:r0 + WINDOW, cs] = (
                    acc_t[pp * WINDOW:(pp + 1) * WINDOW, :] * _silu(bz[r0:r0 + WINDOW, cs])).astype(BF16)

    def hgrn_pair_blockwise(hp, q2, f2, v2, og2, z2):
        stage_scr[0] = q2
        stage_scr[1] = f2
        stage_scr[2] = v2
        for hh in range(2):
            h = 2 * hp + hh
            ls = slice(hh * A_DK, (hh + 1) * A_DK)
            hs = slice(h * A_DK, (h + 1) * A_DK)
            st = _hgrn_blockwise(stage_scr, oraw_scr, ls, lb_all[:, hs], st_scr[h])
            st_scr[h] = st
            st_ref[h] = st.T
            oa = _hgrn_finish(oraw_scr[:, ls], gh_ref[:, hs], og2[:, ls], z2[:, ls])
            oa_ref[:, hs] = oa.astype(BF16)

    def gate_proj(ph):
        for seg, base in ((9, 0), (10, D_MODEL)):
            gab_ref[:, base + ph * gate_w:base + (ph + 1) * gate_w] = proj(
                seg, ph * gate_w, (ph + 1) * gate_w).astype(BF16)

    def swa_proj():
        kv = proj(SEG_BK, 0, 2 * kw)
        bq = (proj(5, 0, D_MODEL) * Q_SCALE).astype(BF16)
        bz = proj(8, 0, D_MODEL)
        return kv, bq, bz

    def keep_window(kv):
        kvprev_scr[...] = kv[MIX_TQ - WINDOW:, :]
        kvwin_ref[...] = kv[MIX_TQ - WINDOW:, :]

    slab_safe = jnp.min(lb_all) >= LB_SAFE

    @pl.when(slab_safe)
    def _():
        kv, bq, bz = swa_proj()
        nxt = hgrn_proj(0)
        for ph in range(n_phase):
            cur = nxt
            if ph + 1 < n_phase:
                nxt = hgrn_proj(ph + 1)
            gate_proj(ph)
            hgrn_pair(ph, *cur)
            swa_block(ph, kv, bq, bz)
        keep_window(kv)

    @pl.when(jnp.logical_not(slab_safe))
    def _():
        for ph in range(n_phase):
            hgrn_pair_blockwise(ph, *hgrn_proj(ph))
            gate_proj(ph)
        kv, bq, bz = swa_proj()
        for j in range(n_blocks):
            swa_block(j, kv, bq, bz)
        keep_window(kv)


def _mix_prompt(bucket, rel_bias, sink, x, g_pre, w, hgrn_lb, hgrn_norm, batch, seq):
    nt = seq // MIX_TQ
    n_lb = hgrn_lb.shape[0]
    kvw = 2 * B_KV_HEADS * B_HD
    smem = pl.BlockSpec(memory_space=pltpu.SMEM)
    tok = lambda width: pl.BlockSpec((MIX_TQ, width), lambda b, t: (b * nt + t, 0))
    return pl.pallas_call(
        _mix_prompt_kernel,
        grid=(batch, nt),
        in_specs=[
            _resident((N_KEYS, WINDOW)),
            smem, smem,
            tok(D_MODEL),
            _resident((1, D_MODEL)),
            _resident(w.shape),
            _resident((n_lb, D_MODEL)),
            _resident((1, D_MODEL)),
        ],
        out_specs=[
            tok(D_MODEL), tok(D_MODEL), tok(2 * D_MODEL),
            pl.BlockSpec((None, A_HEADS, A_DK, A_DV), lambda b, t: (b, 0, 0, 0)),
            pl.BlockSpec((None, WINDOW, kvw), lambda b, t: (b, 0, 0)),
        ],
        out_shape=[
            jax.ShapeDtypeStruct((batch * seq, D_MODEL), BF16),
            jax.ShapeDtypeStruct((batch * seq, D_MODEL), BF16),
            jax.ShapeDtypeStruct((batch * seq, 2 * D_MODEL), BF16),
            jax.ShapeDtypeStruct((batch, A_HEADS, A_DK, A_DV), F32),
            jax.ShapeDtypeStruct((batch, WINDOW, kvw), F32),
        ],
        scratch_shapes=[
            pltpu.VMEM((B_KV_HEADS, 2 * N_KEYS, PAIRS_PER_KV * WINDOW), F32),
            pltpu.VMEM((A_HEADS, A_DV, A_DK), F32),
            pltpu.VMEM((WINDOW, kvw), F32),
            pltpu.VMEM((3, MIX_TQ, 2 * A_DK), F32),
            pltpu.VMEM((MIX_TQ, 2 * A_DK), F32),
        ],
        compiler_params=pltpu.CompilerParams(
            dimension_semantics=("arbitrary", "arbitrary"), vmem_limit_bytes=VMEM_LIMIT),
        name="mix_prompt",
    )(bucket, rel_bias, sink, x, g_pre, w, hgrn_lb, hgrn_norm)


def _outproj_kernel(oa_ref, ob_ref, ga_ref, gb_ref, x_ref, p_ref, wpa_ref, wpb_ref, wo_ref, gpost_ref,
                    wple_ref, wg_ref, y_ref):
    a = jnp.dot(oa_ref[...], wpa_ref[...], preferred_element_type=F32)
    b = jnp.dot(ob_ref[...], wpb_ref[...], preferred_element_type=F32)
    m = _sigmoid(ga_ref[...].astype(F32)) * a + _sigmoid(gb_ref[...].astype(F32)) * b
    y = jnp.dot(m.astype(BF16), wo_ref[...], preferred_element_type=F32)
    y = y * lax.rsqrt(jnp.mean(y * y, axis=-1, keepdims=True) + EPS) * gpost_ref[...]
    x1 = x_ref[...] + y
    gate = _sigmoid(jnp.dot(x1.astype(BF16), wg_ref[...], preferred_element_type=F32))
    e = jnp.dot(p_ref[...].astype(BF16), wple_ref[...], preferred_element_type=F32) * gate
    y_ref[...] = x1 + e


def _outproj(oa, ob, gates, ga_col, gb_col, x, p, wpa, wpb, wo, gpost, wple, wg, tm):
    n = x.shape[0]
    ple = p.shape[1]
    tok = lambda w, c=0: pl.BlockSpec((tm, w), lambda i, c=c: (i, c))
    return pl.pallas_call(
        _outproj_kernel,
        grid=(n // tm,),
        in_specs=[
            tok(D_MODEL), tok(D_MODEL), tok(D_MODEL, ga_col), tok(D_MODEL, gb_col), tok(D_MODEL), tok(ple),
            _resident((D_MODEL, D_MODEL)), _resident((D_MODEL, D_MODEL)), _resident((D_MODEL, D_MODEL)),
            _resident((1, D_MODEL)), _resident((ple, D_MODEL)), _resident((D_MODEL, D_MODEL)),
        ],
        out_specs=tok(D_MODEL),
        out_shape=jax.ShapeDtypeStruct((n, D_MODEL), F32),
        compiler_params=pltpu.CompilerParams(
            dimension_semantics=("arbitrary",), vmem_limit_bytes=VMEM_LIMIT),
        name="outproj",
    )(oa, ob, gates, gates, x, p, wpa, wpb, wo, gpost, wple, wg)


def _rel_bucket(rel):
    n = np.maximum(rel, 0)
    max_exact = REL_BUCKETS // 2
    nf = np.maximum(n, 1).astype(np.float32)
    scaled = (np.log(nf / np.float32(max_exact)) / np.float32(math.log(REL_MAX_DIST / max_exact))
              * np.float32(REL_BUCKETS - max_exact))
    frac = scaled - np.floor(scaled)
    inside = (n > max_exact) & (n < REL_MAX_DIST)
    assert np.all((frac[inside] > 1e-3) & (frac[inside] < 1.0 - 1e-3))
    large = np.minimum(max_exact + scaled.astype(np.int32), REL_BUCKETS - 1)
    return np.where(n < max_exact, n, large)


def _bucket_table(q_pos, k_pos, k_valid):
    rel = q_pos[:, None] - k_pos[None, :]
    ok = (rel >= 0) & (rel < WINDOW) & k_valid[None, :]
    return jnp.asarray(np.where(ok, _rel_bucket(rel), -1).astype(np.int32).T)


TM_PROJ = 512
SAMPLE_BB = 8


def _layer(xp, xs, s_hgrn, win_k, win_v, pp, ps, norm_pre, w_in, hgrn_lb, hgrn_norm, attn_sink,
           rel_bias, w_pa, w_pb, w_o, norm_post, w_ple, w_ple_gate):
    batch, seq, _ = xp.shape
    dbatch, t_len, _ = xs.shape
    kw = B_KV_HEADS * B_HD

    w16 = w_in.astype(BF16)
    g_pre = norm_pre.reshape(1, D_MODEL)
    g_post = norm_post.reshape(1, D_MODEL)
    g_hgrn = hgrn_norm.reshape(1, D_MODEL)
    wpa, wpb, wo = w_pa.astype(BF16), w_pb.astype(BF16), w_o.astype(BF16)
    wple, wg = w_ple.astype(BF16), w_ple_gate.astype(BF16)

    xp2 = xp.reshape(batch * seq, D_MODEL)
    xs2 = xs.reshape(dbatch * t_len, D_MODEL)
    k_all = np.arange(N_KEYS)
    bucket_p = _bucket_table(np.arange(WINDOW) + WINDOW, k_all, np.ones((N_KEYS,), bool))
    rows_s = SAMPLE_BB * t_len
    bucket_s = _bucket_table(WINDOW + np.arange(rows_s) % t_len, k_all, k_all < WINDOW + t_len)

    oa_p, ob_p, gab_p, st_p, kvwin_p = _mix_prompt(bucket_p, rel_bias, attn_sink, xp2, g_pre, w16,
                                                   hgrn_lb, g_hgrn, batch, seq)

    to_minor = lambda c: jnp.transpose(c, (0, 2, 3, 1)).reshape(dbatch, kw, WINDOW)
    from_minor = lambda c: jnp.transpose(c.reshape(dbatch, B_KV_HEADS, B_HD, WINDOW), (0, 3, 1, 2))
    pm_s, f_s, kv_s = _inproj(xs2, g_pre, w16, min(TM_PROJ, dbatch * t_len))
    oa_s, st_s, ob_s, nk_s, nv_s = _mix_sample(bucket_s, rel_bias, attn_sink, pm_s, f_s, kv_s, hgrn_lb, g_hgrn,
                                               s_hgrn, to_minor(win_k), to_minor(win_v),
                                               dbatch, t_len, SAMPLE_BB)

    y_p = _outproj(oa_p, ob_p, gab_p, 0, 1, xp2, pp.reshape(batch * seq, -1),
                   wpa, wpb, wo, g_post, wple, wg, TM_PROJ)
    y_s = _outproj(oa_s, ob_s, pm_s, G_GA, G_GB, xs2, ps.reshape(dbatch * t_len, -1),
                   wpa, wpb, wo, g_post, wple, wg, min(TM_PROJ, dbatch * t_len))

    k_win_p = kvwin_p[:, :, 0:kw].reshape(batch, WINDOW, B_KV_HEADS, B_HD)
    v_win_p = kvwin_p[:, :, kw:].reshape(batch, WINDOW, B_KV_HEADS, B_HD)
    return (y_p.reshape(batch, seq, D_MODEL), y_s.reshape(dbatch, t_len, D_MODEL), st_p, st_s,
            k_win_p, v_win_p,
            from_minor(nk_s), from_minor(nv_s))


def kernel(x_prompt, x_sample, state_hgrn, cache_swa_k, cache_swa_v, p_prompt, p_sample, norm_pre, w_in,
           hgrn_lb, hgrn_norm, attn_sink, rel_bias, w_pa, w_pb, w_o, norm_post, w_ple, w_ple_gate):
    depth = w_in.shape[0]
    assert depth == 1, "the forget-gate lower bound is implemented for a single layer"
    xp, xs = x_prompt, x_sample
    outs = []
    for l in range(depth):
        res = _layer(xp, xs, state_hgrn[l], cache_swa_k[l], cache_swa_v[l], p_prompt[l], p_sample[l],
                     norm_pre[l], w_in[l], hgrn_lb, hgrn_norm[l], attn_sink[l], rel_bias,
                     w_pa[l], w_pb[l], w_o[l], norm_post[l], w_ple[l], w_ple_gate[l])
        xp, xs = res[0], res[1]
        outs.append(res[2:])
    stack = lambda i: jnp.stack([o[i] for o in outs])
    return (xp, xs, stack(0), stack(1), stack(2), stack(3), stack(4), stack(5))
```

```python
import functools
import math

import jax
import jax.numpy as jnp
import numpy as np
from jax import lax
from jax.experimental import pallas as pl
from jax.experimental.pallas import tpu as pltpu

F32 = jnp.float32
BF16 = jnp.bfloat16

D_MODEL = 1024
A_HEADS = 8
A_DK = 128
A_DV = 128
A_CHUNK = 64
A_SUB = 16
B_HEADS = 16
B_KV_HEADS = 2
B_HD = 64
WINDOW = 128
REL_BUCKETS = 32
REL_MAX_DIST = 128
EPS = 1e-6
NEG = float("-inf")
LOG2E = math.log2(math.e)
Q_SCALE = B_HD ** -0.5 * LOG2E

G_AQ, G_AI, G_AOG, G_AZ, G_BQ, G_BZ, G_GA, G_GB = range(8)
N_GROUPS = 8

VMEM_LIMIT = 56 * 1024 * 1024


def _sigmoid(x):
    return 0.5 * jnp.tanh(0.5 * x) + 0.5


def _silu(x):
    h = 0.5 * x
    return h * jnp.tanh(h) + h


def _resident(shape):
    nd = len(shape)
    return pl.BlockSpec(shape, lambda *_: (0,) * nd, pipeline_mode=pl.Buffered(1))


IN_OFFS = (0, 1024, 2048, 3072, 4096, 5120, 6144, 6272, 6400, 7424, 8448, 9472)
SLAB_SEGS = (0, 2, 3, 4, 5, 8, 9, 10)
SEG_AF, SEG_BK, SEG_BV = 1, 6, 7


W_CHUNK = IN_OFFS[SEG_BV + 1] - IN_OFFS[SEG_BK]
assert all(off % W_CHUNK == 0 for s, off in enumerate(IN_OFFS) if s != SEG_BV)


def _inproj_kernel(x_ref, g_ref, w_hbm, pm_ref, f_ref, kv_ref, w16_hbm, wbuf, w16_scr, in_sem, out_sem):
    n_chunks = IN_OFFS[-1] // W_CHUNK

    def fetch(c):
        return pltpu.make_async_copy(w_hbm.at[:, pl.ds(c * W_CHUNK, W_CHUNK)], wbuf.at[c % 2], in_sem.at[c % 2])

    def emit(c):
        cols = pl.ds(c * W_CHUNK, W_CHUNK)
        return pltpu.make_async_copy(w16_scr.at[:, cols], w16_hbm.at[:, cols], out_sem.at[c])

    fetch(0).start()
    x = x_ref[...]
    ms = jnp.mean(x * x, axis=-1, keepdims=True)
    u = (x * lax.rsqrt(ms + EPS) * g_ref[...]).astype(BF16)
    for c in range(n_chunks):
        if c + 1 < n_chunks:
            fetch(c + 1).start()
        fetch(c).wait()
        col = c * W_CHUNK
        w16 = wbuf[c % 2].astype(BF16)
        w16_scr[:, col:col + W_CHUNK] = w16
        emit(c).start()
        y = jnp.dot(u, w16, preferred_element_type=F32)
        seg = max(s for s in range(len(IN_OFFS) - 1) if IN_OFFS[s] <= col)
        off = col - IN_OFFS[seg]
        if seg == SEG_AF:
            f_ref[:, off:off + W_CHUNK] = y
        elif seg == SEG_BK:
            kv_ref[...] = y
        else:
            lo = SLAB_SEGS.index(seg) * D_MODEL + off
            pm_ref[:, lo:lo + W_CHUNK] = y.astype(BF16)
    for c in range(n_chunks):
        emit(c).wait()


def _inproj(x, g, w):
    n = x.shape[0]
    nm = N_GROUPS * D_MODEL
    n_chunks = IN_OFFS[-1] // W_CHUNK
    assert w.shape == (D_MODEL, IN_OFFS[-1]) and w.dtype == F32
    return pl.pallas_call(
        _inproj_kernel,
        grid=(1,),
        in_specs=[
            pl.BlockSpec((n, D_MODEL), lambda i: (0, 0)),
            _resident((1, D_MODEL)),
            pl.BlockSpec(memory_space=pl.ANY),
        ],
        out_specs=[
            pl.BlockSpec((n, nm), lambda i: (0, 0)),
            pl.BlockSpec((n, D_MODEL), lambda i: (0, 0)),
            pl.BlockSpec((n, W_CHUNK), lambda i: (0, 0)),
            pl.BlockSpec(memory_space=pl.ANY),
        ],
        out_shape=[
            jax.ShapeDtypeStruct((n, nm), BF16),
            jax.ShapeDtypeStruct((n, D_MODEL), F32),
            jax.ShapeDtypeStruct((n, W_CHUNK), F32),
            jax.ShapeDtypeStruct(w.shape, BF16),
        ],
        scratch_shapes=[
            pltpu.VMEM((2, D_MODEL, W_CHUNK), F32),
            pltpu.VMEM(w.shape, BF16),
            pltpu.SemaphoreType.DMA((2,)),
            pltpu.SemaphoreType.DMA((n_chunks,)),
        ],
        compiler_params=pltpu.CompilerParams(
            dimension_semantics=("arbitrary",), vmem_limit_bytes=VMEM_LIMIT),
        name="inproj",
    )(x, g, w)


def _lower_bound(lb_ref):
    l = lb_ref[...]
    m = jnp.max(l, axis=0, keepdims=True)
    e = jnp.exp(l - m)
    return e[0:1, :] / jnp.sum(e, axis=0, keepdims=True)


def _forget_gate(f_pre, lb):
    return lb + (1.0 - lb) / (1.0 + jnp.exp(-f_pre))


def _group_cumsum(x, row, period, shifts):
    pos = row & (period - 1)
    for sh in shifts:
        x = x + jnp.where(pos >= sh, pltpu.roll(x, sh, axis=0), 0.0)
    return x


def _hgrn_finish(o, g, og_pre, z_pre):
    o = o * lax.rsqrt(jnp.mean(o * o, axis=-1, keepdims=True) + EPS)
    return o * g * _sigmoid(og_pre) * _silu(z_pre)


def _dot_nt(a, b):
    return lax.dot_general(a, b, (((1,), (1,)), ((), ())), preferred_element_type=F32)


def _dot_tn(a, b):
    return lax.dot_general(a, b, (((0,), (0,)), ((), ())), preferred_element_type=F32)


def _hgrn_sample_kernel(q_ref, v_ref, og_ref, z_ref, f_ref, lb_ref, g_ref, s0_ref, oa_ref, s_ref,
                        *, bb, t_len):
    rows = bb * t_len
    lb_all = _lower_bound(lb_ref)
    row = lax.broadcasted_iota(jnp.int32, (rows, A_DK), 0)
    pos = row & (t_len - 1)
    grp = row >> (t_len.bit_length() - 1)
    shifts = tuple(1 << i for i in range((t_len - 1).bit_length()))
    n_tail = A_DK - rows
    sel_row = lax.broadcasted_iota(jnp.int32, (n_tail, 2 * bb * A_DV), 0)
    sel_blk = lax.broadcasted_iota(jnp.int32, (n_tail, 2 * bb * A_DV), 1) >> (A_DV.bit_length() - 1)
    picks = jnp.logical_and(sel_row < 2 * bb, sel_blk == bb + (sel_row & (bb - 1)))
    rhs_tail = jnp.where(picks, 1.0, 0.0).astype(BF16)
    lhs_pad = jnp.zeros((n_tail - 2 * bb, A_DK), F32)
    v_pad = jnp.zeros((rows, bb * A_DV), F32)
    for h in range(A_HEADS):
        ls = slice(h * A_DK, (h + 1) * A_DK)
        lb = lb_all[:, ls]
        fg = _forget_gate(f_ref[:, ls], lb)
        logf = jnp.log(fg)
        k = 1.0 - fg
        b = _group_cumsum(logf, row, t_len, shifts)
        q = _silu(q_ref[:, ls].astype(F32))
        v = v_ref[:, ls].astype(F32)

        o = jnp.zeros((rows, A_DV), F32)
        for d in range(t_len):
            if d == 0:
                a = jnp.sum(q * k, axis=-1, keepdims=True)
                o = o + a * v
            else:
                w = jnp.exp(jnp.where(pos >= d, b - pltpu.roll(b, d, axis=0), 0.0))
                a = jnp.sum(q * pltpu.roll(k, d, axis=0) * w, axis=-1, keepdims=True)
                o = o + jnp.where(pos >= d, a * pltpu.roll(v, d, axis=0), 0.0)

        b_last = b
        for d in range(1, t_len):
            b_last = jnp.where(pos == t_len - 1 - d, pltpu.roll(b, rows - d, axis=0), b_last)
        e_last = jnp.exp(b_last)
        qs = q * jnp.exp(b)
        kd = k * jnp.exp(b_last - b)
        e_rows = jnp.concatenate([e_last[j * t_len:j * t_len + 1, :] for j in range(bb)], axis=0)
        e_hi = e_rows.astype(BF16).astype(F32)
        lhs_t = jnp.concatenate([kd, e_hi, e_rows - e_hi, lhs_pad], axis=0).T.astype(BF16)
        v_rows = jnp.concatenate([jnp.where(grp == j, v, 0.0) for j in range(bb)] + [v_pad], axis=1)
        res = jnp.dot(lhs_t, jnp.concatenate([v_rows.astype(BF16), rhs_tail], axis=0),
                      preferred_element_type=F32)
        for j in range(bb):
            s0 = s0_ref[j, h]
            o = o + jnp.dot(jnp.where(grp == j, qs, 0.0).astype(BF16), s0.astype(BF16),
                            preferred_element_type=F32)
            s_ref[j, h] = (res[:, (bb + j) * A_DV:(bb + j + 1) * A_DV] * s0
                           + res[:, j * A_DV:(j + 1) * A_DV])

        oa = _hgrn_finish(o, g_ref[:, ls], og_ref[:, ls].astype(F32), z_ref[:, ls].astype(F32))
        oa_ref[:, ls] = oa.astype(BF16)


N_PAIRS = B_HEADS // 2
PAIRS_PER_KV = N_PAIRS // B_KV_HEADS
N_KEYS = 2 * WINDOW


def _build_bias(bucket_ref, relb_ref, bias_ref, r):
    bucket = bucket_ref[...]
    for h in range(B_HEADS):
        def body(kb, acc, h=h):
            return jnp.where(bucket == kb, relb_ref[kb, h] * LOG2E, acc)
        tab = lax.fori_loop(0, REL_BUCKETS, body, jnp.full(bucket.shape, NEG, F32))
        pair, parity = divmod(h, 2)
        kv, pp = divmod(pair, PAIRS_PER_KV)
        bias_ref[kv, parity * N_KEYS:(parity + 1) * N_KEYS, pp * r:(pp + 1) * r] = tab


def _swa_keys(kk):
    lo = lax.broadcasted_iota(jnp.int32, (N_KEYS, 2 * B_HD), 1) < B_HD
    kk_sw = pltpu.roll(kk, B_HD, axis=1)
    slabs = []
    for kv in range(B_KV_HEADS):
        k_lo, k_hi = (kk, kk_sw) if kv == 0 else (kk_sw, kk)
        slabs.append(jnp.concatenate([jnp.where(lo, k_lo, 0.0), jnp.where(lo, 0.0, k_hi)], axis=0).astype(BF16))
    return slabs


def _swa_attend(logits, vals_t, col_masks, sink_ref, kv, r):
    width = PAIRS_PER_KV * r
    lane_pp = lax.broadcasted_iota(jnp.int32, (1, width), 1) >> (r.bit_length() - 1)
    halves = []
    for parity in range(2):
        sink = jnp.zeros((1, width), F32)
        for pp in range(PAIRS_PER_KV):
            sink = jnp.where(lane_pp == pp, sink_ref[(kv * PAIRS_PER_KV + pp) * 2 + parity] * LOG2E, sink)
        l = logits[parity * N_KEYS:(parity + 1) * N_KEYS, :]
        m = jnp.maximum(jnp.max(l, axis=0, keepdims=True), sink)
        p = jnp.exp2(l - m)
        denom = jnp.sum(p, axis=0, keepdims=True) + jnp.exp2(sink - m)
        p16 = p.astype(BF16)
        v_rows = jnp.concatenate([v_t[kv * B_HD:(kv + 1) * B_HD, :] for v_t in vals_t], axis=0)
        pv_all = jnp.dot(v_rows, p16, preferred_element_type=F32)
        pv = None
        for n, mask in enumerate(col_masks):
            part = pv_all[n * B_HD:(n + 1) * B_HD, :]
            pv = part if mask is None else jnp.where(mask, part, 0.0 if pv is None else pv)
        halves.append(pv / denom)
    return jnp.concatenate(halves, axis=0).T


def _swa_sample_kernel(bucket_ref, relb_ref, sink_ref, q_ref, z_ref, kvn_ref, ck_ref, cv_ref,
                       ob_ref, nk_ref, nv_ref, bias_ref, *, bb, t_len):
    rows = bb * t_len
    width = PAIRS_PER_KV * rows

    @pl.when(pl.program_id(0) == 0)
    def _():
        _build_bias(bucket_ref, relb_ref, bias_ref, rows)

    kw = B_KV_HEADS * B_HD
    t_shift = t_len.bit_length() - 1
    row = lax.broadcasted_iota(jnp.int32, (rows, kw), 0)
    seq_of_col = lambda n: (lax.broadcasted_iota(jnp.int32, (n, width), 1) & (rows - 1)) >> t_shift
    seq_l = seq_of_col(2 * N_KEYS)
    seq_v = seq_of_col(B_HD)
    q_pairs = [(q_ref[:, p * 128:(p + 1) * 128].astype(F32) * Q_SCALE).astype(BF16) for p in range(N_PAIRS)]
    q_kv = [jnp.concatenate(q_pairs[kv * PAIRS_PER_KV:(kv + 1) * PAIRS_PER_KV], axis=0)
            for kv in range(B_KV_HEADS)]
    kn = kvn_ref[:, 0:kw]
    vn = kvn_ref[:, kw:2 * kw]
    n_new = N_KEYS - WINDOW
    assert n_new == WINDOW
    pad = jnp.zeros((n_new - rows, kw), F32)
    kn_t = jnp.concatenate([kn, pad], axis=0).T
    vn_t = jnp.concatenate([vn, pad], axis=0).T
    lane = lax.broadcasted_iota(jnp.int32, (kw, n_new), 1)
    logits = [None] * B_KV_HEADS
    vals_t, col_masks = [], []
    for j in range(bb):
        sel = row < t_len
        kj = jnp.where(sel, kn if j == 0 else pltpu.roll(kn, rows - j * t_len, axis=0), 0.0)
        kj_t = jnp.where(lane < t_len, kn_t if j == 0 else pltpu.roll(kn_t, n_new - j * t_len, axis=1), 0.0)
        vj_t = jnp.where(lane < t_len, vn_t if j == 0 else pltpu.roll(vn_t, n_new - j * t_len, axis=1), 0.0)
        ck_t = ck_ref[j]
        cv_t = cv_ref[j]
        keys = _swa_keys(jnp.concatenate([ck_t.T, kj, pad], axis=0))
        for kv in range(B_KV_HEADS):
            lg = _dot_nt(keys[kv], q_kv[kv])
            logits[kv] = jnp.where(seq_l == j, lg, 0.0 if logits[kv] is None else logits[kv])
        vals_t.append(jnp.concatenate([cv_t, vj_t], axis=1).astype(BF16))
        col_masks.append(seq_v == j)
        keep = lane < WINDOW - t_len
        nk_ref[j] = jnp.where(keep, pltpu.roll(ck_t, WINDOW - t_len, axis=1),
                              pltpu.roll(kj_t, WINDOW - t_len, axis=1))
        nv_ref[j] = jnp.where(keep, pltpu.roll(cv_t, WINDOW - t_len, axis=1),
                              pltpu.roll(vj_t, WINDOW - t_len, axis=1))
    for kv in range(B_KV_HEADS):
        acc_t = _swa_attend(logits[kv] + bias_ref[kv], vals_t, col_masks, sink_ref, kv, rows)
        for pp in range(PAIRS_PER_KV):
            p = kv * PAIRS_PER_KV + pp
            z = z_ref[:, p * 128:(p + 1) * 128].astype(F32)
            ob_ref[:, p * 128:(p + 1) * 128] = (acc_t[pp * rows:(pp + 1) * rows, :] * _silu(z)).astype(BF16)


def _mix_sample_kernel(bucket_ref, relb_ref, sink_ref, aq_ref, ai_ref, aog_ref, az_ref, f_ref, lb_ref, g_ref,
                       s0_ref, bq_ref, bz_ref, kvn_ref, ck_ref, cv_ref,
                       oa_ref, s_ref, ob_ref, nk_ref, nv_ref, bias_ref, *, bb, t_len):
    _hgrn_sample_kernel(aq_ref, ai_ref, aog_ref, az_ref, f_ref, lb_ref, g_ref, s0_ref, oa_ref, s_ref,
                        bb=bb, t_len=t_len)
    _swa_sample_kernel(bucket_ref, relb_ref, sink_ref, bq_ref, bz_ref, kvn_ref, ck_ref, cv_ref,
                       ob_ref, nk_ref, nv_ref, bias_ref, bb=bb, t_len=t_len)


def _mix_sample(bucket, rel_bias, sink, pm, f, kvn, hgrn_lb, hgrn_norm, s0, cache_k, cache_v,
                batch, t_len, bb):
    rows = bb * t_len
    kvw = kvn.shape[1]
    kw = kvw // 2
    n_lb = hgrn_lb.shape[0]
    smem = pl.BlockSpec(memory_space=pltpu.SMEM)
    col = lambda grp: pl.BlockSpec((rows, D_MODEL), lambda i, grp=grp: (i, grp))
    tok = pl.BlockSpec((rows, D_MODEL), lambda i: (i, 0))
    st_spec = pl.BlockSpec((bb, A_HEADS, A_DK, A_DV), lambda i: (i, 0, 0, 0))
    cache_spec = pl.BlockSpec((bb, kw, WINDOW), lambda i: (i, 0, 0))
    return pl.pallas_call(
        functools.partial(_mix_sample_kernel, bb=bb, t_len=t_len),
        grid=(batch // bb,),
        in_specs=[
            _resident((N_KEYS, rows)),
            smem, smem,
            col(G_AQ), col(G_AI), col(G_AOG), col(G_AZ),
            tok,
            _resident((n_lb, D_MODEL)),
            _resident((1, D_MODEL)),
            st_spec,
            col(G_BQ), col(G_BZ),
            pl.BlockSpec((rows, kvw), lambda i: (i, 0)),
            cache_spec, cache_spec,
        ],
        out_specs=[tok, st_spec, tok, cache_spec, cache_spec],
        out_shape=[
            jax.ShapeDtypeStruct((batch * t_len, D_MODEL), BF16),
            jax.ShapeDtypeStruct((batch, A_HEADS, A_DK, A_DV), F32),
            jax.ShapeDtypeStruct((batch * t_len, D_MODEL), BF16),
            jax.ShapeDtypeStruct((batch, kw, WINDOW), F32),
            jax.ShapeDtypeStruct((batch, kw, WINDOW), F32),
        ],
        scratch_shapes=[pltpu.VMEM((B_KV_HEADS, 2 * N_KEYS, PAIRS_PER_KV * rows), F32)],
        compiler_params=pltpu.CompilerParams(
            dimension_semantics=("arbitrary",), vmem_limit_bytes=VMEM_LIMIT),
        name="mix_sample",
    )(bucket, rel_bias, sink, pm, pm, pm, pm, f, hgrn_lb, hgrn_norm, s0, pm, pm, kvn, cache_k, cache_v)


MIX_TQ = 512
HGRN_SLAB = 256


class _HgrnConsts:
    def __init__(self, slab):
        shift = A_CHUNK.bit_length() - 1
        self.row = lax.broadcasted_iota(jnp.int32, (slab, A_DK), 0)
        rs = lax.broadcasted_iota(jnp.int32, (slab, slab), 0)
        cs = lax.broadcasted_iota(jnp.int32, (slab, slab), 1)
        self.same_chunk_causal = jnp.logical_and((rs >> shift) == (cs >> shift), cs <= rs)
        self.zero_row = jnp.zeros((1, A_DK), F32)
        self.zero_blk = jnp.zeros((A_SUB, A_DK), BF16)
        self.zero_chunk = jnp.zeros((A_CHUNK, A_DK), BF16)


def _hgrn_slab(f_pre, q_pre, v, lb, st, c):
    slab = f_pre.shape[0]
    n_sub = A_CHUNK // A_SUB
    sc = slab // A_CHUNK
    rep = lambda r, n: jnp.broadcast_to(r, (n, A_DK))
    fg = _forget_gate(f_pre, lb)
    logf = jnp.log(fg)
    k = 1.0 - fg
    b = _group_cumsum(logf, c.row, A_CHUNK, (1, 2, 4, 8, 16, 32))
    q = _silu(q_pre)

    ends = [b[A_SUB * j + A_SUB - 1:A_SUB * (j + 1), :] for j in range(sc * n_sub)]
    endrow = jnp.concatenate([rep(e, A_SUB) for e in ends], axis=0)
    prevrow = jnp.concatenate(
        [rep(c.zero_row if j % n_sub == 0 else ends[j - 1], A_SUB) for j in range(sc * n_sub)], axis=0)
    tot = [ends[n_sub * ch + n_sub - 1] for ch in range(sc)]
    lastrow = jnp.concatenate([rep(t, A_CHUNK) for t in tot], axis=0)
    kend = k * jnp.exp(endrow - b)
    qd = q * jnp.exp(b - prevrow)
    qs = qd * jnp.exp(prevrow)
    kd = kend * jnp.exp(lastrow - endrow)
    kdiag16 = (k * jnp.exp(prevrow - b)).astype(BF16)
    qd16 = qd.astype(BF16)
    kend16 = kend.astype(BF16)
    qs16 = qs.astype(BF16)
    kd16 = kd.astype(BF16)

    blk = lambda a, j: a[A_SUB * j:A_SUB * (j + 1), :]
    chk = lambda a, ch: a[A_CHUNK * ch:A_CHUNK * (ch + 1), :]

    q_groups, k_groups = [], []
    for i in range(n_sub):
        q_groups.append(jnp.concatenate(
            [blk(qd16, j) if j % n_sub == i else c.zero_blk for j in range(sc * n_sub)], axis=0))
        pieces = []
        for j in range(sc * n_sub):
            ch, jj = divmod(j, n_sub)
            if jj == i:
                pieces.append(blk(kdiag16, j))
            elif jj == i - 1:
                pieces.append(blk(kend16, j))
            elif jj < i:
                pieces.append((blk(kend, j) * jnp.exp(ends[ch * n_sub + i - 1] - ends[j])).astype(BF16))
            else:
                pieces.append(c.zero_blk)
        k_groups.append(jnp.concatenate(pieces, axis=0))
    att = _dot_nt(jnp.concatenate(q_groups, axis=1), jnp.concatenate(k_groups, axis=1))
    att = jnp.where(c.same_chunk_causal, att, 0.0)

    if sc > 1:
        q_groups, k_groups = [], []
        for ch in range(1, sc):
            q_groups.append(jnp.concatenate(
                [chk(qs16, c2) if c2 == ch else c.zero_chunk for c2 in range(sc)], axis=0))
            pieces = []
            for c2 in range(sc):
                if c2 == ch - 1:
                    pieces.append(chk(kd16, c2))
                elif c2 < ch:
                    pieces.append((chk(kd, c2) * jnp.exp(sum(tot[c2 + 1:ch]))).astype(BF16))
                else:
                    pieces.append(c.zero_chunk)
            k_groups.append(jnp.concatenate(pieces, axis=0))
        att = att + _dot_nt(jnp.concatenate(q_groups, axis=1), jnp.concatenate(k_groups, axis=1))
    o = jnp.dot(att.astype(BF16), v, preferred_element_type=F32)

    q0 = jnp.concatenate(
        [chk(qs16, 0)] + [(chk(qs, ch) * jnp.exp(sum(tot[:ch]))).astype(BF16) for ch in range(1, sc)], axis=0)
    o = o + _dot_nt(q0, st.astype(BF16))

    k1 = jnp.concatenate(
        [(chk(kd, ch) * jnp.exp(sum(tot[ch + 1:]))).astype(BF16) for ch in range(sc - 1)]
        + [chk(kd16, sc - 1)], axis=0)
    st = st * jnp.exp(sum(tot)) + _dot_tn(v, k1)
    return o, st


def _hgrn_blockwise(stage_ref, oraw_ref, ls, lb, st):
    n_blk = stage_ref.shape[1] // A_SUB
    row = lax.broadcasted_iota(jnp.int32, (A_SUB, A_DK), 0)
    shifts = tuple(1 << i for i in range((A_SUB - 1).bit_length()))

    def body(i, st):
        rows = pl.ds(pl.multiple_of(i * A_SUB, A_SUB), A_SUB)
        fg = _forget_gate(stage_ref[1, rows, ls], lb)
        k = 1.0 - fg
        b = _group_cumsum(jnp.log(fg), row, A_SUB, shifts)
        q = _silu(stage_ref[0, rows, ls])
        v = stage_ref[2, rows, ls]
        o = _dot_nt((q * jnp.exp(b)).astype(BF16), st.astype(BF16))
        o = o + jnp.sum(q * k, axis=-1, keepdims=True) * v
        for d in range(1, A_SUB):
            ok = row >= d
            w = jnp.exp(jnp.where(ok, b - pltpu.roll(b, d, axis=0), 0.0))
            a = jnp.sum(q * pltpu.roll(k, d, axis=0) * w, axis=-1, keepdims=True)
            o = o + jnp.where(ok, a * pltpu.roll(v, d, axis=0), 0.0)
        oraw_ref[rows, ls] = o
        b_end = b[A_SUB - 1:A_SUB, :]
        kd = (k * jnp.exp(b_end - b)).astype(BF16)
        return st * jnp.exp(b_end) + _dot_tn(v.astype(BF16), kd)

    return lax.fori_loop(0, n_blk, body, st)


LB_SAFE = math.exp(-80.0 / A_SUB)


def _mix_prompt_kernel(bucket_ref, relb_ref, sink_ref, x_ref, gpre_ref, w_ref, lb_ref, gh_ref,
                       oa_ref, ob_ref, gab_ref, st_ref, kvwin_ref,
                       bias_ref, st_scr, kvprev_scr, stage_scr, oraw_scr):
    tile = pl.program_id(1)
    at_start = tile == 0

    @pl.when(jnp.logical_and(pl.program_id(0) == 0, at_start))
    def _():
        _build_bias(bucket_ref, relb_ref, bias_ref, WINDOW)

    @pl.when(at_start)
    def _():
        st_scr[...] = jnp.zeros(st_scr.shape, F32)
        kvprev_scr[...] = jnp.zeros(kvprev_scr.shape, F32)

    x = x_ref[...]
    ms = jnp.mean(x * x, axis=-1, keepdims=True)
    u = (x * lax.rsqrt(ms + EPS) * gpre_ref[...]).astype(BF16)
    proj = lambda seg, lo, hi: jnp.dot(u, w_ref[:, IN_OFFS[seg] + lo:IN_OFFS[seg] + hi],
                                       preferred_element_type=F32)

    lb_all = _lower_bound(lb_ref)
    consts = _HgrnConsts(HGRN_SLAB)
    pair_w = 2 * A_DK
    n_phase = A_HEADS // 2
    n_blocks = MIX_TQ // WINDOW
    assert n_blocks == n_phase
    gate_w = D_MODEL // n_phase
    kw = B_KV_HEADS * B_HD

    def hgrn_proj(hp):
        lo, hi = hp * pair_w, (hp + 1) * pair_w
        return tuple(proj(seg, lo, hi) for seg in (0, SEG_AF, 2, 3, 4))

    def hgrn_pair(hp, q2, f2, v2, og2, z2):
        for hh in range(2):
            h = 2 * hp + hh
            ls = slice(hh * A_DK, (hh + 1) * A_DK)
            hs = slice(h * A_DK, (h + 1) * A_DK)
            st = st_scr[h]
            outs = []
            for s0 in range(0, MIX_TQ, HGRN_SLAB):
                rs = slice(s0, s0 + HGRN_SLAB)
                o, st = _hgrn_slab(f2[rs, ls], q2[rs, ls], v2[rs, ls].astype(BF16), lb_all[:, hs], st, consts)
                outs.append(o)
            st_scr[h] = st
            st_ref[h] = st.T
            oa = _hgrn_finish(jnp.concatenate(outs, axis=0), gh_ref[:, hs], og2[:, ls], z2[:, ls])
            oa_ref[:, hs] = oa.astype(BF16)

    key = lax.broadcasted_iota(jnp.int32, (2 * N_KEYS, PAIRS_PER_KV * WINDOW), 0) & (N_KEYS - 1)
    no_prev = jnp.logical_and(key < WINDOW, at_start)

    def swa_block(j, kv, bq, bz):
        r0 = j * WINDOW
        if j == 0:
            kvj = jnp.concatenate([kvprev_scr[...], kv[0:WINDOW, :]], axis=0)
        else:
            kvj = kv[r0 - WINDOW:r0 + WINDOW, :]
        keys = _swa_keys(kvj[:, 0:kw])
        vals_t = [kvj[:, kw:2 * kw].T.astype(BF16)]
        for kvh in range(B_KV_HEADS):
            q = jnp.concatenate([bq[r0:r0 + WINDOW, p * 128:(p + 1) * 128]
                                 for p in range(kvh * PAIRS_PER_KV, (kvh + 1) * PAIRS_PER_KV)], axis=0)
            logits = _dot_nt(keys[kvh], q) + bias_ref[kvh]
            if j == 0:
                logits = jnp.where(no_prev, NEG, logits)
            acc_t = _swa_attend(logits, vals_t, [None], sink_ref, kvh, WINDOW)
            for pp in range(PAIRS_PER_KV):
                cs = slice((kvh * PAIRS_PER_KV + pp) * 128, (kvh * PAIRS_PER_KV + pp + 1) * 128)
                ob_ref[r0:r0 + WINDOW, cs] = (
                    acc_t[pp * WINDOW:(pp + 1) * WINDOW, :] * _silu(bz[r0:r0 + WINDOW, cs])).astype(BF16)

    def hgrn_pair_blockwise(hp, q2, f2, v2, og2, z2):
        stage_scr[0] = q2
        stage_scr[1] = f2
        stage_scr[2] = v2
        for hh in range(2):
            h = 2 * hp + hh
            ls = slice(hh * A_DK, (hh + 1) * A_DK)
            hs = slice(h * A_DK, (h + 1) * A_DK)
            st = _hgrn_blockwise(stage_scr, oraw_scr, ls, lb_all[:, hs], st_scr[h])
            st_scr[h] = st
            st_ref[h] = st.T
            oa = _hgrn_finish(oraw_scr[:, ls], gh_ref[:, hs], og2[:, ls], z2[:, ls])
            oa_ref[:, hs] = oa.astype(BF16)

    def gate_proj(ph):
        for seg, base in ((9, 0), (10, D_MODEL)):
            gab_ref[:, base + ph * gate_w:base + (ph + 1) * gate_w] = proj(
                seg, ph * gate_w, (ph + 1) * gate_w).astype(BF16)

    def swa_proj():
        kv = proj(SEG_BK, 0, 2 * kw)
        bq = (proj(5, 0, D_MODEL) * Q_SCALE).astype(BF16)
        bz = proj(8, 0, D_MODEL)
        return kv, bq, bz

    def keep_window(kv):
        kvprev_scr[...] = kv[MIX_TQ - WINDOW:, :]
        kvwin_ref[...] = kv[MIX_TQ - WINDOW:, :]

    slab_safe = jnp.min(lb_all) >= LB_SAFE

    @pl.when(slab_safe)
    def _():
        kv, bq, bz = swa_proj()
        nxt = hgrn_proj(0)
        for ph in range(n_phase):
            cur = nxt
            if ph + 1 < n_phase:
                nxt = hgrn_proj(ph + 1)
            gate_proj(ph)
            hgrn_pair(ph, *cur)
            swa_block(ph, kv, bq, bz)
        keep_window(kv)

    @pl.when(jnp.logical_not(slab_safe))
    def _():
        for ph in range(n_phase):
            hgrn_pair_blockwise(ph, *hgrn_proj(ph))
            gate_proj(ph)
        kv, bq, bz = swa_proj()
        for j in range(n_blocks):
            swa_block(j, kv, bq, bz)
        keep_window(kv)


def _mix_prompt(bucket, rel_bias, sink, x, g_pre, w, hgrn_lb, hgrn_norm, batch, seq):
    nt = seq // MIX_TQ
    n_lb = hgrn_lb.shape[0]
    kvw = 2 * B_KV_HEADS * B_HD
    smem = pl.BlockSpec(memory_space=pltpu.SMEM)
    tok = lambda width: pl.BlockSpec((MIX_TQ, width), lambda b, t: (b * nt + t, 0))
    return pl.pallas_call(
        _mix_prompt_kernel,
        grid=(batch, nt),
        in_specs=[
            _resident((N_KEYS, WINDOW)),
            smem, smem,
            tok(D_MODEL),
            _resident((1, D_MODEL)),
            _resident(w.shape),
            _resident((n_lb, D_MODEL)),
            _resident((1, D_MODEL)),
        ],
        out_specs=[
            tok(D_MODEL), tok(D_MODEL), tok(2 * D_MODEL),
            pl.BlockSpec((None, A_HEADS, A_DK, A_DV), lambda b, t: (b, 0, 0, 0)),
            pl.BlockSpec((None, WINDOW, kvw), lambda b, t: (b, 0, 0)),
        ],
        out_shape=[
            jax.ShapeDtypeStruct((batch * seq, D_MODEL), BF16),
            jax.ShapeDtypeStruct((batch * seq, D_MODEL), BF16),
            jax.ShapeDtypeStruct((batch * seq, 2 * D_MODEL), BF16),
            jax.ShapeDtypeStruct((batch, A_HEADS, A_DK, A_DV), F32),
            jax.ShapeDtypeStruct((batch, WINDOW, kvw), F32),
        ],
        scratch_shapes=[
            pltpu.VMEM((B_KV_HEADS, 2 * N_KEYS, PAIRS_PER_KV * WINDOW), F32),
            pltpu.VMEM((A_HEADS, A_DV, A_DK), F32),
            pltpu.VMEM((WINDOW, kvw), F32),
            pltpu.VMEM((3, MIX_TQ, 2 * A_DK), F32),
            pltpu.VMEM((MIX_TQ, 2 * A_DK), F32),
        ],
        compiler_params=pltpu.CompilerParams(
            dimension_semantics=("arbitrary", "arbitrary"), vmem_limit_bytes=VMEM_LIMIT),
        name="mix_prompt",
    )(bucket, rel_bias, sink, x, g_pre, w, hgrn_lb, hgrn_norm)


def _outproj_kernel(oa_ref, ob_ref, ga_ref, gb_ref, x_ref, p_ref, wpa_ref, wpb_ref, wo_ref, gpost_ref,
                    wple_ref, wg_ref, y_ref):
    a = jnp.dot(oa_ref[...], wpa_ref[...], preferred_element_type=F32)
    b = jnp.dot(ob_ref[...], wpb_ref[...], preferred_element_type=F32)
    m = _sigmoid(ga_ref[...].astype(F32)) * a + _sigmoid(gb_ref[...].astype(F32)) * b
    y = jnp.dot(m.astype(BF16), wo_ref[...], preferred_element_type=F32)
    y = y * lax.rsqrt(jnp.mean(y * y, axis=-1, keepdims=True) + EPS) * gpost_ref[...]
    x1 = x_ref[...] + y
    gate = _sigmoid(jnp.dot(x1.astype(BF16), wg_ref[...], preferred_element_type=F32))
    e = jnp.dot(p_ref[...].astype(BF16), wple_ref[...], preferred_element_type=F32) * gate
    y_ref[...] = x1 + e


def _outproj(oa, ob, gates, ga_col, gb_col, x, p, wpa, wpb, wo, gpost, wple, wg, tm):
    n = x.shape[0]
    ple = p.shape[1]
    tok = lambda w, c=0: pl.BlockSpec((tm, w), lambda i, c=c: (i, c))
    return pl.pallas_call(
        _outproj_kernel,
        grid=(n // tm,),
        in_specs=[
            tok(D_MODEL), tok(D_MODEL), tok(D_MODEL, ga_col), tok(D_MODEL, gb_col), tok(D_MODEL), tok(ple),
            _resident((D_MODEL, D_MODEL)), _resident((D_MODEL, D_MODEL)), _resident((D_MODEL, D_MODEL)),
            _resident((1, D_MODEL)), _resident((ple, D_MODEL)), _resident((D_MODEL, D_MODEL)),
        ],
        out_specs=tok(D_MODEL),
        out_shape=jax.ShapeDtypeStruct((n, D_MODEL), F32),
        compiler_params=pltpu.CompilerParams(
            dimension_semantics=("arbitrary",), vmem_limit_bytes=VMEM_LIMIT),
        name="outproj",
    )(oa, ob, gates, gates, x, p, wpa, wpb, wo, gpost, wple, wg)


def _rel_bucket(rel):
    n = np.maximum(rel, 0)
    max_exact = REL_BUCKETS // 2
    nf = np.maximum(n, 1).astype(np.float32)
    scaled = (np.log(nf / np.float32(max_exact)) / np.float32(math.log(REL_MAX_DIST / max_exact))
              * np.float32(REL_BUCKETS - max_exact))
    frac = scaled - np.floor(scaled)
    inside = (n > max_exact) & (n < REL_MAX_DIST)
    assert np.all((frac[inside] > 1e-3) & (frac[inside] < 1.0 - 1e-3))
    large = np.minimum(max_exact + scaled.astype(np.int32), REL_BUCKETS - 1)
    return np.where(n < max_exact, n, large)


def _bucket_table(q_pos, k_pos, k_valid):
    rel = q_pos[:, None] - k_pos[None, :]
    ok = (rel >= 0) & (rel < WINDOW) & k_valid[None, :]
    return jnp.asarray(np.where(ok, _rel_bucket(rel), -1).astype(np.int32).T)


TM_PROJ = 512
SAMPLE_BB = 8


def _layer(xp, xs, s_hgrn, win_k, win_v, pp, ps, norm_pre, w_in, hgrn_lb, hgrn_norm, attn_sink,
           rel_bias, w_pa, w_pb, w_o, norm_post, w_ple, w_ple_gate):
    batch, seq, _ = xp.shape
    dbatch, t_len, _ = xs.shape
    kw = B_KV_HEADS * B_HD

    g_pre = norm_pre.reshape(1, D_MODEL)
    g_post = norm_post.reshape(1, D_MODEL)
    g_hgrn = hgrn_norm.reshape(1, D_MODEL)
    wpa, wpb, wo = w_pa.astype(BF16), w_pb.astype(BF16), w_o.astype(BF16)
    wple, wg = w_ple.astype(BF16), w_ple_gate.astype(BF16)

    xp2 = xp.reshape(batch * seq, D_MODEL)
    xs2 = xs.reshape(dbatch * t_len, D_MODEL)
    k_all = np.arange(N_KEYS)
    bucket_p = _bucket_table(np.arange(WINDOW) + WINDOW, k_all, np.ones((N_KEYS,), bool))
    rows_s = SAMPLE_BB * t_len
    bucket_s = _bucket_table(WINDOW + np.arange(rows_s) % t_len, k_all, k_all < WINDOW + t_len)

    pm_s, f_s, kv_s, w16 = _inproj(xs2, g_pre, w_in)

    oa_p, ob_p, gab_p, st_p, kvwin_p = _mix_prompt(bucket_p, rel_bias, attn_sink, xp2, g_pre, w16,
                                                   hgrn_lb, g_hgrn, batch, seq)

    to_minor = lambda c: jnp.transpose(c, (0, 2, 3, 1)).reshape(dbatch, kw, WINDOW)
    from_minor = lambda c: jnp.transpose(c.reshape(dbatch, B_KV_HEADS, B_HD, WINDOW), (0, 3, 1, 2))
    oa_s, st_s, ob_s, nk_s, nv_s = _mix_sample(bucket_s, rel_bias, attn_sink, pm_s, f_s, kv_s, hgrn_lb, g_hgrn,
                                               s_hgrn, to_minor(win_k), to_minor(win_v),
                                               dbatch, t_len, SAMPLE_BB)

    y_p = _outproj(oa_p, ob_p, gab_p, 0, 1, xp2, pp.reshape(batch * seq, -1),
                   wpa, wpb, wo, g_post, wple, wg, TM_PROJ)
    y_s = _outproj(oa_s, ob_s, pm_s, G_GA, G_GB, xs2, ps.reshape(dbatch * t_len, -1),
                   wpa, wpb, wo, g_post, wple, wg, min(TM_PROJ, dbatch * t_len))

    k_win_p = kvwin_p[:, :, 0:kw].reshape(batch, WINDOW, B_KV_HEADS, B_HD)
    v_win_p = kvwin_p[:, :, kw:].reshape(batch, WINDOW, B_KV_HEADS, B_HD)
    return (y_p.reshape(batch, seq, D_MODEL), y_s.reshape(dbatch, t_len, D_MODEL), st_p, st_s,
            k_win_p, v_win_p,
            from_minor(nk_s), from_minor(nv_s))


def kernel(x_prompt, x_sample, state_hgrn, cache_swa_k, cache_swa_v, p_prompt, p_sample, norm_pre, w_in,
           hgrn_lb, hgrn_norm, attn_sink, rel_bias, w_pa, w_pb, w_o, norm_post, w_ple, w_ple_gate):
    depth = w_in.shape[0]
    assert depth == 1, "the forget-gate lower bound is implemented for a single layer"
    xp, xs = x_prompt, x_sample
    outs = []
    for l in range(depth):
        res = _layer(xp, xs, state_hgrn[l], cache_swa_k[l], cache_swa_v[l], p_prompt[l], p_sample[l],
                     norm_pre[l], w_in[l], hgrn_lb, hgrn_norm[l], attn_sink[l], rel_bias,
                     w_pa[l], w_pb[l], w_o[l], norm_post[l], w_ple[l], w_ple_gate[l])
        xp, xs = res[0], res[1]
        outs.append(res[2:])
    stack = lambda i: jnp.stack([o[i] for o in outs])
    return (xp, xs, stack(0), stack(1), stack(2), stack(3), stack(4), stack(5))
```

```python
import functools
import math

import jax
import jax.numpy as jnp
import numpy as np
from jax import lax
from jax.experimental import pallas as pl
from jax.experimental.pallas import tpu as pltpu

F32 = jnp.float32
BF16 = jnp.bfloat16

D_MODEL = 1024
A_HEADS = 8
A_DK = 128
A_DV = 128
A_CHUNK = 64
A_SUB = 16
B_HEADS = 16
B_KV_HEADS = 2
B_HD = 64
WINDOW = 128
REL_BUCKETS = 32
REL_MAX_DIST = 128
EPS = 1e-6
NEG = float("-inf")
LOG2E = math.log2(math.e)
Q_SCALE = B_HD ** -0.5 * LOG2E

G_AQ, G_AI, G_AOG, G_AZ, G_BQ, G_BZ, G_GA, G_GB = range(8)
N_GROUPS = 8

VMEM_LIMIT = 56 * 1024 * 1024


def _sigmoid(x):
    return 0.5 * jnp.tanh(0.5 * x) + 0.5


def _silu(x):
    h = 0.5 * x
    return h * jnp.tanh(h) + h


def _resident(shape):
    nd = len(shape)
    return pl.BlockSpec(shape, lambda *_: (0,) * nd, pipeline_mode=pl.Buffered(1))


IN_OFFS = (0, 1024, 2048, 3072, 4096, 5120, 6144, 6272, 6400, 7424, 8448, 9472)
SLAB_SEGS = (0, 2, 3, 4, 5, 8, 9, 10)
SEG_AF, SEG_BK, SEG_BV = 1, 6, 7


def _inproj_kernel(x_ref, g_ref, w_ref, pm_ref, f_ref, kv_ref):
    x = x_ref[...]
    ms = jnp.mean(x * x, axis=-1, keepdims=True)
    u = (x * lax.rsqrt(ms + EPS) * g_ref[...]).astype(BF16)
    proj = lambda lo, hi: jnp.dot(u, w_ref[:, lo:hi], preferred_element_type=F32)
    for c, seg in enumerate(SLAB_SEGS):
        pm_ref[:, c * D_MODEL:(c + 1) * D_MODEL] = proj(IN_OFFS[seg], IN_OFFS[seg + 1]).astype(BF16)
    f_ref[...] = proj(IN_OFFS[SEG_AF], IN_OFFS[SEG_AF + 1])
    kv_ref[...] = proj(IN_OFFS[SEG_BK], IN_OFFS[SEG_BV + 1])


def _inproj(x, g, w, tm):
    n = x.shape[0]
    nm = N_GROUPS * D_MODEL
    nkv = IN_OFFS[SEG_BV + 1] - IN_OFFS[SEG_BK]
    return pl.pallas_call(
        _inproj_kernel,
        grid=(n // tm,),
        in_specs=[
            pl.BlockSpec((tm, D_MODEL), lambda i: (i, 0)),
            _resident((1, D_MODEL)),
            _resident(w.shape),
        ],
        out_specs=[
            pl.BlockSpec((tm, nm), lambda i: (i, 0)),
            pl.BlockSpec((tm, D_MODEL), lambda i: (i, 0)),
            pl.BlockSpec((tm, nkv), lambda i: (i, 0)),
        ],
        out_shape=[
            jax.ShapeDtypeStruct((n, nm), BF16),
            jax.ShapeDtypeStruct((n, D_MODEL), F32),
            jax.ShapeDtypeStruct((n, nkv), F32),
        ],
        compiler_params=pltpu.CompilerParams(
            dimension_semantics=("arbitrary",), vmem_limit_bytes=VMEM_LIMIT),
        name="inproj",
    )(x, g, w)


def _lower_bound(lb_ref):
    l = lb_ref[...]
    m = jnp.max(l, axis=0, keepdims=True)
    e = jnp.exp(l - m)
    return e[0:1, :] / jnp.sum(e, axis=0, keepdims=True)


def _forget_gate(f_pre, lb):
    return lb + (1.0 - lb) / (1.0 + jnp.exp(-f_pre))


def _group_cumsum(x, row, period, shifts):
    pos = row & (period - 1)
    for sh in shifts:
        x = x + jnp.where(pos >= sh, pltpu.roll(x, sh, axis=0), 0.0)
    return x


def _hgrn_finish(o, g, og_pre, z_pre):
    o = o * lax.rsqrt(jnp.mean(o * o, axis=-1, keepdims=True) + EPS)
    return o * g * _sigmoid(og_pre) * _silu(z_pre)


def _dot_nt(a, b):
    return lax.dot_general(a, b, (((1,), (1,)), ((), ())), preferred_element_type=F32)


def _dot_tn(a, b):
    return lax.dot_general(a, b, (((0,), (0,)), ((), ())), preferred_element_type=F32)


def _hgrn_sample_kernel(q_ref, v_ref, og_ref, z_ref, f_ref, lb_ref, g_ref, s0_ref, oa_ref, s_ref,
                        *, bb, t_len):
    rows = bb * t_len
    lb_all = _lower_bound(lb_ref)
    row = lax.broadcasted_iota(jnp.int32, (rows, A_DK), 0)
    pos = row & (t_len - 1)
    grp = row >> (t_len.bit_length() - 1)
    shifts = tuple(1 << i for i in range((t_len - 1).bit_length()))
    n_tail = A_DK - rows
    sel_row = lax.broadcasted_iota(jnp.int32, (n_tail, 2 * bb * A_DV), 0)
    sel_blk = lax.broadcasted_iota(jnp.int32, (n_tail, 2 * bb * A_DV), 1) >> (A_DV.bit_length() - 1)
    picks = jnp.logical_and(sel_row < 2 * bb, sel_blk == bb + (sel_row & (bb - 1)))
    rhs_tail = jnp.where(picks, 1.0, 0.0).astype(BF16)
    lhs_pad = jnp.zeros((n_tail - 2 * bb, A_DK), F32)
    v_pad = jnp.zeros((rows, bb * A_DV), F32)
    for h in range(A_HEADS):
        ls = slice(h * A_DK, (h + 1) * A_DK)
        lb = lb_all[:, ls]
        fg = _forget_gate(f_ref[:, ls], lb)
        logf = jnp.log(fg)
        k = 1.0 - fg
        b = _group_cumsum(logf, row, t_len, shifts)
        q = _silu(q_ref[:, ls].astype(F32))
        v = v_ref[:, ls].astype(F32)

        o = jnp.zeros((rows, A_DV), F32)
        for d in range(t_len):
            if d == 0:
                a = jnp.sum(q * k, axis=-1, keepdims=True)
                o = o + a * v
            else:
                w = jnp.exp(jnp.where(pos >= d, b - pltpu.roll(b, d, axis=0), 0.0))
                a = jnp.sum(q * pltpu.roll(k, d, axis=0) * w, axis=-1, keepdims=True)
                o = o + jnp.where(pos >= d, a * pltpu.roll(v, d, axis=0), 0.0)

        b_last = b
        for d in range(1, t_len):
            b_last = jnp.where(pos == t_len - 1 - d, pltpu.roll(b, rows - d, axis=0), b_last)
        e_last = jnp.exp(b_last)
        qs = q * jnp.exp(b)
        kd = k * jnp.exp(b_last - b)
        e_rows = jnp.concatenate([e_last[j * t_len:j * t_len + 1, :] for j in range(bb)], axis=0)
        e_hi = e_rows.astype(BF16).astype(F32)
        lhs_t = jnp.concatenate([kd, e_hi, e_rows - e_hi, lhs_pad], axis=0).T.astype(BF16)
        v_rows = jnp.concatenate([jnp.where(grp == j, v, 0.0) for j in range(bb)] + [v_pad], axis=1)
        res = jnp.dot(lhs_t, jnp.concatenate([v_rows.astype(BF16), rhs_tail], axis=0),
                      preferred_element_type=F32)
        for j in range(bb):
            s0 = s0_ref[j, h]
            o = o + jnp.dot(jnp.where(grp == j, qs, 0.0).astype(BF16), s0.astype(BF16),
                            preferred_element_type=F32)
            s_ref[j, h] = (res[:, (bb + j) * A_DV:(bb + j + 1) * A_DV] * s0
                           + res[:, j * A_DV:(j + 1) * A_DV])

        oa = _hgrn_finish(o, g_ref[:, ls], og_ref[:, ls].astype(F32), z_ref[:, ls].astype(F32))
        oa_ref[:, ls] = oa.astype(BF16)


N_PAIRS = B_HEADS // 2
PAIRS_PER_KV = N_PAIRS // B_KV_HEADS
N_KEYS = 2 * WINDOW


def _build_bias(bucket_ref, relb_ref, bias_ref, r):
    bucket = bucket_ref[...]
    for h in range(B_HEADS):
        def body(kb, acc, h=h):
            return jnp.where(bucket == kb, relb_ref[kb, h] * LOG2E, acc)
        tab = lax.fori_loop(0, REL_BUCKETS, body, jnp.full(bucket.shape, NEG, F32))
        pair, parity = divmod(h, 2)
        kv, pp = divmod(pair, PAIRS_PER_KV)
        bias_ref[kv, parity * N_KEYS:(parity + 1) * N_KEYS, pp * r:(pp + 1) * r] = tab


def _swa_keys(kk):
    lo = lax.broadcasted_iota(jnp.int32, (N_KEYS, 2 * B_HD), 1) < B_HD
    kk_sw = pltpu.roll(kk, B_HD, axis=1)
    slabs = []
    for kv in range(B_KV_HEADS):
        k_lo, k_hi = (kk, kk_sw) if kv == 0 else (kk_sw, kk)
        slabs.append(jnp.concatenate([jnp.where(lo, k_lo, 0.0), jnp.where(lo, 0.0, k_hi)], axis=0).astype(BF16))
    return slabs


def _swa_attend(logits, vals_t, col_masks, sink_ref, kv, r):
    width = PAIRS_PER_KV * r
    lane_pp = lax.broadcasted_iota(jnp.int32, (1, width), 1) >> (r.bit_length() - 1)
    halves = []
    for parity in range(2):
        sink = jnp.zeros((1, width), F32)
        for pp in range(PAIRS_PER_KV):
            sink = jnp.where(lane_pp == pp, sink_ref[(kv * PAIRS_PER_KV + pp) * 2 + parity] * LOG2E, sink)
        l = logits[parity * N_KEYS:(parity + 1) * N_KEYS, :]
        m = jnp.maximum(jnp.max(l, axis=0, keepdims=True), sink)
        p = jnp.exp2(l - m)
        denom = jnp.sum(p, axis=0, keepdims=True) + jnp.exp2(sink - m)
        p16 = p.astype(BF16)
        v_rows = jnp.concatenate([v_t[kv * B_HD:(kv + 1) * B_HD, :] for v_t in vals_t], axis=0)
        pv_all = jnp.dot(v_rows, p16, preferred_element_type=F32)
        pv = None
        for n, mask in enumerate(col_masks):
            part = pv_all[n * B_HD:(n + 1) * B_HD, :]
            pv = part if mask is None else jnp.where(mask, part, 0.0 if pv is None else pv)
        halves.append(pv / denom)
    return jnp.concatenate(halves, axis=0).T


def _swa_sample_kernel(bucket_ref, relb_ref, sink_ref, q_ref, z_ref, kvn_ref, ck_ref, cv_ref,
                       ob_ref, nk_ref, nv_ref, bias_ref, *, bb, t_len):
    rows = bb * t_len
    width = PAIRS_PER_KV * rows

    @pl.when(pl.program_id(0) == 0)
    def _():
        _build_bias(bucket_ref, relb_ref, bias_ref, rows)

    kw = B_KV_HEADS * B_HD
    t_shift = t_len.bit_length() - 1
    seq_of_col = lambda n: (lax.broadcasted_iota(jnp.int32, (n, width), 1) & (rows - 1)) >> t_shift
    seq_l = seq_of_col(2 * N_KEYS)
    seq_v = seq_of_col(B_HD)
    q_pairs = [(q_ref[:, p * 128:(p + 1) * 128].astype(F32) * Q_SCALE).astype(BF16) for p in range(N_PAIRS)]
    q_kv = [jnp.concatenate(q_pairs[kv * PAIRS_PER_KV:(kv + 1) * PAIRS_PER_KV], axis=0)
            for kv in range(B_KV_HEADS)]
    kn = kvn_ref[:, 0:kw]
    vn = kvn_ref[:, kw:2 * kw]
    n_new = N_KEYS - WINDOW
    assert n_new == WINDOW
    kn_all = jnp.concatenate([kn, jnp.zeros((n_new - rows, kw), F32)], axis=0)
    vn_all = jnp.concatenate([vn, jnp.zeros((n_new - rows, kw), F32)], axis=0)
    kn_t = kn_all.T
    vn_t = vn_all.T
    keep = lax.broadcasted_iota(jnp.int32, (kw, WINDOW), 1) < WINDOW - t_len
    logits = [None] * B_KV_HEADS
    vals_t, col_masks = [], []
    for j in range(bb):
        ck_t = ck_ref[j]
        cv_t = cv_ref[j]
        keys = _swa_keys(jnp.concatenate([ck_t.T, kn_all], axis=0))
        for kv in range(B_KV_HEADS):
            lg = _dot_nt(keys[kv], q_kv[kv])
            logits[kv] = jnp.where(seq_l == j, lg, 0.0 if logits[kv] is None else logits[kv])
        vals_t.append(jnp.concatenate([cv_t, vn_t], axis=1).astype(BF16))
        col_masks.append(seq_v == j)
        to_end = WINDOW - (j + 1) * t_len
        nk_ref[j] = jnp.where(keep, pltpu.roll(ck_t, WINDOW - t_len, axis=1), pltpu.roll(kn_t, to_end, axis=1))
        nv_ref[j] = jnp.where(keep, pltpu.roll(cv_t, WINDOW - t_len, axis=1), pltpu.roll(vn_t, to_end, axis=1))
    for kv in range(B_KV_HEADS):
        acc_t = _swa_attend(logits[kv] + bias_ref[kv], vals_t, col_masks, sink_ref, kv, rows)
        for pp in range(PAIRS_PER_KV):
            p = kv * PAIRS_PER_KV + pp
            z = z_ref[:, p * 128:(p + 1) * 128].astype(F32)
            ob_ref[:, p * 128:(p + 1) * 128] = (acc_t[pp * rows:(pp + 1) * rows, :] * _silu(z)).astype(BF16)


def _mix_sample_kernel(bucket_ref, relb_ref, sink_ref, aq_ref, ai_ref, aog_ref, az_ref, f_ref, lb_ref, g_ref,
                       s0_ref, bq_ref, bz_ref, kvn_ref, ck_ref, cv_ref,
                       oa_ref, s_ref, ob_ref, nk_ref, nv_ref, bias_ref, *, bb, t_len):
    _hgrn_sample_kernel(aq_ref, ai_ref, aog_ref, az_ref, f_ref, lb_ref, g_ref, s0_ref, oa_ref, s_ref,
                        bb=bb, t_len=t_len)
    _swa_sample_kernel(bucket_ref, relb_ref, sink_ref, bq_ref, bz_ref, kvn_ref, ck_ref, cv_ref,
                       ob_ref, nk_ref, nv_ref, bias_ref, bb=bb, t_len=t_len)


def _mix_sample(bucket, rel_bias, sink, pm, f, kvn, hgrn_lb, hgrn_norm, s0, cache_k, cache_v,
                batch, t_len, bb):
    rows = bb * t_len
    kvw = kvn.shape[1]
    kw = kvw // 2
    n_lb = hgrn_lb.shape[0]
    smem = pl.BlockSpec(memory_space=pltpu.SMEM)
    col = lambda grp: pl.BlockSpec((rows, D_MODEL), lambda i, grp=grp: (i, grp))
    tok = pl.BlockSpec((rows, D_MODEL), lambda i: (i, 0))
    st_spec = pl.BlockSpec((bb, A_HEADS, A_DK, A_DV), lambda i: (i, 0, 0, 0))
    cache_spec = pl.BlockSpec((bb, kw, WINDOW), lambda i: (i, 0, 0))
    return pl.pallas_call(
        functools.partial(_mix_sample_kernel, bb=bb, t_len=t_len),
        grid=(batch // bb,),
        in_specs=[
            _resident((N_KEYS, rows)),
            smem, smem,
            col(G_AQ), col(G_AI), col(G_AOG), col(G_AZ),
            tok,
            _resident((n_lb, D_MODEL)),
            _resident((1, D_MODEL)),
            st_spec,
            col(G_BQ), col(G_BZ),
            pl.BlockSpec((rows, kvw), lambda i: (i, 0)),
            cache_spec, cache_spec,
        ],
        out_specs=[tok, st_spec, tok, cache_spec, cache_spec],
        out_shape=[
            jax.ShapeDtypeStruct((batch * t_len, D_MODEL), BF16),
            jax.ShapeDtypeStruct((batch, A_HEADS, A_DK, A_DV), F32),
            jax.ShapeDtypeStruct((batch * t_len, D_MODEL), BF16),
            jax.ShapeDtypeStruct((batch, kw, WINDOW), F32),
            jax.ShapeDtypeStruct((batch, kw, WINDOW), F32),
        ],
        scratch_shapes=[pltpu.VMEM((B_KV_HEADS, 2 * N_KEYS, PAIRS_PER_KV * rows), F32)],
        compiler_params=pltpu.CompilerParams(
            dimension_semantics=("arbitrary",), vmem_limit_bytes=VMEM_LIMIT),
        name="mix_sample",
    )(bucket, rel_bias, sink, pm, pm, pm, pm, f, hgrn_lb, hgrn_norm, s0, pm, pm, kvn, cache_k, cache_v)


MIX_TQ = 512
HGRN_SLAB = 256


class _HgrnConsts:
    def __init__(self, slab):
        shift = A_CHUNK.bit_length() - 1
        self.row = lax.broadcasted_iota(jnp.int32, (slab, A_DK), 0)
        rs = lax.broadcasted_iota(jnp.int32, (slab, slab), 0)
        cs = lax.broadcasted_iota(jnp.int32, (slab, slab), 1)
        self.same_chunk_causal = jnp.logical_and((rs >> shift) == (cs >> shift), cs <= rs)
        self.zero_row = jnp.zeros((1, A_DK), F32)
        self.zero_blk = jnp.zeros((A_SUB, A_DK), BF16)
        self.zero_chunk = jnp.zeros((A_CHUNK, A_DK), BF16)


def _hgrn_slab(f_pre, q_pre, v, lb, st, c):
    slab = f_pre.shape[0]
    n_sub = A_CHUNK // A_SUB
    sc = slab // A_CHUNK
    rep = lambda r, n: jnp.broadcast_to(r, (n, A_DK))
    fg = _forget_gate(f_pre, lb)
    logf = jnp.log(fg)
    k = 1.0 - fg
    b = _group_cumsum(logf, c.row, A_CHUNK, (1, 2, 4, 8, 16, 32))
    q = _silu(q_pre)

    ends = [b[A_SUB * j + A_SUB - 1:A_SUB * (j + 1), :] for j in range(sc * n_sub)]
    endrow = jnp.concatenate([rep(e, A_SUB) for e in ends], axis=0)
    prevrow = jnp.concatenate(
        [rep(c.zero_row if j % n_sub == 0 else ends[j - 1], A_SUB) for j in range(sc * n_sub)], axis=0)
    tot = [ends[n_sub * ch + n_sub - 1] for ch in range(sc)]
    lastrow = jnp.concatenate([rep(t, A_CHUNK) for t in tot], axis=0)
    kend = k * jnp.exp(endrow - b)
    qd = q * jnp.exp(b - prevrow)
    qs = qd * jnp.exp(prevrow)
    kd = kend * jnp.exp(lastrow - endrow)
    kdiag16 = (k * jnp.exp(prevrow - b)).astype(BF16)
    qd16 = qd.astype(BF16)
    kend16 = kend.astype(BF16)
    qs16 = qs.astype(BF16)
    kd16 = kd.astype(BF16)

    blk = lambda a, j: a[A_SUB * j:A_SUB * (j + 1), :]
    chk = lambda a, ch: a[A_CHUNK * ch:A_CHUNK * (ch + 1), :]

    q_groups, k_groups = [], []
    for i in range(n_sub):
        q_groups.append(jnp.concatenate(
            [blk(qd16, j) if j % n_sub == i else c.zero_blk for j in range(sc * n_sub)], axis=0))
        pieces = []
        for j in range(sc * n_sub):
            ch, jj = divmod(j, n_sub)
            if jj == i:
                pieces.append(blk(kdiag16, j))
            elif jj == i - 1:
                pieces.append(blk(kend16, j))
            elif jj < i:
                pieces.append((blk(kend, j) * jnp.exp(ends[ch * n_sub + i - 1] - ends[j])).astype(BF16))
            else:
                pieces.append(c.zero_blk)
        k_groups.append(jnp.concatenate(pieces, axis=0))
    att = _dot_nt(jnp.concatenate(q_groups, axis=1), jnp.concatenate(k_groups, axis=1))
    att = jnp.where(c.same_chunk_causal, att, 0.0)

    if sc > 1:
        q_groups, k_groups = [], []
        for ch in range(1, sc):
            q_groups.append(jnp.concatenate(
                [chk(qs16, c2) if c2 == ch else c.zero_chunk for c2 in range(sc)], axis=0))
            pieces = []
            for c2 in range(sc):
                if c2 == ch - 1:
                    pieces.append(chk(kd16, c2))
                elif c2 < ch:
                    pieces.append((chk(kd, c2) * jnp.exp(sum(tot[c2 + 1:ch]))).astype(BF16))
                else:
                    pieces.append(c.zero_chunk)
            k_groups.append(jnp.concatenate(pieces, axis=0))
        att = att + _dot_nt(jnp.concatenate(q_groups, axis=1), jnp.concatenate(k_groups, axis=1))
    o = jnp.dot(att.astype(BF16), v, preferred_element_type=F32)

    q0 = jnp.concatenate(
        [chk(qs16, 0)] + [(chk(qs, ch) * jnp.exp(sum(tot[:ch]))).astype(BF16) for ch in range(1, sc)], axis=0)
    o = o + _dot_nt(q0, st.astype(BF16))

    k1 = jnp.concatenate(
        [(chk(kd, ch) * jnp.exp(sum(tot[ch + 1:]))).astype(BF16) for ch in range(sc - 1)]
        + [chk(kd16, sc - 1)], axis=0)
    st = st * jnp.exp(sum(tot)) + _dot_tn(v, k1)
    return o, st


def _hgrn_blockwise(stage_ref, oraw_ref, ls, lb, st):
    n_blk = stage_ref.shape[1] // A_SUB
    row = lax.broadcasted_iota(jnp.int32, (A_SUB, A_DK), 0)
    shifts = tuple(1 << i for i in range((A_SUB - 1).bit_length()))

    def body(i, st):
        rows = pl.ds(pl.multiple_of(i * A_SUB, A_SUB), A_SUB)
        fg = _forget_gate(stage_ref[1, rows, ls], lb)
        k = 1.0 - fg
        b = _group_cumsum(jnp.log(fg), row, A_SUB, shifts)
        q = _silu(stage_ref[0, rows, ls])
        v = stage_ref[2, rows, ls]
        o = _dot_nt((q * jnp.exp(b)).astype(BF16), st.astype(BF16))
        o = o + jnp.sum(q * k, axis=-1, keepdims=True) * v
        for d in range(1, A_SUB):
            ok = row >= d
            w = jnp.exp(jnp.where(ok, b - pltpu.roll(b, d, axis=0), 0.0))
            a = jnp.sum(q * pltpu.roll(k, d, axis=0) * w, axis=-1, keepdims=True)
            o = o + jnp.where(ok, a * pltpu.roll(v, d, axis=0), 0.0)
        oraw_ref[rows, ls] = o
        b_end = b[A_SUB - 1:A_SUB, :]
        kd = (k * jnp.exp(b_end - b)).astype(BF16)
        return st * jnp.exp(b_end) + _dot_tn(v.astype(BF16), kd)

    return lax.fori_loop(0, n_blk, body, st)


LB_SAFE = math.exp(-80.0 / A_SUB)


def _mix_prompt_kernel(bucket_ref, relb_ref, sink_ref, x_ref, gpre_ref, w_ref, lb_ref, gh_ref,
                       oa_ref, ob_ref, gab_ref, st_ref, kvwin_ref,
                       bias_ref, st_scr, kvprev_scr, stage_scr, oraw_scr):
    tile = pl.program_id(1)
    at_start = tile == 0

    @pl.when(jnp.logical_and(pl.program_id(0) == 0, at_start))
    def _():
        _build_bias(bucket_ref, relb_ref, bias_ref, WINDOW)

    @pl.when(at_start)
    def _():
        st_scr[...] = jnp.zeros(st_scr.shape, F32)
        kvprev_scr[...] = jnp.zeros(kvprev_scr.shape, F32)

    x = x_ref[...]
    ms = jnp.mean(x * x, axis=-1, keepdims=True)
    u = (x * lax.rsqrt(ms + EPS) * gpre_ref[...]).astype(BF16)
    proj = lambda seg, lo, hi: jnp.dot(u, w_ref[:, IN_OFFS[seg] + lo:IN_OFFS[seg] + hi],
                                       preferred_element_type=F32)

    lb_all = _lower_bound(lb_ref)
    consts = _HgrnConsts(HGRN_SLAB)
    pair_w = 2 * A_DK
    n_phase = A_HEADS // 2
    n_blocks = MIX_TQ // WINDOW
    assert n_blocks == n_phase
    gate_w = D_MODEL // n_phase
    kw = B_KV_HEADS * B_HD

    def hgrn_proj(hp):
        lo, hi = hp * pair_w, (hp + 1) * pair_w
        return tuple(proj(seg, lo, hi) for seg in (0, SEG_AF, 2, 3, 4))

    def hgrn_pair(hp, q2, f2, v2, og2, z2):
        for hh in range(2):
            h = 2 * hp + hh
            ls = slice(hh * A_DK, (hh + 1) * A_DK)
            hs = slice(h * A_DK, (h + 1) * A_DK)
            st = st_scr[h]
            outs = []
            for s0 in range(0, MIX_TQ, HGRN_SLAB):
                rs = slice(s0, s0 + HGRN_SLAB)
                o, st = _hgrn_slab(f2[rs, ls], q2[rs, ls], v2[rs, ls].astype(BF16), lb_all[:, hs], st, consts)
                outs.append(o)
            st_scr[h] = st
            st_ref[h] = st.T
            oa = _hgrn_finish(jnp.concatenate(outs, axis=0), gh_ref[:, hs], og2[:, ls], z2[:, ls])
            oa_ref[:, hs] = oa.astype(BF16)

    key = lax.broadcasted_iota(jnp.int32, (2 * N_KEYS, PAIRS_PER_KV * WINDOW), 0) & (N_KEYS - 1)
    no_prev = jnp.logical_and(key < WINDOW, at_start)

    def swa_block(j, kv, bq, bz):
        r0 = j * WINDOW
        if j == 0:
            kvj = jnp.concatenate([kvprev_scr[...], kv[0:WINDOW, :]], axis=0)
        else:
            kvj = kv[r0 - WINDOW:r0 + WINDOW, :]
        keys = _swa_keys(kvj[:, 0:kw])
        vals_t = [kvj[:, kw:2 * kw].T.astype(BF16)]
        for kvh in range(B_KV_HEADS):
            q = jnp.concatenate([bq[r0:r0 + WINDOW, p * 128:(p + 1) * 128]
                                 for p in range(kvh * PAIRS_PER_KV, (kvh + 1) * PAIRS_PER_KV)], axis=0)
            logits = _dot_nt(keys[kvh], q) + bias_ref[kvh]
            if j == 0:
                logits = jnp.where(no_prev, NEG, logits)
            acc_t = _swa_attend(logits, vals_t, [None], sink_ref, kvh, WINDOW)
            for pp in range(PAIRS_PER_KV):
                cs = slice((kvh * PAIRS_PER_KV + pp) * 128, (kvh * PAIRS_PER_KV + pp + 1) * 128)
                ob_ref[r0:r0 + WINDOW, cs] = (
                    acc_t[pp * WINDOW:(pp + 1) * WINDOW, :] * _silu(bz[r0:r0 + WINDOW, cs])).astype(BF16)

    def hgrn_pair_blockwise(hp, q2, f2, v2, og2, z2):
        stage_scr[0] = q2
        stage_scr[1] = f2
        stage_scr[2] = v2
        for hh in range(2):
            h = 2 * hp + hh
            ls = slice(hh * A_DK, (hh + 1) * A_DK)
            hs = slice(h * A_DK, (h + 1) * A_DK)
            st = _hgrn_blockwise(stage_scr, oraw_scr, ls, lb_all[:, hs], st_scr[h])
            st_scr[h] = st
            st_ref[h] = st.T
            oa = _hgrn_finish(oraw_scr[:, ls], gh_ref[:, hs], og2[:, ls], z2[:, ls])
            oa_ref[:, hs] = oa.astype(BF16)

    def gate_proj(ph):
        for seg, base in ((9, 0), (10, D_MODEL)):
            gab_ref[:, base + ph * gate_w:base + (ph + 1) * gate_w] = proj(
                seg, ph * gate_w, (ph + 1) * gate_w).astype(BF16)

    def swa_proj():
        kv = proj(SEG_BK, 0, 2 * kw)
        bq = (proj(5, 0, D_MODEL) * Q_SCALE).astype(BF16)
        bz = proj(8, 0, D_MODEL)
        return kv, bq, bz

    def keep_window(kv):
        kvprev_scr[...] = kv[MIX_TQ - WINDOW:, :]
        kvwin_ref[...] = kv[MIX_TQ - WINDOW:, :]

    slab_safe = jnp.min(lb_all) >= LB_SAFE

    @pl.when(slab_safe)
    def _():
        kv, bq, bz = swa_proj()
        nxt = hgrn_proj(0)
        for ph in range(n_phase):
            cur = nxt
            if ph + 1 < n_phase:
                nxt = hgrn_proj(ph + 1)
            gate_proj(ph)
            hgrn_pair(ph, *cur)
            swa_block(ph, kv, bq, bz)
        keep_window(kv)

    @pl.when(jnp.logical_not(slab_safe))
    def _():
        for ph in range(n_phase):
            hgrn_pair_blockwise(ph, *hgrn_proj(ph))
            gate_proj(ph)
        kv, bq, bz = swa_proj()
        for j in range(n_blocks):
            swa_block(j, kv, bq, bz)
        keep_window(kv)


def _mix_prompt(bucket, rel_bias, sink, x, g_pre, w, hgrn_lb, hgrn_norm, batch, seq):
    nt = seq // MIX_TQ
    n_lb = hgrn_lb.shape[0]
    kvw = 2 * B_KV_HEADS * B_HD
    smem = pl.BlockSpec(memory_space=pltpu.SMEM)
    tok = lambda width: pl.BlockSpec((MIX_TQ, width), lambda b, t: (b * nt + t, 0))
    return pl.pallas_call(
        _mix_prompt_kernel,
        grid=(batch, nt),
        in_specs=[
            _resident((N_KEYS, WINDOW)),
            smem, smem,
            tok(D_MODEL),
            _resident((1, D_MODEL)),
            _resident(w.shape),
            _resident((n_lb, D_MODEL)),
            _resident((1, D_MODEL)),
        ],
        out_specs=[
            tok(D_MODEL), tok(D_MODEL), tok(2 * D_MODEL),
            pl.BlockSpec((None, A_HEADS, A_DK, A_DV), lambda b, t: (b, 0, 0, 0)),
            pl.BlockSpec((None, WINDOW, kvw), lambda b, t: (b, 0, 0)),
        ],
        out_shape=[
            jax.ShapeDtypeStruct((batch * seq, D_MODEL), BF16),
            jax.ShapeDtypeStruct((batch * seq, D_MODEL), BF16),
            jax.ShapeDtypeStruct((batch * seq, 2 * D_MODEL), BF16),
            jax.ShapeDtypeStruct((batch, A_HEADS, A_DK, A_DV), F32),
            jax.ShapeDtypeStruct((batch, WINDOW, kvw), F32),
        ],
        scratch_shapes=[
            pltpu.VMEM((B_KV_HEADS, 2 * N_KEYS, PAIRS_PER_KV * WINDOW), F32),
            pltpu.VMEM((A_HEADS, A_DV, A_DK), F32),
            pltpu.VMEM((WINDOW, kvw), F32),
            pltpu.VMEM((3, MIX_TQ, 2 * A_DK), F32),
            pltpu.VMEM((MIX_TQ, 2 * A_DK), F32),
        ],
        compiler_params=pltpu.CompilerParams(
            dimension_semantics=("arbitrary", "arbitrary"), vmem_limit_bytes=VMEM_LIMIT),
        name="mix_prompt",
    )(bucket, rel_bias, sink, x, g_pre, w, hgrn_lb, hgrn_norm)


def _outproj_kernel(oa_ref, ob_ref, ga_ref, gb_ref, x_ref, p_ref, wpa_ref, wpb_ref, wo_ref, gpost_ref,
                    wple_ref, wg_ref, y_ref):
    a = jnp.dot(oa_ref[...], wpa_ref[...], preferred_element_type=F32)
    b = jnp.dot(ob_ref[...], wpb_ref[...], preferred_element_type=F32)
    m = _sigmoid(ga_ref[...].astype(F32)) * a + _sigmoid(gb_ref[...].astype(F32)) * b
    y = jnp.dot(m.astype(BF16), wo_ref[...], preferred_element_type=F32)
    y = y * lax.rsqrt(jnp.mean(y * y, axis=-1, keepdims=True) + EPS) * gpost_ref[...]
    x1 = x_ref[...] + y
    gate = _sigmoid(jnp.dot(x1.astype(BF16), wg_ref[...], preferred_element_type=F32))
    e = jnp.dot(p_ref[...].astype(BF16), wple_ref[...], preferred_element_type=F32) * gate
    y_ref[...] = x1 + e


def _outproj(oa, ob, gates, ga_col, gb_col, x, p, wpa, wpb, wo, gpost, wple, wg, tm):
    n = x.shape[0]
    ple = p.shape[1]
    tok = lambda w, c=0: pl.BlockSpec((tm, w), lambda i, c=c: (i, c))
    return pl.pallas_call(
        _outproj_kernel,
        grid=(n // tm,),
        in_specs=[
            tok(D_MODEL), tok(D_MODEL), tok(D_MODEL, ga_col), tok(D_MODEL, gb_col), tok(D_MODEL), tok(ple),
            _resident((D_MODEL, D_MODEL)), _resident((D_MODEL, D_MODEL)), _resident((D_MODEL, D_MODEL)),
            _resident((1, D_MODEL)), _resident((ple, D_MODEL)), _resident((D_MODEL, D_MODEL)),
        ],
        out_specs=tok(D_MODEL),
        out_shape=jax.ShapeDtypeStruct((n, D_MODEL), F32),
        compiler_params=pltpu.CompilerParams(
            dimension_semantics=("arbitrary",), vmem_limit_bytes=VMEM_LIMIT),
        name="outproj",
    )(oa, ob, gates, gates, x, p, wpa, wpb, wo, gpost, wple, wg)


def _rel_bucket(rel):
    n = np.maximum(rel, 0)
    max_exact = REL_BUCKETS // 2
    nf = np.maximum(n, 1).astype(np.float32)
    scaled = (np.log(nf / np.float32(max_exact)) / np.float32(math.log(REL_MAX_DIST / max_exact))
              * np.float32(REL_BUCKETS - max_exact))
    frac = scaled - np.floor(scaled)
    inside = (n > max_exact) & (n < REL_MAX_DIST)
    assert np.all((frac[inside] > 1e-3) & (frac[inside] < 1.0 - 1e-3))
    large = np.minimum(max_exact + scaled.astype(np.int32), REL_BUCKETS - 1)
    return np.where(n < max_exact, n, large)


def _bucket_table(q_pos, k_pos, k_valid):
    rel = q_pos[:, None] - k_pos[None, :]
    ok = (rel >= 0) & (rel < WINDOW) & np.broadcast_to(k_valid, rel.shape)
    return jnp.asarray(np.where(ok, _rel_bucket(rel), -1).astype(np.int32).T)


TM_PROJ = 512
SAMPLE_BB = 8


def _layer(xp, xs, s_hgrn, win_k, win_v, pp, ps, norm_pre, w_in, hgrn_lb, hgrn_norm, attn_sink,
           rel_bias, w_pa, w_pb, w_o, norm_post, w_ple, w_ple_gate):
    batch, seq, _ = xp.shape
    dbatch, t_len, _ = xs.shape
    kw = B_KV_HEADS * B_HD

    w16 = w_in.astype(BF16)
    g_pre = norm_pre.reshape(1, D_MODEL)
    g_post = norm_post.reshape(1, D_MODEL)
    g_hgrn = hgrn_norm.reshape(1, D_MODEL)
    wpa, wpb, wo = w_pa.astype(BF16), w_pb.astype(BF16), w_o.astype(BF16)
    wple, wg = w_ple.astype(BF16), w_ple_gate.astype(BF16)

    xp2 = xp.reshape(batch * seq, D_MODEL)
    xs2 = xs.reshape(dbatch * t_len, D_MODEL)
    k_all = np.arange(N_KEYS)
    bucket_p = _bucket_table(np.arange(WINDOW) + WINDOW, k_all, np.ones((N_KEYS,), bool))
    rows_s = SAMPLE_BB * t_len
    q_row = np.arange(rows_s)
    new = k_all - WINDOW
    k_pos_s = np.where(new < 0, k_all, WINDOW + new % t_len)
    own = (new[None, :] < 0) | ((new[None, :] < rows_s) & (new[None, :] // t_len == q_row[:, None] // t_len))
    bucket_s = _bucket_table(WINDOW + q_row % t_len, k_pos_s, own)

    oa_p, ob_p, gab_p, st_p, kvwin_p = _mix_prompt(bucket_p, rel_bias, attn_sink, xp2, g_pre, w16,
                                                   hgrn_lb, g_hgrn, batch, seq)

    to_minor = lambda c: jnp.transpose(c, (0, 2, 3, 1)).reshape(dbatch, kw, WINDOW)
    from_minor = lambda c: jnp.transpose(c.reshape(dbatch, B_KV_HEADS, B_HD, WINDOW), (0, 3, 1, 2))
    pm_s, f_s, kv_s = _inproj(xs2, g_pre, w16, min(TM_PROJ, dbatch * t_len))
    oa_s, st_s, ob_s, nk_s, nv_s = _mix_sample(bucket_s, rel_bias, attn_sink, pm_s, f_s, kv_s, hgrn_lb, g_hgrn,
                                               s_hgrn, to_minor(win_k), to_minor(win_v),
                                               dbatch, t_len, SAMPLE_BB)

    y_p = _outproj(oa_p, ob_p, gab_p, 0, 1, xp2, pp.reshape(batch * seq, -1),
                   wpa, wpb, wo, g_post, wple, wg, TM_PROJ)
    y_s = _outproj(oa_s, ob_s, pm_s, G_GA, G_GB, xs2, ps.reshape(dbatch * t_len, -1),
                   wpa, wpb, wo, g_post, wple, wg, min(TM_PROJ, dbatch * t_len))

    k_win_p = kvwin_p[:, :, 0:kw].reshape(batch, WINDOW, B_KV_HEADS, B_HD)
    v_win_p = kvwin_p[:, :, kw:].reshape(batch, WINDOW, B_KV_HEADS, B_HD)
    return (y_p.reshape(batch, seq, D_MODEL), y_s.reshape(dbatch, t_len, D_MODEL), st_p, st_s,
            k_win_p, v_win_p,
            from_minor(nk_s), from_minor(nv_s))


def kernel(x_prompt, x_sample, state_hgrn, cache_swa_k, cache_swa_v, p_prompt, p_sample, norm_pre, w_in,
           hgrn_lb, hgrn_norm, attn_sink, rel_bias, w_pa, w_pb, w_o, norm_post, w_ple, w_ple_gate):
    depth = w_in.shape[0]
    assert depth == 1, "the forget-gate lower bound is implemented for a single layer"
    xp, xs = x_prompt, x_sample
    outs = []
    for l in range(depth):
        res = _layer(xp, xs, state_hgrn[l], cache_swa_k[l], cache_swa_v[l], p_prompt[l], p_sample[l],
                     norm_pre[l], w_in[l], hgrn_lb, hgrn_norm[l], attn_sink[l], rel_bias,
                     w_pa[l], w_pb[l], w_o[l], norm_post[l], w_ple[l], w_ple_gate[l])
        xp, xs = res[0], res[1]
        outs.append(res[2:])
    stack = lambda i: jnp.stack([o[i] for o in outs])
    return (xp, xs, stack(0), stack(1), stack(2), stack(3), stack(4), stack(5))
```

```python
import functools
import math

import jax
import jax.numpy as jnp
import numpy as np
from jax import lax
from jax.experimental import pallas as pl
from jax.experimental.pallas import tpu as pltpu

F32 = jnp.float32
BF16 = jnp.bfloat16

D_MODEL = 1024
A_HEADS = 8
A_DK = 128
A_DV = 128
A_CHUNK = 64
A_SUB = 16
B_HEADS = 16
B_KV_HEADS = 2
B_HD = 64
WINDOW = 128
REL_BUCKETS = 32
REL_MAX_DIST = 128
EPS = 1e-6
NEG = float("-inf")
LOG2E = math.log2(math.e)
Q_SCALE = B_HD ** -0.5 * LOG2E

G_AQ, G_AI, G_AOG, G_AZ, G_BQ, G_BZ, G_GA, G_GB = range(8)
N_GROUPS = 8

VMEM_LIMIT = 56 * 1024 * 1024


def _sigmoid(x):
    return 0.5 * jnp.tanh(0.5 * x) + 0.5


def _silu(x):
    h = 0.5 * x
    return h * jnp.tanh(h) + h


def _resident(shape):
    nd = len(shape)
    return pl.BlockSpec(shape, lambda *_: (0,) * nd, pipeline_mode=pl.Buffered(1))


IN_OFFS = (0, 1024, 2048, 3072, 4096, 5120, 6144, 6272, 6400, 7424, 8448, 9472)
SLAB_SEGS = (0, 2, 3, 4, 5, 8, 9, 10)
SEG_AF, SEG_BK, SEG_BV = 1, 6, 7


def _inproj_kernel(x_ref, g_ref, w_ref, pm_ref, f_ref, kv_ref):
    x = x_ref[...]
    ms = jnp.mean(x * x, axis=-1, keepdims=True)
    u = (x * lax.rsqrt(ms + EPS) * g_ref[...]).astype(BF16)
    proj = lambda lo, hi: jnp.dot(u, w_ref[:, lo:hi], preferred_element_type=F32)
    for c, seg in enumerate(SLAB_SEGS):
        pm_ref[:, c * D_MODEL:(c + 1) * D_MODEL] = proj(IN_OFFS[seg], IN_OFFS[seg + 1]).astype(BF16)
    f_ref[...] = proj(IN_OFFS[SEG_AF], IN_OFFS[SEG_AF + 1])
    kv_ref[...] = proj(IN_OFFS[SEG_BK], IN_OFFS[SEG_BV + 1])


def _inproj(x, g, w, tm):
    n = x.shape[0]
    nm = N_GROUPS * D_MODEL
    nkv = IN_OFFS[SEG_BV + 1] - IN_OFFS[SEG_BK]
    return pl.pallas_call(
        _inproj_kernel,
        grid=(n // tm,),
        in_specs=[
            pl.BlockSpec((tm, D_MODEL), lambda i: (i, 0)),
            _resident((1, D_MODEL)),
            _resident(w.shape),
        ],
        out_specs=[
            pl.BlockSpec((tm, nm), lambda i: (i, 0)),
            pl.BlockSpec((tm, D_MODEL), lambda i: (i, 0)),
            pl.BlockSpec((tm, nkv), lambda i: (i, 0)),
        ],
        out_shape=[
            jax.ShapeDtypeStruct((n, nm), BF16),
            jax.ShapeDtypeStruct((n, D_MODEL), F32),
            jax.ShapeDtypeStruct((n, nkv), F32),
        ],
        compiler_params=pltpu.CompilerParams(
            dimension_semantics=("arbitrary",), vmem_limit_bytes=VMEM_LIMIT),
        name="inproj",
    )(x, g, w)


def _lower_bound(lb_ref):
    l = lb_ref[...]
    m = jnp.max(l, axis=0, keepdims=True)
    e = jnp.exp(l - m)
    return e[0:1, :] / jnp.sum(e, axis=0, keepdims=True)


def _forget_gate(f_pre, lb):
    return lb + (1.0 - lb) / (1.0 + jnp.exp(-f_pre))


def _group_cumsum(x, row, period, shifts):
    pos = row & (period - 1)
    for sh in shifts:
        x = x + jnp.where(pos >= sh, pltpu.roll(x, sh, axis=0), 0.0)
    return x


def _hgrn_finish(o, g, og_pre, z_pre):
    o = o * lax.rsqrt(jnp.mean(o * o, axis=-1, keepdims=True) + EPS)
    return o * g * _sigmoid(og_pre) * _silu(z_pre)


def _dot_nt(a, b):
    return lax.dot_general(a, b, (((1,), (1,)), ((), ())), preferred_element_type=F32)


def _dot_tn(a, b):
    return lax.dot_general(a, b, (((0,), (0,)), ((), ())), preferred_element_type=F32)


def _hgrn_sample_kernel(q_ref, v_ref, og_ref, z_ref, f_ref, lb_ref, g_ref, s0_ref, oa_ref, s_ref,
                        *, bb, t_len):
    rows = bb * t_len
    lb_all = _lower_bound(lb_ref)
    row = lax.broadcasted_iota(jnp.int32, (rows, A_DK), 0)
    pos = row & (t_len - 1)
    grp = row >> (t_len.bit_length() - 1)
    shifts = tuple(1 << i for i in range((t_len - 1).bit_length()))
    n_tail = A_DK - rows
    sel_row = lax.broadcasted_iota(jnp.int32, (n_tail, 2 * bb * A_DV), 0)
    sel_blk = lax.broadcasted_iota(jnp.int32, (n_tail, 2 * bb * A_DV), 1) >> (A_DV.bit_length() - 1)
    picks = jnp.logical_and(sel_row < 2 * bb, sel_blk == bb + (sel_row & (bb - 1)))
    rhs_tail = jnp.where(picks, 1.0, 0.0).astype(BF16)
    lhs_pad = jnp.zeros((n_tail - 2 * bb, A_DK), F32)
    v_pad = jnp.zeros((rows, bb * A_DV), F32)
    for h in range(A_HEADS):
        ls = slice(h * A_DK, (h + 1) * A_DK)
        lb = lb_all[:, ls]
        fg = _forget_gate(f_ref[:, ls], lb)
        logf = jnp.log(fg)
        k = 1.0 - fg
        b = _group_cumsum(logf, row, t_len, shifts)
        q = _silu(q_ref[:, ls].astype(F32))
        v = v_ref[:, ls].astype(F32)

        o = jnp.zeros((rows, A_DV), F32)
        for d in range(t_len):
            if d == 0:
                a = jnp.sum(q * k, axis=-1, keepdims=True)
                o = o + a * v
            else:
                w = jnp.exp(jnp.where(pos >= d, b - pltpu.roll(b, d, axis=0), 0.0))
                a = jnp.sum(q * pltpu.roll(k, d, axis=0) * w, axis=-1, keepdims=True)
                o = o + jnp.where(pos >= d, a * pltpu.roll(v, d, axis=0), 0.0)

        b_last = b
        for d in range(1, t_len):
            b_last = jnp.where(pos == t_len - 1 - d, pltpu.roll(b, rows - d, axis=0), b_last)
        e_last = jnp.exp(b_last)
        qs = q * jnp.exp(b)
        kd = k * jnp.exp(b_last - b)
        e_rows = jnp.concatenate([e_last[j * t_len:j * t_len + 1, :] for j in range(bb)], axis=0)
        e_hi = e_rows.astype(BF16).astype(F32)
        lhs_t = jnp.concatenate([kd, e_hi, e_rows - e_hi, lhs_pad], axis=0).T.astype(BF16)
        v_rows = jnp.concatenate([jnp.where(grp == j, v, 0.0) for j in range(bb)] + [v_pad], axis=1)
        res = jnp.dot(lhs_t, jnp.concatenate([v_rows.astype(BF16), rhs_tail], axis=0),
                      preferred_element_type=F32)
        for j in range(bb):
            s0 = s0_ref[j, h]
            o = o + jnp.dot(jnp.where(grp == j, qs, 0.0).astype(BF16), s0.astype(BF16),
                            preferred_element_type=F32)
            s_ref[j, h] = (res[:, (bb + j) * A_DV:(bb + j + 1) * A_DV] * s0
                           + res[:, j * A_DV:(j + 1) * A_DV])

        oa = _hgrn_finish(o, g_ref[:, ls], og_ref[:, ls].astype(F32), z_ref[:, ls].astype(F32))
        oa_ref[:, ls] = oa.astype(BF16)


N_PAIRS = B_HEADS // 2
PAIRS_PER_KV = N_PAIRS // B_KV_HEADS
N_KEYS = 2 * WINDOW


def _build_bias(bucket_ref, relb_ref, bias_ref, r):
    bucket = bucket_ref[...]
    for h in range(B_HEADS):
        def body(kb, acc, h=h):
            return jnp.where(bucket == kb, relb_ref[kb, h] * LOG2E, acc)
        tab = lax.fori_loop(0, REL_BUCKETS, body, jnp.full(bucket.shape, NEG, F32))
        pair, parity = divmod(h, 2)
        kv, pp = divmod(pair, PAIRS_PER_KV)
        bias_ref[kv, parity * N_KEYS:(parity + 1) * N_KEYS, pp * r:(pp + 1) * r] = tab


def _swa_keys(kk):
    lo = lax.broadcasted_iota(jnp.int32, (N_KEYS, 2 * B_HD), 1) < B_HD
    kk_sw = pltpu.roll(kk, B_HD, axis=1)
    slabs = []
    for kv in range(B_KV_HEADS):
        k_lo, k_hi = (kk, kk_sw) if kv == 0 else (kk_sw, kk)
        slabs.append(jnp.concatenate([jnp.where(lo, k_lo, 0.0), jnp.where(lo, 0.0, k_hi)], axis=0).astype(BF16))
    return slabs


def _swa_attend(logits, vals_t, col_masks, sink_ref, kv, r):
    width = PAIRS_PER_KV * r
    lane_pp = lax.broadcasted_iota(jnp.int32, (1, width), 1) >> (r.bit_length() - 1)
    halves = []
    for parity in range(2):
        sink = jnp.zeros((1, width), F32)
        for pp in range(PAIRS_PER_KV):
            sink = jnp.where(lane_pp == pp, sink_ref[(kv * PAIRS_PER_KV + pp) * 2 + parity] * LOG2E, sink)
        l = logits[parity * N_KEYS:(parity + 1) * N_KEYS, :]
        m = jnp.maximum(jnp.max(l, axis=0, keepdims=True), sink)
        p = jnp.exp2(l - m)
        denom = jnp.sum(p, axis=0, keepdims=True) + jnp.exp2(sink - m)
        p16 = p.astype(BF16)
        v_rows = jnp.concatenate([v_t[kv * B_HD:(kv + 1) * B_HD, :] for v_t in vals_t], axis=0)
        pv_all = jnp.dot(v_rows, p16, preferred_element_type=F32)
        pv = None
        for n, mask in enumerate(col_masks):
            part = pv_all[n * B_HD:(n + 1) * B_HD, :]
            pv = part if mask is None else jnp.where(mask, part, 0.0 if pv is None else pv)
        halves.append(pv / denom)
    return jnp.concatenate(halves, axis=0).T


def _swa_sample_kernel(bucket_ref, relb_ref, sink_ref, q_ref, z_ref, kvn_ref, ck_ref, cv_ref,
                       ob_ref, nk_ref, nv_ref, bias_ref, *, bb, t_len):
    rows = bb * t_len
    width = PAIRS_PER_KV * rows

    @pl.when(pl.program_id(0) == 0)
    def _():
        _build_bias(bucket_ref, relb_ref, bias_ref, rows)

    kw = B_KV_HEADS * B_HD
    t_shift = t_len.bit_length() - 1
    seq_of_col = lambda n: (lax.broadcasted_iota(jnp.int32, (n, width), 1) & (rows - 1)) >> t_shift
    seq_l = seq_of_col(2 * N_KEYS)
    seq_v = seq_of_col(B_HD)
    q_pairs = [(q_ref[:, p * 128:(p + 1) * 128].astype(F32) * Q_SCALE).astype(BF16) for p in range(N_PAIRS)]
    q_kv = [jnp.concatenate(q_pairs[kv * PAIRS_PER_KV:(kv + 1) * PAIRS_PER_KV], axis=0)
            for kv in range(B_KV_HEADS)]
    kn = kvn_ref[:, 0:kw]
    vn = kvn_ref[:, kw:2 * kw]
    n_new = N_KEYS - WINDOW
    assert n_new == WINDOW
    kn_all = jnp.concatenate([kn, jnp.zeros((n_new - rows, kw), F32)], axis=0)
    vn_all = jnp.concatenate([vn, jnp.zeros((n_new - rows, kw), F32)], axis=0)
    kn_t = kn_all.T
    vn_t = vn_all.T
    keep = lax.broadcasted_iota(jnp.int32, (kw, WINDOW), 1) < WINDOW - t_len
    logits = [None] * B_KV_HEADS
    vals_t, col_masks = [], []
    for j in range(bb):
        ck_t = ck_ref[j]
        cv_t = cv_ref[j]
        keys = _swa_keys(jnp.concatenate([ck_t.T, kn_all], axis=0))
        for kv in range(B_KV_HEADS):
            lg = _dot_nt(keys[kv], q_kv[kv])
            logits[kv] = jnp.where(seq_l == j, lg, 0.0 if logits[kv] is None else logits[kv])
        vals_t.append(jnp.concatenate([cv_t, vn_t], axis=1).astype(BF16))
        col_masks.append(seq_v == j)
        to_end = WINDOW - (j + 1) * t_len
        nk_ref[j] = jnp.where(keep, pltpu.roll(ck_t, WINDOW - t_len, axis=1), pltpu.roll(kn_t, to_end, axis=1))
        nv_ref[j] = jnp.where(keep, pltpu.roll(cv_t, WINDOW - t_len, axis=1), pltpu.roll(vn_t, to_end, axis=1))
    for kv in range(B_KV_HEADS):
        acc_t = _swa_attend(logits[kv] + bias_ref[kv], vals_t, col_masks, sink_ref, kv, rows)
        for pp in range(PAIRS_PER_KV):
            p = kv * PAIRS_PER_KV + pp
            z = z_ref[:, p * 128:(p + 1) * 128].astype(F32)
            ob_ref[:, p * 128:(p + 1) * 128] = (acc_t[pp * rows:(pp + 1) * rows, :] * _silu(z)).astype(BF16)


STATE_SLOTS = 3


def _mix_sample_kernel(bucket_ref, relb_ref, sink_ref, aq_ref, ai_ref, aog_ref, az_ref, f_ref, lb_ref, g_ref,
                       s0_hbm, bq_ref, bz_ref, kvn_ref, ck_ref, cv_ref,
                       oa_ref, s_ref, ob_ref, nk_ref, nv_ref, bias_ref, s0_ring, s0_sem, *, bb, t_len):
    step = pl.program_id(0)
    n_steps = pl.num_programs(0)

    def fetch(s):
        slot = s % STATE_SLOTS
        return pltpu.make_async_copy(s0_hbm.at[pl.ds(s * bb, bb)], s0_ring.at[slot], s0_sem.at[slot])

    @pl.when(step == 0)
    def _():
        for s in range(STATE_SLOTS - 1):
            fetch(s).start()

    @pl.when(step + STATE_SLOTS - 1 < n_steps)
    def _():
        fetch(step + STATE_SLOTS - 1).start()

    fetch(step).wait()
    s0_ref = s0_ring.at[step % STATE_SLOTS]
    _hgrn_sample_kernel(aq_ref, ai_ref, aog_ref, az_ref, f_ref, lb_ref, g_ref, s0_ref, oa_ref, s_ref,
                        bb=bb, t_len=t_len)
    _swa_sample_kernel(bucket_ref, relb_ref, sink_ref, bq_ref, bz_ref, kvn_ref, ck_ref, cv_ref,
                       ob_ref, nk_ref, nv_ref, bias_ref, bb=bb, t_len=t_len)


def _mix_sample(bucket, rel_bias, sink, pm, f, kvn, hgrn_lb, hgrn_norm, s0, cache_k, cache_v,
                batch, t_len, bb):
    rows = bb * t_len
    kvw = kvn.shape[1]
    kw = kvw // 2
    n_lb = hgrn_lb.shape[0]
    assert batch // bb >= STATE_SLOTS - 1
    smem = pl.BlockSpec(memory_space=pltpu.SMEM)
    col = lambda grp: pl.BlockSpec((rows, D_MODEL), lambda i, grp=grp: (i, grp))
    tok = pl.BlockSpec((rows, D_MODEL), lambda i: (i, 0))
    st_spec = pl.BlockSpec((bb, A_HEADS, A_DK, A_DV), lambda i: (i, 0, 0, 0))
    cache_spec = pl.BlockSpec((bb, kw, WINDOW), lambda i: (i, 0, 0))
    return pl.pallas_call(
        functools.partial(_mix_sample_kernel, bb=bb, t_len=t_len),
        grid=(batch // bb,),
        in_specs=[
            _resident((N_KEYS, rows)),
            smem, smem,
            col(G_AQ), col(G_AI), col(G_AOG), col(G_AZ),
            tok,
            _resident((n_lb, D_MODEL)),
            _resident((1, D_MODEL)),
            pl.BlockSpec(memory_space=pl.ANY),
            col(G_BQ), col(G_BZ),
            pl.BlockSpec((rows, kvw), lambda i: (i, 0)),
            cache_spec, cache_spec,
        ],
        out_specs=[tok, st_spec, tok, cache_spec, cache_spec],
        out_shape=[
            jax.ShapeDtypeStruct((batch * t_len, D_MODEL), BF16),
            jax.ShapeDtypeStruct((batch, A_HEADS, A_DK, A_DV), F32),
            jax.ShapeDtypeStruct((batch * t_len, D_MODEL), BF16),
            jax.ShapeDtypeStruct((batch, kw, WINDOW), F32),
            jax.ShapeDtypeStruct((batch, kw, WINDOW), F32),
        ],
        scratch_shapes=[
            pltpu.VMEM((B_KV_HEADS, 2 * N_KEYS, PAIRS_PER_KV * rows), F32),
            pltpu.VMEM((STATE_SLOTS, bb, A_HEADS, A_DK, A_DV), F32),
            pltpu.SemaphoreType.DMA((STATE_SLOTS,)),
        ],
        compiler_params=pltpu.CompilerParams(
            dimension_semantics=("arbitrary",), vmem_limit_bytes=VMEM_LIMIT),
        name="mix_sample",
    )(bucket, rel_bias, sink, pm, pm, pm, pm, f, hgrn_lb, hgrn_norm, s0, pm, pm, kvn, cache_k, cache_v)


MIX_TQ = 512
HGRN_SLAB = 256


class _HgrnConsts:
    def __init__(self, slab):
        shift = A_CHUNK.bit_length() - 1
        self.row = lax.broadcasted_iota(jnp.int32, (slab, A_DK), 0)
        rs = lax.broadcasted_iota(jnp.int32, (slab, slab), 0)
        cs = lax.broadcasted_iota(jnp.int32, (slab, slab), 1)
        self.same_chunk_causal = jnp.logical_and((rs >> shift) == (cs >> shift), cs <= rs)
        self.zero_row = jnp.zeros((1, A_DK), F32)
        self.zero_blk = jnp.zeros((A_SUB, A_DK), BF16)
        self.zero_chunk = jnp.zeros((A_CHUNK, A_DK), BF16)


def _hgrn_slab(f_pre, q_pre, v, lb, st, c):
    slab = f_pre.shape[0]
    n_sub = A_CHUNK // A_SUB
    sc = slab // A_CHUNK
    rep = lambda r, n: jnp.broadcast_to(r, (n, A_DK))
    fg = _forget_gate(f_pre, lb)
    logf = jnp.log(fg)
    k = 1.0 - fg
    b = _group_cumsum(logf, c.row, A_CHUNK, (1, 2, 4, 8, 16, 32))
    q = _silu(q_pre)

    ends = [b[A_SUB * j + A_SUB - 1:A_SUB * (j + 1), :] for j in range(sc * n_sub)]
    endrow = jnp.concatenate([rep(e, A_SUB) for e in ends], axis=0)
    prevrow = jnp.concatenate(
        [rep(c.zero_row if j % n_sub == 0 else ends[j - 1], A_SUB) for j in range(sc * n_sub)], axis=0)
    tot = [ends[n_sub * ch + n_sub - 1] for ch in range(sc)]
    lastrow = jnp.concatenate([rep(t, A_CHUNK) for t in tot], axis=0)
    kend = k * jnp.exp(endrow - b)
    qd = q * jnp.exp(b - prevrow)
    qs = qd * jnp.exp(prevrow)
    kd = kend * jnp.exp(lastrow - endrow)
    kdiag16 = (k * jnp.exp(prevrow - b)).astype(BF16)
    qd16 = qd.astype(BF16)
    kend16 = kend.astype(BF16)
    qs16 = qs.astype(BF16)
    kd16 = kd.astype(BF16)

    blk = lambda a, j: a[A_SUB * j:A_SUB * (j + 1), :]
    chk = lambda a, ch: a[A_CHUNK * ch:A_CHUNK * (ch + 1), :]

    q_groups, k_groups = [], []
    for i in range(n_sub):
        q_groups.append(jnp.concatenate(
            [blk(qd16, j) if j % n_sub == i else c.zero_blk for j in range(sc * n_sub)], axis=0))
        pieces = []
        for j in range(sc * n_sub):
            ch, jj = divmod(j, n_sub)
            if jj == i:
                pieces.append(blk(kdiag16, j))
            elif jj == i - 1:
                pieces.append(blk(kend16, j))
            elif jj < i:
                pieces.append((blk(kend, j) * jnp.exp(ends[ch * n_sub + i - 1] - ends[j])).astype(BF16))
            else:
                pieces.append(c.zero_blk)
        k_groups.append(jnp.concatenate(pieces, axis=0))
    att = _dot_nt(jnp.concatenate(q_groups, axis=1), jnp.concatenate(k_groups, axis=1))
    att = jnp.where(c.same_chunk_causal, att, 0.0)

    if sc > 1:
        q_groups, k_groups = [], []
        for ch in range(1, sc):
            q_groups.append(jnp.concatenate(
                [chk(qs16, c2) if c2 == ch else c.zero_chunk for c2 in range(sc)], axis=0))
            pieces = []
            for c2 in range(sc):
                if c2 == ch - 1:
                    pieces.append(chk(kd16, c2))
                elif c2 < ch:
                    pieces.append((chk(kd, c2) * jnp.exp(sum(tot[c2 + 1:ch]))).astype(BF16))
                else:
                    pieces.append(c.zero_chunk)
            k_groups.append(jnp.concatenate(pieces, axis=0))
        att = att + _dot_nt(jnp.concatenate(q_groups, axis=1), jnp.concatenate(k_groups, axis=1))
    o = jnp.dot(att.astype(BF16), v, preferred_element_type=F32)

    q0 = jnp.concatenate(
        [chk(qs16, 0)] + [(chk(qs, ch) * jnp.exp(sum(tot[:ch]))).astype(BF16) for ch in range(1, sc)], axis=0)
    o = o + _dot_nt(q0, st.astype(BF16))

    k1 = jnp.concatenate(
        [(chk(kd, ch) * jnp.exp(sum(tot[ch + 1:]))).astype(BF16) for ch in range(sc - 1)]
        + [chk(kd16, sc - 1)], axis=0)
    st = st * jnp.exp(sum(tot)) + _dot_tn(v, k1)
    return o, st


def _hgrn_blockwise(stage_ref, oraw_ref, ls, lb, st):
    n_blk = stage_ref.shape[1] // A_SUB
    row = lax.broadcasted_iota(jnp.int32, (A_SUB, A_DK), 0)
    shifts = tuple(1 << i for i in range((A_SUB - 1).bit_length()))

    def body(i, st):
        rows = pl.ds(pl.multiple_of(i * A_SUB, A_SUB), A_SUB)
        fg = _forget_gate(stage_ref[1, rows, ls], lb)
        k = 1.0 - fg
        b = _group_cumsum(jnp.log(fg), row, A_SUB, shifts)
        q = _silu(stage_ref[0, rows, ls])
        v = stage_ref[2, rows, ls]
        o = _dot_nt((q * jnp.exp(b)).astype(BF16), st.astype(BF16))
        o = o + jnp.sum(q * k, axis=-1, keepdims=True) * v
        for d in range(1, A_SUB):
            ok = row >= d
            w = jnp.exp(jnp.where(ok, b - pltpu.roll(b, d, axis=0), 0.0))
            a = jnp.sum(q * pltpu.roll(k, d, axis=0) * w, axis=-1, keepdims=True)
            o = o + jnp.where(ok, a * pltpu.roll(v, d, axis=0), 0.0)
        oraw_ref[rows, ls] = o
        b_end = b[A_SUB - 1:A_SUB, :]
        kd = (k * jnp.exp(b_end - b)).astype(BF16)
        return st * jnp.exp(b_end) + _dot_tn(v.astype(BF16), kd)

    return lax.fori_loop(0, n_blk, body, st)


LB_SAFE = math.exp(-80.0 / A_SUB)


def _mix_prompt_kernel(bucket_ref, relb_ref, sink_ref, x_ref, gpre_ref, w_ref, lb_ref, gh_ref,
                       oa_ref, ob_ref, gab_ref, st_ref, kvwin_ref,
                       bias_ref, st_scr, kvprev_scr, stage_scr, oraw_scr):
    tile = pl.program_id(1)
    at_start = tile == 0

    @pl.when(jnp.logical_and(pl.program_id(0) == 0, at_start))
    def _():
        _build_bias(bucket_ref, relb_ref, bias_ref, WINDOW)

    @pl.when(at_start)
    def _():
        st_scr[...] = jnp.zeros(st_scr.shape, F32)
        kvprev_scr[...] = jnp.zeros(kvprev_scr.shape, F32)

    x = x_ref[...]
    ms = jnp.mean(x * x, axis=-1, keepdims=True)
    u = (x * lax.rsqrt(ms + EPS) * gpre_ref[...]).astype(BF16)
    proj = lambda seg, lo, hi: jnp.dot(u, w_ref[:, IN_OFFS[seg] + lo:IN_OFFS[seg] + hi],
                                       preferred_element_type=F32)

    lb_all = _lower_bound(lb_ref)
    consts = _HgrnConsts(HGRN_SLAB)
    pair_w = 2 * A_DK
    n_phase = A_HEADS // 2
    n_blocks = MIX_TQ // WINDOW
    assert n_blocks == n_phase
    gate_w = D_MODEL // n_phase
    kw = B_KV_HEADS * B_HD

    def hgrn_proj(hp):
        lo, hi = hp * pair_w, (hp + 1) * pair_w
        return tuple(proj(seg, lo, hi) for seg in (0, SEG_AF, 2, 3, 4))

    def hgrn_pair(hp, q2, f2, v2, og2, z2):
        for hh in range(2):
            h = 2 * hp + hh
            ls = slice(hh * A_DK, (hh + 1) * A_DK)
            hs = slice(h * A_DK, (h + 1) * A_DK)
            st = st_scr[h]
            outs = []
            for s0 in range(0, MIX_TQ, HGRN_SLAB):
                rs = slice(s0, s0 + HGRN_SLAB)
                o, st = _hgrn_slab(f2[rs, ls], q2[rs, ls], v2[rs, ls].astype(BF16), lb_all[:, hs], st, consts)
                outs.append(o)
            st_scr[h] = st
            st_ref[h] = st.T
            oa = _hgrn_finish(jnp.concatenate(outs, axis=0), gh_ref[:, hs], og2[:, ls], z2[:, ls])
            oa_ref[:, hs] = oa.astype(BF16)

    key = lax.broadcasted_iota(jnp.int32, (2 * N_KEYS, PAIRS_PER_KV * WINDOW), 0) & (N_KEYS - 1)
    no_prev = jnp.logical_and(key < WINDOW, at_start)

    def swa_block(j, kv, bq, bz):
        r0 = j * WINDOW
        if j == 0:
            kvj = jnp.concatenate([kvprev_scr[...], kv[0:WINDOW, :]], axis=0)
        else:
            kvj = kv[r0 - WINDOW:r0 + WINDOW, :]
        keys = _swa_keys(kvj[:, 0:kw])
        vals_t = [kvj[:, kw:2 * kw].T.astype(BF16)]
        for kvh in range(B_KV_HEADS):
            q = jnp.concatenate([bq[r0:r0 + WINDOW, p * 128:(p + 1) * 128]
                                 for p in range(kvh * PAIRS_PER_KV, (kvh + 1) * PAIRS_PER_KV)], axis=0)
            logits = _dot_nt(keys[kvh], q) + bias_ref[kvh]
            if j == 0:
                logits = jnp.where(no_prev, NEG, logits)
            acc_t = _swa_attend(logits, vals_t, [None], sink_ref, kvh, WINDOW)
            for pp in range(PAIRS_PER_KV):
                cs = slice((kvh * PAIRS_PER_KV + pp) * 128, (kvh * PAIRS_PER_KV + pp + 1) * 128)
                ob_ref[r0:r0 + WINDOW, cs] = (
                    acc_t[pp * WINDOW:(pp + 1) * WINDOW, :] * _silu(bz[r0:r0 + WINDOW, cs])).astype(BF16)

    def hgrn_pair_blockwise(hp, q2, f2, v2, og2, z2):
        stage_scr[0] = q2
        stage_scr[1] = f2
        stage_scr[2] = v2
        for hh in range(2):
            h = 2 * hp + hh
            ls = slice(hh * A_DK, (hh + 1) * A_DK)
            hs = slice(h * A_DK, (h + 1) * A_DK)
            st = _hgrn_blockwise(stage_scr, oraw_scr, ls, lb_all[:, hs], st_scr[h])
            st_scr[h] = st
            st_ref[h] = st.T
            oa = _hgrn_finish(oraw_scr[:, ls], gh_ref[:, hs], og2[:, ls], z2[:, ls])
            oa_ref[:, hs] = oa.astype(BF16)

    def gate_proj(ph):
        for seg, base in ((9, 0), (10, D_MODEL)):
            gab_ref[:, base + ph * gate_w:base + (ph + 1) * gate_w] = proj(
                seg, ph * gate_w, (ph + 1) * gate_w).astype(BF16)

    def swa_proj():
        kv = proj(SEG_BK, 0, 2 * kw)
        bq = (proj(5, 0, D_MODEL) * Q_SCALE).astype(BF16)
        bz = proj(8, 0, D_MODEL)
        return kv, bq, bz

    def keep_window(kv):
        kvprev_scr[...] = kv[MIX_TQ - WINDOW:, :]
        kvwin_ref[...] = kv[MIX_TQ - WINDOW:, :]

    slab_safe = jnp.min(lb_all) >= LB_SAFE

    @pl.when(slab_safe)
    def _():
        kv, bq, bz = swa_proj()
        nxt = hgrn_proj(0)
        for ph in range(n_phase):
            cur = nxt
            if ph + 1 < n_phase:
                nxt = hgrn_proj(ph + 1)
            gate_proj(ph)
            hgrn_pair(ph, *cur)
            swa_block(ph, kv, bq, bz)
        keep_window(kv)

    @pl.when(jnp.logical_not(slab_safe))
    def _():
        for ph in range(n_phase):
            hgrn_pair_blockwise(ph, *hgrn_proj(ph))
            gate_proj(ph)
        kv, bq, bz = swa_proj()
        for j in range(n_blocks):
            swa_block(j, kv, bq, bz)
        keep_window(kv)


def _mix_prompt(bucket, rel_bias, sink, x, g_pre, w, hgrn_lb, hgrn_norm, batch, seq):
    nt = seq // MIX_TQ
    n_lb = hgrn_lb.shape[0]
    kvw = 2 * B_KV_HEADS * B_HD
    smem = pl.BlockSpec(memory_space=pltpu.SMEM)
    tok = lambda width: pl.BlockSpec((MIX_TQ, width), lambda b, t: (b * nt + t, 0))
    return pl.pallas_call(
        _mix_prompt_kernel,
        grid=(batch, nt),
        in_specs=[
            _resident((N_KEYS, WINDOW)),
            smem, smem,
            tok(D_MODEL),
            _resident((1, D_MODEL)),
            _resident(w.shape),
            _resident((n_lb, D_MODEL)),
            _resident((1, D_MODEL)),
        ],
        out_specs=[
            tok(D_MODEL), tok(D_MODEL), tok(2 * D_MODEL),
            pl.BlockSpec((None, A_HEADS, A_DK, A_DV), lambda b, t: (b, 0, 0, 0)),
            pl.BlockSpec((None, WINDOW, kvw), lambda b, t: (b, 0, 0)),
        ],
        out_shape=[
            jax.ShapeDtypeStruct((batch * seq, D_MODEL), BF16),
            jax.ShapeDtypeStruct((batch * seq, D_MODEL), BF16),
            jax.ShapeDtypeStruct((batch * seq, 2 * D_MODEL), BF16),
            jax.ShapeDtypeStruct((batch, A_HEADS, A_DK, A_DV), F32),
            jax.ShapeDtypeStruct((batch, WINDOW, kvw), F32),
        ],
        scratch_shapes=[
            pltpu.VMEM((B_KV_HEADS, 2 * N_KEYS, PAIRS_PER_KV * WINDOW), F32),
            pltpu.VMEM((A_HEADS, A_DV, A_DK), F32),
            pltpu.VMEM((WINDOW, kvw), F32),
            pltpu.VMEM((3, MIX_TQ, 2 * A_DK), F32),
            pltpu.VMEM((MIX_TQ, 2 * A_DK), F32),
        ],
        compiler_params=pltpu.CompilerParams(
            dimension_semantics=("arbitrary", "arbitrary"), vmem_limit_bytes=VMEM_LIMIT),
        name="mix_prompt",
    )(bucket, rel_bias, sink, x, g_pre, w, hgrn_lb, hgrn_norm)


def _outproj_kernel(oa_ref, ob_ref, ga_ref, gb_ref, x_ref, p_ref, wpa_ref, wpb_ref, wo_ref, gpost_ref,
                    wple_ref, wg_ref, y_ref):
    a = jnp.dot(oa_ref[...], wpa_ref[...], preferred_element_type=F32)
    b = jnp.dot(ob_ref[...], wpb_ref[...], preferred_element_type=F32)
    m = _sigmoid(ga_ref[...].astype(F32)) * a + _sigmoid(gb_ref[...].astype(F32)) * b
    y = jnp.dot(m.astype(BF16), wo_ref[...], preferred_element_type=F32)
    y = y * lax.rsqrt(jnp.mean(y * y, axis=-1, keepdims=True) + EPS) * gpost_ref[...]
    x1 = x_ref[...] + y
    gate = _sigmoid(jnp.dot(x1.astype(BF16), wg_ref[...], preferred_element_type=F32))
    e = jnp.dot(p_ref[...].astype(BF16), wple_ref[...], preferred_element_type=F32) * gate
    y_ref[...] = x1 + e


def _outproj(oa, ob, gates, ga_col, gb_col, x, p, wpa, wpb, wo, gpost, wple, wg, tm):
    n = x.shape[0]
    ple = p.shape[1]
    tok = lambda w, c=0: pl.BlockSpec((tm, w), lambda i, c=c: (i, c))
    return pl.pallas_call(
        _outproj_kernel,
        grid=(n // tm,),
        in_specs=[
            tok(D_MODEL), tok(D_MODEL), tok(D_MODEL, ga_col), tok(D_MODEL, gb_col), tok(D_MODEL), tok(ple),
            _resident((D_MODEL, D_MODEL)), _resident((D_MODEL, D_MODEL)), _resident((D_MODEL, D_MODEL)),
            _resident((1, D_MODEL)), _resident((ple, D_MODEL)), _resident((D_MODEL, D_MODEL)),
        ],
        out_specs=tok(D_MODEL),
        out_shape=jax.ShapeDtypeStruct((n, D_MODEL), F32),
        compiler_params=pltpu.CompilerParams(
            dimension_semantics=("arbitrary",), vmem_limit_bytes=VMEM_LIMIT),
        name="outproj",
    )(oa, ob, gates, gates, x, p, wpa, wpb, wo, gpost, wple, wg)


def _rel_bucket(rel):
    n = np.maximum(rel, 0)
    max_exact = REL_BUCKETS // 2
    nf = np.maximum(n, 1).astype(np.float32)
    scaled = (np.log(nf / np.float32(max_exact)) / np.float32(math.log(REL_MAX_DIST / max_exact))
              * np.float32(REL_BUCKETS - max_exact))
    frac = scaled - np.floor(scaled)
    inside = (n > max_exact) & (n < REL_MAX_DIST)
    assert np.all((frac[inside] > 1e-3) & (frac[inside] < 1.0 - 1e-3))
    large = np.minimum(max_exact + scaled.astype(np.int32), REL_BUCKETS - 1)
    return np.where(n < max_exact, n, large)


def _bucket_table(q_pos, k_pos, k_valid):
    rel = q_pos[:, None] - k_pos[None, :]
    ok = (rel >= 0) & (rel < WINDOW) & np.broadcast_to(k_valid, rel.shape)
    return jnp.asarray(np.where(ok, _rel_bucket(rel), -1).astype(np.int32).T)


TM_PROJ = 512
SAMPLE_BB = 8


def _layer(xp, xs, s_hgrn, win_k, win_v, pp, ps, norm_pre, w_in, hgrn_lb, hgrn_norm, attn_sink,
           rel_bias, w_pa, w_pb, w_o, norm_post, w_ple, w_ple_gate):
    batch, seq, _ = xp.shape
    dbatch, t_len, _ = xs.shape
    kw = B_KV_HEADS * B_HD

    w16 = w_in.astype(BF16)
    g_pre = norm_pre.reshape(1, D_MODEL)
    g_post = norm_post.reshape(1, D_MODEL)
    g_hgrn = hgrn_norm.reshape(1, D_MODEL)
    wpa, wpb, wo = w_pa.astype(BF16), w_pb.astype(BF16), w_o.astype(BF16)
    wple, wg = w_ple.astype(BF16), w_ple_gate.astype(BF16)

    xp2 = xp.reshape(batch * seq, D_MODEL)
    xs2 = xs.reshape(dbatch * t_len, D_MODEL)
    k_all = np.arange(N_KEYS)
    bucket_p = _bucket_table(np.arange(WINDOW) + WINDOW, k_all, np.ones((N_KEYS,), bool))
    rows_s = SAMPLE_BB * t_len
    q_row = np.arange(rows_s)
    new = k_all - WINDOW
    k_pos_s = np.where(new < 0, k_all, WINDOW + new % t_len)
    own = (new[None, :] < 0) | ((new[None, :] < rows_s) & (new[None, :] // t_len == q_row[:, None] // t_len))
    bucket_s = _bucket_table(WINDOW + q_row % t_len, k_pos_s, own)

    oa_p, ob_p, gab_p, st_p, kvwin_p = _mix_prompt(bucket_p, rel_bias, attn_sink, xp2, g_pre, w16,
                                                   hgrn_lb, g_hgrn, batch, seq)

    to_minor = lambda c: jnp.transpose(c, (0, 2, 3, 1)).reshape(dbatch, kw, WINDOW)
    from_minor = lambda c: jnp.transpose(c.reshape(dbatch, B_KV_HEADS, B_HD, WINDOW), (0, 3, 1, 2))
    pm_s, f_s, kv_s = _inproj(xs2, g_pre, w16, min(TM_PROJ, dbatch * t_len))
    oa_s, st_s, ob_s, nk_s, nv_s = _mix_sample(bucket_s, rel_bias, attn_sink, pm_s, f_s, kv_s, hgrn_lb, g_hgrn,
                                               s_hgrn, to_minor(win_k), to_minor(win_v),
                                               dbatch, t_len, SAMPLE_BB)

    y_p = _outproj(oa_p, ob_p, gab_p, 0, 1, xp2, pp.reshape(batch * seq, -1),
                   wpa, wpb, wo, g_post, wple, wg, TM_PROJ)
    y_s = _outproj(oa_s, ob_s, pm_s, G_GA, G_GB, xs2, ps.reshape(dbatch * t_len, -1),
                   wpa, wpb, wo, g_post, wple, wg, min(TM_PROJ, dbatch * t_len))

    k_win_p = kvwin_p[:, :, 0:kw].reshape(batch, WINDOW, B_KV_HEADS, B_HD)
    v_win_p = kvwin_p[:, :, kw:].reshape(batch, WINDOW, B_KV_HEADS, B_HD)
    return (y_p.reshape(batch, seq, D_MODEL), y_s.reshape(dbatch, t_len, D_MODEL), st_p, st_s,
            k_win_p, v_win_p,
            from_minor(nk_s), from_minor(nv_s))


def kernel(x_prompt, x_sample, state_hgrn, cache_swa_k, cache_swa_v, p_prompt, p_sample, norm_pre, w_in,
           hgrn_lb, hgrn_norm, attn_sink, rel_bias, w_pa, w_pb, w_o, norm_post, w_ple, w_ple_gate):
    depth = w_in.shape[0]
    assert depth == 1, "the forget-gate lower bound is implemented for a single layer"
    xp, xs = x_prompt, x_sample
    outs = []
    for l in range(depth):
        res = _layer(xp, xs, state_hgrn[l], cache_swa_k[l], cache_swa_v[l], p_prompt[l], p_sample[l],
                     norm_pre[l], w_in[l], hgrn_lb, hgrn_norm[l], attn_sink[l], rel_bias,
                     w_pa[l], w_pb[l], w_o[l], norm_post[l], w_ple[l], w_ple_gate[l])
        xp, xs = res[0], res[1]
        outs.append(res[2:])
    stack = lambda i: jnp.stack([o[i] for o in outs])
    return (xp, xs, stack(0), stack(1), stack(2), stack(3), stack(4), stack(5))
```

```python
import functools
import math

import jax
import jax.numpy as jnp
import numpy as np
from jax import lax
from jax.experimental import pallas as pl
from jax.experimental.pallas import tpu as pltpu

F32 = jnp.float32
BF16 = jnp.bfloat16

D_MODEL = 1024
A_HEADS = 8
A_DK = 128
A_DV = 128
A_CHUNK = 64
A_SUB = 16
B_HEADS = 16
B_KV_HEADS = 2
B_HD = 64
WINDOW = 128
REL_BUCKETS = 32
REL_MAX_DIST = 128
EPS = 1e-6
NEG = float("-inf")
LOG2E = math.log2(math.e)
Q_SCALE = B_HD ** -0.5 * LOG2E

G_AQ, G_AI, G_AOG, G_AZ, G_BQ, G_BZ, G_GA, G_GB = range(8)
N_GROUPS = 8

VMEM_LIMIT = 56 * 1024 * 1024


def _sigmoid(x):
    return 0.5 * jnp.tanh(0.5 * x) + 0.5


def _silu(x):
    h = 0.5 * x
    return h * jnp.tanh(h) + h


def _resident(shape):
    nd = len(shape)
    return pl.BlockSpec(shape, lambda *_: (0,) * nd, pipeline_mode=pl.Buffered(1))


IN_OFFS = (0, 1024, 2048, 3072, 4096, 5120, 6144, 6272, 6400, 7424, 8448, 9472)
SLAB_SEGS = (0, 2, 3, 4, 5, 8, 9, 10)
SEG_AF, SEG_BK, SEG_BV = 1, 6, 7


def _inproj_kernel(x_ref, g_ref, w_ref, pm_ref, f_ref, kv_ref):
    x = x_ref[...]
    ms = jnp.mean(x * x, axis=-1, keepdims=True)
    u = (x * lax.rsqrt(ms + EPS) * g_ref[...]).astype(BF16)
    proj = lambda lo, hi: jnp.dot(u, w_ref[:, lo:hi], preferred_element_type=F32)
    for c, seg in enumerate(SLAB_SEGS):
        pm_ref[:, c * D_MODEL:(c + 1) * D_MODEL] = proj(IN_OFFS[seg], IN_OFFS[seg + 1]).astype(BF16)
    f_ref[...] = proj(IN_OFFS[SEG_AF], IN_OFFS[SEG_AF + 1])
    kv_ref[...] = proj(IN_OFFS[SEG_BK], IN_OFFS[SEG_BV + 1])


def _inproj(x, g, w, tm):
    n = x.shape[0]
    nm = N_GROUPS * D_MODEL
    nkv = IN_OFFS[SEG_BV + 1] - IN_OFFS[SEG_BK]
    return pl.pallas_call(
        _inproj_kernel,
        grid=(n // tm,),
        in_specs=[
            pl.BlockSpec((tm, D_MODEL), lambda i: (i, 0)),
            _resident((1, D_MODEL)),
            _resident(w.shape),
        ],
        out_specs=[
            pl.BlockSpec((tm, nm), lambda i: (i, 0)),
            pl.BlockSpec((tm, D_MODEL), lambda i: (i, 0)),
            pl.BlockSpec((tm, nkv), lambda i: (i, 0)),
        ],
        out_shape=[
            jax.ShapeDtypeStruct((n, nm), BF16),
            jax.ShapeDtypeStruct((n, D_MODEL), F32),
            jax.ShapeDtypeStruct((n, nkv), F32),
        ],
        compiler_params=pltpu.CompilerParams(
            dimension_semantics=("arbitrary",), vmem_limit_bytes=VMEM_LIMIT),
        name="inproj",
    )(x, g, w)


def _lower_bound(lb_ref):
    l = lb_ref[...]
    m = jnp.max(l, axis=0, keepdims=True)
    e = jnp.exp(l - m)
    return e[0:1, :] / jnp.sum(e, axis=0, keepdims=True)


def _forget_gate(f_pre, lb):
    return lb + (1.0 - lb) / (1.0 + jnp.exp(-f_pre))


def _group_cumsum(x, row, period, shifts):
    pos = row & (period - 1)
    for sh in shifts:
        x = x + jnp.where(pos >= sh, pltpu.roll(x, sh, axis=0), 0.0)
    return x


def _hgrn_finish(o, g, og_pre, z_pre):
    o = o * lax.rsqrt(jnp.mean(o * o, axis=-1, keepdims=True) + EPS)
    return o * g * _sigmoid(og_pre) * _silu(z_pre)


def _dot_nt(a, b):
    return lax.dot_general(a, b, (((1,), (1,)), ((), ())), preferred_element_type=F32)


def _dot_tn(a, b):
    return lax.dot_general(a, b, (((0,), (0,)), ((), ())), preferred_element_type=F32)


def _hgrn_sample_kernel(q_ref, v_ref, og_ref, z_ref, f_ref, lb_ref, g_ref, s0_ref, oa_ref, s_ref,
                        *, bb, t_len):
    rows = bb * t_len
    lb_all = _lower_bound(lb_ref)
    row = lax.broadcasted_iota(jnp.int32, (rows, A_DK), 0)
    pos = row & (t_len - 1)
    grp = row >> (t_len.bit_length() - 1)
    shifts = tuple(1 << i for i in range((t_len - 1).bit_length()))
    n_tail = A_DK - rows
    sel_row = lax.broadcasted_iota(jnp.int32, (n_tail, 2 * bb * A_DV), 0)
    sel_blk = lax.broadcasted_iota(jnp.int32, (n_tail, 2 * bb * A_DV), 1) >> (A_DV.bit_length() - 1)
    picks = jnp.logical_and(sel_row < 2 * bb, sel_blk == bb + (sel_row & (bb - 1)))
    rhs_tail = jnp.where(picks, 1.0, 0.0).astype(BF16)
    lhs_pad = jnp.zeros((n_tail - 2 * bb, A_DK), F32)
    v_pad = jnp.zeros((rows, bb * A_DV), F32)
    for h in range(A_HEADS):
        ls = slice(h * A_DK, (h + 1) * A_DK)
        lb = lb_all[:, ls]
        fg = _forget_gate(f_ref[:, ls], lb)
        logf = jnp.log(fg)
        k = 1.0 - fg
        b = _group_cumsum(logf, row, t_len, shifts)
        q = _silu(q_ref[:, ls].astype(F32))
        v = v_ref[:, ls].astype(F32)

        o = jnp.zeros((rows, A_DV), F32)
        for d in range(t_len):
            if d == 0:
                a = jnp.sum(q * k, axis=-1, keepdims=True)
                o = o + a * v
            else:
                w = jnp.exp(jnp.where(pos >= d, b - pltpu.roll(b, d, axis=0), 0.0))
                a = jnp.sum(q * pltpu.roll(k, d, axis=0) * w, axis=-1, keepdims=True)
                o = o + jnp.where(pos >= d, a * pltpu.roll(v, d, axis=0), 0.0)

        b_last = b
        for d in range(1, t_len):
            b_last = jnp.where(pos == t_len - 1 - d, pltpu.roll(b, rows - d, axis=0), b_last)
        e_last = jnp.exp(b_last)
        qs = q * jnp.exp(b)
        kd = k * jnp.exp(b_last - b)
        e_rows = jnp.concatenate([e_last[j * t_len:j * t_len + 1, :] for j in range(bb)], axis=0)
        e_hi = e_rows.astype(BF16).astype(F32)
        lhs_t = jnp.concatenate([kd, e_hi, e_rows - e_hi, lhs_pad], axis=0).T.astype(BF16)
        v_rows = jnp.concatenate([jnp.where(grp == j, v, 0.0) for j in range(bb)] + [v_pad], axis=1)
        res = jnp.dot(lhs_t, jnp.concatenate([v_rows.astype(BF16), rhs_tail], axis=0),
                      preferred_element_type=F32)
        for j in range(bb):
            s0 = s0_ref[j, h]
            o = o + jnp.dot(jnp.where(grp == j, qs, 0.0).astype(BF16), s0.astype(BF16),
                            preferred_element_type=F32)
            s_ref[j, h] = (res[:, (bb + j) * A_DV:(bb + j + 1) * A_DV] * s0
                           + res[:, j * A_DV:(j + 1) * A_DV])

        oa = _hgrn_finish(o, g_ref[:, ls], og_ref[:, ls].astype(F32), z_ref[:, ls].astype(F32))
        oa_ref[:, ls] = oa.astype(BF16)


N_PAIRS = B_HEADS // 2
PAIRS_PER_KV = N_PAIRS // B_KV_HEADS
N_KEYS = 2 * WINDOW


def _build_bias(bucket_ref, relb_ref, bias_ref, r):
    bucket = bucket_ref[...]
    for h in range(B_HEADS):
        def body(kb, acc, h=h):
            return jnp.where(bucket == kb, relb_ref[kb, h] * LOG2E, acc)
        tab = lax.fori_loop(0, REL_BUCKETS, body, jnp.full(bucket.shape, NEG, F32))
        pair, parity = divmod(h, 2)
        kv, pp = divmod(pair, PAIRS_PER_KV)
        bias_ref[kv, parity * N_KEYS:(parity + 1) * N_KEYS, pp * r:(pp + 1) * r] = tab


def _swa_keys(kk):
    lo = lax.broadcasted_iota(jnp.int32, (N_KEYS, 2 * B_HD), 1) < B_HD
    kk_sw = pltpu.roll(kk, B_HD, axis=1)
    slabs = []
    for kv in range(B_KV_HEADS):
        k_lo, k_hi = (kk, kk_sw) if kv == 0 else (kk_sw, kk)
        slabs.append(jnp.concatenate([jnp.where(lo, k_lo, 0.0), jnp.where(lo, 0.0, k_hi)], axis=0).astype(BF16))
    return slabs


def _swa_attend(logits, vals_t, col_masks, sink_ref, kv, r):
    width = PAIRS_PER_KV * r
    lane_pp = lax.broadcasted_iota(jnp.int32, (1, width), 1) >> (r.bit_length() - 1)
    halves = []
    for parity in range(2):
        sink = jnp.zeros((1, width), F32)
        for pp in range(PAIRS_PER_KV):
            sink = jnp.where(lane_pp == pp, sink_ref[(kv * PAIRS_PER_KV + pp) * 2 + parity] * LOG2E, sink)
        l = logits[parity * N_KEYS:(parity + 1) * N_KEYS, :]
        m = jnp.maximum(jnp.max(l, axis=0, keepdims=True), sink)
        p = jnp.exp2(l - m)
        denom = jnp.sum(p, axis=0, keepdims=True) + jnp.exp2(sink - m)
        p16 = p.astype(BF16)
        v_rows = jnp.concatenate([v_t[kv * B_HD:(kv + 1) * B_HD, :] for v_t in vals_t], axis=0)
        pv_all = jnp.dot(v_rows, p16, preferred_element_type=F32)
        pv = None
        for n, mask in enumerate(col_masks):
            part = pv_all[n * B_HD:(n + 1) * B_HD, :]
            pv = part if mask is None else jnp.where(mask, part, 0.0 if pv is None else pv)
        halves.append(pv / denom)
    return jnp.concatenate(halves, axis=0).T


def _swa_sample_kernel(bucket_ref, relb_ref, sink_ref, q_ref, z_ref, kvn_ref, ck_ref, cv_ref,
                       ob_ref, nk_ref, nv_ref, bias_ref, *, bb, t_len):
    rows = bb * t_len
    width = PAIRS_PER_KV * rows

    @pl.when(pl.program_id(0) == 0)
    def _():
        _build_bias(bucket_ref, relb_ref, bias_ref, rows)

    kw = B_KV_HEADS * B_HD
    t_shift = t_len.bit_length() - 1
    seq_of_col = lambda n: (lax.broadcasted_iota(jnp.int32, (n, width), 1) & (rows - 1)) >> t_shift
    seq_l = seq_of_col(2 * N_KEYS)
    seq_v = seq_of_col(B_HD)
    q_pairs = [(q_ref[:, p * 128:(p + 1) * 128].astype(F32) * Q_SCALE).astype(BF16) for p in range(N_PAIRS)]
    q_kv = [jnp.concatenate(q_pairs[kv * PAIRS_PER_KV:(kv + 1) * PAIRS_PER_KV], axis=0)
            for kv in range(B_KV_HEADS)]
    kn = kvn_ref[:, 0:kw]
    vn = kvn_ref[:, kw:2 * kw]
    n_new = N_KEYS - WINDOW
    assert n_new == WINDOW
    kn_all = jnp.concatenate([kn, jnp.zeros((n_new - rows, kw), F32)], axis=0)
    vn_all = jnp.concatenate([vn, jnp.zeros((n_new - rows, kw), F32)], axis=0)
    kn_t = kn_all.T
    vn_t = vn_all.T
    keep = lax.broadcasted_iota(jnp.int32, (kw, WINDOW), 1) < WINDOW - t_len
    logits = [None] * B_KV_HEADS
    vals_t, col_masks = [], []
    for j in range(bb):
        ck_t = ck_ref[j]
        cv_t = cv_ref[j]
        keys = _swa_keys(jnp.concatenate([ck_t.T, kn_all], axis=0))
        for kv in range(B_KV_HEADS):
            lg = _dot_nt(keys[kv], q_kv[kv])
            logits[kv] = jnp.where(seq_l == j, lg, 0.0 if logits[kv] is None else logits[kv])
        vals_t.append(jnp.concatenate([cv_t, vn_t], axis=1).astype(BF16))
        col_masks.append(seq_v == j)
        to_end = WINDOW - (j + 1) * t_len
        nk_ref[j] = jnp.where(keep, pltpu.roll(ck_t, WINDOW - t_len, axis=1), pltpu.roll(kn_t, to_end, axis=1))
        nv_ref[j] = jnp.where(keep, pltpu.roll(cv_t, WINDOW - t_len, axis=1), pltpu.roll(vn_t, to_end, axis=1))
    for kv in range(B_KV_HEADS):
        acc_t = _swa_attend(logits[kv] + bias_ref[kv], vals_t, col_masks, sink_ref, kv, rows)
        for pp in range(PAIRS_PER_KV):
            p = kv * PAIRS_PER_KV + pp
            z = z_ref[:, p * 128:(p + 1) * 128].astype(F32)
            ob_ref[:, p * 128:(p + 1) * 128] = (acc_t[pp * rows:(pp + 1) * rows, :] * _silu(z)).astype(BF16)


STATE_SLOTS = 3


def _mix_sample_kernel(bucket_ref, relb_ref, sink_ref, aq_ref, ai_ref, aog_ref, az_ref, f_ref, lb_ref, g_ref,
                       s0_hbm, bq_ref, bz_ref, kvn_ref, ck_ref, cv_ref,
                       oa_ref, s_ref, ob_ref, nk_ref, nv_ref, bias_ref, s0_ring, s0_sem, *, bb, t_len):
    step = pl.program_id(0)
    n_steps = pl.num_programs(0)

    def fetch(s):
        slot = s % STATE_SLOTS
        return pltpu.make_async_copy(s0_hbm.at[pl.ds(s * bb, bb)], s0_ring.at[slot], s0_sem.at[slot])

    @pl.when(step == 0)
    def _():
        for s in range(STATE_SLOTS - 1):
            fetch(s).start(priority=1)

    @pl.when(step + STATE_SLOTS - 1 < n_steps)
    def _():
        fetch(step + STATE_SLOTS - 1).start(priority=1)

    fetch(step).wait()
    s0_ref = s0_ring.at[step % STATE_SLOTS]
    _hgrn_sample_kernel(aq_ref, ai_ref, aog_ref, az_ref, f_ref, lb_ref, g_ref, s0_ref, oa_ref, s_ref,
                        bb=bb, t_len=t_len)
    _swa_sample_kernel(bucket_ref, relb_ref, sink_ref, bq_ref, bz_ref, kvn_ref, ck_ref, cv_ref,
                       ob_ref, nk_ref, nv_ref, bias_ref, bb=bb, t_len=t_len)


def _mix_sample(bucket, rel_bias, sink, pm, f, kvn, hgrn_lb, hgrn_norm, s0, cache_k, cache_v,
                batch, t_len, bb):
    rows = bb * t_len
    kvw = kvn.shape[1]
    kw = kvw // 2
    n_lb = hgrn_lb.shape[0]
    assert batch // bb >= STATE_SLOTS - 1
    smem = pl.BlockSpec(memory_space=pltpu.SMEM)
    col = lambda grp: pl.BlockSpec((rows, D_MODEL), lambda i, grp=grp: (i, grp))
    tok = pl.BlockSpec((rows, D_MODEL), lambda i: (i, 0))
    st_spec = pl.BlockSpec((bb, A_HEADS, A_DK, A_DV), lambda i: (i, 0, 0, 0))
    cache_spec = pl.BlockSpec((bb, kw, WINDOW), lambda i: (i, 0, 0))
    return pl.pallas_call(
        functools.partial(_mix_sample_kernel, bb=bb, t_len=t_len),
        grid=(batch // bb,),
        in_specs=[
            _resident((N_KEYS, rows)),
            smem, smem,
            col(G_AQ), col(G_AI), col(G_AOG), col(G_AZ),
            tok,
            _resident((n_lb, D_MODEL)),
            _resident((1, D_MODEL)),
            pl.BlockSpec(memory_space=pl.ANY),
            col(G_BQ), col(G_BZ),
            pl.BlockSpec((rows, kvw), lambda i: (i, 0)),
            cache_spec, cache_spec,
        ],
        out_specs=[tok, st_spec, tok, cache_spec, cache_spec],
        out_shape=[
            jax.ShapeDtypeStruct((batch * t_len, D_MODEL), BF16),
            jax.ShapeDtypeStruct((batch, A_HEADS, A_DK, A_DV), F32),
            jax.ShapeDtypeStruct((batch * t_len, D_MODEL), BF16),
            jax.ShapeDtypeStruct((batch, kw, WINDOW), F32),
            jax.ShapeDtypeStruct((batch, kw, WINDOW), F32),
        ],
        scratch_shapes=[
            pltpu.VMEM((B_KV_HEADS, 2 * N_KEYS, PAIRS_PER_KV * rows), F32),
            pltpu.VMEM((STATE_SLOTS, bb, A_HEADS, A_DK, A_DV), F32),
            pltpu.SemaphoreType.DMA((STATE_SLOTS,)),
        ],
        compiler_params=pltpu.CompilerParams(
            dimension_semantics=("arbitrary",), vmem_limit_bytes=VMEM_LIMIT),
        name="mix_sample",
    )(bucket, rel_bias, sink, pm, pm, pm, pm, f, hgrn_lb, hgrn_norm, s0, pm, pm, kvn, cache_k, cache_v)


MIX_TQ = 512
HGRN_SLAB = 256


class _HgrnConsts:
    def __init__(self, slab):
        shift = A_CHUNK.bit_length() - 1
        self.row = lax.broadcasted_iota(jnp.int32, (slab, A_DK), 0)
        rs = lax.broadcasted_iota(jnp.int32, (slab, slab), 0)
        cs = lax.broadcasted_iota(jnp.int32, (slab, slab), 1)
        self.same_chunk_causal = jnp.logical_and((rs >> shift) == (cs >> shift), cs <= rs)
        self.zero_row = jnp.zeros((1, A_DK), F32)
        self.zero_blk = jnp.zeros((A_SUB, A_DK), BF16)
        self.zero_chunk = jnp.zeros((A_CHUNK, A_DK), BF16)


def _hgrn_slab(f_pre, q_pre, v, lb, st, c):
    slab = f_pre.shape[0]
    n_sub = A_CHUNK // A_SUB
    sc = slab // A_CHUNK
    rep = lambda r, n: jnp.broadcast_to(r, (n, A_DK))
    fg = _forget_gate(f_pre, lb)
    logf = jnp.log(fg)
    k = 1.0 - fg
    b = _group_cumsum(logf, c.row, A_CHUNK, (1, 2, 4, 8, 16, 32))
    q = _silu(q_pre)

    ends = [b[A_SUB * j + A_SUB - 1:A_SUB * (j + 1), :] for j in range(sc * n_sub)]
    endrow = jnp.concatenate([rep(e, A_SUB) for e in ends], axis=0)
    prevrow = jnp.concatenate(
        [rep(c.zero_row if j % n_sub == 0 else ends[j - 1], A_SUB) for j in range(sc * n_sub)], axis=0)
    tot = [ends[n_sub * ch + n_sub - 1] for ch in range(sc)]
    lastrow = jnp.concatenate([rep(t, A_CHUNK) for t in tot], axis=0)
    kend = k * jnp.exp(endrow - b)
    qd = q * jnp.exp(b - prevrow)
    qs = qd * jnp.exp(prevrow)
    kd = kend * jnp.exp(lastrow - endrow)
    kdiag16 = (k * jnp.exp(prevrow - b)).astype(BF16)
    qd16 = qd.astype(BF16)
    kend16 = kend.astype(BF16)
    qs16 = qs.astype(BF16)
    kd16 = kd.astype(BF16)

    blk = lambda a, j: a[A_SUB * j:A_SUB * (j + 1), :]
    chk = lambda a, ch: a[A_CHUNK * ch:A_CHUNK * (ch + 1), :]

    q_groups, k_groups = [], []
    for i in range(n_sub):
        q_groups.append(jnp.concatenate(
            [blk(qd16, j) if j % n_sub == i else c.zero_blk for j in range(sc * n_sub)], axis=0))
        pieces = []
        for j in range(sc * n_sub):
            ch, jj = divmod(j, n_sub)
            if jj == i:
                pieces.append(blk(kdiag16, j))
            elif jj == i - 1:
                pieces.append(blk(kend16, j))
            elif jj < i:
                pieces.append((blk(kend, j) * jnp.exp(ends[ch * n_sub + i - 1] - ends[j])).astype(BF16))
            else:
                pieces.append(c.zero_blk)
        k_groups.append(jnp.concatenate(pieces, axis=0))
    att = _dot_nt(jnp.concatenate(q_groups, axis=1), jnp.concatenate(k_groups, axis=1))
    att = jnp.where(c.same_chunk_causal, att, 0.0)

    if sc > 1:
        q_groups, k_groups = [], []
        for ch in range(1, sc):
            q_groups.append(jnp.concatenate(
                [chk(qs16, c2) if c2 == ch else c.zero_chunk for c2 in range(sc)], axis=0))
            pieces = []
            for c2 in range(sc):
                if c2 == ch - 1:
                    pieces.append(chk(kd16, c2))
                elif c2 < ch:
                    pieces.append((chk(kd, c2) * jnp.exp(sum(tot[c2 + 1:ch]))).astype(BF16))
                else:
                    pieces.append(c.zero_chunk)
            k_groups.append(jnp.concatenate(pieces, axis=0))
        att = att + _dot_nt(jnp.concatenate(q_groups, axis=1), jnp.concatenate(k_groups, axis=1))
    o = jnp.dot(att.astype(BF16), v, preferred_element_type=F32)

    q0 = jnp.concatenate(
        [chk(qs16, 0)] + [(chk(qs, ch) * jnp.exp(sum(tot[:ch]))).astype(BF16) for ch in range(1, sc)], axis=0)
    o = o + _dot_nt(q0, st.astype(BF16))

    k1 = jnp.concatenate(
        [(chk(kd, ch) * jnp.exp(sum(tot[ch + 1:]))).astype(BF16) for ch in range(sc - 1)]
        + [chk(kd16, sc - 1)], axis=0)
    st = st * jnp.exp(sum(tot)) + _dot_tn(v, k1)
    return o, st


def _hgrn_blockwise(stage_ref, oraw_ref, ls, lb, st):
    n_blk = stage_ref.shape[1] // A_SUB
    row = lax.broadcasted_iota(jnp.int32, (A_SUB, A_DK), 0)
    shifts = tuple(1 << i for i in range((A_SUB - 1).bit_length()))

    def body(i, st):
        rows = pl.ds(pl.multiple_of(i * A_SUB, A_SUB), A_SUB)
        fg = _forget_gate(stage_ref[1, rows, ls], lb)
        k = 1.0 - fg
        b = _group_cumsum(jnp.log(fg), row, A_SUB, shifts)
        q = _silu(stage_ref[0, rows, ls])
        v = stage_ref[2, rows, ls]
        o = _dot_nt((q * jnp.exp(b)).astype(BF16), st.astype(BF16))
        o = o + jnp.sum(q * k, axis=-1, keepdims=True) * v
        for d in range(1, A_SUB):
            ok = row >= d
            w = jnp.exp(jnp.where(ok, b - pltpu.roll(b, d, axis=0), 0.0))
            a = jnp.sum(q * pltpu.roll(k, d, axis=0) * w, axis=-1, keepdims=True)
            o = o + jnp.where(ok, a * pltpu.roll(v, d, axis=0), 0.0)
        oraw_ref[rows, ls] = o
        b_end = b[A_SUB - 1:A_SUB, :]
        kd = (k * jnp.exp(b_end - b)).astype(BF16)
        return st * jnp.exp(b_end) + _dot_tn(v.astype(BF16), kd)

    return lax.fori_loop(0, n_blk, body, st)


LB_SAFE = math.exp(-80.0 / A_SUB)


def _mix_prompt_kernel(bucket_ref, relb_ref, sink_ref, x_ref, gpre_ref, w_ref, lb_ref, gh_ref,
                       oa_ref, ob_ref, gab_ref, st_ref, kvwin_ref,
                       bias_ref, st_scr, kvprev_scr, stage_scr, oraw_scr):
    tile = pl.program_id(1)
    at_start = tile == 0

    @pl.when(jnp.logical_and(pl.program_id(0) == 0, at_start))
    def _():
        _build_bias(bucket_ref, relb_ref, bias_ref, WINDOW)

    @pl.when(at_start)
    def _():
        st_scr[...] = jnp.zeros(st_scr.shape, F32)
        kvprev_scr[...] = jnp.zeros(kvprev_scr.shape, F32)

    x = x_ref[...]
    ms = jnp.mean(x * x, axis=-1, keepdims=True)
    u = (x * lax.rsqrt(ms + EPS) * gpre_ref[...]).astype(BF16)
    proj = lambda seg, lo, hi: jnp.dot(u, w_ref[:, IN_OFFS[seg] + lo:IN_OFFS[seg] + hi],
                                       preferred_element_type=F32)

    lb_all = _lower_bound(lb_ref)
    consts = _HgrnConsts(HGRN_SLAB)
    pair_w = 2 * A_DK
    n_phase = A_HEADS // 2
    n_blocks = MIX_TQ // WINDOW
    assert n_blocks == n_phase
    gate_w = D_MODEL // n_phase
    kw = B_KV_HEADS * B_HD

    def hgrn_proj(hp):
        lo, hi = hp * pair_w, (hp + 1) * pair_w
        return tuple(proj(seg, lo, hi) for seg in (0, SEG_AF, 2, 3, 4))

    def hgrn_pair(hp, q2, f2, v2, og2, z2):
        for hh in range(2):
            h = 2 * hp + hh
            ls = slice(hh * A_DK, (hh + 1) * A_DK)
            hs = slice(h * A_DK, (h + 1) * A_DK)
            st = st_scr[h]
            outs = []
            for s0 in range(0, MIX_TQ, HGRN_SLAB):
                rs = slice(s0, s0 + HGRN_SLAB)
                o, st = _hgrn_slab(f2[rs, ls], q2[rs, ls], v2[rs, ls].astype(BF16), lb_all[:, hs], st, consts)
                outs.append(o)
            st_scr[h] = st
            st_ref[h] = st.T
            oa = _hgrn_finish(jnp.concatenate(outs, axis=0), gh_ref[:, hs], og2[:, ls], z2[:, ls])
            oa_ref[:, hs] = oa.astype(BF16)

    key = lax.broadcasted_iota(jnp.int32, (2 * N_KEYS, PAIRS_PER_KV * WINDOW), 0) & (N_KEYS - 1)
    no_prev = jnp.logical_and(key < WINDOW, at_start)

    def swa_block(j, kv, bq, bz):
        r0 = j * WINDOW
        if j == 0:
            kvj = jnp.concatenate([kvprev_scr[...], kv[0:WINDOW, :]], axis=0)
        else:
            kvj = kv[r0 - WINDOW:r0 + WINDOW, :]
        keys = _swa_keys(kvj[:, 0:kw])
        vals_t = [kvj[:, kw:2 * kw].T.astype(BF16)]
        for kvh in range(B_KV_HEADS):
            q = jnp.concatenate([bq[r0:r0 + WINDOW, p * 128:(p + 1) * 128]
                                 for p in range(kvh * PAIRS_PER_KV, (kvh + 1) * PAIRS_PER_KV)], axis=0)
            logits = _dot_nt(keys[kvh], q) + bias_ref[kvh]
            if j == 0:
                logits = jnp.where(no_prev, NEG, logits)
            acc_t = _swa_attend(logits, vals_t, [None], sink_ref, kvh, WINDOW)
            for pp in range(PAIRS_PER_KV):
                cs = slice((kvh * PAIRS_PER_KV + pp) * 128, (kvh * PAIRS_PER_KV + pp + 1) * 128)
                ob_ref[r0:r0 + WINDOW, cs] = (
                    acc_t[pp * WINDOW:(pp + 1) * WINDOW, :] * _silu(bz[r0:r0 + WINDOW, cs])).astype(BF16)

    def hgrn_pair_blockwise(hp, q2, f2, v2, og2, z2):
        stage_scr[0] = q2
        stage_scr[1] = f2
        stage_scr[2] = v2
        for hh in range(2):
            h = 2 * hp + hh
            ls = slice(hh * A_DK, (hh + 1) * A_DK)
            hs = slice(h * A_DK, (h + 1) * A_DK)
            st = _hgrn_blockwise(stage_scr, oraw_scr, ls, lb_all[:, hs], st_scr[h])
            st_scr[h] = st
            st_ref[h] = st.T
            oa = _hgrn_finish(oraw_scr[:, ls], gh_ref[:, hs], og2[:, ls], z2[:, ls])
            oa_ref[:, hs] = oa.astype(BF16)

    def gate_proj(ph):
        for seg, base in ((9, 0), (10, D_MODEL)):
            gab_ref[:, base + ph * gate_w:base + (ph + 1) * gate_w] = proj(
                seg, ph * gate_w, (ph + 1) * gate_w).astype(BF16)

    def swa_proj():
        kv = proj(SEG_BK, 0, 2 * kw)
        bq = (proj(5, 0, D_MODEL) * Q_SCALE).astype(BF16)
        bz = proj(8, 0, D_MODEL)
        return kv, bq, bz

    def keep_window(kv):
        kvprev_scr[...] = kv[MIX_TQ - WINDOW:, :]
        kvwin_ref[...] = kv[MIX_TQ - WINDOW:, :]

    slab_safe = jnp.min(lb_all) >= LB_SAFE

    @pl.when(slab_safe)
    def _():
        kv, bq, bz = swa_proj()
        nxt = hgrn_proj(0)
        for ph in range(n_phase):
            cur = nxt
            if ph + 1 < n_phase:
                nxt = hgrn_proj(ph + 1)
            gate_proj(ph)
            hgrn_pair(ph, *cur)
            swa_block(ph, kv, bq, bz)
        keep_window(kv)

    @pl.when(jnp.logical_not(slab_safe))
    def _():
        for ph in range(n_phase):
            hgrn_pair_blockwise(ph, *hgrn_proj(ph))
            gate_proj(ph)
        kv, bq, bz = swa_proj()
        for j in range(n_blocks):
            swa_block(j, kv, bq, bz)
        keep_window(kv)


def _mix_prompt(bucket, rel_bias, sink, x, g_pre, w, hgrn_lb, hgrn_norm, batch, seq):
    nt = seq // MIX_TQ
    n_lb = hgrn_lb.shape[0]
    kvw = 2 * B_KV_HEADS * B_HD
    smem = pl.BlockSpec(memory_space=pltpu.SMEM)
    tok = lambda width: pl.BlockSpec((MIX_TQ, width), lambda b, t: (b * nt + t, 0))
    return pl.pallas_call(
        _mix_prompt_kernel,
        grid=(batch, nt),
        in_specs=[
            _resident((N_KEYS, WINDOW)),
            smem, smem,
            tok(D_MODEL),
            _resident((1, D_MODEL)),
            _resident(w.shape),
            _resident((n_lb, D_MODEL)),
            _resident((1, D_MODEL)),
        ],
        out_specs=[
            tok(D_MODEL), tok(D_MODEL), tok(2 * D_MODEL),
            pl.BlockSpec((None, A_HEADS, A_DK, A_DV), lambda b, t: (b, 0, 0, 0)),
            pl.BlockSpec((None, WINDOW, kvw), lambda b, t: (b, 0, 0)),
        ],
        out_shape=[
            jax.ShapeDtypeStruct((batch * seq, D_MODEL), BF16),
            jax.ShapeDtypeStruct((batch * seq, D_MODEL), BF16),
            jax.ShapeDtypeStruct((batch * seq, 2 * D_MODEL), BF16),
            jax.ShapeDtypeStruct((batch, A_HEADS, A_DK, A_DV), F32),
            jax.ShapeDtypeStruct((batch, WINDOW, kvw), F32),
        ],
        scratch_shapes=[
            pltpu.VMEM((B_KV_HEADS, 2 * N_KEYS, PAIRS_PER_KV * WINDOW), F32),
            pltpu.VMEM((A_HEADS, A_DV, A_DK), F32),
            pltpu.VMEM((WINDOW, kvw), F32),
            pltpu.VMEM((3, MIX_TQ, 2 * A_DK), F32),
            pltpu.VMEM((MIX_TQ, 2 * A_DK), F32),
        ],
        compiler_params=pltpu.CompilerParams(
            dimension_semantics=("arbitrary", "arbitrary"), vmem_limit_bytes=VMEM_LIMIT),
        name="mix_prompt",
    )(bucket, rel_bias, sink, x, g_pre, w, hgrn_lb, hgrn_norm)


def _outproj_kernel(oa_ref, ob_ref, ga_ref, gb_ref, x_ref, p_ref, wpa_ref, wpb_ref, wo_ref, gpost_ref,
                    wple_ref, wg_ref, y_ref):
    a = jnp.dot(oa_ref[...], wpa_ref[...], preferred_element_type=F32)
    b = jnp.dot(ob_ref[...], wpb_ref[...], preferred_element_type=F32)
    m = _sigmoid(ga_ref[...].astype(F32)) * a + _sigmoid(gb_ref[...].astype(F32)) * b
    y = jnp.dot(m.astype(BF16), wo_ref[...], preferred_element_type=F32)
    y = y * lax.rsqrt(jnp.mean(y * y, axis=-1, keepdims=True) + EPS) * gpost_ref[...]
    x1 = x_ref[...] + y
    gate = _sigmoid(jnp.dot(x1.astype(BF16), wg_ref[...], preferred_element_type=F32))
    e = jnp.dot(p_ref[...].astype(BF16), wple_ref[...], preferred_element_type=F32) * gate
    y_ref[...] = x1 + e


def _outproj(oa, ob, gates, ga_col, gb_col, x, p, wpa, wpb, wo, gpost, wple, wg, tm):
    n = x.shape[0]
    ple = p.shape[1]
    tok = lambda w, c=0: pl.BlockSpec((tm, w), lambda i, c=c: (i, c))
    return pl.pallas_call(
        _outproj_kernel,
        grid=(n // tm,),
        in_specs=[
            tok(D_MODEL), tok(D_MODEL), tok(D_MODEL, ga_col), tok(D_MODEL, gb_col), tok(D_MODEL), tok(ple),
            _resident((D_MODEL, D_MODEL)), _resident((D_MODEL, D_MODEL)), _resident((D_MODEL, D_MODEL)),
            _resident((1, D_MODEL)), _resident((ple, D_MODEL)), _resident((D_MODEL, D_MODEL)),
        ],
        out_specs=tok(D_MODEL),
        out_shape=jax.ShapeDtypeStruct((n, D_MODEL), F32),
        compiler_params=pltpu.CompilerParams(
            dimension_semantics=("arbitrary",), vmem_limit_bytes=VMEM_LIMIT),
        name="outproj",
    )(oa, ob, gates, gates, x, p, wpa, wpb, wo, gpost, wple, wg)


def _rel_bucket(rel):
    n = np.maximum(rel, 0)
    max_exact = REL_BUCKETS // 2
    nf = np.maximum(n, 1).astype(np.float32)
    scaled = (np.log(nf / np.float32(max_exact)) / np.float32(math.log(REL_MAX_DIST / max_exact))
              * np.float32(REL_BUCKETS - max_exact))
    frac = scaled - np.floor(scaled)
    inside = (n > max_exact) & (n < REL_MAX_DIST)
    assert np.all((frac[inside] > 1e-3) & (frac[inside] < 1.0 - 1e-3))
    large = np.minimum(max_exact + scaled.astype(np.int32), REL_BUCKETS - 1)
    return np.where(n < max_exact, n, large)


def _bucket_table(q_pos, k_pos, k_valid):
    rel = q_pos[:, None] - k_pos[None, :]
    ok = (rel >= 0) & (rel < WINDOW) & np.broadcast_to(k_valid, rel.shape)
    return jnp.asarray(np.where(ok, _rel_bucket(rel), -1).astype(np.int32).T)


TM_PROJ = 512
SAMPLE_BB = 8


def _layer(xp, xs, s_hgrn, win_k, win_v, pp, ps, norm_pre, w_in, hgrn_lb, hgrn_norm, attn_sink,
           rel_bias, w_pa, w_pb, w_o, norm_post, w_ple, w_ple_gate):
    batch, seq, _ = xp.shape
    dbatch, t_len, _ = xs.shape
    kw = B_KV_HEADS * B_HD

    w16 = w_in.astype(BF16)
    g_pre = norm_pre.reshape(1, D_MODEL)
    g_post = norm_post.reshape(1, D_MODEL)
    g_hgrn = hgrn_norm.reshape(1, D_MODEL)
    wpa, wpb, wo = w_pa.astype(BF16), w_pb.astype(BF16), w_o.astype(BF16)
    wple, wg = w_ple.astype(BF16), w_ple_gate.astype(BF16)

    xp2 = xp.reshape(batch * seq, D_MODEL)
    xs2 = xs.reshape(dbatch * t_len, D_MODEL)
    k_all = np.arange(N_KEYS)
    bucket_p = _bucket_table(np.arange(WINDOW) + WINDOW, k_all, np.ones((N_KEYS,), bool))
    rows_s = SAMPLE_BB * t_len
    q_row = np.arange(rows_s)
    new = k_all - WINDOW
    k_pos_s = np.where(new < 0, k_all, WINDOW + new % t_len)
    own = (new[None, :] < 0) | ((new[None, :] < rows_s) & (new[None, :] // t_len == q_row[:, None] // t_len))
    bucket_s = _bucket_table(WINDOW + q_row % t_len, k_pos_s, own)

    oa_p, ob_p, gab_p, st_p, kvwin_p = _mix_prompt(bucket_p, rel_bias, attn_sink, xp2, g_pre, w16,
                                                   hgrn_lb, g_hgrn, batch, seq)

    to_minor = lambda c: jnp.transpose(c, (0, 2, 3, 1)).reshape(dbatch, kw, WINDOW)
    from_minor = lambda c: jnp.transpose(c.reshape(dbatch, B_KV_HEADS, B_HD, WINDOW), (0, 3, 1, 2))
    pm_s, f_s, kv_s = _inproj(xs2, g_pre, w16, min(TM_PROJ, dbatch * t_len))
    oa_s, st_s, ob_s, nk_s, nv_s = _mix_sample(bucket_s, rel_bias, attn_sink, pm_s, f_s, kv_s, hgrn_lb, g_hgrn,
                                               s_hgrn, to_minor(win_k), to_minor(win_v),
                                               dbatch, t_len, SAMPLE_BB)

    y_p = _outproj(oa_p, ob_p, gab_p, 0, 1, xp2, pp.reshape(batch * seq, -1),
                   wpa, wpb, wo, g_post, wple, wg, TM_PROJ)
    y_s = _outproj(oa_s, ob_s, pm_s, G_GA, G_GB, xs2, ps.reshape(dbatch * t_len, -1),
                   wpa, wpb, wo, g_post, wple, wg, min(TM_PROJ, dbatch * t_len))

    k_win_p = kvwin_p[:, :, 0:kw].reshape(batch, WINDOW, B_KV_HEADS, B_HD)
    v_win_p = kvwin_p[:, :, kw:].reshape(batch, WINDOW, B_KV_HEADS, B_HD)
    return (y_p.reshape(batch, seq, D_MODEL), y_s.reshape(dbatch, t_len, D_MODEL), st_p, st_s,
            k_win_p, v_win_p,
            from_minor(nk_s), from_minor(nv_s))


def kernel(x_prompt, x_sample, state_hgrn, cache_swa_k, cache_swa_v, p_prompt, p_sample, norm_pre, w_in,
           hgrn_lb, hgrn_norm, attn_sink, rel_bias, w_pa, w_pb, w_o, norm_post, w_ple, w_ple_gate):
    depth = w_in.shape[0]
    assert depth == 1, "the forget-gate lower bound is implemented for a single layer"
    xp, xs = x_prompt, x_sample
    outs = []
    for l in range(depth):
        res = _layer(xp, xs, state_hgrn[l], cache_swa_k[l], cache_swa_v[l], p_prompt[l], p_sample[l],
                     norm_pre[l], w_in[l], hgrn_lb, hgrn_norm[l], attn_sink[l], rel_bias,
                     w_pa[l], w_pb[l], w_o[l], norm_post[l], w_ple[l], w_ple_gate[l])
        xp, xs = res[0], res[1]
        outs.append(res[2:])
    stack = lambda i: jnp.stack([o[i] for o in outs])
    return (xp, xs, stack(0), stack(1), stack(2), stack(3), stack(4), stack(5))
```
